```python
import math
import jax
import jax.numpy as jnp
from jax import lax
import numpy as np

D_MODEL = 1024
BATCH = 4
SEQ = 4096
DEPTH = 2

GRID_W = 64
HEAD_DIM = 64
QUERY_BLOCK = 128
ROPE_THETA = 500000.0
ROPE_FRACTION = 4
LN_EPS = 1e-5
NA_HEADS = 4
NA_ROWS = 8
NA_COLS = 16
DIFF_HEADS = 4
DIFF_QK_DIM = 32
DIFF_V_DIM = 2 * DIFF_QK_DIM
POOL_WINDOWS = (2, 4, 8, 16)
POOL_GROUP = 64
POOL_WIDTH = len(POOL_WINDOWS) * POOL_GROUP
DIL_HEADS = 4
DIL_PATTERNS = ((128, 1), (512, 4), (2048, 16))
N_BRANCHES = 4
BRANCH_WIDTH = 256
COLS_NA = 3 * NA_HEADS * HEAD_DIM
COLS_DIFF = 2 * DIFF_HEADS * 2 * DIFF_QK_DIM + DIFF_HEADS * DIFF_V_DIM
COLS_POOL = POOL_WIDTH
COLS_DIL = len(DIL_PATTERNS) * 3 * DIL_HEADS * HEAD_DIM
COLS_GATE = N_BRANCHES * D_MODEL
IN_SPLITS = (COLS_NA, COLS_NA + COLS_DIFF, COLS_NA + COLS_DIFF + COLS_POOL,
             COLS_NA + COLS_DIFF + COLS_POOL + COLS_DIL)
IN_COLS = IN_SPLITS[-1] + COLS_GATE
N_GROUPS = 4
EXPERTS_PER_GROUP = 8
N_EXPERTS = N_GROUPS * EXPERTS_PER_GROUP
TOP_K = 2
D_EXPERT = 512
MOE_BLOCK = 128
ALPHA = (2 * DEPTH) ** 0.25
BETA = (8 * DEPTH) ** -0.25

kernel_name = 'hybrid_gated_encoder_hmoe'


def layer_norm(x, g, b):
    xf = x.astype(jnp.float32)
    mu = jnp.mean(xf, axis=-1, keepdims=True)
    var = jnp.mean(jnp.square(xf - mu), axis=-1, keepdims=True)
    return ((xf - mu) * lax.rsqrt(var + LN_EPS) * g + b).astype(x.dtype)


def partial_rope(x, pos):
    rot = x.shape[-1] // ROPE_FRACTION
    half = rot // 2
    inv_freq = jnp.exp(jnp.arange(half, dtype=jnp.float32) * (-2.0 * math.log(ROPE_THETA) / rot))
    ang = pos.astype(jnp.float32)[:, None] * inv_freq[None, :]
    cos, sin = jnp.cos(ang), jnp.sin(ang)
    x1 = x[..., :half].astype(jnp.float32)
    x2 = x[..., half:rot].astype(jnp.float32)
    return jnp.concatenate([(x1 * cos - x2 * sin).astype(x.dtype),
                            (x1 * sin + x2 * cos).astype(x.dtype), x[..., rot:]], axis=-1)


def neighbourhood_attention(q, k, v, rpb):
    b, h, s, d = q.shape
    rows = s // GRID_W
    kr = min(NA_ROWS, rows)
    kc = NA_COLS
    qg = q.reshape(b, h, rows, GRID_W, d)
    kg = k.reshape(b, h, rows, GRID_W, d)
    vg = v.reshape(b, h, rows, GRID_W, d)
    r = jnp.arange(rows)
    row_idx = jnp.clip(r - kr // 2, 0, rows - kr)[:, None] + jnp.arange(kr)[None, :]
    k_rows = kg[:, :, row_idx]
    v_rows = vg[:, :, row_idx]
    sc = jnp.einsum('bhrqd,bhrkcd->bhrqkc', qg, k_rows).astype(jnp.float32) * d ** -0.5
    col = jnp.arange(GRID_W)
    col_start = jnp.clip(col - kc // 2, 0, GRID_W - kc)
    in_win = (col[None, :] >= col_start[:, None]) & (col[None, :] < col_start[:, None] + kc)
    dr = row_idx - r[:, None] + NA_ROWS - 1
    dc = jnp.clip(col[None, :] - col[:, None] + kc - 1, 0, 2 * kc - 2)
    bias = rpb[:, dr][..., dc].transpose(0, 1, 3, 2, 4).astype(jnp.float32)
    sc = jnp.where(in_win[:, None, :], sc + bias[None], -jnp.inf)
    p = jax.nn.softmax(sc, axis=(-2, -1))
    o = jnp.einsum('bhrqkc,bhrkcd->bhrqd', p.astype(v.dtype), v_rows)
    return o.reshape(b, h, s, d)


def differential_attention(q, k, v, lam, lam_init, subln_g):
    b, h, _, s, dq = q.shape
    nb = s // QUERY_BLOCK
    qb = q.reshape(b, h, 2, nb, QUERY_BLOCK, dq).transpose(3, 0, 1, 2, 4, 5)

    def block(qblk):
        sc = jnp.einsum('bhmqd,bhmkd->bhmqk', qblk, k).astype(jnp.float32) * dq ** -0.5
        p = jax.nn.softmax(sc, axis=-1)
        a = p[:, :, 0] - lam * p[:, :, 1]
        return jnp.einsum('bhqk,bhkd->bhqd', a.astype(v.dtype), v)

    o = lax.map(block, qb)
    o = o.transpose(1, 2, 0, 3, 4).reshape(b, h, s, -1).astype(jnp.float32)
    o = o * lax.rsqrt(jnp.mean(o * o, axis=-1, keepdims=True) + LN_EPS) * subln_g * (1.0 - lam_init)
    return o.astype(v.dtype)


def multiscale_pool(u, pool_w, pool_scale):
    b, s, _ = u.shape
    ng = len(POOL_WINDOWS)
    ug = u.reshape(b, s, ng, POOL_GROUP).astype(jnp.float32)
    cs = jnp.concatenate([jnp.zeros((b, 1, ng, POOL_GROUP), jnp.float32), jnp.cumsum(ug, axis=1)], axis=1)
    t = jnp.arange(s)
    outs = []
    for gi, w in enumerate(POOL_WINDOWS):
        lo = jnp.maximum(t - w // 2, 0)
        hi = jnp.minimum(t + w - 1 - w // 2, s - 1)
        csg = cs[:, :, gi]
        mean = (csg[:, hi + 1] - csg[:, lo]) / (hi - lo + 1).astype(jnp.float32)[:, None]
        outs.append(mean - ug[:, :, gi])
    dlt = jnp.stack(outs, axis=2).astype(u.dtype)
    y = jnp.einsum('bsgc,gce->bsge', dlt, pool_w).reshape(b, s, -1)
    return y * pool_scale


def dilated_pattern(q, k, v, window, dilation):
    b, h, s, d = q.shape
    half = window // (2 * dilation)
    length = s // dilation
    qb = math.gcd(QUERY_BLOCK, length)
    nb = length // qb
    band = qb + 2 * half

    def sub(x):
        return x.reshape(b, h, length, dilation, d).transpose(0, 1, 3, 2, 4)

    pad = ((0, 0), (0, 0), (0, 0), (half, half), (0, 0))
    kp = jnp.pad(sub(k), pad)
    vp = jnp.pad(sub(v), pad)
    idx = jnp.arange(nb)[:, None] * qb + jnp.arange(band)[None, :]
    kb = kp[:, :, :, idx]
    vb = vp[:, :, :, idx]
    qblk = sub(q).reshape(b, h, dilation, nb, qb, d)
    sc = jnp.einsum('bhmnqd,bhmnkd->bhmnqk', qblk, kb).astype(jnp.float32) * d ** -0.5
    a = jnp.arange(qb)[:, None]
    c = jnp.arange(band)[None, :]
    in_win = (c >= a) & (c <= a + 2 * half)
    in_seq = (idx >= half) & (idx < length + half)
    sc = jnp.where(in_win[None] & in_seq[:, None, :], sc, -jnp.inf)
    m = jnp.max(sc, axis=-1, keepdims=True)
    e = jnp.exp(sc - m)
    den = jnp.sum(e, axis=-1, keepdims=True)
    o = jnp.einsum('bhmnqk,bhmnkd->bhmnqd', (e / den).astype(v.dtype), vb)
    lse = (m + jnp.log(den))[..., 0]
    o = o.reshape(b, h, dilation, length, d).transpose(0, 1, 3, 2, 4).reshape(b, h, s, d)
    lse = lse.reshape(b, h, dilation, length).transpose(0, 1, 3, 2).reshape(b, h, s)
    return o, lse


def token_mixing(hs, pos, lam_init, w_in, b_gate, na_rpb, diff_lam, diff_subln_g,
                 pool_w, pool_scale, w_branch, w_out):
    b, s, dm = hs.shape
    z = hs @ w_in
    z_na, z_diff, z_pool, z_dil, z_gate = jnp.split(z, IN_SPLITS, axis=-1)

    def heads(t, n):
        return t.reshape(b, s, n, -1).transpose(0, 2, 1, 3)

    def unheads(t):
        return t.transpose(0, 2, 1, 3).reshape(b, s, -1).astype(hs.dtype)

    qa, ka, va = jnp.split(z_na, 3, axis=-1)
    y_a = unheads(neighbourhood_attention(heads(qa, NA_HEADS), heads(ka, NA_HEADS), heads(va, NA_HEADS), na_rpb))

    qkw = DIFF_HEADS * 2 * DIFF_QK_DIM
    qd = z_diff[..., :qkw].reshape(b, s, DIFF_HEADS, 2, DIFF_QK_DIM).transpose(0, 2, 3, 1, 4)
    kd = z_diff[..., qkw:2 * qkw].reshape(b, s, DIFF_HEADS, 2, DIFF_QK_DIM).transpose(0, 2, 3, 1, 4)
    vd = heads(z_diff[..., 2 * qkw:], DIFF_HEADS)
    dl = diff_lam.astype(jnp.float32)
    lam = jnp.exp(jnp.sum(dl[0] * dl[1])) - jnp.exp(jnp.sum(dl[2] * dl[3])) + lam_init
    y_b = unheads(differential_attention(partial_rope(qd, pos), partial_rope(kd, pos), vd,
                                         lam, lam_init, diff_subln_g))

    y_c = multiscale_pool(z_pool, pool_w, pool_scale).astype(hs.dtype)

    zd = z_dil.reshape(b, s, len(DIL_PATTERNS), 3, DIL_HEADS, HEAD_DIM).transpose(2, 3, 0, 4, 1, 5)
    outs, lses = [], []
    for pi, (w, r) in enumerate(DIL_PATTERNS):
        o, l = dilated_pattern(partial_rope(zd[pi, 0], pos), partial_rope(zd[pi, 1], pos), zd[pi, 2], w, r)
        outs.append(o)
        lses.append(l)
    o_all = jnp.stack(outs)
    w_all = jax.nn.softmax(jnp.stack(lses), axis=0)
    y_d = jnp.einsum('pbhs,pbhsd->bshd', w_all.astype(o_all.dtype), o_all).reshape(b, s, -1).astype(hs.dtype)

    ys = jnp.stack([y_a, y_b, y_c, y_d], axis=2)
    gates = jax.nn.sigmoid((z_gate + b_gate).astype(jnp.float32)).reshape(b, s, N_BRANCHES, dm)
    proj = jnp.einsum('bsnc,ncd->bsnd', ys, w_branch)
    merged = jnp.sum(gates.astype(proj.dtype) * proj, axis=2)
    return merged @ w_out


def routed_experts(t, eid, gate, w_gate, w_up, w_down):
    n_tok, dm = t.shape
    n_slots = n_tok * TOP_K
    n_exp = w_gate.shape[0]
    flat_e = eid.reshape(-1).astype(jnp.int32)
    order = jnp.argsort(flat_e)
    sorted_e = flat_e[order]
    counts = jnp.zeros((n_exp,), jnp.int32).at[flat_e].add(1)
    padded = (counts + MOE_BLOCK - 1) // MOE_BLOCK * MOE_BLOCK
    pad_end = jnp.cumsum(padded)
    pad_start = pad_end - padded
    cnt_start = jnp.cumsum(counts) - counts
    dest = pad_start[sorted_e] + jnp.arange(n_slots, dtype=jnp.int32) - cnt_start[sorted_e]
    n_blocks = (n_slots + n_exp * (MOE_BLOCK - 1) + MOE_BLOCK - 1) // MOE_BLOCK
    cap = n_blocks * MOE_BLOCK
    buf_tok = jnp.full((cap,), n_tok, jnp.int32).at[dest].set((order // TOP_K).astype(jnp.int32))
    t_pad = jnp.concatenate([t, jnp.zeros((1, dm), t.dtype)], axis=0)
    xs = t_pad[buf_tok].reshape(n_blocks, MOE_BLOCK, dm)
    block_e = jnp.minimum(jnp.searchsorted(pad_end, jnp.arange(n_blocks, dtype=jnp.int32) * MOE_BLOCK,
                                           side='right'), n_exp - 1)

    def expert_block(args):
        xb, e = args
        return (jax.nn.silu(xb @ w_gate[e]) * (xb @ w_up[e])) @ w_down[e]

    ys = lax.map(expert_block, (xs, block_e)).reshape(cap, dm)
    slot_dest = jnp.zeros((n_slots,), jnp.int32).at[order].set(dest)
    y = ys[slot_dest].reshape(n_tok, TOP_K, dm)
    return jnp.einsum('tk,tkd->td', gate.astype(y.dtype), y)


def hierarchical_moe(hs, rg_w, rg_b, re_w, re_b, w_gate, w_up, w_down):
    b, s, dm = hs.shape
    t = hs.reshape(-1, dm)
    n_tok = t.shape[0]
    tok = jnp.arange(n_tok)
    glog = (t @ rg_w + rg_b).astype(jnp.float32)
    gsel = jnp.argmax(glog, axis=-1)
    pg = jax.nn.softmax(glog, axis=-1)[tok, gsel]
    elog = (t @ re_w + re_b).astype(jnp.float32).reshape(n_tok, N_GROUPS, EXPERTS_PER_GROUP)
    esel = elog[tok, gsel]
    top_v, top_i = lax.top_k(esel, TOP_K)
    gate = jax.nn.softmax(top_v, axis=-1) * pg[:, None]
    eid = gsel[:, None] * EXPERTS_PER_GROUP + top_i
    return routed_experts(t, eid, gate, w_gate, w_up, w_down).reshape(b, s, dm)


def setup_inputs(seed: int = 0) -> dict:
    key = jax.random.key(seed)
    ks = jax.random.split(key, 23)

    def nrm(k, shape, scale):
        return jax.random.normal(k, shape, jnp.float32) * scale

    L = DEPTH
    return {
        'x': nrm(ks[0], (BATCH, SEQ, D_MODEL), 1.0),
        'emb_ln_g': 1.0 + nrm(ks[1], (D_MODEL,), 0.02),
        'emb_ln_b': nrm(ks[2], (D_MODEL,), 0.02),
        'w_in': nrm(ks[3], (L, D_MODEL, IN_COLS), D_MODEL ** -0.5),
        'b_gate': nrm(ks[4], (L, COLS_GATE), 0.02),
        'na_rpb': nrm(ks[5], (L, NA_HEADS, 2 * NA_ROWS - 1, 2 * NA_COLS - 1), 0.1),
        'diff_lam': nrm(ks[6], (L, 4, DIFF_QK_DIM), 0.1),
        'diff_subln_g': 1.0 + nrm(ks[7], (L, DIFF_V_DIM), 0.02),
        'pool_w': nrm(ks[8], (L, len(POOL_WINDOWS), POOL_GROUP, POOL_GROUP), POOL_GROUP ** -0.5),
        'pool_scale': 1.0 + nrm(ks[9], (L, POOL_WIDTH), 0.02),
        'w_branch': nrm(ks[10], (L, N_BRANCHES, BRANCH_WIDTH, D_MODEL), BETA * BRANCH_WIDTH ** -0.5),
        'w_out': nrm(ks[11], (L, D_MODEL, D_MODEL), BETA * D_MODEL ** -0.5),
        'ln1_g': 1.0 + nrm(ks[12], (L, D_MODEL), 0.02),
        'ln1_b': nrm(ks[13], (L, D_MODEL), 0.02),
        'router_group_w': nrm(ks[14], (L, D_MODEL, N_GROUPS), D_MODEL ** -0.5),
        'router_group_b': nrm(ks[15], (L, N_GROUPS), 0.01),
        'router_expert_w': nrm(ks[16], (L, D_MODEL, N_EXPERTS), D_MODEL ** -0.5),
        'router_expert_b': nrm(ks[17], (L, N_EXPERTS), 0.01),
        'expert_w_gate': nrm(ks[18], (L, N_EXPERTS, D_MODEL, D_EXPERT), D_MODEL ** -0.5),
        'expert_w_up': nrm(ks[19], (L, N_EXPERTS, D_MODEL, D_EXPERT), BETA * D_MODEL ** -0.5),
        'expert_w_down': nrm(ks[20], (L, N_EXPERTS, D_EXPERT, D_MODEL), BETA * D_EXPERT ** -0.5),
        'ln2_g': 1.0 + nrm(ks[21], (L, D_MODEL), 0.02),
        'ln2_b': nrm(ks[22], (L, D_MODEL), 0.02),
    }


def reference(x, emb_ln_g, emb_ln_b, w_in, b_gate, na_rpb, diff_lam, diff_subln_g, pool_w, pool_scale,
              w_branch, w_out, ln1_g, ln1_b, router_group_w, router_group_b, router_expert_w,
              router_expert_b, expert_w_gate, expert_w_up, expert_w_down, ln2_g, ln2_b):
    pos = jnp.arange(x.shape[1], dtype=jnp.int32)
    h = layer_norm(x, emb_ln_g, emb_ln_b)
    for l in range(DEPTH):
        lam_init = 0.8 - 0.6 * math.exp(-0.3 * l)
        mix = token_mixing(h, pos, lam_init, w_in[l], b_gate[l], na_rpb[l], diff_lam[l], diff_subln_g[l],
                           pool_w[l], pool_scale[l], w_branch[l], w_out[l])
        h = layer_norm(ALPHA * h + mix, ln1_g[l], ln1_b[l])
        ffn = hierarchical_moe(h, router_group_w[l], router_group_b[l], router_expert_w[l], router_expert_b[l],
                               expert_w_gate[l], expert_w_up[l], expert_w_down[l])
        h = layer_norm(ALPHA * h + ffn, ln2_g[l], ln2_b[l])
    return h
```

```python
import functools
import math

import jax
import jax.numpy as jnp
import numpy as np
from jax import lax
from jax.experimental import pallas as pl
from jax.experimental.pallas import tpu as pltpu

F32 = jnp.float32
BF16 = jnp.bfloat16

LANES = 128
GRID_W = 64
HEAD_DIM = 64
ROPE_THETA = 500000.0
LN_EPS = 1e-5
NA_ROWS = 8
NA_COLS = 16
DIFF_QK_DIM = 32
POOL_WINDOWS = (2, 4, 8, 16)
POOL_GROUP = 64
DIL_PATTERNS = ((128, 1), (512, 4), (2048, 16))
DIL_HALF = 64
N_GROUPS = 4
EXPERTS_PER_GROUP = 8
N_EXPERTS = N_GROUPS * EXPERTS_PER_GROUP
TOP_K = 2
MOE_ROWS = 256
NEG = -1e30

CB_NA = 0
CB_DIFF = 6
CB_POOL256 = 6
CB_DIL = 14
CB_GATE4096 = 1

VMEM_LIMIT = 56 * 1024 * 1024


def _cparams(sem, vmem=VMEM_LIMIT):
    return pltpu.CompilerParams(dimension_semantics=sem, vmem_limit_bytes=vmem)


def _layer_norm(x, g, b):
    mu = jnp.mean(x, axis=-1, keepdims=True)
    xc = x - mu
    var = jnp.mean(xc * xc, axis=-1, keepdims=True)
    return xc * lax.rsqrt(var + LN_EPS) * g + b


def _dot(a, b):
    return jnp.dot(a, b, preferred_element_type=F32)


def _dot_nt(a, b):
    return lax.dot_general(a, b, (((1,), (1,)), ((), ())), preferred_element_type=F32)


def _lane(shape):
    return lax.broadcasted_iota(jnp.int32, shape, len(shape) - 1)


def _ln_kernel(x_ref, g_ref, b_ref, o_ref):
    o_ref[...] = _layer_norm(x_ref[...], g_ref[...], b_ref[...])


def _embed_ln(x2, g, b):
    t, d = x2.shape
    tm = 1024
    return pl.pallas_call(
        _ln_kernel,
        grid=(t // tm,),
        in_specs=[pl.BlockSpec((tm, d), lambda i: (i, 0)),
                  pl.BlockSpec((1, d), lambda i: (0, 0)),
                  pl.BlockSpec((1, d), lambda i: (0, 0))],
        out_specs=pl.BlockSpec((tm, d), lambda i: (i, 0)),
        out_shape=jax.ShapeDtypeStruct((t, d), F32),
        compiler_params=_cparams(("parallel",)),
        name="embed_ln",
    )(x2, g.reshape(1, d), b.reshape(1, d))


def _in_proj_kernel(x_ref, w_ref, o_ref):
    o_ref[...] = _dot(x_ref[...].astype(BF16), w_ref[...])


def _in_proj(h2, w_bf16):
    t, d = h2.shape
    n = w_bf16.shape[1]
    tm, tn = 1024, 1024
    return pl.pallas_call(
        _in_proj_kernel,
        grid=(t // tm, n // tn),
        in_specs=[pl.BlockSpec((tm, d), lambda i, j: (i, 0)),
                  pl.BlockSpec((d, tn), lambda i, j: (0, j))],
        out_specs=pl.BlockSpec((tm, tn), lambda i, j: (i, j)),
        out_shape=jax.ShapeDtypeStruct((t, n), F32),
        compiler_params=_cparams(("parallel", "parallel")),
        name="in_proj",
    )(h2, w_bf16)


def _rope_tables(seq, head_w):
    rot = head_w // 4
    half = rot // 2
    inv_freq = jnp.exp(jnp.arange(half, dtype=F32) * (-2.0 * math.log(ROPE_THETA) / rot))
    ang = jnp.arange(seq, dtype=jnp.int32).astype(F32)[:, None] * inv_freq[None, :]
    cos, sin = jnp.cos(ang), jnp.sin(ang)
    zero = jnp.zeros((seq, head_w - rot), F32)
    zh = jnp.zeros((seq, half), F32)
    t0 = jnp.concatenate([cos, cos, jnp.ones((seq, head_w - rot), F32)], axis=1)
    t1 = jnp.concatenate([-sin, zh, zero], axis=1)
    t2 = jnp.concatenate([zh, sin, zero], axis=1)
    reps = LANES // head_w
    return jnp.stack([jnp.tile(t0, (1, reps)), jnp.tile(t1, (1, reps)), jnp.tile(t2, (1, reps))])


def _rope(x, t_ref, half):
    return (x * t_ref[0] + pltpu.roll(x, LANES - half, 1) * t_ref[1]
            + pltpu.roll(x, half, 1) * t_ref[2])


def _prep_kernel(z_ref, t_ref, o_ref, s_ref, *, r, half):
    c = pl.program_id(2)
    rows = z_ref.shape[1]

    @pl.when(c < 4)
    def _():
        s_ref[...] = _rope(z_ref[0], t_ref, half)

    @pl.when(c >= 4)
    def _():
        s_ref[...] = z_ref[0]

    if r == 1:
        o_ref[0, 0] = s_ref[...].astype(BF16)
    else:
        for m in range(r):
            o_ref[0, m] = s_ref[pl.ds(m, rows // r, stride=r), :].astype(BF16)


def _prep(z3, tables, col_block, r, half):
    b, s, _ = z3.shape
    tr = 2048
    out = pl.pallas_call(
        functools.partial(_prep_kernel, r=r, half=half),
        grid=(b, s // tr, 6),
        in_specs=[pl.BlockSpec((1, tr, LANES), lambda bi, i, c: (bi, i, col_block + c)),
                  pl.BlockSpec((3, tr, LANES), lambda bi, i, c: (0, i, 0))],
        out_specs=pl.BlockSpec((1, r, tr // r, LANES), lambda bi, i, c: (bi, 0, i, c)),
        out_shape=jax.ShapeDtypeStruct((b, r, s // r, 6 * LANES), BF16),
        scratch_shapes=[pltpu.VMEM((tr, LANES), F32)],
        compiler_params=_cparams(("parallel", "parallel", "arbitrary")),
        name=f"prep_r{r}",
    )(z3, tables)
    return out.reshape(b, s, 6 * LANES)


def _na_bias_table(rpb):
    kr, kc = NA_ROWS, NA_COLS
    col = np.arange(GRID_W)
    col_start = np.clip(col - kc // 2, 0, GRID_W - kc)
    in_win = (col[None, :] >= col_start[:, None]) & (col[None, :] < col_start[:, None] + kc)
    dc = np.clip(col[None, :] - col[:, None] + kc - 1, 0, 2 * kc - 2)
    case = np.arange(kr)
    dr = np.arange(kr)[None, :] - case[:, None] + NA_ROWS - 1
    bias = rpb[:, dr][..., dc]
    bias = bias.transpose(0, 1, 3, 2, 4)
    bias = jnp.where(jnp.asarray(in_win)[None, None, :, None, :], bias.astype(F32), NEG)
    return bias.reshape(rpb.shape[0], kr, GRID_W, kr * GRID_W)


def _na_kernel(q_ref, k_ref, v_ref, bias_ref, o_ref, *, rows_per_step, n_rows):
    step = pl.program_id(2)
    lane = _lane((GRID_W, LANES))
    kwin = NA_ROWS * GRID_W

    def body(i, carry):
        r = step * rows_per_step + i
        start = jnp.clip(r - NA_ROWS // 2, 0, n_rows - NA_ROWS)
        case = r - start
        k0 = pl.multiple_of(start * GRID_W, GRID_W)
        q0 = pl.multiple_of(i * GRID_W, GRID_W)
        q = q_ref[0, pl.ds(q0, GRID_W), :]
        kw = k_ref[0, pl.ds(k0, kwin), :].astype(BF16)
        vw = v_ref[0, pl.ds(k0, kwin), :].astype(BF16)
        outs = []
        for hh in range(2):
            in_head = (lane >= hh * HEAD_DIM) & (lane < (hh + 1) * HEAD_DIM)
            qm = jnp.where(in_head, q, 0.0).astype(BF16)
            sc = _dot_nt(qm, kw) * HEAD_DIM ** -0.5 + bias_ref[hh, case]
            m = jnp.max(sc, axis=-1, keepdims=True)
            e = jnp.exp(sc - m)
            p = e / jnp.sum(e, axis=-1, keepdims=True)
            outs.append(_dot(p.astype(BF16), vw))
        o = jnp.where(lane < HEAD_DIM, outs[0], outs[1])
        o_ref[0, pl.ds(q0, GRID_W), :] = o.astype(BF16)
        return carry

    lax.fori_loop(0, rows_per_step, body, 0)


def _na_attention(z3, bias):
    b, s, _ = z3.shape
    n_rows = s // GRID_W
    rps = 8
    tq = rps * GRID_W
    return pl.pallas_call(
        functools.partial(_na_kernel, rows_per_step=rps, n_rows=n_rows),
        grid=(b, 2, n_rows // rps),
        in_specs=[pl.BlockSpec((1, tq, LANES), lambda bi, hp, i: (bi, i, CB_NA + hp)),
                  pl.BlockSpec((1, s, LANES), lambda bi, hp, i: (bi, 0, CB_NA + 2 + hp)),
                  pl.BlockSpec((1, s, LANES), lambda bi, hp, i: (bi, 0, CB_NA + 4 + hp)),
                  pl.BlockSpec((2, NA_ROWS, GRID_W, NA_ROWS * GRID_W), lambda bi, hp, i: (hp, 0, 0, 0))],
        out_specs=pl.BlockSpec((1, tq, LANES), lambda bi, hp, i: (bi, i, hp)),
        out_shape=jax.ShapeDtypeStruct((b, s, 2 * LANES), BF16),
        compiler_params=_cparams(("parallel", "parallel", "arbitrary")),
        name="na_attn",
    )(z3, z3, z3, bias)


def _diff_kernel(lam_ref, q_ref, k_ref, v_ref, g_ref, o_ref, *, lam_init):
    tq = q_ref.shape[1]
    lane = _lane((tq, LANES))
    dl = lam_ref[...]
    lam = (jnp.exp(jnp.sum(dl[0:1] * dl[1:2], axis=-1, keepdims=True))
           - jnp.exp(jnp.sum(dl[2:3] * dl[3:4], axis=-1, keepdims=True)) + lam_init)
    q = q_ref[0]
    k = k_ref[0]
    v = v_ref[0]
    outs = []
    for hh in range(2):
        ps = []
        for m in range(2):
            lo = hh * HEAD_DIM + m * DIFF_QK_DIM
            qm = jnp.where((lane >= lo) & (lane < lo + DIFF_QK_DIM), q, jnp.zeros_like(q))
            sc = _dot_nt(qm, k) * DIFF_QK_DIM ** -0.5
            mx = jnp.max(sc, axis=-1, keepdims=True)
            e = jnp.exp(sc - mx)
            ps.append(e / jnp.sum(e, axis=-1, keepdims=True))
        a = ps[0] - lam * ps[1]
        outs.append(_dot(a.astype(BF16), v))
    in_h0 = lane < HEAD_DIM
    o = jnp.where(in_h0, outs[0], outs[1])
    o2 = o * o
    ms0 = jnp.sum(jnp.where(in_h0, o2, 0.0), axis=-1, keepdims=True) / HEAD_DIM
    ms1 = jnp.sum(jnp.where(in_h0, 0.0, o2), axis=-1, keepdims=True) / HEAD_DIM
    ms = jnp.where(in_h0, ms0, ms1)
    o = o * lax.rsqrt(ms + LN_EPS) * g_ref[...] * (1.0 - lam_init)
    o_ref[0] = o.astype(BF16)


def _diff_attention(qkv, diff_lam, subln_g, lam_init):
    b, s, _ = qkv.shape
    tq = 128
    g2 = jnp.tile(subln_g.reshape(1, HEAD_DIM), (1, 2))
    return pl.pallas_call(
        functools.partial(_diff_kernel, lam_init=lam_init),
        grid=(b, 2, s // tq),
        in_specs=[pl.BlockSpec((4, DIFF_QK_DIM), lambda bi, hp, i: (0, 0)),
                  pl.BlockSpec((1, tq, LANES), lambda bi, hp, i: (bi, i, hp)),
                  pl.BlockSpec((1, s, LANES), lambda bi, hp, i: (bi, 0, 2 + hp)),
                  pl.BlockSpec((1, s, LANES), lambda bi, hp, i: (bi, 0, 4 + hp)),
                  pl.BlockSpec((1, LANES), lambda bi, hp, i: (0, 0))],
        out_specs=pl.BlockSpec((1, tq, LANES), lambda bi, hp, i: (bi, i, hp)),
        out_shape=jax.ShapeDtypeStruct((b, s, 2 * LANES), BF16),
        compiler_params=_cparams(("parallel", "parallel", "arbitrary")),
        name="diff_attn",
    )(diff_lam, qkv, qkv, qkv, g2)


POOL_PAD = 16
POOL_CHUNK = 512
POOL_HALO = 8


def _pool_kernel(u_ref, w_ref, sc_ref, o_ref, p_ref):
    s = u_ref.shape[1]
    width = u_ref.shape[2]
    p_ref[0:POOL_PAD, :] = jnp.zeros((POOL_PAD, width), F32)
    p_ref[POOL_PAD + s:POOL_PAD + s + POOL_PAD, :] = jnp.zeros((POOL_PAD, width), F32)
    p_ref[POOL_PAD:POOL_PAD + s, :] = u_ref[0]
    n = POOL_CHUNK + 2 * POOL_HALO
    lane = _lane((POOL_CHUNK, width))
    row = lax.broadcasted_iota(jnp.int32, (POOL_CHUNK, width), 0)
    w_of_lane = jnp.where(lane < POOL_GROUP, 2, jnp.where(lane < 2 * POOL_GROUP, 4,
                          jnp.where(lane < 3 * POOL_GROUP, 8, 16)))

    def body(ci, carry):
        c0 = pl.multiple_of(ci * POOL_CHUNK, POOL_CHUNK)
        x = p_ref[pl.ds(c0 + POOL_PAD - POOL_HALO, n), :]
        w2 = x + pltpu.roll(x, 1, 0)
        w4 = pltpu.roll(w2, 1, 0) + pltpu.roll(w2, n - 1, 0)
        w8 = pltpu.roll(w4, 2, 0) + pltpu.roll(w4, n - 2, 0)
        w16 = pltpu.roll(w8, 4, 0) + pltpu.roll(w8, n - 4, 0)
        u = x[POOL_HALO:POOL_HALO + POOL_CHUNK]
        wsum = jnp.where(lane < POOL_GROUP, w2[POOL_HALO:POOL_HALO + POOL_CHUNK],
                         jnp.where(lane < 2 * POOL_GROUP, w4[POOL_HALO:POOL_HALO + POOL_CHUNK],
                                   jnp.where(lane < 3 * POOL_GROUP, w8[POOL_HALO:POOL_HALO + POOL_CHUNK],
                                             w16[POOL_HALO:POOL_HALO + POOL_CHUNK])))
        t = row + c0
        half_w = w_of_lane // 2
        lo = jnp.maximum(t - half_w, 0)
        hi = jnp.minimum(t + w_of_lane - 1 - half_w, s - 1)
        cnt = (hi - lo + 1).astype(F32)
        dlt = wsum / cnt - u
        y = _dot(dlt.astype(BF16), w_ref[...]) * sc_ref[...]
        o_ref[0, pl.ds(c0, POOL_CHUNK), :] = y.astype(BF16)
        return carry

    lax.fori_loop(0, s // POOL_CHUNK, body, 0)


def _pool(z3, pool_w, pool_scale):
    b, s, _ = z3.shape
    width = len(POOL_WINDOWS) * POOL_GROUP
    wbd = jax.scipy.linalg.block_diag(*[pool_w[g] for g in range(len(POOL_WINDOWS))]).astype(BF16)
    return pl.pallas_call(
        _pool_kernel,
        grid=(b,),
        in_specs=[pl.BlockSpec((1, s, width), lambda bi: (bi, 0, CB_POOL256)),
                  pl.BlockSpec((width, width), lambda bi: (0, 0)),
                  pl.BlockSpec((1, width), lambda bi: (0, 0))],
        out_specs=pl.BlockSpec((1, s, width), lambda bi: (bi, 0, 0)),
        out_shape=jax.ShapeDtypeStruct((b, s, width), BF16),
        scratch_shapes=[pltpu.VMEM((s + 2 * POOL_PAD, width), F32)],
        compiler_params=_cparams(("parallel",)),
        name="pool",
    )(z3, wbd, pool_scale.reshape(1, width))


SWA_Q = 128
SWA_BAND = SWA_Q + 2 * DIL_HALF


def _swa_kernel(q_ref, k_ref, v_ref, o_ref, l_ref, *, length, chunk):
    c = pl.program_id(2)
    lane = _lane((SWA_Q, LANES))
    rel = (lax.broadcasted_iota(jnp.int32, (SWA_Q, SWA_BAND), 1)
           - lax.broadcasted_iota(jnp.int32, (SWA_Q, SWA_BAND), 0))
    for i in range(chunk // SWA_Q):
        r0 = c * chunk + i * SWA_Q
        l0 = r0 & (length - 1)
        lo = r0 - l0 + jnp.clip(l0 - DIL_HALF, 0, length - SWA_BAND)
        lo = pl.multiple_of(lo, DIL_HALF)
        q = q_ref[0, i * SWA_Q:(i + 1) * SWA_Q, :]
        kw = k_ref[0, pl.ds(lo, SWA_BAND), :]
        vw = v_ref[0, pl.ds(lo, SWA_BAND), :]
        d = rel + (lo - r0)
        valid = (d >= -DIL_HALF) & (d <= DIL_HALF)
        outs, lses = [], []
        for hh in range(2):
            in_head = (lane >= hh * HEAD_DIM) & (lane < (hh + 1) * HEAD_DIM)
            qm = jnp.where(in_head, q, jnp.zeros_like(q))
            sc = jnp.where(valid, _dot_nt(qm, kw) * HEAD_DIM ** -0.5, NEG)
            m = jnp.max(sc, axis=-1, keepdims=True)
            e = jnp.exp(sc - m)
            den = jnp.sum(e, axis=-1, keepdims=True)
            outs.append(_dot((e / den).astype(BF16), vw))
            lses.append(m + jnp.log(den))
        o_ref[0, i * SWA_Q:(i + 1) * SWA_Q, :] = jnp.where(lane < HEAD_DIM, outs[0], outs[1])
        l_ref[0, i * SWA_Q:(i + 1) * SWA_Q, :] = jnp.where(lane < HEAD_DIM, lses[0], lses[1])


def _swa(qkv, length):
    b, s, _ = qkv.shape
    chunk = 1024
    shp = jax.ShapeDtypeStruct((b, s, 2 * LANES), F32)
    return pl.pallas_call(
        functools.partial(_swa_kernel, length=length, chunk=chunk),
        grid=(b, 2, s // chunk),
        in_specs=[pl.BlockSpec((1, chunk, LANES), lambda bi, hp, c: (bi, c, hp)),
                  pl.BlockSpec((1, s, LANES), lambda bi, hp, c: (bi, 0, 2 + hp)),
                  pl.BlockSpec((1, s, LANES), lambda bi, hp, c: (bi, 0, 4 + hp))],
        out_specs=[pl.BlockSpec((1, chunk, LANES), lambda bi, hp, c: (bi, c, hp)),
                   pl.BlockSpec((1, chunk, LANES), lambda bi, hp, c: (bi, c, hp))],
        out_shape=[shp, shp],
        compiler_params=_cparams(("parallel", "parallel", "arbitrary")),
        name=f"swa_l{length}",
    )(qkv, qkv, qkv)


def _dil_merge_kernel(o0_ref, l0_ref, o1_ref, l1_ref, o2_ref, l2_ref, y_ref, so1, sl1, so2, sl2):
    tm = y_ref.shape[1]
    r1 = DIL_PATTERNS[1][1]
    r2 = DIL_PATTERNS[2][1]
    for m in range(r1):
        so1[pl.ds(m, tm // r1, stride=r1), :] = o1_ref[0, m]
        sl1[pl.ds(m, tm // r1, stride=r1), :] = l1_ref[0, m]
    for m in range(r2):
        so2[pl.ds(m, tm // r2, stride=r2), :] = o2_ref[0, m]
        sl2[pl.ds(m, tm // r2, stride=r2), :] = l2_ref[0, m]
    l0, l1, l2 = l0_ref[0], sl1[...], sl2[...]
    mx = jnp.maximum(jnp.maximum(l0, l1), l2)
    e0, e1, e2 = jnp.exp(l0 - mx), jnp.exp(l1 - mx), jnp.exp(l2 - mx)
    den = e0 + e1 + e2
    y = (e0 / den) * o0_ref[0] + (e1 / den) * so1[...] + (e2 / den) * so2[...]
    y_ref[0] = y.astype(BF16)


def _dil_merge(outs):
    (o0, l0), (o1, l1), (o2, l2) = outs
    b, s, w = o0.shape
    tm = 1024
    r1 = DIL_PATTERNS[1][1]
    r2 = DIL_PATTERNS[2][1]
    v1 = lambda a: a.reshape(b, r1, s // r1, w)
    v2 = lambda a: a.reshape(b, r2, s // r2, w)
    nat = pl.BlockSpec((1, tm, LANES), lambda bi, i, hp: (bi, i, hp))
    g1 = pl.BlockSpec((1, r1, tm // r1, LANES), lambda bi, i, hp: (bi, 0, i, hp))
    g2 = pl.BlockSpec((1, r2, tm // r2, LANES), lambda bi, i, hp: (bi, 0, i, hp))
    return pl.pallas_call(
        _dil_merge_kernel,
        grid=(b, s // tm, w // LANES),
        in_specs=[nat, nat, g1, g1, g2, g2],
        out_specs=nat,
        out_shape=jax.ShapeDtypeStruct((b, s, w), BF16),
        scratch_shapes=[pltpu.VMEM((tm, LANES), F32)] * 4,
        compiler_params=_cparams(("parallel", "parallel", "parallel")),
        name="dil_merge",
    )(o0, l0, v1(o1), v1(l1), v2(o2), v2(l2))


def _merge_kernel(ya_ref, yb_ref, yc_ref, yd_ref, zg_ref, bg_ref, wb_ref, wo_ref, h_ref, g_ref, b_ref,
                  o_ref, *, alpha):
    d = h_ref.shape[1]
    merged = None
    for n, y_ref in enumerate((ya_ref, yb_ref, yc_ref, yd_ref)):
        zg = zg_ref[:, n * d:(n + 1) * d] + bg_ref[:, n * d:(n + 1) * d]
        gate = 1.0 / (1.0 + jnp.exp(-zg))
        term = gate * _dot(y_ref[...], wb_ref[n])
        merged = term if merged is None else merged + term
    mix = _dot(merged.astype(BF16), wo_ref[...])
    o_ref[...] = _layer_norm(alpha * h_ref[...] + mix, g_ref[...], b_ref[...])


def _merge(ys, z2, b_gate, wb, wo, h2, g, bb, alpha):
    t, d = h2.shape
    bw = ys[0].shape[1]
    tm = 256
    yspec = pl.BlockSpec((tm, bw), lambda i: (i, 0))
    return pl.pallas_call(
        functools.partial(_merge_kernel, alpha=alpha),
        grid=(t // tm,),
        in_specs=[yspec, yspec, yspec, yspec,
                  pl.BlockSpec((tm, 4 * d), lambda i: (i, CB_GATE4096)),
                  pl.BlockSpec((1, 4 * d), lambda i: (0, 0)),
                  pl.BlockSpec((4, bw, d), lambda i: (0, 0, 0)),
                  pl.BlockSpec((d, d), lambda i: (0, 0)),
                  pl.BlockSpec((tm, d), lambda i: (i, 0)),
                  pl.BlockSpec((1, d), lambda i: (0, 0)),
                  pl.BlockSpec((1, d), lambda i: (0, 0))],
        out_specs=pl.BlockSpec((tm, d), lambda i: (i, 0)),
        out_shape=jax.ShapeDtypeStruct((t, d), F32),
        compiler_params=_cparams(("parallel",)),
        name="merge",
    )(*ys, z2, b_gate.reshape(1, 4 * d), wb, wo, h2, g.reshape(1, d), bb.reshape(1, d))


ROUTER_TM = 256


def _router_kernel(h_ref, w_ref, b_ref, o_ref, cnt_ref, carry):
    @pl.when(pl.program_id(0) == 0)
    def _():
        carry[...] = jnp.zeros_like(carry)

    tm = h_ref.shape[0]
    logits = jnp.dot(h_ref[...], w_ref[...], preferred_element_type=F32,
                     precision=lax.Precision.HIGHEST) + b_ref[...]
    lane = _lane((tm, LANES))
    big = jnp.int32(1 << 20)
    is_g = lane < N_GROUPS
    gl = jnp.where(is_g, logits, -jnp.inf)
    gmax = jnp.max(gl, axis=-1, keepdims=True)
    gsel = jnp.min(jnp.where(is_g & (gl == gmax), lane, big), axis=-1, keepdims=True)
    pg = 1.0 / jnp.sum(jnp.exp(gl - gmax), axis=-1, keepdims=True)
    e_lo = N_GROUPS + gsel * EXPERTS_PER_GROUP
    in_grp = (lane >= e_lo) & (lane < e_lo + EXPERTS_PER_GROUP)
    el = jnp.where(in_grp, logits, -jnp.inf)
    v1 = jnp.max(el, axis=-1, keepdims=True)
    i1 = jnp.min(jnp.where(in_grp & (el == v1), lane, big), axis=-1, keepdims=True)
    el2 = jnp.where(lane == i1, -jnp.inf, el)
    v2 = jnp.max(el2, axis=-1, keepdims=True)
    i2 = jnp.min(jnp.where(in_grp & (lane != i1) & (el2 == v2), lane, big), axis=-1, keepdims=True)
    t2 = jnp.exp(v2 - v1)
    g1 = pg / (1.0 + t2)
    g2 = pg * t2 / (1.0 + t2)
    oh1 = (lane == i1)
    oh2 = (lane == i2)
    tri = (lax.broadcasted_iota(jnp.int32, (tm, tm), 1)
           < lax.broadcasted_iota(jnp.int32, (tm, tm), 0)).astype(BF16)
    oh1b = oh1.astype(BF16)
    oh2b = oh2.astype(BF16)
    c0 = carry[...]
    c1 = c0 + jnp.sum(oh1b.astype(F32), axis=0, keepdims=True)
    rank1 = jnp.sum(jnp.where(oh1, _dot(tri, oh1b) + c0, 0.0), axis=-1, keepdims=True)
    rank2 = jnp.sum(jnp.where(oh2, _dot(tri, oh2b) + c1, 0.0), axis=-1, keepdims=True)
    c2 = c1 + jnp.sum(oh2b.astype(F32), axis=0, keepdims=True)
    carry[...] = c2
    cnt_ref[...] = c2
    e1 = (i1 - N_GROUPS).astype(F32)
    e2 = (i2 - N_GROUPS).astype(F32)
    out = jnp.where(lane == 0, e1, jnp.where(lane == 1, e2, jnp.where(lane == 2, g1, jnp.where(
        lane == 3, g2, jnp.where(lane == 4, rank1, jnp.where(lane == 5, rank2, 0.0))))))
    o_ref[...] = out


def _router(h2, rg_w, rg_b, re_w, re_b):
    t, d = h2.shape
    pad = LANES - N_GROUPS - N_EXPERTS
    w = jnp.concatenate([rg_w, re_w, jnp.zeros((d, pad), F32)], axis=1)
    bias = jnp.concatenate([rg_b, re_b, jnp.zeros((pad,), F32)]).reshape(1, LANES)
    tm = ROUTER_TM
    return pl.pallas_call(
        _router_kernel,
        grid=(t // tm,),
        in_specs=[pl.BlockSpec((tm, d), lambda i: (i, 0)),
                  pl.BlockSpec((d, LANES), lambda i: (0, 0)),
                  pl.BlockSpec((1, LANES), lambda i: (0, 0))],
        out_specs=[pl.BlockSpec((tm, LANES), lambda i: (i, 0)),
                   pl.BlockSpec((1, LANES), lambda i: (0, 0))],
        out_shape=[jax.ShapeDtypeStruct((t, LANES), F32), jax.ShapeDtypeStruct((1, LANES), F32)],
        scratch_shapes=[pltpu.VMEM((1, LANES), F32)],
        compiler_params=_cparams(("arbitrary",)),
        name="router",
    )(h2, w, bias)


def _moe_kernel(be_ref, nu_ref, x_ref, wg_ref, wu_ref, wd_ref, o_ref):
    j = pl.program_id(0)

    @pl.when(j < nu_ref[0])
    def _():
        x = x_ref[...]
        g = _dot(x, wg_ref[0])
        u = _dot(x, wu_ref[0])
        hmid = (g * (1.0 / (1.0 + jnp.exp(-g)))) * u
        o_ref[...] = _dot(hmid.astype(BF16), wd_ref[0])

    @pl.when(j >= nu_ref[0])
    def _():
        o_ref[...] = jnp.zeros_like(o_ref)


def _moe_experts(xs, block_e, n_used, wg, wu, wd):
    cap, d = xs.shape
    de = wg.shape[2]
    n_blocks = cap // MOE_ROWS
    grid_spec = pltpu.PrefetchScalarGridSpec(
        num_scalar_prefetch=2,
        grid=(n_blocks,),
        in_specs=[pl.BlockSpec((MOE_ROWS, d), lambda j, be, nu: (j, 0)),
                  pl.BlockSpec((1, d, de), lambda j, be, nu: (be[j], 0, 0)),
                  pl.BlockSpec((1, d, de), lambda j, be, nu: (be[j], 0, 0)),
                  pl.BlockSpec((1, de, d), lambda j, be, nu: (be[j], 0, 0))],
        out_specs=pl.BlockSpec((MOE_ROWS, d), lambda j, be, nu: (j, 0)),
    )
    return pl.pallas_call(
        _moe_kernel,
        grid_spec=grid_spec,
        out_shape=jax.ShapeDtypeStruct((cap, d), F32),
        compiler_params=_cparams(("arbitrary",)),
        name="moe_experts",
    )(block_e, n_used, xs, wg, wu, wd)


def _combine_kernel(h_ref, r_ref, y1_ref, y2_ref, g_ref, b_ref, o_ref, *, alpha):
    g1 = r_ref[:, 2:3]
    g2 = r_ref[:, 3:4]
    ffn = g1 * y1_ref[...] + g2 * y2_ref[...]
    o_ref[...] = _layer_norm(alpha * h_ref[...] + ffn, g_ref[...], b_ref[...])


def _combine(h2, routing, y1, y2, g, bb, alpha):
    t, d = h2.shape
    tm = 512
    row = pl.BlockSpec((tm, d), lambda i: (i, 0))
    vec = pl.BlockSpec((1, d), lambda i: (0, 0))
    return pl.pallas_call(
        functools.partial(_combine_kernel, alpha=alpha),
        grid=(t // tm,),
        in_specs=[row, pl.BlockSpec((tm, LANES), lambda i: (i, 0)), row, row, vec, vec],
        out_specs=row,
        out_shape=jax.ShapeDtypeStruct((t, d), F32),
        compiler_params=_cparams(("parallel",)),
        name="combine_ln",
    )(h2, routing, y1, y2, g.reshape(1, d), bb.reshape(1, d))


def _moe(h2, rg_w, rg_b, re_w, re_b, wg, wu, wd, g, bb, alpha):
    t, d = h2.shape
    routing, counts = _router(h2, rg_w, rg_b, re_w, re_b)
    eid = routing[:, 0:2].astype(jnp.int32)
    rank = routing[:, 4:6].astype(jnp.int32)
    cnt = counts[0, N_GROUPS:N_GROUPS + N_EXPERTS].astype(jnp.int32)
    padded = (cnt + MOE_ROWS - 1) // MOE_ROWS * MOE_ROWS
    pad_end = jnp.cumsum(padded)
    pad_start = pad_end - padded
    dest = pad_start[eid] + rank
    n_slots = t * TOP_K
    n_blocks = (n_slots + N_EXPERTS * (MOE_ROWS - 1) + MOE_ROWS - 1) // MOE_ROWS
    cap = n_blocks * MOE_ROWS
    block_e = jnp.minimum(jnp.searchsorted(pad_end, jnp.arange(n_blocks, dtype=jnp.int32) * MOE_ROWS,
                                           side='right'), N_EXPERTS - 1).astype(jnp.int32)
    n_used = (pad_end[-1:] // MOE_ROWS).astype(jnp.int32)
    tok = jnp.broadcast_to(jnp.arange(t, dtype=jnp.int32)[:, None], (t, TOP_K))
    buf_tok = jnp.full((cap,), t, jnp.int32).at[dest.reshape(-1)].set(tok.reshape(-1))
    h_pad = jnp.concatenate([h2.astype(BF16), jnp.zeros((1, d), BF16)], axis=0)
    xs = h_pad[buf_tok]
    ys = _moe_experts(xs, block_e, n_used, wg, wu, wd)
    y1 = ys[dest[:, 0]]
    y2 = ys[dest[:, 1]]
    return _combine(h2, routing, y1, y2, g, bb, alpha)


def kernel(x, emb_ln_g, emb_ln_b, w_in, b_gate, na_rpb, diff_lam, diff_subln_g, pool_w, pool_scale,
           w_branch, w_out, ln1_g, ln1_b, router_group_w, router_group_b, router_expert_w,
           router_expert_b, expert_w_gate, expert_w_up, expert_w_down, ln2_g, ln2_b):
    b, s, d = x.shape
    depth = w_in.shape[0]
    alpha = (2 * depth) ** 0.25
    t = b * s
    tab_diff = _rope_tables(s, DIFF_QK_DIM)
    tab_dil = _rope_tables(s, HEAD_DIM)
    h = _embed_ln(x.reshape(t, d), emb_ln_g, emb_ln_b)
    for l in range(depth):
        lam_init = 0.8 - 0.6 * math.exp(-0.3 * l)
        z = _in_proj(h, w_in[l].astype(BF16))
        z3 = z.reshape(b, s, -1)
        y_a = _na_attention(z3, _na_bias_table(na_rpb[l]))
        qkv_diff = _prep(z3, tab_diff, CB_DIFF, 1, DIFF_QK_DIM // 8)
        y_b = _diff_attention(qkv_diff, diff_lam[l], diff_subln_g[l], lam_init)
        y_c = _pool(z3, pool_w[l], pool_scale[l])
        dil = []
        for pi, (_, r) in enumerate(DIL_PATTERNS):
            qkv = _prep(z3, tab_dil, CB_DIL + 6 * pi, r, HEAD_DIM // 8)
            dil.append(_swa(qkv, s // r))
        y_d = _dil_merge(dil)
        ys = [a.reshape(t, -1) for a in (y_a, y_b, y_c, y_d)]
        h = _merge(ys, z, b_gate[l], w_branch[l].astype(BF16), w_out[l].astype(BF16), h,
                   ln1_g[l], ln1_b[l], alpha)
        h = _moe(h, router_group_w[l], router_group_b[l], router_expert_w[l], router_expert_b[l],
                 expert_w_gate[l].astype(BF16), expert_w_up[l].astype(BF16),
                 expert_w_down[l].astype(BF16), ln2_g[l], ln2_b[l], alpha)
    return h.reshape(b, s, d)
```

```python
import functools
import math

import jax
import jax.numpy as jnp
import numpy as np
from jax import lax
from jax.experimental import pallas as pl
from jax.experimental.pallas import tpu as pltpu
from jax.experimental.pallas import tpu_sc as plsc

F32 = jnp.float32
BF16 = jnp.bfloat16

LANES = 128
GRID_W = 64
HEAD_DIM = 64
ROPE_THETA = 500000.0
LN_EPS = 1e-5
NA_ROWS = 8
NA_COLS = 16
DIFF_QK_DIM = 32
POOL_WINDOWS = (2, 4, 8, 16)
POOL_GROUP = 64
DIL_PATTERNS = ((128, 1), (512, 4), (2048, 16))
DIL_HALF = 64
N_GROUPS = 4
EXPERTS_PER_GROUP = 8
N_EXPERTS = N_GROUPS * EXPERTS_PER_GROUP
TOP_K = 2
MOE_ROWS = 256
NEG = -1e30

CB_NA = 0
CB_DIFF = 6
CB_POOL256 = 6
CB_DIL = 14
CB_GATE4096 = 1

VMEM_LIMIT = 56 * 1024 * 1024


def _cparams(sem, vmem=VMEM_LIMIT):
    return pltpu.CompilerParams(dimension_semantics=sem, vmem_limit_bytes=vmem)


def _layer_norm(x, g, b):
    mu = jnp.mean(x, axis=-1, keepdims=True)
    xc = x - mu
    var = jnp.mean(xc * xc, axis=-1, keepdims=True)
    return xc * lax.rsqrt(var + LN_EPS) * g + b


def _dot(a, b):
    return jnp.dot(a, b, preferred_element_type=F32)


def _dot_nt(a, b):
    return lax.dot_general(a, b, (((1,), (1,)), ((), ())), preferred_element_type=F32)


def _lane(shape):
    return lax.broadcasted_iota(jnp.int32, shape, len(shape) - 1)


def _ln_kernel(x_ref, g_ref, b_ref, o_ref):
    o_ref[...] = _layer_norm(x_ref[...], g_ref[...], b_ref[...])


def _embed_ln(x2, g, b):
    t, d = x2.shape
    tm = 1024
    return pl.pallas_call(
        _ln_kernel,
        grid=(t // tm,),
        in_specs=[pl.BlockSpec((tm, d), lambda i: (i, 0)),
                  pl.BlockSpec((1, d), lambda i: (0, 0)),
                  pl.BlockSpec((1, d), lambda i: (0, 0))],
        out_specs=pl.BlockSpec((tm, d), lambda i: (i, 0)),
        out_shape=jax.ShapeDtypeStruct((t, d), F32),
        compiler_params=_cparams(("parallel",)),
        name="embed_ln",
    )(x2, g.reshape(1, d), b.reshape(1, d))


def _in_proj_kernel(x_ref, w_ref, o_ref):
    o_ref[...] = _dot(x_ref[...].astype(BF16), w_ref[...])


def _in_proj(h2, w_bf16):
    t, d = h2.shape
    n = w_bf16.shape[1]
    tm, tn = 1024, 1024
    return pl.pallas_call(
        _in_proj_kernel,
        grid=(t // tm, n // tn),
        in_specs=[pl.BlockSpec((tm, d), lambda i, j: (i, 0)),
                  pl.BlockSpec((d, tn), lambda i, j: (0, j))],
        out_specs=pl.BlockSpec((tm, tn), lambda i, j: (i, j)),
        out_shape=jax.ShapeDtypeStruct((t, n), F32),
        compiler_params=_cparams(("parallel", "parallel")),
        name="in_proj",
    )(h2, w_bf16)


def _rope_tables(seq, head_w):
    rot = head_w // 4
    half = rot // 2
    inv_freq = jnp.exp(jnp.arange(half, dtype=F32) * (-2.0 * math.log(ROPE_THETA) / rot))
    ang = jnp.arange(seq, dtype=jnp.int32).astype(F32)[:, None] * inv_freq[None, :]
    cos, sin = jnp.cos(ang), jnp.sin(ang)
    zero = jnp.zeros((seq, head_w - rot), F32)
    zh = jnp.zeros((seq, half), F32)
    t0 = jnp.concatenate([cos, cos, jnp.ones((seq, head_w - rot), F32)], axis=1)
    t1 = jnp.concatenate([-sin, zh, zero], axis=1)
    t2 = jnp.concatenate([zh, sin, zero], axis=1)
    reps = LANES // head_w
    return jnp.stack([jnp.tile(t0, (1, reps)), jnp.tile(t1, (1, reps)), jnp.tile(t2, (1, reps))])


def _rope(x, t_ref, half):
    return (x * t_ref[0] + pltpu.roll(x, LANES - half, 1) * t_ref[1]
            + pltpu.roll(x, half, 1) * t_ref[2])


def _prep_kernel(z_ref, t_ref, o_ref, s_ref, *, r, half):
    c = pl.program_id(2)
    rows = z_ref.shape[1]

    @pl.when(c < 4)
    def _():
        s_ref[...] = _rope(z_ref[0], t_ref, half)

    @pl.when(c >= 4)
    def _():
        s_ref[...] = z_ref[0]

    if r == 1:
        o_ref[0, 0] = s_ref[...].astype(BF16)
    else:
        for m in range(r):
            o_ref[0, m] = s_ref[pl.ds(m, rows // r, stride=r), :].astype(BF16)


def _prep(z3, tables, col_block, r, half):
    b, s, _ = z3.shape
    tr = 2048
    out = pl.pallas_call(
        functools.partial(_prep_kernel, r=r, half=half),
        grid=(b, s // tr, 6),
        in_specs=[pl.BlockSpec((1, tr, LANES), lambda bi, i, c: (bi, i, col_block + c)),
                  pl.BlockSpec((3, tr, LANES), lambda bi, i, c: (0, i, 0))],
        out_specs=pl.BlockSpec((1, r, tr // r, LANES), lambda bi, i, c: (bi, 0, i, c)),
        out_shape=jax.ShapeDtypeStruct((b, r, s // r, 6 * LANES), BF16),
        scratch_shapes=[pltpu.VMEM((tr, LANES), F32)],
        compiler_params=_cparams(("parallel", "parallel", "arbitrary")),
        name=f"prep_r{r}",
    )(z3, tables)
    return out.reshape(b, s, 6 * LANES)


def _na_bias_table(rpb):
    kr, kc = NA_ROWS, NA_COLS
    col = np.arange(GRID_W)
    col_start = np.clip(col - kc // 2, 0, GRID_W - kc)
    in_win = (col[None, :] >= col_start[:, None]) & (col[None, :] < col_start[:, None] + kc)
    dc = np.clip(col[None, :] - col[:, None] + kc - 1, 0, 2 * kc - 2)
    case = np.arange(kr)
    dr = np.arange(kr)[None, :] - case[:, None] + NA_ROWS - 1
    bias = rpb[:, dr][..., dc]
    bias = bias.transpose(0, 1, 3, 2, 4)
    bias = jnp.where(jnp.asarray(in_win)[None, None, :, None, :], bias.astype(F32), NEG)
    return bias.reshape(rpb.shape[0], kr, GRID_W, kr * GRID_W)


def _na_kernel(q_ref, k_ref, v_ref, bias_ref, o_ref, *, rows_per_step, n_rows):
    step = pl.program_id(2)
    lane = _lane((GRID_W, LANES))
    kwin = NA_ROWS * GRID_W

    def body(i, carry):
        r = step * rows_per_step + i
        start = jnp.clip(r - NA_ROWS // 2, 0, n_rows - NA_ROWS)
        case = r - start
        k0 = pl.multiple_of(start * GRID_W, GRID_W)
        q0 = pl.multiple_of(i * GRID_W, GRID_W)
        q = q_ref[0, pl.ds(q0, GRID_W), :]
        kw = k_ref[0, pl.ds(k0, kwin), :].astype(BF16)
        vw = v_ref[0, pl.ds(k0, kwin), :].astype(BF16)
        outs = []
        for hh in range(2):
            in_head = (lane >= hh * HEAD_DIM) & (lane < (hh + 1) * HEAD_DIM)
            qm = jnp.where(in_head, q, 0.0).astype(BF16)
            sc = _dot_nt(qm, kw) * HEAD_DIM ** -0.5 + bias_ref[hh, case]
            m = jnp.max(sc, axis=-1, keepdims=True)
            e = jnp.exp(sc - m)
            p = e / jnp.sum(e, axis=-1, keepdims=True)
            outs.append(_dot(p.astype(BF16), vw))
        o = jnp.where(lane < HEAD_DIM, outs[0], outs[1])
        o_ref[0, pl.ds(q0, GRID_W), :] = o.astype(BF16)
        return carry

    lax.fori_loop(0, rows_per_step, body, 0)


def _na_attention(z3, bias):
    b, s, _ = z3.shape
    n_rows = s // GRID_W
    rps = 8
    tq = rps * GRID_W
    return pl.pallas_call(
        functools.partial(_na_kernel, rows_per_step=rps, n_rows=n_rows),
        grid=(b, 2, n_rows // rps),
        in_specs=[pl.BlockSpec((1, tq, LANES), lambda bi, hp, i: (bi, i, CB_NA + hp)),
                  pl.BlockSpec((1, s, LANES), lambda bi, hp, i: (bi, 0, CB_NA + 2 + hp)),
                  pl.BlockSpec((1, s, LANES), lambda bi, hp, i: (bi, 0, CB_NA + 4 + hp)),
                  pl.BlockSpec((2, NA_ROWS, GRID_W, NA_ROWS * GRID_W), lambda bi, hp, i: (hp, 0, 0, 0))],
        out_specs=pl.BlockSpec((1, tq, LANES), lambda bi, hp, i: (bi, i, hp)),
        out_shape=jax.ShapeDtypeStruct((b, s, 2 * LANES), BF16),
        compiler_params=_cparams(("parallel", "parallel", "arbitrary")),
        name="na_attn",
    )(z3, z3, z3, bias)


def _diff_kernel(lam_ref, q_ref, k_ref, v_ref, g_ref, o_ref, *, lam_init):
    tq = q_ref.shape[1]
    lane = _lane((tq, LANES))
    dl = lam_ref[...]
    lam = (jnp.exp(jnp.sum(dl[0:1] * dl[1:2], axis=-1, keepdims=True))
           - jnp.exp(jnp.sum(dl[2:3] * dl[3:4], axis=-1, keepdims=True)) + lam_init)
    q = q_ref[0]
    k = k_ref[0]
    v = v_ref[0]
    outs = []
    for hh in range(2):
        ps = []
        for m in range(2):
            lo = hh * HEAD_DIM + m * DIFF_QK_DIM
            qm = jnp.where((lane >= lo) & (lane < lo + DIFF_QK_DIM), q, jnp.zeros_like(q))
            sc = _dot_nt(qm, k) * DIFF_QK_DIM ** -0.5
            mx = jnp.max(sc, axis=-1, keepdims=True)
            e = jnp.exp(sc - mx)
            ps.append(e / jnp.sum(e, axis=-1, keepdims=True))
        a = ps[0] - lam * ps[1]
        outs.append(_dot(a.astype(BF16), v))
    in_h0 = lane < HEAD_DIM
    o = jnp.where(in_h0, outs[0], outs[1])
    o2 = o * o
    ms0 = jnp.sum(jnp.where(in_h0, o2, 0.0), axis=-1, keepdims=True) / HEAD_DIM
    ms1 = jnp.sum(jnp.where(in_h0, 0.0, o2), axis=-1, keepdims=True) / HEAD_DIM
    ms = jnp.where(in_h0, ms0, ms1)
    o = o * lax.rsqrt(ms + LN_EPS) * g_ref[...] * (1.0 - lam_init)
    o_ref[0] = o.astype(BF16)


def _diff_attention(qkv, diff_lam, subln_g, lam_init):
    b, s, _ = qkv.shape
    tq = 128
    g2 = jnp.tile(subln_g.reshape(1, HEAD_DIM), (1, 2))
    return pl.pallas_call(
        functools.partial(_diff_kernel, lam_init=lam_init),
        grid=(b, 2, s // tq),
        in_specs=[pl.BlockSpec((4, DIFF_QK_DIM), lambda bi, hp, i: (0, 0)),
                  pl.BlockSpec((1, tq, LANES), lambda bi, hp, i: (bi, i, hp)),
                  pl.BlockSpec((1, s, LANES), lambda bi, hp, i: (bi, 0, 2 + hp)),
                  pl.BlockSpec((1, s, LANES), lambda bi, hp, i: (bi, 0, 4 + hp)),
                  pl.BlockSpec((1, LANES), lambda bi, hp, i: (0, 0))],
        out_specs=pl.BlockSpec((1, tq, LANES), lambda bi, hp, i: (bi, i, hp)),
        out_shape=jax.ShapeDtypeStruct((b, s, 2 * LANES), BF16),
        compiler_params=_cparams(("parallel", "parallel", "arbitrary")),
        name="diff_attn",
    )(diff_lam, qkv, qkv, qkv, g2)


POOL_PAD = 16
POOL_CHUNK = 512
POOL_HALO = 8


def _pool_kernel(u_ref, w_ref, sc_ref, o_ref, p_ref):
    s = u_ref.shape[1]
    width = u_ref.shape[2]
    p_ref[0:POOL_PAD, :] = jnp.zeros((POOL_PAD, width), F32)
    p_ref[POOL_PAD + s:POOL_PAD + s + POOL_PAD, :] = jnp.zeros((POOL_PAD, width), F32)
    p_ref[POOL_PAD:POOL_PAD + s, :] = u_ref[0]
    n = POOL_CHUNK + 2 * POOL_HALO
    lane = _lane((POOL_CHUNK, width))
    row = lax.broadcasted_iota(jnp.int32, (POOL_CHUNK, width), 0)
    w_of_lane = jnp.where(lane < POOL_GROUP, 2, jnp.where(lane < 2 * POOL_GROUP, 4,
                          jnp.where(lane < 3 * POOL_GROUP, 8, 16)))

    def body(ci, carry):
        c0 = pl.multiple_of(ci * POOL_CHUNK, POOL_CHUNK)
        x = p_ref[pl.ds(c0 + POOL_PAD - POOL_HALO, n), :]
        w2 = x + pltpu.roll(x, 1, 0)
        w4 = pltpu.roll(w2, 1, 0) + pltpu.roll(w2, n - 1, 0)
        w8 = pltpu.roll(w4, 2, 0) + pltpu.roll(w4, n - 2, 0)
        w16 = pltpu.roll(w8, 4, 0) + pltpu.roll(w8, n - 4, 0)
        u = x[POOL_HALO:POOL_HALO + POOL_CHUNK]
        wsum = jnp.where(lane < POOL_GROUP, w2[POOL_HALO:POOL_HALO + POOL_CHUNK],
                         jnp.where(lane < 2 * POOL_GROUP, w4[POOL_HALO:POOL_HALO + POOL_CHUNK],
                                   jnp.where(lane < 3 * POOL_GROUP, w8[POOL_HALO:POOL_HALO + POOL_CHUNK],
                                             w16[POOL_HALO:POOL_HALO + POOL_CHUNK])))
        t = row + c0
        half_w = w_of_lane // 2
        lo = jnp.maximum(t - half_w, 0)
        hi = jnp.minimum(t + w_of_lane - 1 - half_w, s - 1)
        cnt = (hi - lo + 1).astype(F32)
        dlt = wsum / cnt - u
        y = _dot(dlt.astype(BF16), w_ref[...]) * sc_ref[...]
        o_ref[0, pl.ds(c0, POOL_CHUNK), :] = y.astype(BF16)
        return carry

    lax.fori_loop(0, s // POOL_CHUNK, body, 0)


def _pool(z3, pool_w, pool_scale):
    b, s, _ = z3.shape
    width = len(POOL_WINDOWS) * POOL_GROUP
    wbd = jax.scipy.linalg.block_diag(*[pool_w[g] for g in range(len(POOL_WINDOWS))]).astype(BF16)
    return pl.pallas_call(
        _pool_kernel,
        grid=(b,),
        in_specs=[pl.BlockSpec((1, s, width), lambda bi: (bi, 0, CB_POOL256)),
                  pl.BlockSpec((width, width), lambda bi: (0, 0)),
                  pl.BlockSpec((1, width), lambda bi: (0, 0))],
        out_specs=pl.BlockSpec((1, s, width), lambda bi: (bi, 0, 0)),
        out_shape=jax.ShapeDtypeStruct((b, s, width), BF16),
        scratch_shapes=[pltpu.VMEM((s + 2 * POOL_PAD, width), F32)],
        compiler_params=_cparams(("parallel",)),
        name="pool",
    )(z3, wbd, pool_scale.reshape(1, width))


SWA_Q = 128
SWA_BAND = SWA_Q + 2 * DIL_HALF


def _swa_kernel(q_ref, k_ref, v_ref, o_ref, l_ref, *, length, chunk):
    c = pl.program_id(2)
    lane = _lane((SWA_Q, LANES))
    rel = (lax.broadcasted_iota(jnp.int32, (SWA_Q, SWA_BAND), 1)
           - lax.broadcasted_iota(jnp.int32, (SWA_Q, SWA_BAND), 0))
    for i in range(chunk // SWA_Q):
        r0 = c * chunk + i * SWA_Q
        l0 = r0 & (length - 1)
        lo = r0 - l0 + jnp.clip(l0 - DIL_HALF, 0, length - SWA_BAND)
        lo = pl.multiple_of(lo, DIL_HALF)
        q = q_ref[0, i * SWA_Q:(i + 1) * SWA_Q, :]
        kw = k_ref[0, pl.ds(lo, SWA_BAND), :]
        vw = v_ref[0, pl.ds(lo, SWA_BAND), :]
        d = rel + (lo - r0)
        valid = (d >= -DIL_HALF) & (d <= DIL_HALF)
        outs, lses = [], []
        for hh in range(2):
            in_head = (lane >= hh * HEAD_DIM) & (lane < (hh + 1) * HEAD_DIM)
            qm = jnp.where(in_head, q, jnp.zeros_like(q))
            sc = jnp.where(valid, _dot_nt(qm, kw) * HEAD_DIM ** -0.5, NEG)
            m = jnp.max(sc, axis=-1, keepdims=True)
            e = jnp.exp(sc - m)
            den = jnp.sum(e, axis=-1, keepdims=True)
            outs.append(_dot((e / den).astype(BF16), vw))
            lses.append(m + jnp.log(den))
        o_ref[0, i * SWA_Q:(i + 1) * SWA_Q, :] = jnp.where(lane < HEAD_DIM, outs[0], outs[1])
        l_ref[0, i * SWA_Q:(i + 1) * SWA_Q, :] = jnp.where(lane < HEAD_DIM, lses[0], lses[1])


def _swa(qkv, length):
    b, s, _ = qkv.shape
    chunk = 1024
    shp = jax.ShapeDtypeStruct((b, s, 2 * LANES), F32)
    return pl.pallas_call(
        functools.partial(_swa_kernel, length=length, chunk=chunk),
        grid=(b, 2, s // chunk),
        in_specs=[pl.BlockSpec((1, chunk, LANES), lambda bi, hp, c: (bi, c, hp)),
                  pl.BlockSpec((1, s, LANES), lambda bi, hp, c: (bi, 0, 2 + hp)),
                  pl.BlockSpec((1, s, LANES), lambda bi, hp, c: (bi, 0, 4 + hp))],
        out_specs=[pl.BlockSpec((1, chunk, LANES), lambda bi, hp, c: (bi, c, hp)),
                   pl.BlockSpec((1, chunk, LANES), lambda bi, hp, c: (bi, c, hp))],
        out_shape=[shp, shp],
        compiler_params=_cparams(("parallel", "parallel", "arbitrary")),
        name=f"swa_l{length}",
    )(qkv, qkv, qkv)


def _dil_merge_kernel(o0_ref, l0_ref, o1_ref, l1_ref, o2_ref, l2_ref, y_ref, so1, sl1, so2, sl2):
    tm = y_ref.shape[1]
    r1 = DIL_PATTERNS[1][1]
    r2 = DIL_PATTERNS[2][1]
    for m in range(r1):
        so1[pl.ds(m, tm // r1, stride=r1), :] = o1_ref[0, m]
        sl1[pl.ds(m, tm // r1, stride=r1), :] = l1_ref[0, m]
    for m in range(r2):
        so2[pl.ds(m, tm // r2, stride=r2), :] = o2_ref[0, m]
        sl2[pl.ds(m, tm // r2, stride=r2), :] = l2_ref[0, m]
    l0, l1, l2 = l0_ref[0], sl1[...], sl2[...]
    mx = jnp.maximum(jnp.maximum(l0, l1), l2)
    e0, e1, e2 = jnp.exp(l0 - mx), jnp.exp(l1 - mx), jnp.exp(l2 - mx)
    den = e0 + e1 + e2
    y = (e0 / den) * o0_ref[0] + (e1 / den) * so1[...] + (e2 / den) * so2[...]
    y_ref[0] = y.astype(BF16)


def _dil_merge(outs):
    (o0, l0), (o1, l1), (o2, l2) = outs
    b, s, w = o0.shape
    tm = 1024
    r1 = DIL_PATTERNS[1][1]
    r2 = DIL_PATTERNS[2][1]
    v1 = lambda a: a.reshape(b, r1, s // r1, w)
    v2 = lambda a: a.reshape(b, r2, s // r2, w)
    nat = pl.BlockSpec((1, tm, LANES), lambda bi, i, hp: (bi, i, hp))
    g1 = pl.BlockSpec((1, r1, tm // r1, LANES), lambda bi, i, hp: (bi, 0, i, hp))
    g2 = pl.BlockSpec((1, r2, tm // r2, LANES), lambda bi, i, hp: (bi, 0, i, hp))
    return pl.pallas_call(
        _dil_merge_kernel,
        grid=(b, s // tm, w // LANES),
        in_specs=[nat, nat, g1, g1, g2, g2],
        out_specs=nat,
        out_shape=jax.ShapeDtypeStruct((b, s, w), BF16),
        scratch_shapes=[pltpu.VMEM((tm, LANES), F32)] * 4,
        compiler_params=_cparams(("parallel", "parallel", "parallel")),
        name="dil_merge",
    )(o0, l0, v1(o1), v1(l1), v2(o2), v2(l2))


def _merge_kernel(ya_ref, yb_ref, yc_ref, yd_ref, zg_ref, bg_ref, wb_ref, wo_ref, h_ref, g_ref, b_ref,
                  o_ref, *, alpha):
    d = h_ref.shape[1]
    merged = None
    for n, y_ref in enumerate((ya_ref, yb_ref, yc_ref, yd_ref)):
        zg = zg_ref[:, n * d:(n + 1) * d] + bg_ref[:, n * d:(n + 1) * d]
        gate = 1.0 / (1.0 + jnp.exp(-zg))
        term = gate * _dot(y_ref[...], wb_ref[n])
        merged = term if merged is None else merged + term
    mix = _dot(merged.astype(BF16), wo_ref[...])
    o_ref[...] = _layer_norm(alpha * h_ref[...] + mix, g_ref[...], b_ref[...])


def _merge(ys, z2, b_gate, wb, wo, h2, g, bb, alpha):
    t, d = h2.shape
    bw = ys[0].shape[1]
    tm = 256
    yspec = pl.BlockSpec((tm, bw), lambda i: (i, 0))
    return pl.pallas_call(
        functools.partial(_merge_kernel, alpha=alpha),
        grid=(t // tm,),
        in_specs=[yspec, yspec, yspec, yspec,
                  pl.BlockSpec((tm, 4 * d), lambda i: (i, CB_GATE4096)),
                  pl.BlockSpec((1, 4 * d), lambda i: (0, 0)),
                  pl.BlockSpec((4, bw, d), lambda i: (0, 0, 0)),
                  pl.BlockSpec((d, d), lambda i: (0, 0)),
                  pl.BlockSpec((tm, d), lambda i: (i, 0)),
                  pl.BlockSpec((1, d), lambda i: (0, 0)),
                  pl.BlockSpec((1, d), lambda i: (0, 0))],
        out_specs=pl.BlockSpec((tm, d), lambda i: (i, 0)),
        out_shape=jax.ShapeDtypeStruct((t, d), F32),
        compiler_params=_cparams(("parallel",)),
        name="merge",
    )(*ys, z2, b_gate.reshape(1, 4 * d), wb, wo, h2, g.reshape(1, d), bb.reshape(1, d))


ROUTER_TM = 256


def _router_kernel(h_ref, w_ref, b_ref, o_ref, cnt_ref, carry):
    @pl.when(pl.program_id(0) == 0)
    def _():
        carry[...] = jnp.zeros_like(carry)

    tm = h_ref.shape[0]
    logits = jnp.dot(h_ref[...], w_ref[...], preferred_element_type=F32,
                     precision=lax.Precision.HIGHEST) + b_ref[...]
    lane = _lane((tm, LANES))
    big = jnp.int32(1 << 20)
    is_g = lane < N_GROUPS
    gl = jnp.where(is_g, logits, -jnp.inf)
    gmax = jnp.max(gl, axis=-1, keepdims=True)
    gsel = jnp.min(jnp.where(is_g & (gl == gmax), lane, big), axis=-1, keepdims=True)
    pg = 1.0 / jnp.sum(jnp.exp(gl - gmax), axis=-1, keepdims=True)
    e_lo = N_GROUPS + gsel * EXPERTS_PER_GROUP
    in_grp = (lane >= e_lo) & (lane < e_lo + EXPERTS_PER_GROUP)
    el = jnp.where(in_grp, logits, -jnp.inf)
    v1 = jnp.max(el, axis=-1, keepdims=True)
    i1 = jnp.min(jnp.where(in_grp & (el == v1), lane, big), axis=-1, keepdims=True)
    el2 = jnp.where(lane == i1, -jnp.inf, el)
    v2 = jnp.max(el2, axis=-1, keepdims=True)
    i2 = jnp.min(jnp.where(in_grp & (lane != i1) & (el2 == v2), lane, big), axis=-1, keepdims=True)
    t2 = jnp.exp(v2 - v1)
    g1 = pg / (1.0 + t2)
    g2 = pg * t2 / (1.0 + t2)
    oh1 = (lane == i1)
    oh2 = (lane == i2)
    tri = (lax.broadcasted_iota(jnp.int32, (tm, tm), 1)
           < lax.broadcasted_iota(jnp.int32, (tm, tm), 0)).astype(BF16)
    oh1b = oh1.astype(BF16)
    oh2b = oh2.astype(BF16)
    c0 = carry[...]
    c1 = c0 + jnp.sum(oh1b.astype(F32), axis=0, keepdims=True)
    rank1 = jnp.sum(jnp.where(oh1, _dot(tri, oh1b) + c0, 0.0), axis=-1, keepdims=True)
    rank2 = jnp.sum(jnp.where(oh2, _dot(tri, oh2b) + c1, 0.0), axis=-1, keepdims=True)
    c2 = c1 + jnp.sum(oh2b.astype(F32), axis=0, keepdims=True)
    carry[...] = c2
    cnt_ref[...] = c2
    e1 = (i1 - N_GROUPS).astype(F32)
    e2 = (i2 - N_GROUPS).astype(F32)
    out = jnp.where(lane == 0, e1, jnp.where(lane == 1, e2, jnp.where(lane == 2, g1, jnp.where(
        lane == 3, g2, jnp.where(lane == 4, rank1, jnp.where(lane == 5, rank2, 0.0))))))
    o_ref[...] = out


def _router(h2, rg_w, rg_b, re_w, re_b):
    t, d = h2.shape
    pad = LANES - N_GROUPS - N_EXPERTS
    w = jnp.concatenate([rg_w, re_w, jnp.zeros((d, pad), F32)], axis=1)
    bias = jnp.concatenate([rg_b, re_b, jnp.zeros((pad,), F32)]).reshape(1, LANES)
    tm = ROUTER_TM
    return pl.pallas_call(
        _router_kernel,
        grid=(t // tm,),
        in_specs=[pl.BlockSpec((tm, d), lambda i: (i, 0)),
                  pl.BlockSpec((d, LANES), lambda i: (0, 0)),
                  pl.BlockSpec((1, LANES), lambda i: (0, 0))],
        out_specs=[pl.BlockSpec((tm, LANES), lambda i: (i, 0)),
                   pl.BlockSpec((1, LANES), lambda i: (0, 0))],
        out_shape=[jax.ShapeDtypeStruct((t, LANES), F32), jax.ShapeDtypeStruct((1, LANES), F32)],
        scratch_shapes=[pltpu.VMEM((1, LANES), F32)],
        compiler_params=_cparams(("arbitrary",)),
        name="router",
    )(h2, w, bias)


def _moe_kernel(be_ref, nv_ref, x_ref, wg_ref, wu_ref, wd_ref, o_ref):
    j = pl.program_id(0)
    n_valid = nv_ref[j]

    @pl.when(n_valid > 0)
    def _():
        row = lax.broadcasted_iota(jnp.int32, x_ref.shape, 0)
        x = jnp.where(row < n_valid, x_ref[...], 0.0).astype(BF16)
        g = _dot(x, wg_ref[0])
        u = _dot(x, wu_ref[0])
        hmid = (g * (1.0 / (1.0 + jnp.exp(-g)))) * u
        o_ref[...] = _dot(hmid.astype(BF16), wd_ref[0])

    @pl.when(n_valid <= 0)
    def _():
        o_ref[...] = jnp.zeros_like(o_ref)


def _moe_experts(xs, block_e, n_valid, wg, wu, wd):
    cap, d = xs.shape
    de = wg.shape[2]
    n_blocks = cap // MOE_ROWS
    grid_spec = pltpu.PrefetchScalarGridSpec(
        num_scalar_prefetch=2,
        grid=(n_blocks,),
        in_specs=[pl.BlockSpec((MOE_ROWS, d), lambda j, be, nv: (j, 0)),
                  pl.BlockSpec((1, d, de), lambda j, be, nv: (be[j], 0, 0)),
                  pl.BlockSpec((1, d, de), lambda j, be, nv: (be[j], 0, 0)),
                  pl.BlockSpec((1, de, d), lambda j, be, nv: (be[j], 0, 0))],
        out_specs=pl.BlockSpec((MOE_ROWS, d), lambda j, be, nv: (j, 0)),
    )
    return pl.pallas_call(
        _moe_kernel,
        grid_spec=grid_spec,
        out_shape=jax.ShapeDtypeStruct((cap, d), F32),
        compiler_params=_cparams(("arbitrary",)),
        name="moe_experts",
    )(block_e, n_valid, xs, wg, wu, wd)


SC_CORES = 2
SC_SUBCORES = 16
SC_WORKERS = SC_CORES * SC_SUBCORES
SC_CHUNK = 64


def _sc_mesh():
    return plsc.VectorSubcoreMesh(core_axis_name="c", subcore_axis_name="s",
                                  num_cores=SC_CORES, num_subcores=SC_SUBCORES)


def _sc_scatter_rows(src, idx0, idx1, n_out):
    t, d = src.shape
    per_w = t // SC_WORKERS
    n_chunks = per_w // SC_CHUNK

    @functools.partial(
        pl.kernel, mesh=_sc_mesh(),
        out_type=jax.ShapeDtypeStruct((n_out, d), src.dtype),
        scratch_types=[pltpu.VMEM((SC_CHUNK,), jnp.int32), pltpu.VMEM((SC_CHUNK,), jnp.int32),
                       pltpu.VMEM((SC_CHUNK, d), src.dtype)],
        name="sc_dispatch",
    )
    def k(src_hbm, i0_hbm, i1_hbm, out_hbm, i0_v, i1_v, rows_v):
        wid = lax.axis_index("s") * SC_CORES + lax.axis_index("c")
        base = wid * per_w

        @pl.loop(0, n_chunks)
        def _(ci):
            off = pl.multiple_of(base + ci * SC_CHUNK, SC_CHUNK)
            pltpu.sync_copy(src_hbm.at[pl.ds(off, SC_CHUNK)], rows_v)
            pltpu.sync_copy(i0_hbm.at[pl.ds(off, SC_CHUNK)], i0_v)
            pltpu.sync_copy(i1_hbm.at[pl.ds(off, SC_CHUNK)], i1_v)
            pltpu.sync_copy(rows_v, out_hbm.at[i0_v])
            pltpu.sync_copy(rows_v, out_hbm.at[i1_v])

    return k(src, idx0, idx1)


def _sc_gather_rows(table, idx):
    n = idx.shape[0]
    d = table.shape[1]
    per_w = n // SC_WORKERS
    n_chunks = per_w // SC_CHUNK

    @functools.partial(
        pl.kernel, mesh=_sc_mesh(),
        out_type=jax.ShapeDtypeStruct((n, d), table.dtype),
        scratch_types=[pltpu.VMEM((SC_CHUNK,), jnp.int32), pltpu.VMEM((SC_CHUNK, d), table.dtype)],
        name="sc_collect",
    )
    def k(table_hbm, idx_hbm, out_hbm, idx_v, rows_v):
        wid = lax.axis_index("s") * SC_CORES + lax.axis_index("c")
        base = wid * per_w

        @pl.loop(0, n_chunks)
        def _(ci):
            off = pl.multiple_of(base + ci * SC_CHUNK, SC_CHUNK)
            pltpu.sync_copy(idx_hbm.at[pl.ds(off, SC_CHUNK)], idx_v)
            pltpu.sync_copy(table_hbm.at[idx_v], rows_v)
            pltpu.sync_copy(rows_v, out_hbm.at[pl.ds(off, SC_CHUNK)])

    return k(table, idx)


def _combine_kernel(h_ref, r_ref, y1_ref, y2_ref, g_ref, b_ref, o_ref, *, alpha):
    g1 = r_ref[:, 2:3]
    g2 = r_ref[:, 3:4]
    ffn = g1 * y1_ref[...] + g2 * y2_ref[...]
    o_ref[...] = _layer_norm(alpha * h_ref[...] + ffn, g_ref[...], b_ref[...])


def _combine(h2, routing, y12, g, bb, alpha):
    t, d = h2.shape
    tm = 512
    nt = t // tm
    row = pl.BlockSpec((tm, d), lambda i: (i, 0))
    vec = pl.BlockSpec((1, d), lambda i: (0, 0))
    return pl.pallas_call(
        functools.partial(_combine_kernel, alpha=alpha),
        grid=(nt,),
        in_specs=[row, pl.BlockSpec((tm, LANES), lambda i: (i, 0)),
                  row, pl.BlockSpec((tm, d), lambda i: (nt + i, 0)), vec, vec],
        out_specs=row,
        out_shape=jax.ShapeDtypeStruct((t, d), F32),
        compiler_params=_cparams(("parallel",)),
        name="combine_ln",
    )(h2, routing, y12, y12, g.reshape(1, d), bb.reshape(1, d))


def _moe(h2, rg_w, rg_b, re_w, re_b, wg, wu, wd, g, bb, alpha):
    t, d = h2.shape
    routing, counts = _router(h2, rg_w, rg_b, re_w, re_b)
    eid = routing[:, 0:2].astype(jnp.int32)
    rank = routing[:, 4:6].astype(jnp.int32)
    cnt = counts[0, N_GROUPS:N_GROUPS + N_EXPERTS].astype(jnp.int32)
    padded = (cnt + MOE_ROWS - 1) // MOE_ROWS * MOE_ROWS
    pad_end = jnp.cumsum(padded)
    pad_start = pad_end - padded
    experts = jnp.arange(N_EXPERTS, dtype=jnp.int32)
    dest = jnp.sum(jnp.where(eid[..., None] == experts, pad_start, 0), axis=-1) + rank
    n_slots = t * TOP_K
    n_blocks = (n_slots + N_EXPERTS * (MOE_ROWS - 1) + MOE_ROWS - 1) // MOE_ROWS
    cap = n_blocks * MOE_ROWS
    blk_row = jnp.arange(n_blocks, dtype=jnp.int32) * MOE_ROWS
    block_e = jnp.minimum(jnp.sum((pad_end[None, :] <= blk_row[:, None]).astype(jnp.int32), axis=1),
                          N_EXPERTS - 1)
    is_e = block_e[:, None] == experts
    blk_cnt = jnp.sum(jnp.where(is_e, cnt, 0), axis=1)
    blk_start = jnp.sum(jnp.where(is_e, pad_start, 0), axis=1)
    n_valid = jnp.clip(blk_cnt - (blk_row - blk_start), 0, MOE_ROWS).astype(jnp.int32)
    xs = _sc_scatter_rows(h2, dest[:, 0], dest[:, 1], cap)
    ys = _moe_experts(xs, block_e, n_valid, wg, wu, wd)
    y12 = _sc_gather_rows(ys, jnp.concatenate([dest[:, 0], dest[:, 1]]))
    return _combine(h2, routing, y12, g, bb, alpha)


def kernel(x, emb_ln_g, emb_ln_b, w_in, b_gate, na_rpb, diff_lam, diff_subln_g, pool_w, pool_scale,
           w_branch, w_out, ln1_g, ln1_b, router_group_w, router_group_b, router_expert_w,
           router_expert_b, expert_w_gate, expert_w_up, expert_w_down, ln2_g, ln2_b):
    b, s, d = x.shape
    depth = w_in.shape[0]
    alpha = (2 * depth) ** 0.25
    t = b * s
    tab_diff = _rope_tables(s, DIFF_QK_DIM)
    tab_dil = _rope_tables(s, HEAD_DIM)
    h = _embed_ln(x.reshape(t, d), emb_ln_g, emb_ln_b)
    for l in range(depth):
        lam_init = 0.8 - 0.6 * math.exp(-0.3 * l)
        z = _in_proj(h, w_in[l].astype(BF16))
        z3 = z.reshape(b, s, -1)
        y_a = _na_attention(z3, _na_bias_table(na_rpb[l]))
        qkv_diff = _prep(z3, tab_diff, CB_DIFF, 1, DIFF_QK_DIM // 8)
        y_b = _diff_attention(qkv_diff, diff_lam[l], diff_subln_g[l], lam_init)
        y_c = _pool(z3, pool_w[l], pool_scale[l])
        dil = []
        for pi, (_, r) in enumerate(DIL_PATTERNS):
            qkv = _prep(z3, tab_dil, CB_DIL + 6 * pi, r, HEAD_DIM // 8)
            dil.append(_swa(qkv, s // r))
        y_d = _dil_merge(dil)
        ys = [a.reshape(t, -1) for a in (y_a, y_b, y_c, y_d)]
        h = _merge(ys, z, b_gate[l], w_branch[l].astype(BF16), w_out[l].astype(BF16), h,
                   ln1_g[l], ln1_b[l], alpha)
        h = _moe(h, router_group_w[l], router_group_b[l], router_expert_w[l], router_expert_b[l],
                 expert_w_gate[l].astype(BF16), expert_w_up[l].astype(BF16),
                 expert_w_down[l].astype(BF16), ln2_g[l], ln2_b[l], alpha)
    return h.reshape(b, s, d)
```

```python
import functools
import math

import jax
import jax.numpy as jnp
import numpy as np
from jax import lax
from jax.experimental import pallas as pl
from jax.experimental.pallas import tpu as pltpu
from jax.experimental.pallas import tpu_sc as plsc

F32 = jnp.float32
BF16 = jnp.bfloat16

LANES = 128
GRID_W = 64
HEAD_DIM = 64
ROPE_THETA = 500000.0
LN_EPS = 1e-5
NA_ROWS = 8
NA_COLS = 16
DIFF_QK_DIM = 32
POOL_WINDOWS = (2, 4, 8, 16)
POOL_GROUP = 64
DIL_PATTERNS = ((128, 1), (512, 4), (2048, 16))
DIL_HALF = 64
N_GROUPS = 4
EXPERTS_PER_GROUP = 8
N_EXPERTS = N_GROUPS * EXPERTS_PER_GROUP
TOP_K = 2
MOE_ROWS = 256
NEG = -1e30

CB_NA = 0
CB_DIFF = 6
CB_POOL256 = 6
CB_DIL = 14
CB_GATE4096 = 1

VMEM_LIMIT = 56 * 1024 * 1024


def _cparams(sem, vmem=VMEM_LIMIT):
    return pltpu.CompilerParams(dimension_semantics=sem, vmem_limit_bytes=vmem)


def _layer_norm(x, g, b):
    mu = jnp.mean(x, axis=-1, keepdims=True)
    xc = x - mu
    var = jnp.mean(xc * xc, axis=-1, keepdims=True)
    return xc * lax.rsqrt(var + LN_EPS) * g + b


def _dot(a, b):
    return jnp.dot(a, b, preferred_element_type=F32)


def _dot_nt(a, b):
    return lax.dot_general(a, b, (((1,), (1,)), ((), ())), preferred_element_type=F32)


def _lane(shape):
    return lax.broadcasted_iota(jnp.int32, shape, len(shape) - 1)


def _ln_kernel(x_ref, g_ref, b_ref, o_ref):
    o_ref[...] = _layer_norm(x_ref[...], g_ref[...], b_ref[...])


def _embed_ln(x2, g, b):
    t, d = x2.shape
    tm = 1024
    return pl.pallas_call(
        _ln_kernel,
        grid=(t // tm,),
        in_specs=[pl.BlockSpec((tm, d), lambda i: (i, 0)),
                  pl.BlockSpec((1, d), lambda i: (0, 0)),
                  pl.BlockSpec((1, d), lambda i: (0, 0))],
        out_specs=pl.BlockSpec((tm, d), lambda i: (i, 0)),
        out_shape=jax.ShapeDtypeStruct((t, d), F32),
        compiler_params=_cparams(("parallel",)),
        name="embed_ln",
    )(x2, g.reshape(1, d), b.reshape(1, d))


def _in_proj_kernel(x_ref, w_ref, o_ref):
    o_ref[...] = _dot(x_ref[...].astype(BF16), w_ref[...])


def _in_proj(h2, w_bf16):
    t, d = h2.shape
    n = w_bf16.shape[1]
    tm, tn = 1024, 1024
    return pl.pallas_call(
        _in_proj_kernel,
        grid=(t // tm, n // tn),
        in_specs=[pl.BlockSpec((tm, d), lambda i, j: (i, 0)),
                  pl.BlockSpec((d, tn), lambda i, j: (0, j))],
        out_specs=pl.BlockSpec((tm, tn), lambda i, j: (i, j)),
        out_shape=jax.ShapeDtypeStruct((t, n), F32),
        compiler_params=_cparams(("parallel", "parallel")),
        name="in_proj",
    )(h2, w_bf16)


def _rope_tables(seq, head_w):
    rot = head_w // 4
    half = rot // 2
    inv_freq = jnp.exp(jnp.arange(half, dtype=F32) * (-2.0 * math.log(ROPE_THETA) / rot))
    ang = jnp.arange(seq, dtype=jnp.int32).astype(F32)[:, None] * inv_freq[None, :]
    cos, sin = jnp.cos(ang), jnp.sin(ang)
    zero = jnp.zeros((seq, head_w - rot), F32)
    zh = jnp.zeros((seq, half), F32)
    t0 = jnp.concatenate([cos, cos, jnp.ones((seq, head_w - rot), F32)], axis=1)
    t1 = jnp.concatenate([-sin, zh, zero], axis=1)
    t2 = jnp.concatenate([zh, sin, zero], axis=1)
    reps = LANES // head_w
    return jnp.stack([jnp.tile(t0, (1, reps)), jnp.tile(t1, (1, reps)), jnp.tile(t2, (1, reps))])


def _rope(x, t_ref, half):
    return (x * t_ref[0] + pltpu.roll(x, LANES - half, 1) * t_ref[1]
            + pltpu.roll(x, half, 1) * t_ref[2])


LOG2E = math.log2(math.e)
LN2 = math.log(2.0)


def _prep_kernel(z_ref, t_ref, o_ref, s_ref, *, r, half, q_scale):
    c = pl.program_id(2)
    rows = z_ref.shape[1]

    @pl.when(c < 2)
    def _():
        s_ref[...] = _rope(z_ref[0], t_ref, half) * q_scale

    @pl.when((c >= 2) & (c < 4))
    def _():
        s_ref[...] = _rope(z_ref[0], t_ref, half)

    @pl.when(c >= 4)
    def _():
        s_ref[...] = z_ref[0]

    if r == 1:
        o_ref[0, 0] = s_ref[...].astype(BF16)
    else:
        for m in range(r):
            o_ref[0, m] = s_ref[pl.ds(m, rows // r, stride=r), :].astype(BF16)


def _prep_plain_kernel(z_ref, o_ref, *, q_scale):
    c = pl.program_id(2)

    @pl.when(c < 2)
    def _():
        o_ref[0] = (z_ref[0] * q_scale).astype(BF16)

    @pl.when(c >= 2)
    def _():
        o_ref[0] = z_ref[0].astype(BF16)


def _prep(z3, tables, col_block, r, half, q_scale):
    b, s, _ = z3.shape
    tr = 2048
    if tables is None:
        return pl.pallas_call(
            functools.partial(_prep_plain_kernel, q_scale=q_scale),
            grid=(b, s // tr, 6),
            in_specs=[pl.BlockSpec((1, tr, LANES), lambda bi, i, c: (bi, i, col_block + c))],
            out_specs=pl.BlockSpec((1, tr, LANES), lambda bi, i, c: (bi, i, c)),
            out_shape=jax.ShapeDtypeStruct((b, s, 6 * LANES), BF16),
            compiler_params=_cparams(("parallel", "parallel", "arbitrary")),
            name="prep_plain",
        )(z3)
    out = pl.pallas_call(
        functools.partial(_prep_kernel, r=r, half=half, q_scale=q_scale),
        grid=(b, s // tr, 6),
        in_specs=[pl.BlockSpec((1, tr, LANES), lambda bi, i, c: (bi, i, col_block + c)),
                  pl.BlockSpec((3, tr, LANES), lambda bi, i, c: (0, i, 0))],
        out_specs=pl.BlockSpec((1, r, tr // r, LANES), lambda bi, i, c: (bi, 0, i, c)),
        out_shape=jax.ShapeDtypeStruct((b, r, s // r, 6 * LANES), BF16),
        scratch_shapes=[pltpu.VMEM((tr, LANES), F32)],
        compiler_params=_cparams(("parallel", "parallel", "arbitrary")),
        name=f"prep_r{r}",
    )(z3, tables)
    return out.reshape(b, s, 6 * LANES)


def _na_bias_table(rpb):
    kr, kc = NA_ROWS, NA_COLS
    col = np.arange(GRID_W)
    col_start = np.clip(col - kc // 2, 0, GRID_W - kc)
    in_win = (col[None, :] >= col_start[:, None]) & (col[None, :] < col_start[:, None] + kc)
    dc = np.clip(col[None, :] - col[:, None] + kc - 1, 0, 2 * kc - 2)
    case = np.arange(kr)
    dr = np.arange(kr)[None, :] - case[:, None] + NA_ROWS - 1
    bias = rpb[:, dr][..., dc]
    bias = bias.transpose(0, 1, 3, 2, 4)
    bias = jnp.where(jnp.asarray(in_win)[None, None, :, None, :], bias.astype(F32) * LOG2E, NEG)
    n_heads = rpb.shape[0]
    bias = bias.reshape(n_heads // 2, 2, kr, GRID_W, kr * GRID_W).transpose(0, 2, 1, 3, 4)
    return bias.reshape(n_heads // 2, kr, 2 * GRID_W, kr * GRID_W)


def _stack_heads(q, lane):
    zero = jnp.zeros_like(q)
    return jnp.concatenate([jnp.where(lane < HEAD_DIM, q, zero), jnp.where(lane < HEAD_DIM, zero, q)],
                           axis=0)


def _na_kernel(q_ref, k_ref, v_ref, bias_ref, o_ref, *, rows_per_step, n_rows):
    step = pl.program_id(2)
    lane = _lane((GRID_W, LANES))
    kwin = NA_ROWS * GRID_W
    for i in range(rows_per_step):
        r = step * rows_per_step + i
        start = jnp.clip(r - NA_ROWS // 2, 0, n_rows - NA_ROWS)
        case = r - start
        k0 = pl.multiple_of(start * GRID_W, GRID_W)
        q2 = _stack_heads(q_ref[0, i * GRID_W:(i + 1) * GRID_W, :], lane)
        sc = _dot_nt(q2, k_ref[0, pl.ds(k0, kwin), :]) + bias_ref[0, case]
        m = jnp.max(sc, axis=-1, keepdims=True)
        e = jnp.exp2(sc - m)
        den = jnp.sum(e, axis=-1, keepdims=True)
        pv = _dot(e.astype(BF16), v_ref[0, pl.ds(k0, kwin), :]) / den
        o = jnp.where(lane < HEAD_DIM, pv[0:GRID_W], pv[GRID_W:2 * GRID_W])
        o_ref[0, i * GRID_W:(i + 1) * GRID_W, :] = o.astype(BF16)


def _na_attention(qkv, bias):
    b, s, _ = qkv.shape
    n_rows = s // GRID_W
    rps = 8
    tq = rps * GRID_W
    return pl.pallas_call(
        functools.partial(_na_kernel, rows_per_step=rps, n_rows=n_rows),
        grid=(b, 2, n_rows // rps),
        in_specs=[pl.BlockSpec((1, tq, LANES), lambda bi, hp, i: (bi, i, hp)),
                  pl.BlockSpec((1, s, LANES), lambda bi, hp, i: (bi, 0, 2 + hp)),
                  pl.BlockSpec((1, s, LANES), lambda bi, hp, i: (bi, 0, 4 + hp)),
                  pl.BlockSpec((1, NA_ROWS, 2 * GRID_W, NA_ROWS * GRID_W),
                               lambda bi, hp, i: (hp, 0, 0, 0))],
        out_specs=pl.BlockSpec((1, tq, LANES), lambda bi, hp, i: (bi, i, hp)),
        out_shape=jax.ShapeDtypeStruct((b, s, 2 * LANES), BF16),
        compiler_params=_cparams(("parallel", "parallel", "arbitrary")),
        name="na_attn",
    )(qkv, qkv, qkv, bias)


def _diff_kernel(lam_ref, q_ref, k_ref, v_ref, g_ref, o_ref, *, lam_init):
    tq = q_ref.shape[1]
    lane = _lane((tq, LANES))
    dl = lam_ref[...]
    lam = (jnp.exp(jnp.sum(dl[0:1] * dl[1:2], axis=-1, keepdims=True))
           - jnp.exp(jnp.sum(dl[2:3] * dl[3:4], axis=-1, keepdims=True)) + lam_init)
    q = q_ref[0]
    zero = jnp.zeros_like(q)
    v = v_ref[0]
    v_lane = _lane(v.shape)
    one = jnp.ones_like(v)
    v_ext = (jnp.where(v_lane < HEAD_DIM, v, one), jnp.where(v_lane < HEAD_DIM, one, v))
    pv = []
    for lo in range(0, LANES, DIFF_QK_DIM):
        qm = jnp.where((lane >= lo) & (lane < lo + DIFF_QK_DIM), q, zero)
        sc = _dot_nt(qm, k_ref[0])
        e = jnp.exp2(sc - jnp.max(sc, axis=-1, keepdims=True))
        ev = _dot(e.astype(BF16), v_ext[lo // HEAD_DIM])
        pv.append(ev / pltpu.roll(ev, HEAD_DIM, 1))
    in_h0 = lane < HEAD_DIM
    o = jnp.where(in_h0, pv[0] - lam * pv[1], pv[2] - lam * pv[3])
    o2 = o * o
    ms0 = jnp.sum(jnp.where(in_h0, o2, 0.0), axis=-1, keepdims=True) / HEAD_DIM
    ms1 = jnp.sum(jnp.where(in_h0, 0.0, o2), axis=-1, keepdims=True) / HEAD_DIM
    ms = jnp.where(in_h0, ms0, ms1)
    o = o * lax.rsqrt(ms + LN_EPS) * g_ref[...] * (1.0 - lam_init)
    o_ref[0] = o.astype(BF16)


def _diff_attention(qkv, diff_lam, subln_g, lam_init):
    b, s, _ = qkv.shape
    tq = 512
    g2 = jnp.tile(subln_g.reshape(1, HEAD_DIM), (1, 2))
    return pl.pallas_call(
        functools.partial(_diff_kernel, lam_init=lam_init),
        grid=(b, 2, s // tq),
        in_specs=[pl.BlockSpec((4, DIFF_QK_DIM), lambda bi, hp, i: (0, 0)),
                  pl.BlockSpec((1, tq, LANES), lambda bi, hp, i: (bi, i, hp)),
                  pl.BlockSpec((1, s, LANES), lambda bi, hp, i: (bi, 0, 2 + hp)),
                  pl.BlockSpec((1, s, LANES), lambda bi, hp, i: (bi, 0, 4 + hp)),
                  pl.BlockSpec((1, LANES), lambda bi, hp, i: (0, 0))],
        out_specs=pl.BlockSpec((1, tq, LANES), lambda bi, hp, i: (bi, i, hp)),
        out_shape=jax.ShapeDtypeStruct((b, s, 2 * LANES), BF16),
        compiler_params=_cparams(("parallel", "parallel", "arbitrary")),
        name="diff_attn",
    )(diff_lam, qkv, qkv, qkv, g2)


POOL_PAD = 16
POOL_CHUNK = 512
POOL_HALO = 8


def _pool_kernel(u_ref, w_ref, sc_ref, o_ref, p_ref):
    s = u_ref.shape[1]
    width = u_ref.shape[2]
    p_ref[0:POOL_PAD, :] = jnp.zeros((POOL_PAD, width), F32)
    p_ref[POOL_PAD + s:POOL_PAD + s + POOL_PAD, :] = jnp.zeros((POOL_PAD, width), F32)
    p_ref[POOL_PAD:POOL_PAD + s, :] = u_ref[0]
    n = POOL_CHUNK + 2 * POOL_HALO
    lane = _lane((POOL_CHUNK, width))
    row = lax.broadcasted_iota(jnp.int32, (POOL_CHUNK, width), 0)
    w_of_lane = jnp.where(lane < POOL_GROUP, 2, jnp.where(lane < 2 * POOL_GROUP, 4,
                          jnp.where(lane < 3 * POOL_GROUP, 8, 16)))

    def body(ci, carry):
        c0 = pl.multiple_of(ci * POOL_CHUNK, POOL_CHUNK)
        x = p_ref[pl.ds(c0 + POOL_PAD - POOL_HALO, n), :]
        w2 = x + pltpu.roll(x, 1, 0)
        w4 = pltpu.roll(w2, 1, 0) + pltpu.roll(w2, n - 1, 0)
        w8 = pltpu.roll(w4, 2, 0) + pltpu.roll(w4, n - 2, 0)
        w16 = pltpu.roll(w8, 4, 0) + pltpu.roll(w8, n - 4, 0)
        u = x[POOL_HALO:POOL_HALO + POOL_CHUNK]
        wsum = jnp.where(lane < POOL_GROUP, w2[POOL_HALO:POOL_HALO + POOL_CHUNK],
                         jnp.where(lane < 2 * POOL_GROUP, w4[POOL_HALO:POOL_HALO + POOL_CHUNK],
                                   jnp.where(lane < 3 * POOL_GROUP, w8[POOL_HALO:POOL_HALO + POOL_CHUNK],
                                             w16[POOL_HALO:POOL_HALO + POOL_CHUNK])))
        t = row + c0
        half_w = w_of_lane // 2
        lo = jnp.maximum(t - half_w, 0)
        hi = jnp.minimum(t + w_of_lane - 1 - half_w, s - 1)
        cnt = (hi - lo + 1).astype(F32)
        dlt = wsum / cnt - u
        y = _dot(dlt.astype(BF16), w_ref[...]) * sc_ref[...]
        o_ref[0, pl.ds(c0, POOL_CHUNK), :] = y.astype(BF16)
        return carry

    lax.fori_loop(0, s // POOL_CHUNK, body, 0)


def _pool(z3, pool_w, pool_scale):
    b, s, _ = z3.shape
    width = len(POOL_WINDOWS) * POOL_GROUP
    wbd = jax.scipy.linalg.block_diag(*[pool_w[g] for g in range(len(POOL_WINDOWS))]).astype(BF16)
    return pl.pallas_call(
        _pool_kernel,
        grid=(b,),
        in_specs=[pl.BlockSpec((1, s, width), lambda bi: (bi, 0, CB_POOL256)),
                  pl.BlockSpec((width, width), lambda bi: (0, 0)),
                  pl.BlockSpec((1, width), lambda bi: (0, 0))],
        out_specs=pl.BlockSpec((1, s, width), lambda bi: (bi, 0, 0)),
        out_shape=jax.ShapeDtypeStruct((b, s, width), BF16),
        scratch_shapes=[pltpu.VMEM((s + 2 * POOL_PAD, width), F32)],
        compiler_params=_cparams(("parallel",)),
        name="pool",
    )(z3, wbd, pool_scale.reshape(1, width))


SWA_Q = 128
SWA_BAND = SWA_Q + 2 * DIL_HALF


def _swa_kernel(q_ref, k_ref, v_ref, o_ref, l_ref, *, length, chunk):
    c = pl.program_id(2)
    lane = _lane((SWA_Q, LANES))
    rel = (lax.broadcasted_iota(jnp.int32, (SWA_Q, SWA_BAND), 1)
           - lax.broadcasted_iota(jnp.int32, (SWA_Q, SWA_BAND), 0))
    for i in range(chunk // SWA_Q):
        r0 = c * chunk + i * SWA_Q
        l0 = r0 & (length - 1)
        lo = r0 - l0 + jnp.clip(l0 - DIL_HALF, 0, length - SWA_BAND)
        lo = pl.multiple_of(lo, DIL_HALF)
        q2 = _stack_heads(q_ref[0, i * SWA_Q:(i + 1) * SWA_Q, :], lane)
        d = rel + (lo - r0)
        valid = (d >= -DIL_HALF) & (d <= DIL_HALF)
        valid2 = jnp.concatenate([valid, valid], axis=0)
        sc = jnp.where(valid2, _dot_nt(q2, k_ref[0, pl.ds(lo, SWA_BAND), :]), NEG)
        m = jnp.max(sc, axis=-1, keepdims=True)
        e = jnp.exp2(sc - m)
        den = jnp.sum(e, axis=-1, keepdims=True)
        pv = _dot(e.astype(BF16), v_ref[0, pl.ds(lo, SWA_BAND), :]) / den
        lse2 = m + jnp.log2(den)
        o_ref[0, i * SWA_Q:(i + 1) * SWA_Q, :] = jnp.where(lane < HEAD_DIM, pv[0:SWA_Q], pv[SWA_Q:])
        l_ref[0, i * SWA_Q:(i + 1) * SWA_Q, :] = jnp.where(lane < HEAD_DIM, lse2[0:SWA_Q], lse2[SWA_Q:])


def _swa(qkv, length):
    b, s, _ = qkv.shape
    chunk = 1024
    shp = jax.ShapeDtypeStruct((b, s, 2 * LANES), F32)
    return pl.pallas_call(
        functools.partial(_swa_kernel, length=length, chunk=chunk),
        grid=(b, 2, s // chunk),
        in_specs=[pl.BlockSpec((1, chunk, LANES), lambda bi, hp, c: (bi, c, hp)),
                  pl.BlockSpec((1, s, LANES), lambda bi, hp, c: (bi, 0, 2 + hp)),
                  pl.BlockSpec((1, s, LANES), lambda bi, hp, c: (bi, 0, 4 + hp))],
        out_specs=[pl.BlockSpec((1, chunk, LANES), lambda bi, hp, c: (bi, c, hp)),
                   pl.BlockSpec((1, chunk, LANES), lambda bi, hp, c: (bi, c, hp))],
        out_shape=[shp, shp],
        compiler_params=_cparams(("parallel", "parallel", "arbitrary")),
        name=f"swa_l{length}",
    )(qkv, qkv, qkv)


def _dil_merge_kernel(o0_ref, l0_ref, o1_ref, l1_ref, o2_ref, l2_ref, y_ref, so1, sl1, so2, sl2):
    tm = y_ref.shape[1]
    r1 = DIL_PATTERNS[1][1]
    r2 = DIL_PATTERNS[2][1]
    for m in range(r1):
        so1[pl.ds(m, tm // r1, stride=r1), :] = o1_ref[0, m]
        sl1[pl.ds(m, tm // r1, stride=r1), :] = l1_ref[0, m]
    for m in range(r2):
        so2[pl.ds(m, tm // r2, stride=r2), :] = o2_ref[0, m]
        sl2[pl.ds(m, tm // r2, stride=r2), :] = l2_ref[0, m]
    l0, l1, l2 = l0_ref[0], sl1[...], sl2[...]
    mx = jnp.maximum(jnp.maximum(l0, l1), l2)
    e0, e1, e2 = jnp.exp2(l0 - mx), jnp.exp2(l1 - mx), jnp.exp2(l2 - mx)
    den = e0 + e1 + e2
    y = (e0 / den) * o0_ref[0] + (e1 / den) * so1[...] + (e2 / den) * so2[...]
    y_ref[0] = y.astype(BF16)


def _dil_merge(outs):
    (o0, l0), (o1, l1), (o2, l2) = outs
    b, s, w = o0.shape
    tm = 1024
    r1 = DIL_PATTERNS[1][1]
    r2 = DIL_PATTERNS[2][1]
    v1 = lambda a: a.reshape(b, r1, s // r1, w)
    v2 = lambda a: a.reshape(b, r2, s // r2, w)
    nat = pl.BlockSpec((1, tm, LANES), lambda bi, i, hp: (bi, i, hp))
    g1 = pl.BlockSpec((1, r1, tm // r1, LANES), lambda bi, i, hp: (bi, 0, i, hp))
    g2 = pl.BlockSpec((1, r2, tm // r2, LANES), lambda bi, i, hp: (bi, 0, i, hp))
    return pl.pallas_call(
        _dil_merge_kernel,
        grid=(b, s // tm, w // LANES),
        in_specs=[nat, nat, g1, g1, g2, g2],
        out_specs=nat,
        out_shape=jax.ShapeDtypeStruct((b, s, w), BF16),
        scratch_shapes=[pltpu.VMEM((tm, LANES), F32)] * 4,
        compiler_params=_cparams(("parallel", "parallel", "parallel")),
        name="dil_merge",
    )(o0, l0, v1(o1), v1(l1), v2(o2), v2(l2))


def _merge_kernel(ya_ref, yb_ref, yc_ref, yd_ref, zg_ref, bg_ref, wb_ref, wo_ref, h_ref, g_ref, b_ref,
                  o_ref, *, alpha):
    d = h_ref.shape[1]
    merged = None
    for n, y_ref in enumerate((ya_ref, yb_ref, yc_ref, yd_ref)):
        zg = zg_ref[:, n * d:(n + 1) * d] + bg_ref[:, n * d:(n + 1) * d]
        gate = 1.0 / (1.0 + jnp.exp(-zg))
        term = gate * _dot(y_ref[...], wb_ref[n])
        merged = term if merged is None else merged + term
    mix = _dot(merged.astype(BF16), wo_ref[...])
    o_ref[...] = _layer_norm(alpha * h_ref[...] + mix, g_ref[...], b_ref[...])


def _merge(ys, z2, b_gate, wb, wo, h2, g, bb, alpha):
    t, d = h2.shape
    bw = ys[0].shape[1]
    tm = 256
    yspec = pl.BlockSpec((tm, bw), lambda i: (i, 0))
    return pl.pallas_call(
        functools.partial(_merge_kernel, alpha=alpha),
        grid=(t // tm,),
        in_specs=[yspec, yspec, yspec, yspec,
                  pl.BlockSpec((tm, 4 * d), lambda i: (i, CB_GATE4096)),
                  pl.BlockSpec((1, 4 * d), lambda i: (0, 0)),
                  pl.BlockSpec((4, bw, d), lambda i: (0, 0, 0)),
                  pl.BlockSpec((d, d), lambda i: (0, 0)),
                  pl.BlockSpec((tm, d), lambda i: (i, 0)),
                  pl.BlockSpec((1, d), lambda i: (0, 0)),
                  pl.BlockSpec((1, d), lambda i: (0, 0))],
        out_specs=pl.BlockSpec((tm, d), lambda i: (i, 0)),
        out_shape=jax.ShapeDtypeStruct((t, d), F32),
        compiler_params=_cparams(("parallel",)),
        name="merge",
    )(*ys, z2, b_gate.reshape(1, 4 * d), wb, wo, h2, g.reshape(1, d), bb.reshape(1, d))


ROUTER_TM = 256


def _router_kernel(h_ref, w_ref, b_ref, o_ref, cnt_ref, carry):
    @pl.when(pl.program_id(0) == 0)
    def _():
        carry[...] = jnp.zeros_like(carry)

    tm = h_ref.shape[0]
    logits = jnp.dot(h_ref[...], w_ref[...], preferred_element_type=F32,
                     precision=lax.Precision.HIGHEST) + b_ref[...]
    lane = _lane((tm, LANES))
    big = jnp.int32(1 << 20)
    is_g = lane < N_GROUPS
    gl = jnp.where(is_g, logits, -jnp.inf)
    gmax = jnp.max(gl, axis=-1, keepdims=True)
    gsel = jnp.min(jnp.where(is_g & (gl == gmax), lane, big), axis=-1, keepdims=True)
    pg = 1.0 / jnp.sum(jnp.exp(gl - gmax), axis=-1, keepdims=True)
    e_lo = N_GROUPS + gsel * EXPERTS_PER_GROUP
    in_grp = (lane >= e_lo) & (lane < e_lo + EXPERTS_PER_GROUP)
    el = jnp.where(in_grp, logits, -jnp.inf)
    v1 = jnp.max(el, axis=-1, keepdims=True)
    i1 = jnp.min(jnp.where(in_grp & (el == v1), lane, big), axis=-1, keepdims=True)
    el2 = jnp.where(lane == i1, -jnp.inf, el)
    v2 = jnp.max(el2, axis=-1, keepdims=True)
    i2 = jnp.min(jnp.where(in_grp & (lane != i1) & (el2 == v2), lane, big), axis=-1, keepdims=True)
    t2 = jnp.exp(v2 - v1)
    g1 = pg / (1.0 + t2)
    g2 = pg * t2 / (1.0 + t2)
    oh1 = (lane == i1)
    oh2 = (lane == i2)
    tri = (lax.broadcasted_iota(jnp.int32, (tm, tm), 1)
           < lax.broadcasted_iota(jnp.int32, (tm, tm), 0)).astype(BF16)
    oh1b = oh1.astype(BF16)
    oh2b = oh2.astype(BF16)
    c0 = carry[...]
    c1 = c0 + jnp.sum(oh1b.astype(F32), axis=0, keepdims=True)
    rank1 = jnp.sum(jnp.where(oh1, _dot(tri, oh1b) + c0, 0.0), axis=-1, keepdims=True)
    rank2 = jnp.sum(jnp.where(oh2, _dot(tri, oh2b) + c1, 0.0), axis=-1, keepdims=True)
    c2 = c1 + jnp.sum(oh2b.astype(F32), axis=0, keepdims=True)
    carry[...] = c2
    cnt_ref[...] = c2
    e1 = (i1 - N_GROUPS).astype(F32)
    e2 = (i2 - N_GROUPS).astype(F32)
    out = jnp.where(lane == 0, e1, jnp.where(lane == 1, e2, jnp.where(lane == 2, g1, jnp.where(
        lane == 3, g2, jnp.where(lane == 4, rank1, jnp.where(lane == 5, rank2, 0.0))))))
    o_ref[...] = out


def _router(h2, rg_w, rg_b, re_w, re_b):
    t, d = h2.shape
    pad = LANES - N_GROUPS - N_EXPERTS
    w = jnp.concatenate([rg_w, re_w, jnp.zeros((d, pad), F32)], axis=1)
    bias = jnp.concatenate([rg_b, re_b, jnp.zeros((pad,), F32)]).reshape(1, LANES)
    tm = ROUTER_TM
    return pl.pallas_call(
        _router_kernel,
        grid=(t // tm,),
        in_specs=[pl.BlockSpec((tm, d), lambda i: (i, 0)),
                  pl.BlockSpec((d, LANES), lambda i: (0, 0)),
                  pl.BlockSpec((1, LANES), lambda i: (0, 0))],
        out_specs=[pl.BlockSpec((tm, LANES), lambda i: (i, 0)),
                   pl.BlockSpec((1, LANES), lambda i: (0, 0))],
        out_shape=[jax.ShapeDtypeStruct((t, LANES), F32), jax.ShapeDtypeStruct((1, LANES), F32)],
        scratch_shapes=[pltpu.VMEM((1, LANES), F32)],
        compiler_params=_cparams(("arbitrary",)),
        name="router",
    )(h2, w, bias)


def _moe_kernel(be_ref, nv_ref, x_ref, wg_ref, wu_ref, wd_ref, o_ref):
    j = pl.program_id(0)
    n_valid = nv_ref[j]

    @pl.when(n_valid > 0)
    def _():
        row = lax.broadcasted_iota(jnp.int32, x_ref.shape, 0)
        x = jnp.where(row < n_valid, x_ref[...], 0.0).astype(BF16)
        g = _dot(x, wg_ref[0])
        u = _dot(x, wu_ref[0])
        hmid = (g * (1.0 / (1.0 + jnp.exp(-g)))) * u
        o_ref[...] = _dot(hmid.astype(BF16), wd_ref[0])

    @pl.when(n_valid <= 0)
    def _():
        o_ref[...] = jnp.zeros_like(o_ref)


def _moe_experts(xs, block_e, n_valid, wg, wu, wd):
    cap, d = xs.shape
    de = wg.shape[2]
    n_blocks = cap // MOE_ROWS
    grid_spec = pltpu.PrefetchScalarGridSpec(
        num_scalar_prefetch=2,
        grid=(n_blocks,),
        in_specs=[pl.BlockSpec((MOE_ROWS, d), lambda j, be, nv: (j, 0)),
                  pl.BlockSpec((1, d, de), lambda j, be, nv: (be[j], 0, 0)),
                  pl.BlockSpec((1, d, de), lambda j, be, nv: (be[j], 0, 0)),
                  pl.BlockSpec((1, de, d), lambda j, be, nv: (be[j], 0, 0))],
        out_specs=pl.BlockSpec((MOE_ROWS, d), lambda j, be, nv: (j, 0)),
    )
    return pl.pallas_call(
        _moe_kernel,
        grid_spec=grid_spec,
        out_shape=jax.ShapeDtypeStruct((cap, d), F32),
        compiler_params=_cparams(("arbitrary",)),
        name="moe_experts",
    )(block_e, n_valid, xs, wg, wu, wd)


SC_CORES = 2
SC_SUBCORES = 16
SC_WORKERS = SC_CORES * SC_SUBCORES
SC_CHUNK = 64


def _sc_mesh():
    return plsc.VectorSubcoreMesh(core_axis_name="c", subcore_axis_name="s",
                                  num_cores=SC_CORES, num_subcores=SC_SUBCORES)


def _sc_scatter_rows(src, idx0, idx1, n_out):
    t, d = src.shape
    per_w = t // SC_WORKERS
    n_chunks = per_w // SC_CHUNK

    @functools.partial(
        pl.kernel, mesh=_sc_mesh(),
        out_type=jax.ShapeDtypeStruct((n_out, d), src.dtype),
        scratch_types=[pltpu.VMEM((SC_CHUNK,), jnp.int32), pltpu.VMEM((SC_CHUNK,), jnp.int32),
                       pltpu.VMEM((SC_CHUNK, d), src.dtype)],
        name="sc_dispatch",
    )
    def k(src_hbm, i0_hbm, i1_hbm, out_hbm, i0_v, i1_v, rows_v):
        wid = lax.axis_index("s") * SC_CORES + lax.axis_index("c")
        base = wid * per_w

        @pl.loop(0, n_chunks)
        def _(ci):
            off = pl.multiple_of(base + ci * SC_CHUNK, SC_CHUNK)
            pltpu.sync_copy(src_hbm.at[pl.ds(off, SC_CHUNK)], rows_v)
            pltpu.sync_copy(i0_hbm.at[pl.ds(off, SC_CHUNK)], i0_v)
            pltpu.sync_copy(i1_hbm.at[pl.ds(off, SC_CHUNK)], i1_v)
            pltpu.sync_copy(rows_v, out_hbm.at[i0_v])
            pltpu.sync_copy(rows_v, out_hbm.at[i1_v])

    return k(src, idx0, idx1)


def _sc_gather_rows(table, idx):
    n = idx.shape[0]
    d = table.shape[1]
    per_w = n // SC_WORKERS
    n_chunks = per_w // SC_CHUNK

    @functools.partial(
        pl.kernel, mesh=_sc_mesh(),
        out_type=jax.ShapeDtypeStruct((n, d), table.dtype),
        scratch_types=[pltpu.VMEM((SC_CHUNK,), jnp.int32), pltpu.VMEM((SC_CHUNK, d), table.dtype)],
        name="sc_collect",
    )
    def k(table_hbm, idx_hbm, out_hbm, idx_v, rows_v):
        wid = lax.axis_index("s") * SC_CORES + lax.axis_index("c")
        base = wid * per_w

        @pl.loop(0, n_chunks)
        def _(ci):
            off = pl.multiple_of(base + ci * SC_CHUNK, SC_CHUNK)
            pltpu.sync_copy(idx_hbm.at[pl.ds(off, SC_CHUNK)], idx_v)
            pltpu.sync_copy(table_hbm.at[idx_v], rows_v)
            pltpu.sync_copy(rows_v, out_hbm.at[pl.ds(off, SC_CHUNK)])

    return k(table, idx)


def _combine_kernel(h_ref, r_ref, y1_ref, y2_ref, g_ref, b_ref, o_ref, *, alpha):
    g1 = r_ref[:, 2:3]
    g2 = r_ref[:, 3:4]
    ffn = g1 * y1_ref[...] + g2 * y2_ref[...]
    o_ref[...] = _layer_norm(alpha * h_ref[...] + ffn, g_ref[...], b_ref[...])


def _combine(h2, routing, y12, g, bb, alpha):
    t, d = h2.shape
    tm = 512
    nt = t // tm
    row = pl.BlockSpec((tm, d), lambda i: (i, 0))
    vec = pl.BlockSpec((1, d), lambda i: (0, 0))
    return pl.pallas_call(
        functools.partial(_combine_kernel, alpha=alpha),
        grid=(nt,),
        in_specs=[row, pl.BlockSpec((tm, LANES), lambda i: (i, 0)),
                  row, pl.BlockSpec((tm, d), lambda i: (nt + i, 0)), vec, vec],
        out_specs=row,
        out_shape=jax.ShapeDtypeStruct((t, d), F32),
        compiler_params=_cparams(("parallel",)),
        name="combine_ln",
    )(h2, routing, y12, y12, g.reshape(1, d), bb.reshape(1, d))


def _moe(h2, rg_w, rg_b, re_w, re_b, wg, wu, wd, g, bb, alpha):
    t, d = h2.shape
    routing, counts = _router(h2, rg_w, rg_b, re_w, re_b)
    eid = routing[:, 0:2].astype(jnp.int32)
    rank = routing[:, 4:6].astype(jnp.int32)
    cnt = counts[0, N_GROUPS:N_GROUPS + N_EXPERTS].astype(jnp.int32)
    padded = (cnt + MOE_ROWS - 1) // MOE_ROWS * MOE_ROWS
    pad_end = jnp.cumsum(padded)
    pad_start = pad_end - padded
    experts = jnp.arange(N_EXPERTS, dtype=jnp.int32)
    dest = jnp.sum(jnp.where(eid[..., None] == experts, pad_start, 0), axis=-1) + rank
    n_slots = t * TOP_K
    n_blocks = (n_slots + N_EXPERTS * (MOE_ROWS - 1) + MOE_ROWS - 1) // MOE_ROWS
    cap = n_blocks * MOE_ROWS
    blk_row = jnp.arange(n_blocks, dtype=jnp.int32) * MOE_ROWS
    block_e = jnp.minimum(jnp.sum((pad_end[None, :] <= blk_row[:, None]).astype(jnp.int32), axis=1),
                          N_EXPERTS - 1)
    is_e = block_e[:, None] == experts
    blk_cnt = jnp.sum(jnp.where(is_e, cnt, 0), axis=1)
    blk_start = jnp.sum(jnp.where(is_e, pad_start, 0), axis=1)
    n_valid = jnp.clip(blk_cnt - (blk_row - blk_start), 0, MOE_ROWS).astype(jnp.int32)
    xs = _sc_scatter_rows(h2, dest[:, 0], dest[:, 1], cap)
    ys = _moe_experts(xs, block_e, n_valid, wg, wu, wd)
    y12 = _sc_gather_rows(ys, jnp.concatenate([dest[:, 0], dest[:, 1]]))
    return _combine(h2, routing, y12, g, bb, alpha)


def _mixing_layer(h, b, s, l, p, lam_init):
    t = b * s
    alpha = (2 * p['w_in'].shape[0]) ** 0.25
    tab_diff = _rope_tables(s, DIFF_QK_DIM)
    tab_dil = _rope_tables(s, HEAD_DIM)
    z = _in_proj(h, p['w_in'][l].astype(BF16))
    z3 = z.reshape(b, s, -1)
    qkv_na = _prep(z3, None, CB_NA, 1, 0, HEAD_DIM ** -0.5 * LOG2E)
    y_a = _na_attention(qkv_na, _na_bias_table(p['na_rpb'][l]))
    qkv_diff = _prep(z3, tab_diff, CB_DIFF, 1, DIFF_QK_DIM // 8, DIFF_QK_DIM ** -0.5 * LOG2E)
    y_b = _diff_attention(qkv_diff, p['diff_lam'][l], p['diff_subln_g'][l], lam_init)
    y_c = _pool(z3, p['pool_w'][l], p['pool_scale'][l])
    dil = []
    for pi, (_, r) in enumerate(DIL_PATTERNS):
        qkv = _prep(z3, tab_dil, CB_DIL + 6 * pi, r, HEAD_DIM // 8, HEAD_DIM ** -0.5 * LOG2E)
        dil.append(_swa(qkv, s // r))
    y_d = _dil_merge(dil)
    ys = [a.reshape(t, -1) for a in (y_a, y_b, y_c, y_d)]
    return _merge(ys, z, p['b_gate'][l], p['w_branch'][l].astype(BF16), p['w_out'][l].astype(BF16), h,
                  p['ln1_g'][l], p['ln1_b'][l], alpha)


def kernel(x, emb_ln_g, emb_ln_b, w_in, b_gate, na_rpb, diff_lam, diff_subln_g, pool_w, pool_scale,
           w_branch, w_out, ln1_g, ln1_b, router_group_w, router_group_b, router_expert_w,
           router_expert_b, expert_w_gate, expert_w_up, expert_w_down, ln2_g, ln2_b):
    b, s, d = x.shape
    depth = w_in.shape[0]
    alpha = (2 * depth) ** 0.25
    p = dict(w_in=w_in, b_gate=b_gate, na_rpb=na_rpb, diff_lam=diff_lam, diff_subln_g=diff_subln_g,
             pool_w=pool_w, pool_scale=pool_scale, w_branch=w_branch, w_out=w_out, ln1_g=ln1_g, ln1_b=ln1_b)
    h = _embed_ln(x.reshape(b * s, d), emb_ln_g, emb_ln_b)
    for l in range(depth):
        lam_init = 0.8 - 0.6 * math.exp(-0.3 * l)
        h = _mixing_layer(h, b, s, l, p, lam_init)
        h = _moe(h, router_group_w[l], router_group_b[l], router_expert_w[l], router_expert_b[l],
                 expert_w_gate[l].astype(BF16), expert_w_up[l].astype(BF16),
                 expert_w_down[l].astype(BF16), ln2_g[l], ln2_b[l], alpha)
    return h.reshape(b, s, d)
```

```python
import functools
import math

import jax
import jax.numpy as jnp
import numpy as np
from jax import lax
from jax.experimental import pallas as pl
from jax.experimental.pallas import tpu as pltpu
from jax.experimental.pallas import tpu_sc as plsc

F32 = jnp.float32
BF16 = jnp.bfloat16

LANES = 128
GRID_W = 64
HEAD_DIM = 64
ROPE_THETA = 500000.0
LN_EPS = 1e-5
NA_ROWS = 8
NA_COLS = 16
DIFF_QK_DIM = 32
POOL_WINDOWS = (2, 4, 8, 16)
POOL_GROUP = 64
DIL_PATTERNS = ((128, 1), (512, 4), (2048, 16))
DIL_HALF = 64
N_GROUPS = 4
EXPERTS_PER_GROUP = 8
N_EXPERTS = N_GROUPS * EXPERTS_PER_GROUP
TOP_K = 2
MOE_ROWS = 256
NEG = -1e30

COL_NA = 0
COL_DIFF = 768
COL_POOL = 1536
COL_DIL = 1792
COL_GATE = 4096

VMEM_LIMIT = 56 * 1024 * 1024


def _cparams(sem, vmem=VMEM_LIMIT):
    return pltpu.CompilerParams(dimension_semantics=sem, vmem_limit_bytes=vmem)


def _layer_norm(x, g, b):
    mu = jnp.mean(x, axis=-1, keepdims=True)
    xc = x - mu
    var = jnp.mean(xc * xc, axis=-1, keepdims=True)
    return xc * lax.rsqrt(var + LN_EPS) * g + b


def _dot(a, b):
    return jnp.dot(a, b, preferred_element_type=F32)


def _dot_nt(a, b):
    return lax.dot_general(a, b, (((1,), (1,)), ((), ())), preferred_element_type=F32)


def _lane(shape):
    return lax.broadcasted_iota(jnp.int32, shape, len(shape) - 1)


def _ln_kernel(x_ref, g_ref, b_ref, o_ref):
    o_ref[...] = _layer_norm(x_ref[...], g_ref[...], b_ref[...])


def _embed_ln(x2, g, b):
    t, d = x2.shape
    tm = 1024
    return pl.pallas_call(
        _ln_kernel,
        grid=(t // tm,),
        in_specs=[pl.BlockSpec((tm, d), lambda i: (i, 0)),
                  pl.BlockSpec((1, d), lambda i: (0, 0)),
                  pl.BlockSpec((1, d), lambda i: (0, 0))],
        out_specs=pl.BlockSpec((tm, d), lambda i: (i, 0)),
        out_shape=jax.ShapeDtypeStruct((t, d), F32),
        compiler_params=_cparams(("parallel",)),
        name="embed_ln",
    )(x2, g.reshape(1, d), b.reshape(1, d))


PROJ_TM = 1024
QKV_W = 6 * LANES
WCOL = 256


def _proj_qkv_kernel(x_ref, wq_ref, wk_ref, wv_ref, *rest, r, half, q_scale):
    if half:
        t_ref, o_ref, wb_ref, zs_ref = rest
    else:
        (o_ref, wb_ref, zs_ref), t_ref = rest, None
    first = (pl.program_id(0) == 0) & (pl.program_id(1) == 0)

    @pl.when(first)
    def _():
        for j, w_ref in enumerate((wq_ref, wk_ref, wv_ref)):
            wb_ref[:, j * WCOL:(j + 1) * WCOL] = w_ref[0].astype(BF16)

    tm = x_ref.shape[1]
    n = tm // r
    z = _dot(x_ref[0].astype(BF16), wb_ref[...])
    for c in range(QKV_W // LANES):
        blk = z[:, c * LANES:(c + 1) * LANES]
        if half and c < 4:
            blk = _rope(blk, t_ref, half)
        if c < 2:
            blk = blk * q_scale
        if r == 1:
            o_ref[0, 0, :, c * LANES:(c + 1) * LANES] = blk.astype(BF16)
        else:
            zs_ref[c] = blk
            for m in range(r):
                o_ref[0, m, :, c * LANES:(c + 1) * LANES] = zs_ref[
                    c, pl.ds(m, n, stride=r), :].astype(BF16)


def _proj_qkv(h3, w_in, l, col0, tables, r, half, q_scale):
    b, s, d = h3.shape
    tm = PROJ_TM
    wblk = col0 // WCOL
    wspec = lambda j: pl.BlockSpec((1, d, WCOL), lambda bi, i: (l, 0, wblk + j))
    in_specs = [pl.BlockSpec((1, tm, d), lambda bi, i: (bi, i, 0)), wspec(0), wspec(1), wspec(2)]
    args = [h3, w_in, w_in, w_in]
    if half:
        in_specs.append(pl.BlockSpec((3, tm, LANES), lambda bi, i: (0, i, 0)))
        args.append(tables)
    out = pl.pallas_call(
        functools.partial(_proj_qkv_kernel, r=r, half=half, q_scale=q_scale),
        grid=(b, s // tm),
        in_specs=in_specs,
        out_specs=pl.BlockSpec((1, r, tm // r, QKV_W), lambda bi, i: (bi, 0, i, 0)),
        out_shape=jax.ShapeDtypeStruct((b, r, s // r, QKV_W), BF16),
        scratch_shapes=[pltpu.VMEM((d, QKV_W), BF16), pltpu.VMEM((QKV_W // LANES, tm, LANES), F32)],
        compiler_params=_cparams(("arbitrary", "arbitrary")),
        name=f"proj_qkv_c{col0}",
    )(*args)
    return out.reshape(b, s, QKV_W)


def _proj_pool_kernel(x_ref, w_ref, o_ref, wb_ref):
    @pl.when(pl.program_id(0) == 0)
    def _():
        wb_ref[...] = w_ref[0].astype(BF16)

    o_ref[...] = _dot(x_ref[...].astype(BF16), wb_ref[...])


def _proj_pool(h2, w_in, l, col0):
    t, d = h2.shape
    tm = PROJ_TM
    return pl.pallas_call(
        _proj_pool_kernel,
        grid=(t // tm,),
        in_specs=[pl.BlockSpec((tm, d), lambda i: (i, 0)),
                  pl.BlockSpec((1, d, WCOL), lambda i: (l, 0, col0 // WCOL))],
        out_specs=pl.BlockSpec((tm, WCOL), lambda i: (i, 0)),
        out_shape=jax.ShapeDtypeStruct((t, WCOL), F32),
        scratch_shapes=[pltpu.VMEM((d, WCOL), BF16)],
        compiler_params=_cparams(("arbitrary",)),
        name="proj_pool",
    )(h2, w_in)


def _rope_tables(seq, head_w):
    rot = head_w // 4
    half = rot // 2
    inv_freq = jnp.exp(jnp.arange(half, dtype=F32) * (-2.0 * math.log(ROPE_THETA) / rot))
    ang = jnp.arange(seq, dtype=jnp.int32).astype(F32)[:, None] * inv_freq[None, :]
    cos, sin = jnp.cos(ang), jnp.sin(ang)
    zero = jnp.zeros((seq, head_w - rot), F32)
    zh = jnp.zeros((seq, half), F32)
    t0 = jnp.concatenate([cos, cos, jnp.ones((seq, head_w - rot), F32)], axis=1)
    t1 = jnp.concatenate([-sin, zh, zero], axis=1)
    t2 = jnp.concatenate([zh, sin, zero], axis=1)
    reps = LANES // head_w
    return jnp.stack([jnp.tile(t0, (1, reps)), jnp.tile(t1, (1, reps)), jnp.tile(t2, (1, reps))])


def _rope(x, t_ref, half):
    return (x * t_ref[0] + pltpu.roll(x, LANES - half, 1) * t_ref[1]
            + pltpu.roll(x, half, 1) * t_ref[2])


LOG2E = math.log2(math.e)
LN2 = math.log(2.0)


def _na_bias_table(rpb):
    kr, kc = NA_ROWS, NA_COLS
    n_heads = rpb.shape[0]
    col = np.arange(GRID_W)
    col_start = np.clip(col - kc // 2, 0, GRID_W - kc)
    in_win = (col[None, :] >= col_start[:, None]) & (col[None, :] < col_start[:, None] + kc)
    dc = np.clip(col[None, :] - col[:, None] + kc - 1, 0, 2 * kc - 2)
    case = np.arange(kr)
    dr = np.arange(kr)[None, :] - case[:, None] + NA_ROWS - 1
    sel_r = jnp.asarray((dr[..., None] == np.arange(2 * kr - 1)).astype(np.float32))
    sel_c = jnp.asarray(((dc[..., None] == np.arange(2 * kc - 1)) & in_win[..., None]).astype(np.float32))
    hi = lax.Precision.HIGHEST
    rp = rpb.astype(F32).reshape(n_heads // 2, 2, 2 * kr - 1, 2 * kc - 1) * LOG2E
    t1 = jnp.einsum('phij,aki->pahkj', rp, sel_r, precision=hi)
    bias = jnp.einsum('pahkj,qcj->pahqkc', t1, sel_c, precision=hi)
    bias = bias + jnp.asarray(np.where(in_win, 0.0, NEG).astype(np.float32))[:, None, :]
    return bias.reshape(n_heads // 2, kr, 2 * GRID_W, kr * GRID_W)


def _stack_heads(q, lane):
    zero = jnp.zeros_like(q)
    return jnp.concatenate([jnp.where(lane < HEAD_DIM, q, zero), jnp.where(lane < HEAD_DIM, zero, q)],
                           axis=0)


def _na_kernel(q_ref, k_ref, v_ref, bias_ref, o_ref, *, rows_per_step, n_rows):
    step = pl.program_id(2)
    lane = _lane((GRID_W, LANES))
    kwin = NA_ROWS * GRID_W
    for i in range(rows_per_step):
        r = step * rows_per_step + i
        start = jnp.clip(r - NA_ROWS // 2, 0, n_rows - NA_ROWS)
        case = r - start
        k0 = pl.multiple_of(start * GRID_W, GRID_W)
        q2 = _stack_heads(q_ref[0, i * GRID_W:(i + 1) * GRID_W, :], lane)
        sc = _dot_nt(q2, k_ref[0, pl.ds(k0, kwin), :]) + bias_ref[0, case]
        m = jnp.max(sc, axis=-1, keepdims=True)
        e = jnp.exp2(sc - m)
        den = jnp.sum(e, axis=-1, keepdims=True)
        pv = _dot(e.astype(BF16), v_ref[0, pl.ds(k0, kwin), :]) / den
        o = jnp.where(lane < HEAD_DIM, pv[0:GRID_W], pv[GRID_W:2 * GRID_W])
        o_ref[0, i * GRID_W:(i + 1) * GRID_W, :] = o.astype(BF16)


def _na_attention(qkv, bias):
    b, s, _ = qkv.shape
    n_rows = s // GRID_W
    rps = 8
    tq = rps * GRID_W
    return pl.pallas_call(
        functools.partial(_na_kernel, rows_per_step=rps, n_rows=n_rows),
        grid=(b, 2, n_rows // rps),
        in_specs=[pl.BlockSpec((1, tq, LANES), lambda bi, hp, i: (bi, i, hp)),
                  pl.BlockSpec((1, s, LANES), lambda bi, hp, i: (bi, 0, 2 + hp)),
                  pl.BlockSpec((1, s, LANES), lambda bi, hp, i: (bi, 0, 4 + hp)),
                  pl.BlockSpec((1, NA_ROWS, 2 * GRID_W, NA_ROWS * GRID_W),
                               lambda bi, hp, i: (hp, 0, 0, 0))],
        out_specs=pl.BlockSpec((1, tq, LANES), lambda bi, hp, i: (bi, i, hp)),
        out_shape=jax.ShapeDtypeStruct((b, s, 2 * LANES), BF16),
        compiler_params=_cparams(("parallel", "parallel", "arbitrary")),
        name="na_attn",
    )(qkv, qkv, qkv, bias)


def _diff_kernel(lam_ref, q_ref, k_ref, v_ref, g_ref, o_ref, *, lam_init):
    tq = q_ref.shape[1]
    lane = _lane((tq, LANES))
    dl = lam_ref[...]
    lam = (jnp.exp(jnp.sum(dl[0:1] * dl[1:2], axis=-1, keepdims=True))
           - jnp.exp(jnp.sum(dl[2:3] * dl[3:4], axis=-1, keepdims=True)) + lam_init)
    q = q_ref[0]
    zero = jnp.zeros_like(q)
    v = v_ref[0]
    v_lane = _lane(v.shape)
    one = jnp.ones_like(v)
    v_ext = (jnp.where(v_lane < HEAD_DIM, v, one), jnp.where(v_lane < HEAD_DIM, one, v))
    pv = []
    for lo in range(0, LANES, DIFF_QK_DIM):
        qm = jnp.where((lane >= lo) & (lane < lo + DIFF_QK_DIM), q, zero)
        sc = _dot_nt(qm, k_ref[0])
        e = jnp.exp2(sc - jnp.max(sc, axis=-1, keepdims=True))
        ev = _dot(e.astype(BF16), v_ext[lo // HEAD_DIM])
        pv.append(ev / pltpu.roll(ev, HEAD_DIM, 1))
    in_h0 = lane < HEAD_DIM
    o = jnp.where(in_h0, pv[0] - lam * pv[1], pv[2] - lam * pv[3])
    o2 = o * o
    ms0 = jnp.sum(jnp.where(in_h0, o2, 0.0), axis=-1, keepdims=True) / HEAD_DIM
    ms1 = jnp.sum(jnp.where(in_h0, 0.0, o2), axis=-1, keepdims=True) / HEAD_DIM
    ms = jnp.where(in_h0, ms0, ms1)
    o = o * lax.rsqrt(ms + LN_EPS) * g_ref[...] * (1.0 - lam_init)
    o_ref[0] = o.astype(BF16)


def _diff_attention(qkv, diff_lam, subln_g, lam_init):
    b, s, _ = qkv.shape
    tq = 512
    g2 = jnp.tile(subln_g.reshape(1, HEAD_DIM), (1, 2))
    return pl.pallas_call(
        functools.partial(_diff_kernel, lam_init=lam_init),
        grid=(b, 2, s // tq),
        in_specs=[pl.BlockSpec((4, DIFF_QK_DIM), lambda bi, hp, i: (0, 0)),
                  pl.BlockSpec((1, tq, LANES), lambda bi, hp, i: (bi, i, hp)),
                  pl.BlockSpec((1, s, LANES), lambda bi, hp, i: (bi, 0, 2 + hp)),
                  pl.BlockSpec((1, s, LANES), lambda bi, hp, i: (bi, 0, 4 + hp)),
                  pl.BlockSpec((1, LANES), lambda bi, hp, i: (0, 0))],
        out_specs=pl.BlockSpec((1, tq, LANES), lambda bi, hp, i: (bi, i, hp)),
        out_shape=jax.ShapeDtypeStruct((b, s, 2 * LANES), BF16),
        compiler_params=_cparams(("parallel", "parallel", "arbitrary")),
        name="diff_attn",
    )(diff_lam, qkv, qkv, qkv, g2)


POOL_PAD = 16
POOL_CHUNK = 512
POOL_HALO = 8


def _pool_kernel(u_ref, w_ref, sc_ref, o_ref, p_ref):
    s = u_ref.shape[1]
    width = u_ref.shape[2]
    p_ref[0:POOL_PAD, :] = jnp.zeros((POOL_PAD, width), F32)
    p_ref[POOL_PAD + s:POOL_PAD + s + POOL_PAD, :] = jnp.zeros((POOL_PAD, width), F32)
    p_ref[POOL_PAD:POOL_PAD + s, :] = u_ref[0]
    n = POOL_CHUNK + 2 * POOL_HALO
    lane = _lane((POOL_CHUNK, width))
    row = lax.broadcasted_iota(jnp.int32, (POOL_CHUNK, width), 0)
    w_of_lane = jnp.where(lane < POOL_GROUP, 2, jnp.where(lane < 2 * POOL_GROUP, 4,
                          jnp.where(lane < 3 * POOL_GROUP, 8, 16)))

    def body(ci, carry):
        c0 = pl.multiple_of(ci * POOL_CHUNK, POOL_CHUNK)
        x = p_ref[pl.ds(c0 + POOL_PAD - POOL_HALO, n), :]
        w2 = x + pltpu.roll(x, 1, 0)
        w4 = pltpu.roll(w2, 1, 0) + pltpu.roll(w2, n - 1, 0)
        w8 = pltpu.roll(w4, 2, 0) + pltpu.roll(w4, n - 2, 0)
        w16 = pltpu.roll(w8, 4, 0) + pltpu.roll(w8, n - 4, 0)
        u = x[POOL_HALO:POOL_HALO + POOL_CHUNK]
        wsum = jnp.where(lane < POOL_GROUP, w2[POOL_HALO:POOL_HALO + POOL_CHUNK],
                         jnp.where(lane < 2 * POOL_GROUP, w4[POOL_HALO:POOL_HALO + POOL_CHUNK],
                                   jnp.where(lane < 3 * POOL_GROUP, w8[POOL_HALO:POOL_HALO + POOL_CHUNK],
                                             w16[POOL_HALO:POOL_HALO + POOL_CHUNK])))
        t = row + c0
        half_w = w_of_lane // 2
        lo = jnp.maximum(t - half_w, 0)
        hi = jnp.minimum(t + w_of_lane - 1 - half_w, s - 1)
        cnt = (hi - lo + 1).astype(F32)
        dlt = wsum / cnt - u
        y = _dot(dlt.astype(BF16), w_ref[...]) * sc_ref[...]
        o_ref[0, pl.ds(c0, POOL_CHUNK), :] = y.astype(BF16)
        return carry

    lax.fori_loop(0, s // POOL_CHUNK, body, 0)


def _pool(u3, pool_w, pool_scale):
    b, s, width = u3.shape
    wbd = jax.scipy.linalg.block_diag(*[pool_w[g] for g in range(len(POOL_WINDOWS))]).astype(BF16)
    return pl.pallas_call(
        _pool_kernel,
        grid=(b,),
        in_specs=[pl.BlockSpec((1, s, width), lambda bi: (bi, 0, 0)),
                  pl.BlockSpec((width, width), lambda bi: (0, 0)),
                  pl.BlockSpec((1, width), lambda bi: (0, 0))],
        out_specs=pl.BlockSpec((1, s, width), lambda bi: (bi, 0, 0)),
        out_shape=jax.ShapeDtypeStruct((b, s, width), BF16),
        scratch_shapes=[pltpu.VMEM((s + 2 * POOL_PAD, width), F32)],
        compiler_params=_cparams(("parallel",)),
        name="pool",
    )(u3, wbd, pool_scale.reshape(1, width))


SWA_Q = 128
SWA_BAND = SWA_Q + 2 * DIL_HALF


def _swa_kernel(q_ref, k_ref, v_ref, o_ref, l_ref, *, length, chunk):
    c = pl.program_id(2)
    lane = _lane((SWA_Q, LANES))
    rel = (lax.broadcasted_iota(jnp.int32, (SWA_Q, SWA_BAND), 1)
           - lax.broadcasted_iota(jnp.int32, (SWA_Q, SWA_BAND), 0))
    for i in range(chunk // SWA_Q):
        r0 = c * chunk + i * SWA_Q
        l0 = r0 & (length - 1)
        lo = r0 - l0 + jnp.clip(l0 - DIL_HALF, 0, length - SWA_BAND)
        lo = pl.multiple_of(lo, DIL_HALF)
        q2 = _stack_heads(q_ref[0, i * SWA_Q:(i + 1) * SWA_Q, :], lane)
        d = rel + (lo - r0)
        valid = (d >= -DIL_HALF) & (d <= DIL_HALF)
        valid2 = jnp.concatenate([valid, valid], axis=0)
        sc = jnp.where(valid2, _dot_nt(q2, k_ref[0, pl.ds(lo, SWA_BAND), :]), NEG)
        m = jnp.max(sc, axis=-1, keepdims=True)
        e = jnp.exp2(sc - m)
        den = jnp.sum(e, axis=-1, keepdims=True)
        pv = _dot(e.astype(BF16), v_ref[0, pl.ds(lo, SWA_BAND), :]) / den
        lse2 = m + jnp.log2(den)
        o_ref[0, i * SWA_Q:(i + 1) * SWA_Q, :] = jnp.where(lane < HEAD_DIM, pv[0:SWA_Q], pv[SWA_Q:])
        l_ref[0, i * SWA_Q:(i + 1) * SWA_Q, :] = jnp.where(lane < HEAD_DIM, lse2[0:SWA_Q], lse2[SWA_Q:])


def _swa(qkv, length):
    b, s, _ = qkv.shape
    chunk = 1024
    shp = jax.ShapeDtypeStruct((b, s, 2 * LANES), F32)
    return pl.pallas_call(
        functools.partial(_swa_kernel, length=length, chunk=chunk),
        grid=(b, 2, s // chunk),
        in_specs=[pl.BlockSpec((1, chunk, LANES), lambda bi, hp, c: (bi, c, hp)),
                  pl.BlockSpec((1, s, LANES), lambda bi, hp, c: (bi, 0, 2 + hp)),
                  pl.BlockSpec((1, s, LANES), lambda bi, hp, c: (bi, 0, 4 + hp))],
        out_specs=[pl.BlockSpec((1, chunk, LANES), lambda bi, hp, c: (bi, c, hp)),
                   pl.BlockSpec((1, chunk, LANES), lambda bi, hp, c: (bi, c, hp))],
        out_shape=[shp, shp],
        compiler_params=_cparams(("parallel", "parallel", "arbitrary")),
        name=f"swa_l{length}",
    )(qkv, qkv, qkv)


def _dil_merge_kernel(o0_ref, l0_ref, o1_ref, l1_ref, o2_ref, l2_ref, y_ref, so1, sl1, so2, sl2):
    tm = y_ref.shape[1]
    r1 = DIL_PATTERNS[1][1]
    r2 = DIL_PATTERNS[2][1]
    for m in range(r1):
        so1[pl.ds(m, tm // r1, stride=r1), :] = o1_ref[0, m]
        sl1[pl.ds(m, tm // r1, stride=r1), :] = l1_ref[0, m]
    for m in range(r2):
        so2[pl.ds(m, tm // r2, stride=r2), :] = o2_ref[0, m]
        sl2[pl.ds(m, tm // r2, stride=r2), :] = l2_ref[0, m]
    l0, l1, l2 = l0_ref[0], sl1[...], sl2[...]
    mx = jnp.maximum(jnp.maximum(l0, l1), l2)
    e0, e1, e2 = jnp.exp2(l0 - mx), jnp.exp2(l1 - mx), jnp.exp2(l2 - mx)
    den = e0 + e1 + e2
    y = (e0 / den) * o0_ref[0] + (e1 / den) * so1[...] + (e2 / den) * so2[...]
    y_ref[0] = y.astype(BF16)


def _dil_merge(outs):
    (o0, l0), (o1, l1), (o2, l2) = outs
    b, s, w = o0.shape
    tm = 1024
    r1 = DIL_PATTERNS[1][1]
    r2 = DIL_PATTERNS[2][1]
    v1 = lambda a: a.reshape(b, r1, s // r1, w)
    v2 = lambda a: a.reshape(b, r2, s // r2, w)
    nat = pl.BlockSpec((1, tm, LANES), lambda bi, i, hp: (bi, i, hp))
    g1 = pl.BlockSpec((1, r1, tm // r1, LANES), lambda bi, i, hp: (bi, 0, i, hp))
    g2 = pl.BlockSpec((1, r2, tm // r2, LANES), lambda bi, i, hp: (bi, 0, i, hp))
    return pl.pallas_call(
        _dil_merge_kernel,
        grid=(b, s // tm, w // LANES),
        in_specs=[nat, nat, g1, g1, g2, g2],
        out_specs=nat,
        out_shape=jax.ShapeDtypeStruct((b, s, w), BF16),
        scratch_shapes=[pltpu.VMEM((tm, LANES), F32)] * 4,
        compiler_params=_cparams(("parallel", "parallel", "parallel")),
        name="dil_merge",
    )(o0, l0, v1(o1), v1(l1), v2(o2), v2(l2))


def _merge_kernel(ya_ref, yb_ref, yc_ref, yd_ref, wg_ref, bg_ref, wb_ref, wo_ref, h_ref, g_ref, b_ref,
                  o_ref, hb_ref, acc_ref, *, alpha):
    n = pl.program_id(1)

    @pl.when(n == 0)
    def _():
        hb_ref[...] = h_ref[...].astype(BF16)

    zg = _dot(hb_ref[...], wg_ref[...]) + bg_ref[0]
    gate = 1.0 / (1.0 + jnp.exp(-zg))
    for k, y_ref in enumerate((ya_ref, yb_ref, yc_ref, yd_ref)):
        @pl.when(n == k)
        def _(y_ref=y_ref, k=k):
            term = gate * _dot(y_ref[...], wb_ref[0])
            if k == 0:
                acc_ref[...] = term
            else:
                acc_ref[...] += term

    @pl.when(n == pl.num_programs(1) - 1)
    def _():
        mix = _dot(acc_ref[...].astype(BF16), wo_ref[...])
        o_ref[...] = _layer_norm(alpha * h_ref[...] + mix, g_ref[...], b_ref[...])


def _merge(ys, wg, b_gate, wb, wo, h2, g, bb, alpha):
    t, d = h2.shape
    bw = ys[0].shape[1]
    nb = len(ys)
    tm = PROJ_TM
    yspec = pl.BlockSpec((tm, bw), lambda i, n: (i, 0))
    vec = pl.BlockSpec((1, d), lambda i, n: (0, 0))
    return pl.pallas_call(
        functools.partial(_merge_kernel, alpha=alpha),
        grid=(t // tm, nb),
        in_specs=[yspec, yspec, yspec, yspec,
                  pl.BlockSpec((d, d), lambda i, n: (0, n)),
                  pl.BlockSpec((1, 1, d), lambda i, n: (n, 0, 0)),
                  pl.BlockSpec((1, bw, d), lambda i, n: (n, 0, 0)),
                  pl.BlockSpec((d, d), lambda i, n: (0, 0)),
                  pl.BlockSpec((tm, d), lambda i, n: (i, 0)),
                  vec, vec],
        out_specs=pl.BlockSpec((tm, d), lambda i, n: (i, 0)),
        out_shape=jax.ShapeDtypeStruct((t, d), F32),
        scratch_shapes=[pltpu.VMEM((tm, d), BF16), pltpu.VMEM((tm, d), F32)],
        compiler_params=_cparams(("parallel", "arbitrary")),
        name="merge",
    )(*ys, wg, b_gate.reshape(nb, 1, d), wb, wo, h2, g.reshape(1, d), bb.reshape(1, d))


ROUTER_TM = 256


def _router_kernel(h_ref, w_ref, b_ref, o_ref, cnt_ref, carry):
    @pl.when(pl.program_id(0) == 0)
    def _():
        carry[...] = jnp.zeros_like(carry)

    tm = h_ref.shape[0]
    logits = jnp.dot(h_ref[...], w_ref[...], preferred_element_type=F32,
                     precision=lax.Precision.HIGHEST) + b_ref[...]
    lane = _lane((tm, LANES))
    big = jnp.int32(1 << 20)
    is_g = lane < N_GROUPS
    gl = jnp.where(is_g, logits, -jnp.inf)
    gmax = jnp.max(gl, axis=-1, keepdims=True)
    gsel = jnp.min(jnp.where(is_g & (gl == gmax), lane, big), axis=-1, keepdims=True)
    pg = 1.0 / jnp.sum(jnp.exp(gl - gmax), axis=-1, keepdims=True)
    e_lo = N_GROUPS + gsel * EXPERTS_PER_GROUP
    in_grp = (lane >= e_lo) & (lane < e_lo + EXPERTS_PER_GROUP)
    el = jnp.where(in_grp, logits, -jnp.inf)
    v1 = jnp.max(el, axis=-1, keepdims=True)
    i1 = jnp.min(jnp.where(in_grp & (el == v1), lane, big), axis=-1, keepdims=True)
    el2 = jnp.where(lane == i1, -jnp.inf, el)
    v2 = jnp.max(el2, axis=-1, keepdims=True)
    i2 = jnp.min(jnp.where(in_grp & (lane != i1) & (el2 == v2), lane, big), axis=-1, keepdims=True)
    t2 = jnp.exp(v2 - v1)
    g1 = pg / (1.0 + t2)
    g2 = pg * t2 / (1.0 + t2)
    oh1 = (lane == i1)
    oh2 = (lane == i2)
    tri = (lax.broadcasted_iota(jnp.int32, (tm, tm), 1)
           < lax.broadcasted_iota(jnp.int32, (tm, tm), 0)).astype(BF16)
    oh1b = oh1.astype(BF16)
    oh2b = oh2.astype(BF16)
    c0 = carry[...]
    c1 = c0 + jnp.sum(oh1b.astype(F32), axis=0, keepdims=True)
    rank1 = jnp.sum(jnp.where(oh1, _dot(tri, oh1b) + c0, 0.0), axis=-1, keepdims=True)
    rank2 = jnp.sum(jnp.where(oh2, _dot(tri, oh2b) + c1, 0.0), axis=-1, keepdims=True)
    c2 = c1 + jnp.sum(oh2b.astype(F32), axis=0, keepdims=True)
    carry[...] = c2
    cnt_ref[...] = c2
    e1 = (i1 - N_GROUPS).astype(F32)
    e2 = (i2 - N_GROUPS).astype(F32)
    out = jnp.where(lane == 0, e1, jnp.where(lane == 1, e2, jnp.where(lane == 2, g1, jnp.where(
        lane == 3, g2, jnp.where(lane == 4, rank1, jnp.where(lane == 5, rank2, 0.0))))))
    o_ref[...] = out


def _router(h2, rg_w, rg_b, re_w, re_b):
    t, d = h2.shape
    pad = LANES - N_GROUPS - N_EXPERTS
    w = jnp.concatenate([rg_w, re_w, jnp.zeros((d, pad), F32)], axis=1)
    bias = jnp.concatenate([rg_b, re_b, jnp.zeros((pad,), F32)]).reshape(1, LANES)
    tm = ROUTER_TM
    return pl.pallas_call(
        _router_kernel,
        grid=(t // tm,),
        in_specs=[pl.BlockSpec((tm, d), lambda i: (i, 0)),
                  pl.BlockSpec((d, LANES), lambda i: (0, 0)),
                  pl.BlockSpec((1, LANES), lambda i: (0, 0))],
        out_specs=[pl.BlockSpec((tm, LANES), lambda i: (i, 0)),
                   pl.BlockSpec((1, LANES), lambda i: (0, 0))],
        out_shape=[jax.ShapeDtypeStruct((t, LANES), F32), jax.ShapeDtypeStruct((1, LANES), F32)],
        scratch_shapes=[pltpu.VMEM((1, LANES), F32)],
        compiler_params=_cparams(("arbitrary",)),
        name="router",
    )(h2, w, bias)


def _moe_kernel(be_ref, nv_ref, x_ref, wg_ref, wu_ref, wd_ref, o_ref, wgb, wub, wdb):
    j = pl.program_id(0)
    n_valid = nv_ref[j]
    new_expert = (j == 0) | (be_ref[j] != be_ref[jnp.maximum(j - 1, 0)])

    @pl.when(new_expert)
    def _():
        wgb[...] = wg_ref[0].astype(BF16)
        wub[...] = wu_ref[0].astype(BF16)
        wdb[...] = wd_ref[0].astype(BF16)

    @pl.when(n_valid > 0)
    def _():
        row = lax.broadcasted_iota(jnp.int32, x_ref.shape, 0)
        x = jnp.where(row < n_valid, x_ref[...], 0.0).astype(BF16)
        g = _dot(x, wgb[...])
        u = _dot(x, wub[...])
        hmid = (g * (1.0 / (1.0 + jnp.exp(-g)))) * u
        o_ref[...] = _dot(hmid.astype(BF16), wdb[...])

    @pl.when(n_valid <= 0)
    def _():
        o_ref[...] = jnp.zeros_like(o_ref)


def _moe_experts(xs, block_e, n_valid, wg, wu, wd, l):
    cap, d = xs.shape
    de = wg.shape[3]
    n_blocks = cap // MOE_ROWS
    grid_spec = pltpu.PrefetchScalarGridSpec(
        num_scalar_prefetch=2,
        grid=(n_blocks,),
        in_specs=[pl.BlockSpec((MOE_ROWS, d), lambda j, be, nv: (j, 0)),
                  pl.BlockSpec((None, 1, d, de), lambda j, be, nv: (l, be[j], 0, 0)),
                  pl.BlockSpec((None, 1, d, de), lambda j, be, nv: (l, be[j], 0, 0)),
                  pl.BlockSpec((None, 1, de, d), lambda j, be, nv: (l, be[j], 0, 0))],
        out_specs=pl.BlockSpec((MOE_ROWS, d), lambda j, be, nv: (j, 0)),
        scratch_shapes=[pltpu.VMEM((d, de), BF16), pltpu.VMEM((d, de), BF16), pltpu.VMEM((de, d), BF16)],
    )
    return pl.pallas_call(
        _moe_kernel,
        grid_spec=grid_spec,
        out_shape=jax.ShapeDtypeStruct((cap, d), F32),
        compiler_params=_cparams(("arbitrary",)),
        name="moe_experts",
    )(block_e, n_valid, xs, wg, wu, wd)


SC_CORES = 2
SC_SUBCORES = 16
SC_WORKERS = SC_CORES * SC_SUBCORES
SC_CHUNK = 64


def _sc_mesh():
    return plsc.VectorSubcoreMesh(core_axis_name="c", subcore_axis_name="s",
                                  num_cores=SC_CORES, num_subcores=SC_SUBCORES)


def _sc_scatter_rows(src, idx0, idx1, n_out):
    t, d = src.shape
    per_w = t // SC_WORKERS
    n_chunks = per_w // SC_CHUNK

    @functools.partial(
        pl.kernel, mesh=_sc_mesh(),
        out_type=jax.ShapeDtypeStruct((n_out, d), src.dtype),
        scratch_types=[pltpu.VMEM((SC_CHUNK,), jnp.int32), pltpu.VMEM((SC_CHUNK,), jnp.int32),
                       pltpu.VMEM((SC_CHUNK, d), src.dtype)],
        name="sc_dispatch",
    )
    def k(src_hbm, i0_hbm, i1_hbm, out_hbm, i0_v, i1_v, rows_v):
        wid = lax.axis_index("s") * SC_CORES + lax.axis_index("c")
        base = wid * per_w

        @pl.loop(0, n_chunks)
        def _(ci):
            off = pl.multiple_of(base + ci * SC_CHUNK, SC_CHUNK)
            pltpu.sync_copy(src_hbm.at[pl.ds(off, SC_CHUNK)], rows_v)
            pltpu.sync_copy(i0_hbm.at[pl.ds(off, SC_CHUNK)], i0_v)
            pltpu.sync_copy(i1_hbm.at[pl.ds(off, SC_CHUNK)], i1_v)
            pltpu.sync_copy(rows_v, out_hbm.at[i0_v])
            pltpu.sync_copy(rows_v, out_hbm.at[i1_v])

    return k(src, idx0, idx1)


def _sc_gather_rows(table, idx):
    n = idx.shape[0]
    d = table.shape[1]
    per_w = n // SC_WORKERS
    n_chunks = per_w // SC_CHUNK

    @functools.partial(
        pl.kernel, mesh=_sc_mesh(),
        out_type=jax.ShapeDtypeStruct((n, d), table.dtype),
        scratch_types=[pltpu.VMEM((SC_CHUNK,), jnp.int32), pltpu.VMEM((SC_CHUNK, d), table.dtype)],
        name="sc_collect",
    )
    def k(table_hbm, idx_hbm, out_hbm, idx_v, rows_v):
        wid = lax.axis_index("s") * SC_CORES + lax.axis_index("c")
        base = wid * per_w

        @pl.loop(0, n_chunks)
        def _(ci):
            off = pl.multiple_of(base + ci * SC_CHUNK, SC_CHUNK)
            pltpu.sync_copy(idx_hbm.at[pl.ds(off, SC_CHUNK)], idx_v)
            pltpu.sync_copy(table_hbm.at[idx_v], rows_v)
            pltpu.sync_copy(rows_v, out_hbm.at[pl.ds(off, SC_CHUNK)])

    return k(table, idx)


def _combine_kernel(h_ref, r_ref, y1_ref, y2_ref, g_ref, b_ref, o_ref, *, alpha):
    g1 = r_ref[:, 2:3]
    g2 = r_ref[:, 3:4]
    ffn = g1 * y1_ref[...] + g2 * y2_ref[...]
    o_ref[...] = _layer_norm(alpha * h_ref[...] + ffn, g_ref[...], b_ref[...])


def _combine(h2, routing, y12, g, bb, alpha):
    t, d = h2.shape
    tm = 512
    nt = t // tm
    row = pl.BlockSpec((tm, d), lambda i: (i, 0))
    vec = pl.BlockSpec((1, d), lambda i: (0, 0))
    return pl.pallas_call(
        functools.partial(_combine_kernel, alpha=alpha),
        grid=(nt,),
        in_specs=[row, pl.BlockSpec((tm, LANES), lambda i: (i, 0)),
                  row, pl.BlockSpec((tm, d), lambda i: (nt + i, 0)), vec, vec],
        out_specs=row,
        out_shape=jax.ShapeDtypeStruct((t, d), F32),
        compiler_params=_cparams(("parallel",)),
        name="combine_ln",
    )(h2, routing, y12, y12, g.reshape(1, d), bb.reshape(1, d))


def _moe(h2, rg_w, rg_b, re_w, re_b, wg, wu, wd, l, g, bb, alpha):
    t, d = h2.shape
    routing, counts = _router(h2, rg_w, rg_b, re_w, re_b)
    eid = routing[:, 0:2].astype(jnp.int32)
    rank = routing[:, 4:6].astype(jnp.int32)
    cnt = counts[0, N_GROUPS:N_GROUPS + N_EXPERTS].astype(jnp.int32)
    padded = (cnt + MOE_ROWS - 1) // MOE_ROWS * MOE_ROWS
    pad_end = jnp.cumsum(padded)
    pad_start = pad_end - padded
    experts = jnp.arange(N_EXPERTS, dtype=jnp.int32)
    dest = jnp.sum(jnp.where(eid[..., None] == experts, pad_start, 0), axis=-1) + rank
    n_slots = t * TOP_K
    n_blocks = (n_slots + N_EXPERTS * (MOE_ROWS - 1) + MOE_ROWS - 1) // MOE_ROWS
    cap = n_blocks * MOE_ROWS
    blk_row = jnp.arange(n_blocks, dtype=jnp.int32) * MOE_ROWS
    block_e = jnp.minimum(jnp.sum((pad_end[None, :] <= blk_row[:, None]).astype(jnp.int32), axis=1),
                          N_EXPERTS - 1)
    is_e = block_e[:, None] == experts
    blk_cnt = jnp.sum(jnp.where(is_e, cnt, 0), axis=1)
    blk_start = jnp.sum(jnp.where(is_e, pad_start, 0), axis=1)
    n_valid = jnp.clip(blk_cnt - (blk_row - blk_start), 0, MOE_ROWS).astype(jnp.int32)
    xs = _sc_scatter_rows(h2, dest[:, 0], dest[:, 1], cap)
    ys = _moe_experts(xs, block_e, n_valid, wg, wu, wd, l)
    y12 = _sc_gather_rows(ys, jnp.concatenate([dest[:, 0], dest[:, 1]]))
    return _combine(h2, routing, y12, g, bb, alpha)


def _mixing_layer(h, b, s, l, p, lam_init):
    t = b * s
    alpha = (2 * p['w_in'].shape[0]) ** 0.25
    d = h.shape[1]
    w_in = p['w_in']
    tab_diff = _rope_tables(s, DIFF_QK_DIM)
    tab_dil = _rope_tables(s, HEAD_DIM)
    h3 = h.reshape(b, s, d)
    qkv_na = _proj_qkv(h3, w_in, l, COL_NA, None, 1, 0, HEAD_DIM ** -0.5 * LOG2E)
    y_a = _na_attention(qkv_na, _na_bias_table(p['na_rpb'][l]))
    qkv_diff = _proj_qkv(h3, w_in, l, COL_DIFF, tab_diff, 1, DIFF_QK_DIM // 8, DIFF_QK_DIM ** -0.5 * LOG2E)
    y_b = _diff_attention(qkv_diff, p['diff_lam'][l], p['diff_subln_g'][l], lam_init)
    u = _proj_pool(h, w_in, l, COL_POOL)
    y_c = _pool(u.reshape(b, s, -1), p['pool_w'][l], p['pool_scale'][l])
    dil = []
    for pi, (_, r) in enumerate(DIL_PATTERNS):
        qkv = _proj_qkv(h3, w_in, l, COL_DIL + QKV_W * pi, tab_dil, r, HEAD_DIM // 8,
                        HEAD_DIM ** -0.5 * LOG2E)
        dil.append(_swa(qkv, s // r))
    y_d = _dil_merge(dil)
    ys = [a.reshape(t, -1) for a in (y_a, y_b, y_c, y_d)]
    wg = lax.slice_in_dim(w_in[l], COL_GATE, COL_GATE + len(ys) * d, axis=1).astype(BF16)
    return _merge(ys, wg, p['b_gate'][l], p['w_branch'][l].astype(BF16), p['w_out'][l].astype(BF16), h,
                  p['ln1_g'][l], p['ln1_b'][l], alpha)


def kernel(x, emb_ln_g, emb_ln_b, w_in, b_gate, na_rpb, diff_lam, diff_subln_g, pool_w, pool_scale,
           w_branch, w_out, ln1_g, ln1_b, router_group_w, router_group_b, router_expert_w,
           router_expert_b, expert_w_gate, expert_w_up, expert_w_down, ln2_g, ln2_b):
    b, s, d = x.shape
    depth = w_in.shape[0]
    alpha = (2 * depth) ** 0.25
    p = dict(w_in=w_in, b_gate=b_gate, na_rpb=na_rpb, diff_lam=diff_lam, diff_subln_g=diff_subln_g,
             pool_w=pool_w, pool_scale=pool_scale, w_branch=w_branch, w_out=w_out, ln1_g=ln1_g, ln1_b=ln1_b)
    h = _embed_ln(x.reshape(b * s, d), emb_ln_g, emb_ln_b)
    for l in range(depth):
        lam_init = 0.8 - 0.6 * math.exp(-0.3 * l)
        h = _mixing_layer(h, b, s, l, p, lam_init)
        h = _moe(h, router_group_w[l], router_group_b[l], router_expert_w[l], router_expert_b[l],
                 expert_w_gate, expert_w_up, expert_w_down, l, ln2_g[l], ln2_b[l], alpha)
    return h.reshape(b, s, d)
```

```python
import functools
import math

import jax
import jax.numpy as jnp
import numpy as np
from jax import lax
from jax.experimental import pallas as pl
from jax.experimental.pallas import tpu as pltpu
from jax.experimental.pallas import tpu_sc as plsc

F32 = jnp.float32
BF16 = jnp.bfloat16

LANES = 128
GRID_W = 64
HEAD_DIM = 64
ROPE_THETA = 500000.0
LN_EPS = 1e-5
NA_ROWS = 8
NA_COLS = 16
DIFF_QK_DIM = 32
POOL_WINDOWS = (2, 4, 8, 16)
POOL_GROUP = 64
DIL_PATTERNS = ((128, 1), (512, 4), (2048, 16))
DIL_HALF = 64
N_GROUPS = 4
EXPERTS_PER_GROUP = 8
N_EXPERTS = N_GROUPS * EXPERTS_PER_GROUP
TOP_K = 2
MOE_ROWS = 256
NEG = -1e30

COL_NA = 0
COL_DIFF = 768
COL_POOL = 1536
COL_DIL = 1792
COL_GATE = 4096

VMEM_LIMIT = 56 * 1024 * 1024


def _cparams(sem, vmem=VMEM_LIMIT):
    return pltpu.CompilerParams(dimension_semantics=sem, vmem_limit_bytes=vmem)


def _layer_norm(x, g, b):
    mu = jnp.mean(x, axis=-1, keepdims=True)
    xc = x - mu
    var = jnp.mean(xc * xc, axis=-1, keepdims=True)
    return xc * lax.rsqrt(var + LN_EPS) * g + b


def _dot(a, b):
    return jnp.dot(a, b, preferred_element_type=F32)


def _dot_nt(a, b):
    return lax.dot_general(a, b, (((1,), (1,)), ((), ())), preferred_element_type=F32)


def _lane(shape):
    return lax.broadcasted_iota(jnp.int32, shape, len(shape) - 1)


HI16 = -65536


def _pack_bf16_pairs(x):
    w = x.shape[1] // 2
    hi = lax.bitcast_convert_type(x[:, :w].astype(BF16).astype(F32), jnp.int32)
    lo = lax.bitcast_convert_type(x[:, w:].astype(BF16).astype(F32), jnp.int32)
    return (hi & HI16) | lax.shift_right_logical(lo, 16)


def _unpack_bf16_pairs(p):
    hi = lax.bitcast_convert_type(p & HI16, F32)
    lo = lax.bitcast_convert_type(lax.shift_left(p, 16), F32)
    return jnp.concatenate([hi, lo], axis=1)


def _ln_kernel(x_ref, g_ref, b_ref, o_ref):
    o_ref[...] = _layer_norm(x_ref[...], g_ref[...], b_ref[...])


def _embed_ln(x2, g, b):
    t, d = x2.shape
    tm = 1024
    return pl.pallas_call(
        _ln_kernel,
        grid=(t // tm,),
        in_specs=[pl.BlockSpec((tm, d), lambda i: (i, 0)),
                  pl.BlockSpec((1, d), lambda i: (0, 0)),
                  pl.BlockSpec((1, d), lambda i: (0, 0))],
        out_specs=pl.BlockSpec((tm, d), lambda i: (i, 0)),
        out_shape=jax.ShapeDtypeStruct((t, d), F32),
        compiler_params=_cparams(("parallel",)),
        name="embed_ln",
    )(x2, g.reshape(1, d), b.reshape(1, d))


PROJ_TM = 1024
QKV_W = 6 * LANES
WCOL = 256


def _proj_qkv_kernel(x_ref, wq_ref, wk_ref, wv_ref, *rest, r, half, q_scale):
    if half:
        t_ref, o_ref, wb_ref, zs_ref = rest
    else:
        (o_ref, wb_ref, zs_ref), t_ref = rest, None
    first = (pl.program_id(0) == 0) & (pl.program_id(1) == 0)

    @pl.when(first)
    def _():
        for j, w_ref in enumerate((wq_ref, wk_ref, wv_ref)):
            wb_ref[:, j * WCOL:(j + 1) * WCOL] = w_ref[0].astype(BF16)

    tm = x_ref.shape[1]
    n = tm // r
    z = _dot(x_ref[0].astype(BF16), wb_ref[...])
    for c in range(QKV_W // LANES):
        blk = z[:, c * LANES:(c + 1) * LANES]
        if half and c < 4:
            blk = _rope(blk, t_ref, half)
        if c < 2:
            blk = blk * q_scale
        if r == 1:
            o_ref[0, 0, :, c * LANES:(c + 1) * LANES] = blk.astype(BF16)
        else:
            zs_ref[c] = blk
            for m in range(r):
                o_ref[0, m, :, c * LANES:(c + 1) * LANES] = zs_ref[
                    c, pl.ds(m, n, stride=r), :].astype(BF16)


def _proj_qkv(h3, w_in, l, col0, tables, r, half, q_scale):
    b, s, d = h3.shape
    tm = PROJ_TM
    wblk = col0 // WCOL
    wspec = lambda j: pl.BlockSpec((1, d, WCOL), lambda bi, i: (l, 0, wblk + j))
    in_specs = [pl.BlockSpec((1, tm, d), lambda bi, i: (bi, i, 0)), wspec(0), wspec(1), wspec(2)]
    args = [h3, w_in, w_in, w_in]
    if half:
        in_specs.append(pl.BlockSpec((3, tm, LANES), lambda bi, i: (0, i, 0)))
        args.append(tables)
    out = pl.pallas_call(
        functools.partial(_proj_qkv_kernel, r=r, half=half, q_scale=q_scale),
        grid=(b, s // tm),
        in_specs=in_specs,
        out_specs=pl.BlockSpec((1, r, tm // r, QKV_W), lambda bi, i: (bi, 0, i, 0)),
        out_shape=jax.ShapeDtypeStruct((b, r, s // r, QKV_W), BF16),
        scratch_shapes=[pltpu.VMEM((d, QKV_W), BF16), pltpu.VMEM((QKV_W // LANES, tm, LANES), F32)],
        compiler_params=_cparams(("arbitrary", "arbitrary")),
        name=f"proj_qkv_c{col0}",
    )(*args)
    return out.reshape(b, s, QKV_W)


def _proj_pool_kernel(x_ref, w_ref, o_ref, wb_ref):
    @pl.when(pl.program_id(0) == 0)
    def _():
        wb_ref[...] = w_ref[0].astype(BF16)

    o_ref[...] = _dot(x_ref[...].astype(BF16), wb_ref[...])


def _proj_pool(h2, w_in, l, col0):
    t, d = h2.shape
    tm = PROJ_TM
    return pl.pallas_call(
        _proj_pool_kernel,
        grid=(t // tm,),
        in_specs=[pl.BlockSpec((tm, d), lambda i: (i, 0)),
                  pl.BlockSpec((1, d, WCOL), lambda i: (l, 0, col0 // WCOL))],
        out_specs=pl.BlockSpec((tm, WCOL), lambda i: (i, 0)),
        out_shape=jax.ShapeDtypeStruct((t, WCOL), F32),
        scratch_shapes=[pltpu.VMEM((d, WCOL), BF16)],
        compiler_params=_cparams(("arbitrary",)),
        name="proj_pool",
    )(h2, w_in)


def _rope_tables(seq, head_w):
    rot = head_w // 4
    half = rot // 2
    inv_freq = jnp.exp(jnp.arange(half, dtype=F32) * (-2.0 * math.log(ROPE_THETA) / rot))
    ang = jnp.arange(seq, dtype=jnp.int32).astype(F32)[:, None] * inv_freq[None, :]
    cos, sin = jnp.cos(ang), jnp.sin(ang)
    zero = jnp.zeros((seq, head_w - rot), F32)
    zh = jnp.zeros((seq, half), F32)
    t0 = jnp.concatenate([cos, cos, jnp.ones((seq, head_w - rot), F32)], axis=1)
    t1 = jnp.concatenate([-sin, zh, zero], axis=1)
    t2 = jnp.concatenate([zh, sin, zero], axis=1)
    reps = LANES // head_w
    return jnp.stack([jnp.tile(t0, (1, reps)), jnp.tile(t1, (1, reps)), jnp.tile(t2, (1, reps))])


def _rope(x, t_ref, half):
    return (x * t_ref[0] + pltpu.roll(x, LANES - half, 1) * t_ref[1]
            + pltpu.roll(x, half, 1) * t_ref[2])


LOG2E = math.log2(math.e)
LN2 = math.log(2.0)


def _na_bias_table(rpb):
    kr, kc = NA_ROWS, NA_COLS
    n_heads = rpb.shape[0]
    col = np.arange(GRID_W)
    col_start = np.clip(col - kc // 2, 0, GRID_W - kc)
    in_win = (col[None, :] >= col_start[:, None]) & (col[None, :] < col_start[:, None] + kc)
    dc = np.clip(col[None, :] - col[:, None] + kc - 1, 0, 2 * kc - 2)
    case = np.arange(kr)
    dr = np.arange(kr)[None, :] - case[:, None] + NA_ROWS - 1
    sel_r = jnp.asarray((dr[..., None] == np.arange(2 * kr - 1)).astype(np.float32))
    sel_c = jnp.asarray(((dc[..., None] == np.arange(2 * kc - 1)) & in_win[..., None]).astype(np.float32))
    hi = lax.Precision.HIGHEST
    rp = rpb.astype(F32).reshape(n_heads // 2, 2, 2 * kr - 1, 2 * kc - 1) * LOG2E
    t1 = jnp.einsum('phij,aki->pahkj', rp, sel_r, precision=hi)
    bias = jnp.einsum('pahkj,qcj->pahqkc', t1, sel_c, precision=hi)
    bias = bias + jnp.asarray(np.where(in_win, 0.0, NEG).astype(np.float32))[:, None, :]
    return bias.reshape(n_heads // 2, kr, 2 * GRID_W, kr * GRID_W)


def _stack_heads(q, lane):
    zero = jnp.zeros_like(q)
    return jnp.concatenate([jnp.where(lane < HEAD_DIM, q, zero), jnp.where(lane < HEAD_DIM, zero, q)],
                           axis=0)


def _na_kernel(q_ref, k_ref, v_ref, bias_ref, o_ref, *, rows_per_step, n_rows):
    step = pl.program_id(2)
    lane = _lane((GRID_W, LANES))
    kwin = NA_ROWS * GRID_W
    for i in range(rows_per_step):
        r = step * rows_per_step + i
        start = jnp.clip(r - NA_ROWS // 2, 0, n_rows - NA_ROWS)
        case = r - start
        k0 = pl.multiple_of(start * GRID_W, GRID_W)
        q2 = _stack_heads(q_ref[0, i * GRID_W:(i + 1) * GRID_W, :], lane)
        sc = _dot_nt(q2, k_ref[0, pl.ds(k0, kwin), :]) + bias_ref[0, case]
        m = jnp.max(sc, axis=-1, keepdims=True)
        e = jnp.exp2(sc - m)
        den = jnp.sum(e, axis=-1, keepdims=True)
        pv = _dot(e.astype(BF16), v_ref[0, pl.ds(k0, kwin), :]) / den
        o = jnp.where(lane < HEAD_DIM, pv[0:GRID_W], pv[GRID_W:2 * GRID_W])
        o_ref[0, i * GRID_W:(i + 1) * GRID_W, :] = o.astype(BF16)


def _na_attention(qkv, bias):
    b, s, _ = qkv.shape
    n_rows = s // GRID_W
    rps = 16
    tq = rps * GRID_W
    return pl.pallas_call(
        functools.partial(_na_kernel, rows_per_step=rps, n_rows=n_rows),
        grid=(b, 2, n_rows // rps),
        in_specs=[pl.BlockSpec((1, tq, LANES), lambda bi, hp, i: (bi, i, hp)),
                  pl.BlockSpec((1, s, LANES), lambda bi, hp, i: (bi, 0, 2 + hp)),
                  pl.BlockSpec((1, s, LANES), lambda bi, hp, i: (bi, 0, 4 + hp)),
                  pl.BlockSpec((1, NA_ROWS, 2 * GRID_W, NA_ROWS * GRID_W),
                               lambda bi, hp, i: (hp, 0, 0, 0))],
        out_specs=pl.BlockSpec((1, tq, LANES), lambda bi, hp, i: (bi, i, hp)),
        out_shape=jax.ShapeDtypeStruct((b, s, 2 * LANES), BF16),
        compiler_params=_cparams(("parallel", "parallel", "arbitrary")),
        name="na_attn",
    )(qkv, qkv, qkv, bias)


def _diff_kernel(lam_ref, q_ref, k_ref, v_ref, g_ref, o_ref, *, lam_init):
    tq = q_ref.shape[1]
    lane = _lane((tq, LANES))
    dl = lam_ref[...]
    lam = (jnp.exp(jnp.sum(dl[0:1] * dl[1:2], axis=-1, keepdims=True))
           - jnp.exp(jnp.sum(dl[2:3] * dl[3:4], axis=-1, keepdims=True)) + lam_init)
    q = q_ref[0]
    zero = jnp.zeros_like(q)
    v = v_ref[0]
    v_lane = _lane(v.shape)
    one = jnp.ones_like(v)
    v_ext = (jnp.where(v_lane < HEAD_DIM, v, one), jnp.where(v_lane < HEAD_DIM, one, v))
    pv = []
    for lo in range(0, LANES, DIFF_QK_DIM):
        qm = jnp.where((lane >= lo) & (lane < lo + DIFF_QK_DIM), q, zero)
        sc = _dot_nt(qm, k_ref[0])
        e = jnp.exp2(sc - jnp.max(sc, axis=-1, keepdims=True))
        ev = _dot(e.astype(BF16), v_ext[lo // HEAD_DIM])
        pv.append(ev / pltpu.roll(ev, HEAD_DIM, 1))
    in_h0 = lane < HEAD_DIM
    o = jnp.where(in_h0, pv[0] - lam * pv[1], pv[2] - lam * pv[3])
    o2 = o * o
    ms0 = jnp.sum(jnp.where(in_h0, o2, 0.0), axis=-1, keepdims=True) / HEAD_DIM
    ms1 = jnp.sum(jnp.where(in_h0, 0.0, o2), axis=-1, keepdims=True) / HEAD_DIM
    ms = jnp.where(in_h0, ms0, ms1)
    o = o * lax.rsqrt(ms + LN_EPS) * g_ref[...] * (1.0 - lam_init)
    o_ref[0] = o.astype(BF16)


def _diff_attention(qkv, diff_lam, subln_g, lam_init):
    b, s, _ = qkv.shape
    tq = 512
    g2 = jnp.tile(subln_g.reshape(1, HEAD_DIM), (1, 2))
    return pl.pallas_call(
        functools.partial(_diff_kernel, lam_init=lam_init),
        grid=(b, 2, s // tq),
        in_specs=[pl.BlockSpec((4, DIFF_QK_DIM), lambda bi, hp, i: (0, 0)),
                  pl.BlockSpec((1, tq, LANES), lambda bi, hp, i: (bi, i, hp)),
                  pl.BlockSpec((1, s, LANES), lambda bi, hp, i: (bi, 0, 2 + hp)),
                  pl.BlockSpec((1, s, LANES), lambda bi, hp, i: (bi, 0, 4 + hp)),
                  pl.BlockSpec((1, LANES), lambda bi, hp, i: (0, 0))],
        out_specs=pl.BlockSpec((1, tq, LANES), lambda bi, hp, i: (bi, i, hp)),
        out_shape=jax.ShapeDtypeStruct((b, s, 2 * LANES), BF16),
        compiler_params=_cparams(("parallel", "parallel", "arbitrary")),
        name="diff_attn",
    )(diff_lam, qkv, qkv, qkv, g2)


POOL_PAD = 16
POOL_CHUNK = 512
POOL_HALO = 8


def _pool_kernel(u_ref, w_ref, sc_ref, o_ref, p_ref):
    s = u_ref.shape[1]
    width = u_ref.shape[2]
    p_ref[0:POOL_PAD, :] = jnp.zeros((POOL_PAD, width), F32)
    p_ref[POOL_PAD + s:POOL_PAD + s + POOL_PAD, :] = jnp.zeros((POOL_PAD, width), F32)
    p_ref[POOL_PAD:POOL_PAD + s, :] = u_ref[0]
    n = POOL_CHUNK + 2 * POOL_HALO
    lane = _lane((POOL_CHUNK, width))
    row = lax.broadcasted_iota(jnp.int32, (POOL_CHUNK, width), 0)
    w_of_lane = jnp.where(lane < POOL_GROUP, 2, jnp.where(lane < 2 * POOL_GROUP, 4,
                          jnp.where(lane < 3 * POOL_GROUP, 8, 16)))

    def body(ci, carry):
        c0 = pl.multiple_of(ci * POOL_CHUNK, POOL_CHUNK)
        x = p_ref[pl.ds(c0 + POOL_PAD - POOL_HALO, n), :]
        w2 = x + pltpu.roll(x, 1, 0)
        w4 = pltpu.roll(w2, 1, 0) + pltpu.roll(w2, n - 1, 0)
        w8 = pltpu.roll(w4, 2, 0) + pltpu.roll(w4, n - 2, 0)
        w16 = pltpu.roll(w8, 4, 0) + pltpu.roll(w8, n - 4, 0)
        u = x[POOL_HALO:POOL_HALO + POOL_CHUNK]
        wsum = jnp.where(lane < POOL_GROUP, w2[POOL_HALO:POOL_HALO + POOL_CHUNK],
                         jnp.where(lane < 2 * POOL_GROUP, w4[POOL_HALO:POOL_HALO + POOL_CHUNK],
                                   jnp.where(lane < 3 * POOL_GROUP, w8[POOL_HALO:POOL_HALO + POOL_CHUNK],
                                             w16[POOL_HALO:POOL_HALO + POOL_CHUNK])))
        t = row + c0
        half_w = w_of_lane // 2
        lo = jnp.maximum(t - half_w, 0)
        hi = jnp.minimum(t + w_of_lane - 1 - half_w, s - 1)
        cnt = (hi - lo + 1).astype(F32)
        dlt = wsum / cnt - u
        y = _dot(dlt.astype(BF16), w_ref[...]) * sc_ref[...]
        o_ref[0, pl.ds(c0, POOL_CHUNK), :] = y.astype(BF16)
        return carry

    lax.fori_loop(0, s // POOL_CHUNK, body, 0)


def _pool(u3, pool_w, pool_scale):
    b, s, width = u3.shape
    wbd = jax.scipy.linalg.block_diag(*[pool_w[g] for g in range(len(POOL_WINDOWS))]).astype(BF16)
    return pl.pallas_call(
        _pool_kernel,
        grid=(b,),
        in_specs=[pl.BlockSpec((1, s, width), lambda bi: (bi, 0, 0)),
                  pl.BlockSpec((width, width), lambda bi: (0, 0)),
                  pl.BlockSpec((1, width), lambda bi: (0, 0))],
        out_specs=pl.BlockSpec((1, s, width), lambda bi: (bi, 0, 0)),
        out_shape=jax.ShapeDtypeStruct((b, s, width), BF16),
        scratch_shapes=[pltpu.VMEM((s + 2 * POOL_PAD, width), F32)],
        compiler_params=_cparams(("parallel",)),
        name="pool",
    )(u3, wbd, pool_scale.reshape(1, width))


SWA_Q = 128
SWA_BAND = SWA_Q + 2 * DIL_HALF


def _swa_kernel(q_ref, k_ref, v_ref, o_ref, l_ref, *, length, chunk):
    c = pl.program_id(2)
    lane = _lane((SWA_Q, LANES))
    rel = (lax.broadcasted_iota(jnp.int32, (SWA_Q, SWA_BAND), 1)
           - lax.broadcasted_iota(jnp.int32, (SWA_Q, SWA_BAND), 0))
    for i in range(chunk // SWA_Q):
        r0 = c * chunk + i * SWA_Q
        l0 = r0 & (length - 1)
        lo = r0 - l0 + jnp.clip(l0 - DIL_HALF, 0, length - SWA_BAND)
        lo = pl.multiple_of(lo, DIL_HALF)
        q2 = _stack_heads(q_ref[0, i * SWA_Q:(i + 1) * SWA_Q, :], lane)
        d = rel + (lo - r0)
        valid = (d >= -DIL_HALF) & (d <= DIL_HALF)
        valid2 = jnp.concatenate([valid, valid], axis=0)
        sc = jnp.where(valid2, _dot_nt(q2, k_ref[0, pl.ds(lo, SWA_BAND), :]), NEG)
        m = jnp.max(sc, axis=-1, keepdims=True)
        e = jnp.exp2(sc - m)
        den = jnp.sum(e, axis=-1, keepdims=True)
        pv = _dot(e.astype(BF16), v_ref[0, pl.ds(lo, SWA_BAND), :]) / den
        lse2 = m + jnp.log2(den)
        o_ref[0, i * SWA_Q:(i + 1) * SWA_Q, :] = jnp.where(lane < HEAD_DIM, pv[0:SWA_Q], pv[SWA_Q:])
        l_ref[0, i * SWA_Q:(i + 1) * SWA_Q, :] = jnp.where(lane < HEAD_DIM, lse2[0:SWA_Q], lse2[SWA_Q:])


def _swa(qkv, length):
    b, s, _ = qkv.shape
    chunk = 1024
    shp = jax.ShapeDtypeStruct((b, s, 2 * LANES), F32)
    return pl.pallas_call(
        functools.partial(_swa_kernel, length=length, chunk=chunk),
        grid=(b, 2, s // chunk),
        in_specs=[pl.BlockSpec((1, chunk, LANES), lambda bi, hp, c: (bi, c, hp)),
                  pl.BlockSpec((1, s, LANES), lambda bi, hp, c: (bi, 0, 2 + hp)),
                  pl.BlockSpec((1, s, LANES), lambda bi, hp, c: (bi, 0, 4 + hp))],
        out_specs=[pl.BlockSpec((1, chunk, LANES), lambda bi, hp, c: (bi, c, hp)),
                   pl.BlockSpec((1, chunk, LANES), lambda bi, hp, c: (bi, c, hp))],
        out_shape=[shp, shp],
        compiler_params=_cparams(("parallel", "parallel", "arbitrary")),
        name=f"swa_l{length}",
    )(qkv, qkv, qkv)


def _dil_merge_kernel(o0_ref, l0_ref, o1_ref, l1_ref, o2_ref, l2_ref, y_ref, so1, sl1, so2, sl2):
    tm = y_ref.shape[1]
    r1 = DIL_PATTERNS[1][1]
    r2 = DIL_PATTERNS[2][1]
    for m in range(r1):
        so1[pl.ds(m, tm // r1, stride=r1), :] = o1_ref[0, m]
        sl1[pl.ds(m, tm // r1, stride=r1), :] = l1_ref[0, m]
    for m in range(r2):
        so2[pl.ds(m, tm // r2, stride=r2), :] = o2_ref[0, m]
        sl2[pl.ds(m, tm // r2, stride=r2), :] = l2_ref[0, m]
    l0, l1, l2 = l0_ref[0], sl1[...], sl2[...]
    mx = jnp.maximum(jnp.maximum(l0, l1), l2)
    e0, e1, e2 = jnp.exp2(l0 - mx), jnp.exp2(l1 - mx), jnp.exp2(l2 - mx)
    den = e0 + e1 + e2
    y = (e0 / den) * o0_ref[0] + (e1 / den) * so1[...] + (e2 / den) * so2[...]
    y_ref[0] = y.astype(BF16)


def _dil_merge(outs):
    (o0, l0), (o1, l1), (o2, l2) = outs
    b, s, w = o0.shape
    tm = 1024
    r1 = DIL_PATTERNS[1][1]
    r2 = DIL_PATTERNS[2][1]
    v1 = lambda a: a.reshape(b, r1, s // r1, w)
    v2 = lambda a: a.reshape(b, r2, s // r2, w)
    nat = pl.BlockSpec((1, tm, LANES), lambda bi, i, hp: (bi, i, hp))
    g1 = pl.BlockSpec((1, r1, tm // r1, LANES), lambda bi, i, hp: (bi, 0, i, hp))
    g2 = pl.BlockSpec((1, r2, tm // r2, LANES), lambda bi, i, hp: (bi, 0, i, hp))
    return pl.pallas_call(
        _dil_merge_kernel,
        grid=(b, s // tm, w // LANES),
        in_specs=[nat, nat, g1, g1, g2, g2],
        out_specs=nat,
        out_shape=jax.ShapeDtypeStruct((b, s, w), BF16),
        scratch_shapes=[pltpu.VMEM((tm, LANES), F32)] * 4,
        compiler_params=_cparams(("parallel", "parallel", "parallel")),
        name="dil_merge",
    )(o0, l0, v1(o1), v1(l1), v2(o2), v2(l2))


def _merge_kernel(ya_ref, yb_ref, yc_ref, yd_ref, wg_ref, bg_ref, wb_ref, wo_ref, h_ref, g_ref, b_ref,
                  o_ref, op_ref, hb_ref, acc_ref, *, alpha):
    n = pl.program_id(1)

    @pl.when(n == 0)
    def _():
        hb_ref[...] = h_ref[...].astype(BF16)

    zg = _dot(hb_ref[...], wg_ref[...]) + bg_ref[0]
    gate = 1.0 / (1.0 + jnp.exp(-zg))
    for k, y_ref in enumerate((ya_ref, yb_ref, yc_ref, yd_ref)):
        @pl.when(n == k)
        def _(y_ref=y_ref, k=k):
            term = gate * _dot(y_ref[...], wb_ref[0])
            if k == 0:
                acc_ref[...] = term
            else:
                acc_ref[...] += term

    @pl.when(n == pl.num_programs(1) - 1)
    def _():
        mix = _dot(acc_ref[...].astype(BF16), wo_ref[...])
        h1 = _layer_norm(alpha * h_ref[...] + mix, g_ref[...], b_ref[...])
        o_ref[...] = h1
        op_ref[...] = _pack_bf16_pairs(h1)


def _merge(ys, wg, b_gate, wb, wo, h2, g, bb, alpha):
    t, d = h2.shape
    bw = ys[0].shape[1]
    nb = len(ys)
    tm = PROJ_TM
    yspec = pl.BlockSpec((tm, bw), lambda i, n: (i, 0))
    vec = pl.BlockSpec((1, d), lambda i, n: (0, 0))
    return pl.pallas_call(
        functools.partial(_merge_kernel, alpha=alpha),
        grid=(t // tm, nb),
        in_specs=[yspec, yspec, yspec, yspec,
                  pl.BlockSpec((d, d), lambda i, n: (0, n)),
                  pl.BlockSpec((1, 1, d), lambda i, n: (n, 0, 0)),
                  pl.BlockSpec((1, bw, d), lambda i, n: (n, 0, 0)),
                  pl.BlockSpec((d, d), lambda i, n: (0, 0)),
                  pl.BlockSpec((tm, d), lambda i, n: (i, 0)),
                  vec, vec],
        out_specs=[pl.BlockSpec((tm, d), lambda i, n: (i, 0)),
                   pl.BlockSpec((tm, d // 2), lambda i, n: (i, 0))],
        out_shape=[jax.ShapeDtypeStruct((t, d), F32), jax.ShapeDtypeStruct((t, d // 2), jnp.int32)],
        scratch_shapes=[pltpu.VMEM((tm, d), BF16), pltpu.VMEM((tm, d), F32)],
        compiler_params=_cparams(("parallel", "arbitrary")),
        name="merge",
    )(*ys, wg, b_gate.reshape(nb, 1, d), wb, wo, h2, g.reshape(1, d), bb.reshape(1, d))


ROUTER_TM = 256


def _router_kernel(h_ref, w_ref, b_ref, o_ref, cnt_ref, carry):
    @pl.when(pl.program_id(0) == 0)
    def _():
        carry[...] = jnp.zeros_like(carry)

    tm = h_ref.shape[0]
    logits = jnp.dot(h_ref[...], w_ref[...], preferred_element_type=F32,
                     precision=lax.Precision.HIGHEST) + b_ref[...]
    lane = _lane((tm, LANES))
    big = jnp.int32(1 << 20)
    is_g = lane < N_GROUPS
    gl = jnp.where(is_g, logits, -jnp.inf)
    gmax = jnp.max(gl, axis=-1, keepdims=True)
    gsel = jnp.min(jnp.where(is_g & (gl == gmax), lane, big), axis=-1, keepdims=True)
    pg = 1.0 / jnp.sum(jnp.exp(gl - gmax), axis=-1, keepdims=True)
    e_lo = N_GROUPS + gsel * EXPERTS_PER_GROUP
    in_grp = (lane >= e_lo) & (lane < e_lo + EXPERTS_PER_GROUP)
    el = jnp.where(in_grp, logits, -jnp.inf)
    v1 = jnp.max(el, axis=-1, keepdims=True)
    i1 = jnp.min(jnp.where(in_grp & (el == v1), lane, big), axis=-1, keepdims=True)
    el2 = jnp.where(lane == i1, -jnp.inf, el)
    v2 = jnp.max(el2, axis=-1, keepdims=True)
    i2 = jnp.min(jnp.where(in_grp & (lane != i1) & (el2 == v2), lane, big), axis=-1, keepdims=True)
    t2 = jnp.exp(v2 - v1)
    g1 = pg / (1.0 + t2)
    g2 = pg * t2 / (1.0 + t2)
    oh1 = (lane == i1)
    oh2 = (lane == i2)
    tri = (lax.broadcasted_iota(jnp.int32, (tm, tm), 1)
           < lax.broadcasted_iota(jnp.int32, (tm, tm), 0)).astype(BF16)
    oh1b = oh1.astype(BF16)
    oh2b = oh2.astype(BF16)
    c0 = carry[...]
    c1 = c0 + jnp.sum(oh1b.astype(F32), axis=0, keepdims=True)
    rank1 = jnp.sum(jnp.where(oh1, _dot(tri, oh1b) + c0, 0.0), axis=-1, keepdims=True)
    rank2 = jnp.sum(jnp.where(oh2, _dot(tri, oh2b) + c1, 0.0), axis=-1, keepdims=True)
    c2 = c1 + jnp.sum(oh2b.astype(F32), axis=0, keepdims=True)
    carry[...] = c2
    cnt_ref[...] = c2
    e1 = (i1 - N_GROUPS).astype(F32)
    e2 = (i2 - N_GROUPS).astype(F32)
    out = jnp.where(lane == 0, e1, jnp.where(lane == 1, e2, jnp.where(lane == 2, g1, jnp.where(
        lane == 3, g2, jnp.where(lane == 4, rank1, jnp.where(lane == 5, rank2, 0.0))))))
    o_ref[...] = out


def _router(h2, rg_w, rg_b, re_w, re_b):
    t, d = h2.shape
    pad = LANES - N_GROUPS - N_EXPERTS
    w = jnp.concatenate([rg_w, re_w, jnp.zeros((d, pad), F32)], axis=1)
    bias = jnp.concatenate([rg_b, re_b, jnp.zeros((pad,), F32)]).reshape(1, LANES)
    tm = ROUTER_TM
    return pl.pallas_call(
        _router_kernel,
        grid=(t // tm,),
        in_specs=[pl.BlockSpec((tm, d), lambda i: (i, 0)),
                  pl.BlockSpec((d, LANES), lambda i: (0, 0)),
                  pl.BlockSpec((1, LANES), lambda i: (0, 0))],
        out_specs=[pl.BlockSpec((tm, LANES), lambda i: (i, 0)),
                   pl.BlockSpec((1, LANES), lambda i: (0, 0))],
        out_shape=[jax.ShapeDtypeStruct((t, LANES), F32), jax.ShapeDtypeStruct((1, LANES), F32)],
        scratch_shapes=[pltpu.VMEM((1, LANES), F32)],
        compiler_params=_cparams(("arbitrary",)),
        name="router",
    )(h2, w, bias)


def _moe_kernel(be_ref, nv_ref, x_ref, wg_ref, wu_ref, wd_ref, o_ref, wgb, wub, wdb):
    j = pl.program_id(0)
    n_valid = nv_ref[j]
    new_expert = (j == 0) | (be_ref[j] != be_ref[jnp.maximum(j - 1, 0)])

    @pl.when(new_expert)
    def _():
        wgb[...] = wg_ref[0].astype(BF16)
        wub[...] = wu_ref[0].astype(BF16)
        wdb[...] = wd_ref[0].astype(BF16)

    @pl.when(n_valid > 0)
    def _():
        row = lax.broadcasted_iota(jnp.int32, x_ref.shape, 0)
        x = _unpack_bf16_pairs(jnp.where(row < n_valid, x_ref[...], 0)).astype(BF16)
        g = _dot(x, wgb[...])
        u = _dot(x, wub[...])
        hmid = (g * (1.0 / (1.0 + jnp.exp(-g)))) * u
        o_ref[...] = _pack_bf16_pairs(_dot(hmid.astype(BF16), wdb[...]))

    @pl.when(n_valid <= 0)
    def _():
        o_ref[...] = jnp.zeros_like(o_ref)


def _moe_experts(xs, block_e, n_valid, wg, wu, wd, l):
    cap, dp = xs.shape
    d = 2 * dp
    de = wg.shape[3]
    n_blocks = cap // MOE_ROWS
    grid_spec = pltpu.PrefetchScalarGridSpec(
        num_scalar_prefetch=2,
        grid=(n_blocks,),
        in_specs=[pl.BlockSpec((MOE_ROWS, dp), lambda j, be, nv: (j, 0)),
                  pl.BlockSpec((None, 1, d, de), lambda j, be, nv: (l, be[j], 0, 0)),
                  pl.BlockSpec((None, 1, d, de), lambda j, be, nv: (l, be[j], 0, 0)),
                  pl.BlockSpec((None, 1, de, d), lambda j, be, nv: (l, be[j], 0, 0))],
        out_specs=pl.BlockSpec((MOE_ROWS, dp), lambda j, be, nv: (j, 0)),
        scratch_shapes=[pltpu.VMEM((d, de), BF16), pltpu.VMEM((d, de), BF16), pltpu.VMEM((de, d), BF16)],
    )
    return pl.pallas_call(
        _moe_kernel,
        grid_spec=grid_spec,
        out_shape=jax.ShapeDtypeStruct((cap, dp), jnp.int32),
        compiler_params=_cparams(("arbitrary",)),
        name="moe_experts",
    )(block_e, n_valid, xs, wg, wu, wd)


SC_CORES = 2
SC_SUBCORES = 16
SC_WORKERS = SC_CORES * SC_SUBCORES
SC_CHUNK = 64


def _sc_mesh():
    return plsc.VectorSubcoreMesh(core_axis_name="c", subcore_axis_name="s",
                                  num_cores=SC_CORES, num_subcores=SC_SUBCORES)


def _sc_scatter_rows(src, idx0, idx1, n_out):
    t, d = src.shape
    per_w = t // SC_WORKERS
    n_chunks = per_w // SC_CHUNK
    idx_shape = (SC_WORKERS, n_chunks, SC_CHUNK)

    @functools.partial(
        pl.kernel, mesh=_sc_mesh(),
        out_type=jax.ShapeDtypeStruct((n_out, d), src.dtype),
        scratch_types=[pltpu.VMEM((n_chunks, SC_CHUNK), jnp.int32),
                       pltpu.VMEM((n_chunks, SC_CHUNK), jnp.int32),
                       pltpu.VMEM((2, SC_CHUNK, d), src.dtype),
                       pltpu.SemaphoreType.DMA((2,)), pltpu.SemaphoreType.DMA((2,))],
        name="sc_dispatch",
    )
    def k(src_hbm, i0_hbm, i1_hbm, out_hbm, i0_v, i1_v, rows_v, sem_in, sem_out):
        wid = lax.axis_index("s") * SC_CORES + lax.axis_index("c")
        base = wid * per_w
        pltpu.sync_copy(i0_hbm.at[wid], i0_v)
        pltpu.sync_copy(i1_hbm.at[wid], i1_v)

        def load(ci):
            s = ci % 2
            return pltpu.make_async_copy(src_hbm.at[pl.ds(base + ci * SC_CHUNK, SC_CHUNK)],
                                         rows_v.at[s], sem_in.at[s])

        def scatter(ci, idx_v):
            s = ci % 2
            return pltpu.make_async_copy(rows_v.at[s], out_hbm.at[idx_v.at[ci]], sem_out.at[s])

        load(0).start()
        for ci in range(n_chunks):
            load(ci).wait()
            if ci >= 1:
                scatter(ci - 1, i0_v).wait()
                scatter(ci - 1, i1_v).wait()
            if ci + 1 < n_chunks:
                load(ci + 1).start()
            scatter(ci, i0_v).start()
            scatter(ci, i1_v).start()
        scatter(n_chunks - 1, i0_v).wait()
        scatter(n_chunks - 1, i1_v).wait()

    return k(src, idx0.reshape(idx_shape), idx1.reshape(idx_shape))


def _sc_gather_rows(table, idx):
    n = idx.shape[0]
    d = table.shape[1]
    per_w = n // SC_WORKERS
    n_chunks = per_w // SC_CHUNK

    @functools.partial(
        pl.kernel, mesh=_sc_mesh(),
        out_type=jax.ShapeDtypeStruct((n, d), table.dtype),
        scratch_types=[pltpu.VMEM((n_chunks, SC_CHUNK), jnp.int32),
                       pltpu.VMEM((2, SC_CHUNK, d), table.dtype),
                       pltpu.SemaphoreType.DMA((2,)), pltpu.SemaphoreType.DMA((2,))],
        name="sc_collect",
    )
    def k(table_hbm, idx_hbm, out_hbm, idx_v, rows_v, sem_in, sem_out):
        wid = lax.axis_index("s") * SC_CORES + lax.axis_index("c")
        base = wid * per_w
        pltpu.sync_copy(idx_hbm.at[wid], idx_v)

        def gather(ci):
            s = ci % 2
            return pltpu.make_async_copy(table_hbm.at[idx_v.at[ci]], rows_v.at[s], sem_in.at[s])

        def store(ci):
            s = ci % 2
            return pltpu.make_async_copy(rows_v.at[s], out_hbm.at[pl.ds(base + ci * SC_CHUNK, SC_CHUNK)],
                                         sem_out.at[s])

        gather(0).start()
        for ci in range(n_chunks):
            gather(ci).wait()
            if ci >= 1:
                store(ci - 1).wait()
            if ci + 1 < n_chunks:
                gather(ci + 1).start()
            store(ci).start()
        store(n_chunks - 1).wait()

    return k(table, idx.reshape(SC_WORKERS, n_chunks, SC_CHUNK))


def _combine_kernel(h_ref, r_ref, y1_ref, y2_ref, g_ref, b_ref, o_ref, *, alpha):
    g1 = r_ref[:, 2:3]
    g2 = r_ref[:, 3:4]
    ffn = g1 * _unpack_bf16_pairs(y1_ref[...]) + g2 * _unpack_bf16_pairs(y2_ref[...])
    o_ref[...] = _layer_norm(alpha * h_ref[...] + ffn, g_ref[...], b_ref[...])


def _combine(h2, routing, y12, g, bb, alpha):
    t, d = h2.shape
    tm = 512
    nt = t // tm
    row = pl.BlockSpec((tm, d), lambda i: (i, 0))
    vec = pl.BlockSpec((1, d), lambda i: (0, 0))
    return pl.pallas_call(
        functools.partial(_combine_kernel, alpha=alpha),
        grid=(nt,),
        in_specs=[row, pl.BlockSpec((tm, LANES), lambda i: (i, 0)),
                  pl.BlockSpec((tm, d // 2), lambda i: (i, 0)),
                  pl.BlockSpec((tm, d // 2), lambda i: (nt + i, 0)), vec, vec],
        out_specs=row,
        out_shape=jax.ShapeDtypeStruct((t, d), F32),
        compiler_params=_cparams(("parallel",)),
        name="combine_ln",
    )(h2, routing, y12, y12, g.reshape(1, d), bb.reshape(1, d))


def _moe(h2, h2_packed, rg_w, rg_b, re_w, re_b, wg, wu, wd, l, g, bb, alpha):
    t, d = h2.shape
    routing, counts = _router(h2, rg_w, rg_b, re_w, re_b)
    eid = routing[:, 0:2].astype(jnp.int32)
    rank = routing[:, 4:6].astype(jnp.int32)
    cnt = counts[0, N_GROUPS:N_GROUPS + N_EXPERTS].astype(jnp.int32)
    padded = (cnt + MOE_ROWS - 1) // MOE_ROWS * MOE_ROWS
    pad_end = jnp.cumsum(padded)
    pad_start = pad_end - padded
    experts = jnp.arange(N_EXPERTS, dtype=jnp.int32)
    dest = jnp.sum(jnp.where(eid[..., None] == experts, pad_start, 0), axis=-1) + rank
    n_slots = t * TOP_K
    n_blocks = (n_slots + N_EXPERTS * (MOE_ROWS - 1) + MOE_ROWS - 1) // MOE_ROWS
    cap = n_blocks * MOE_ROWS
    blk_row = jnp.arange(n_blocks, dtype=jnp.int32) * MOE_ROWS
    block_e = jnp.minimum(jnp.sum((pad_end[None, :] <= blk_row[:, None]).astype(jnp.int32), axis=1),
                          N_EXPERTS - 1)
    is_e = block_e[:, None] == experts
    blk_cnt = jnp.sum(jnp.where(is_e, cnt, 0), axis=1)
    blk_start = jnp.sum(jnp.where(is_e, pad_start, 0), axis=1)
    n_valid = jnp.clip(blk_cnt - (blk_row - blk_start), 0, MOE_ROWS).astype(jnp.int32)
    xs = _sc_scatter_rows(h2_packed, dest[:, 0], dest[:, 1], cap)
    ys = _moe_experts(xs, block_e, n_valid, wg, wu, wd, l)
    y12 = _sc_gather_rows(ys, jnp.concatenate([dest[:, 0], dest[:, 1]]))
    return _combine(h2, routing, y12, g, bb, alpha)


def _mixing_layer(h, b, s, l, p, lam_init):
    t = b * s
    alpha = (2 * p['w_in'].shape[0]) ** 0.25
    d = h.shape[1]
    w_in = p['w_in']
    tab_diff = _rope_tables(s, DIFF_QK_DIM)
    tab_dil = _rope_tables(s, HEAD_DIM)
    h3 = h.reshape(b, s, d)
    qkv_na = _proj_qkv(h3, w_in, l, COL_NA, None, 1, 0, HEAD_DIM ** -0.5 * LOG2E)
    y_a = _na_attention(qkv_na, _na_bias_table(p['na_rpb'][l]))
    qkv_diff = _proj_qkv(h3, w_in, l, COL_DIFF, tab_diff, 1, DIFF_QK_DIM // 8, DIFF_QK_DIM ** -0.5 * LOG2E)
    y_b = _diff_attention(qkv_diff, p['diff_lam'][l], p['diff_subln_g'][l], lam_init)
    u = _proj_pool(h, w_in, l, COL_POOL)
    y_c = _pool(u.reshape(b, s, -1), p['pool_w'][l], p['pool_scale'][l])
    dil = []
    for pi, (_, r) in enumerate(DIL_PATTERNS):
        qkv = _proj_qkv(h3, w_in, l, COL_DIL + QKV_W * pi, tab_dil, r, HEAD_DIM // 8,
                        HEAD_DIM ** -0.5 * LOG2E)
        dil.append(_swa(qkv, s // r))
    y_d = _dil_merge(dil)
    ys = [a.reshape(t, -1) for a in (y_a, y_b, y_c, y_d)]
    wg = lax.slice_in_dim(w_in[l], COL_GATE, COL_GATE + len(ys) * d, axis=1).astype(BF16)
    return _merge(ys, wg, p['b_gate'][l], p['w_branch'][l].astype(BF16), p['w_out'][l].astype(BF16), h,
                  p['ln1_g'][l], p['ln1_b'][l], alpha)


def kernel(x, emb_ln_g, emb_ln_b, w_in, b_gate, na_rpb, diff_lam, diff_subln_g, pool_w, pool_scale,
           w_branch, w_out, ln1_g, ln1_b, router_group_w, router_group_b, router_expert_w,
           router_expert_b, expert_w_gate, expert_w_up, expert_w_down, ln2_g, ln2_b):
    b, s, d = x.shape
    depth = w_in.shape[0]
    alpha = (2 * depth) ** 0.25
    p = dict(w_in=w_in, b_gate=b_gate, na_rpb=na_rpb, diff_lam=diff_lam, diff_subln_g=diff_subln_g,
             pool_w=pool_w, pool_scale=pool_scale, w_branch=w_branch, w_out=w_out, ln1_g=ln1_g, ln1_b=ln1_b)
    h = _embed_ln(x.reshape(b * s, d), emb_ln_g, emb_ln_b)
    for l in range(depth):
        lam_init = 0.8 - 0.6 * math.exp(-0.3 * l)
        h, h_packed = _mixing_layer(h, b, s, l, p, lam_init)
        h = _moe(h, h_packed, router_group_w[l], router_group_b[l], router_expert_w[l], router_expert_b[l],
                 expert_w_gate, expert_w_up, expert_w_down, l, ln2_g[l], ln2_b[l], alpha)
    return h.reshape(b, s, d)
```

```python
import functools
import math

import jax
import jax.numpy as jnp
import numpy as np
from jax import lax
from jax.experimental import pallas as pl
from jax.experimental.pallas import tpu as pltpu
from jax.experimental.pallas import tpu_sc as plsc

F32 = jnp.float32
BF16 = jnp.bfloat16

LANES = 128
GRID_W = 64
HEAD_DIM = 64
ROPE_THETA = 500000.0
LN_EPS = 1e-5
NA_ROWS = 8
NA_COLS = 16
DIFF_QK_DIM = 32
POOL_WINDOWS = (2, 4, 8, 16)
POOL_GROUP = 64
DIL_PATTERNS = ((128, 1), (512, 4), (2048, 16))
DIL_HALF = 64
N_GROUPS = 4
EXPERTS_PER_GROUP = 8
N_EXPERTS = N_GROUPS * EXPERTS_PER_GROUP
TOP_K = 2
MOE_ROWS = 256
NEG = -1e30

COL_NA = 0
COL_DIFF = 768
COL_POOL = 1536
COL_DIL = 1792
COL_GATE = 4096

VMEM_LIMIT = 56 * 1024 * 1024


def _cparams(sem, vmem=VMEM_LIMIT, flags=None):
    return pltpu.CompilerParams(dimension_semantics=sem, vmem_limit_bytes=vmem, flags=flags)


def _layer_norm(x, g, b):
    mu = jnp.mean(x, axis=-1, keepdims=True)
    xc = x - mu
    var = jnp.mean(xc * xc, axis=-1, keepdims=True)
    return xc * lax.rsqrt(var + LN_EPS) * g + b


def _dot(a, b):
    return jnp.dot(a, b, preferred_element_type=F32)


def _dot_nt(a, b):
    return lax.dot_general(a, b, (((1,), (1,)), ((), ())), preferred_element_type=F32)


def _lane(shape):
    return lax.broadcasted_iota(jnp.int32, shape, len(shape) - 1)


HI16 = -65536


def _pack_bf16_pairs(x):
    w = x.shape[1] // 2
    hi = lax.bitcast_convert_type(x[:, :w].astype(BF16).astype(F32), jnp.int32)
    lo = lax.bitcast_convert_type(x[:, w:].astype(BF16).astype(F32), jnp.int32)
    return (hi & HI16) | lax.shift_right_logical(lo, 16)


def _unpack_bf16_pairs(p):
    hi = lax.bitcast_convert_type(p & HI16, F32)
    lo = lax.bitcast_convert_type(lax.shift_left(p, 16), F32)
    return jnp.concatenate([hi, lo], axis=1)


def _ln_kernel(x_ref, g_ref, b_ref, o_ref):
    o_ref[...] = _layer_norm(x_ref[...], g_ref[...], b_ref[...])


def _embed_ln(x2, g, b):
    t, d = x2.shape
    tm = 1024
    return pl.pallas_call(
        _ln_kernel,
        grid=(t // tm,),
        in_specs=[pl.BlockSpec((tm, d), lambda i: (i, 0)),
                  pl.BlockSpec((1, d), lambda i: (0, 0)),
                  pl.BlockSpec((1, d), lambda i: (0, 0))],
        out_specs=pl.BlockSpec((tm, d), lambda i: (i, 0)),
        out_shape=jax.ShapeDtypeStruct((t, d), F32),
        compiler_params=_cparams(("parallel",)),
        name="embed_ln",
    )(x2, g.reshape(1, d), b.reshape(1, d))


PROJ_TM = 1024
QKV_W = 6 * LANES
WCOL = 256


def _proj_qkv_kernel(x_ref, wq_ref, wk_ref, wv_ref, *rest, r, half, q_scale):
    if half:
        t_ref, o_ref, wb_ref, zs_ref = rest
    else:
        (o_ref, wb_ref, zs_ref), t_ref = rest, None
    first = (pl.program_id(0) == 0) & (pl.program_id(1) == 0)

    @pl.when(first)
    def _():
        for j, w_ref in enumerate((wq_ref, wk_ref, wv_ref)):
            wb_ref[:, j * WCOL:(j + 1) * WCOL] = w_ref[0].astype(BF16)

    tm = x_ref.shape[1]
    n = tm // r
    z = _dot(x_ref[0].astype(BF16), wb_ref[...])
    for c in range(QKV_W // LANES):
        blk = z[:, c * LANES:(c + 1) * LANES]
        if half and c < 4:
            blk = _rope(blk, t_ref, half)
        if c < 2:
            blk = blk * q_scale
        if r == 1:
            o_ref[0, 0, :, c * LANES:(c + 1) * LANES] = blk.astype(BF16)
        else:
            zs_ref[c] = blk
            for m in range(r):
                o_ref[0, m, :, c * LANES:(c + 1) * LANES] = zs_ref[
                    c, pl.ds(m, n, stride=r), :].astype(BF16)


def _proj_qkv(h3, w_in, l, col0, tables, r, half, q_scale):
    b, s, d = h3.shape
    tm = PROJ_TM
    wblk = col0 // WCOL
    wspec = lambda j: pl.BlockSpec((1, d, WCOL), lambda bi, i: (l, 0, wblk + j))
    in_specs = [pl.BlockSpec((1, tm, d), lambda bi, i: (bi, i, 0)), wspec(0), wspec(1), wspec(2)]
    args = [h3, w_in, w_in, w_in]
    if half:
        in_specs.append(pl.BlockSpec((3, tm, LANES), lambda bi, i: (0, i, 0)))
        args.append(tables)
    out = pl.pallas_call(
        functools.partial(_proj_qkv_kernel, r=r, half=half, q_scale=q_scale),
        grid=(b, s // tm),
        in_specs=in_specs,
        out_specs=pl.BlockSpec((1, r, tm // r, QKV_W), lambda bi, i: (bi, 0, i, 0)),
        out_shape=jax.ShapeDtypeStruct((b, r, s // r, QKV_W), BF16),
        scratch_shapes=[pltpu.VMEM((d, QKV_W), BF16), pltpu.VMEM((QKV_W // LANES, tm, LANES), F32)],
        compiler_params=_cparams(("arbitrary", "arbitrary")),
        name=f"proj_qkv_c{col0}",
    )(*args)
    return out.reshape(b, s, QKV_W)


def _proj_pool_kernel(x_ref, w_ref, o_ref, wb_ref):
    @pl.when(pl.program_id(0) == 0)
    def _():
        wb_ref[...] = w_ref[0].astype(BF16)

    o_ref[...] = _dot(x_ref[...].astype(BF16), wb_ref[...])


def _proj_pool(h2, w_in, l, col0):
    t, d = h2.shape
    tm = PROJ_TM
    return pl.pallas_call(
        _proj_pool_kernel,
        grid=(t // tm,),
        in_specs=[pl.BlockSpec((tm, d), lambda i: (i, 0)),
                  pl.BlockSpec((1, d, WCOL), lambda i: (l, 0, col0 // WCOL))],
        out_specs=pl.BlockSpec((tm, WCOL), lambda i: (i, 0)),
        out_shape=jax.ShapeDtypeStruct((t, WCOL), F32),
        scratch_shapes=[pltpu.VMEM((d, WCOL), BF16)],
        compiler_params=_cparams(("arbitrary",)),
        name="proj_pool",
    )(h2, w_in)


def _rope_tables(seq, head_w):
    rot = head_w // 4
    half = rot // 2
    inv_freq = jnp.exp(jnp.arange(half, dtype=F32) * (-2.0 * math.log(ROPE_THETA) / rot))
    ang = jnp.arange(seq, dtype=jnp.int32).astype(F32)[:, None] * inv_freq[None, :]
    cos, sin = jnp.cos(ang), jnp.sin(ang)
    zero = jnp.zeros((seq, head_w - rot), F32)
    zh = jnp.zeros((seq, half), F32)
    t0 = jnp.concatenate([cos, cos, jnp.ones((seq, head_w - rot), F32)], axis=1)
    t1 = jnp.concatenate([-sin, zh, zero], axis=1)
    t2 = jnp.concatenate([zh, sin, zero], axis=1)
    reps = LANES // head_w
    return jnp.stack([jnp.tile(t0, (1, reps)), jnp.tile(t1, (1, reps)), jnp.tile(t2, (1, reps))])


def _rope(x, t_ref, half):
    return (x * t_ref[0] + pltpu.roll(x, LANES - half, 1) * t_ref[1]
            + pltpu.roll(x, half, 1) * t_ref[2])


LOG2E = math.log2(math.e)
LN2 = math.log(2.0)


def _na_bias_table(rpb):
    kr, kc = NA_ROWS, NA_COLS
    n_heads = rpb.shape[0]
    col = np.arange(GRID_W)
    col_start = np.clip(col - kc // 2, 0, GRID_W - kc)
    in_win = (col[None, :] >= col_start[:, None]) & (col[None, :] < col_start[:, None] + kc)
    dc = np.clip(col[None, :] - col[:, None] + kc - 1, 0, 2 * kc - 2)
    case = np.arange(kr)
    dr = np.arange(kr)[None, :] - case[:, None] + NA_ROWS - 1
    sel_r = jnp.asarray((dr[..., None] == np.arange(2 * kr - 1)).astype(np.float32))
    sel_c = jnp.asarray(((dc[..., None] == np.arange(2 * kc - 1)) & in_win[..., None]).astype(np.float32))
    hi = lax.Precision.HIGHEST
    rp = rpb.astype(F32).reshape(n_heads // 2, 2, 2 * kr - 1, 2 * kc - 1) * LOG2E
    t1 = jnp.einsum('phij,aki->pahkj', rp, sel_r, precision=hi)
    bias = jnp.einsum('pahkj,qcj->pahqkc', t1, sel_c, precision=hi)
    bias = bias + jnp.asarray(np.where(in_win, 0.0, NEG).astype(np.float32))[:, None, :]
    return bias.reshape(n_heads // 2, kr, 2 * GRID_W, kr * GRID_W)


def _stack_heads(q, lane):
    zero = jnp.zeros_like(q)
    return jnp.concatenate([jnp.where(lane < HEAD_DIM, q, zero), jnp.where(lane < HEAD_DIM, zero, q)],
                           axis=0)


def _na_kernel(q_ref, k_ref, v_ref, bias_ref, o_ref, *, rows_per_step, n_rows):
    step = pl.program_id(2)
    lane = _lane((GRID_W, LANES))
    kwin = NA_ROWS * GRID_W
    for i in range(rows_per_step):
        r = step * rows_per_step + i
        start = jnp.clip(r - NA_ROWS // 2, 0, n_rows - NA_ROWS)
        case = r - start
        k0 = pl.multiple_of(start * GRID_W, GRID_W)
        q2 = _stack_heads(q_ref[0, i * GRID_W:(i + 1) * GRID_W, :], lane)
        sc = _dot_nt(q2, k_ref[0, pl.ds(k0, kwin), :]) + bias_ref[0, case]
        m = jnp.max(sc, axis=-1, keepdims=True)
        e = jnp.exp2(sc - m)
        den = jnp.sum(e, axis=-1, keepdims=True)
        pv = _dot(e.astype(BF16), v_ref[0, pl.ds(k0, kwin), :]) / den
        o = jnp.where(lane < HEAD_DIM, pv[0:GRID_W], pv[GRID_W:2 * GRID_W])
        o_ref[0, i * GRID_W:(i + 1) * GRID_W, :] = o.astype(BF16)


def _na_attention(qkv, bias):
    b, s, _ = qkv.shape
    n_rows = s // GRID_W
    rps = 16
    tq = rps * GRID_W
    return pl.pallas_call(
        functools.partial(_na_kernel, rows_per_step=rps, n_rows=n_rows),
        grid=(b, 2, n_rows // rps),
        in_specs=[pl.BlockSpec((1, tq, LANES), lambda bi, hp, i: (bi, i, hp)),
                  pl.BlockSpec((1, s, LANES), lambda bi, hp, i: (bi, 0, 2 + hp)),
                  pl.BlockSpec((1, s, LANES), lambda bi, hp, i: (bi, 0, 4 + hp)),
                  pl.BlockSpec((1, NA_ROWS, 2 * GRID_W, NA_ROWS * GRID_W),
                               lambda bi, hp, i: (hp, 0, 0, 0))],
        out_specs=pl.BlockSpec((1, tq, LANES), lambda bi, hp, i: (bi, i, hp)),
        out_shape=jax.ShapeDtypeStruct((b, s, 2 * LANES), BF16),
        compiler_params=_cparams(("parallel", "parallel", "arbitrary")),
        name="na_attn",
    )(qkv, qkv, qkv, bias)


def _diff_kernel(lam_ref, q_ref, k_ref, v_ref, g_ref, o_ref, *, lam_init, rows):
    tq = q_ref.shape[1]
    lane = _lane((rows, LANES))
    dl = lam_ref[...]
    lam = (jnp.exp(jnp.sum(dl[0:1] * dl[1:2], axis=-1, keepdims=True))
           - jnp.exp(jnp.sum(dl[2:3] * dl[3:4], axis=-1, keepdims=True)) + lam_init)
    v = v_ref[0]
    v_lane = _lane(v.shape)
    one = jnp.ones_like(v)
    v_ext = (jnp.where(v_lane < HEAD_DIM, v, one), jnp.where(v_lane < HEAD_DIM, one, v))
    in_h0 = lane < HEAD_DIM
    for r0 in range(0, tq, rows):
        q = q_ref[0, r0:r0 + rows, :]
        zero = jnp.zeros_like(q)
        pv = []
        for lo in range(0, LANES, DIFF_QK_DIM):
            qm = jnp.where((lane >= lo) & (lane < lo + DIFF_QK_DIM), q, zero)
            sc = _dot_nt(qm, k_ref[0])
            e = jnp.exp2(sc - jnp.max(sc, axis=-1, keepdims=True))
            ev = _dot(e.astype(BF16), v_ext[lo // HEAD_DIM])
            pv.append(ev / pltpu.roll(ev, HEAD_DIM, 1))
        o = jnp.where(in_h0, pv[0] - lam * pv[1], pv[2] - lam * pv[3])
        o2 = o * o
        ms0 = jnp.sum(jnp.where(in_h0, o2, 0.0), axis=-1, keepdims=True) / HEAD_DIM
        ms1 = jnp.sum(jnp.where(in_h0, 0.0, o2), axis=-1, keepdims=True) / HEAD_DIM
        ms = jnp.where(in_h0, ms0, ms1)
        o = o * lax.rsqrt(ms + LN_EPS) * g_ref[...] * (1.0 - lam_init)
        o_ref[0, r0:r0 + rows, :] = o.astype(BF16)


def _diff_attention(qkv, diff_lam, subln_g, lam_init):
    b, s, _ = qkv.shape
    tq = 2048
    g2 = jnp.tile(subln_g.reshape(1, HEAD_DIM), (1, 2))
    return pl.pallas_call(
        functools.partial(_diff_kernel, lam_init=lam_init, rows=512),
        grid=(b, 2, s // tq),
        in_specs=[pl.BlockSpec((4, DIFF_QK_DIM), lambda bi, hp, i: (0, 0)),
                  pl.BlockSpec((1, tq, LANES), lambda bi, hp, i: (bi, i, hp)),
                  pl.BlockSpec((1, s, LANES), lambda bi, hp, i: (bi, 0, 2 + hp)),
                  pl.BlockSpec((1, s, LANES), lambda bi, hp, i: (bi, 0, 4 + hp)),
                  pl.BlockSpec((1, LANES), lambda bi, hp, i: (0, 0))],
        out_specs=pl.BlockSpec((1, tq, LANES), lambda bi, hp, i: (bi, i, hp)),
        out_shape=jax.ShapeDtypeStruct((b, s, 2 * LANES), BF16),
        compiler_params=_cparams(("parallel", "parallel", "arbitrary")),
        name="diff_attn",
    )(diff_lam, qkv, qkv, qkv, g2)


POOL_PAD = 16
POOL_CHUNK = 512
POOL_HALO = 8


def _pool_kernel(u_ref, w_ref, sc_ref, o_ref, p_ref):
    s = u_ref.shape[1]
    width = u_ref.shape[2]
    p_ref[0:POOL_PAD, :] = jnp.zeros((POOL_PAD, width), F32)
    p_ref[POOL_PAD + s:POOL_PAD + s + POOL_PAD, :] = jnp.zeros((POOL_PAD, width), F32)
    p_ref[POOL_PAD:POOL_PAD + s, :] = u_ref[0]
    n = POOL_CHUNK + 2 * POOL_HALO
    lane = _lane((POOL_CHUNK, width))
    row = lax.broadcasted_iota(jnp.int32, (POOL_CHUNK, width), 0)
    w_of_lane = jnp.where(lane < POOL_GROUP, 2, jnp.where(lane < 2 * POOL_GROUP, 4,
                          jnp.where(lane < 3 * POOL_GROUP, 8, 16)))

    def body(ci, carry):
        c0 = pl.multiple_of(ci * POOL_CHUNK, POOL_CHUNK)
        x = p_ref[pl.ds(c0 + POOL_PAD - POOL_HALO, n), :]
        w2 = x + pltpu.roll(x, 1, 0)
        w4 = pltpu.roll(w2, 1, 0) + pltpu.roll(w2, n - 1, 0)
        w8 = pltpu.roll(w4, 2, 0) + pltpu.roll(w4, n - 2, 0)
        w16 = pltpu.roll(w8, 4, 0) + pltpu.roll(w8, n - 4, 0)
        u = x[POOL_HALO:POOL_HALO + POOL_CHUNK]
        wsum = jnp.where(lane < POOL_GROUP, w2[POOL_HALO:POOL_HALO + POOL_CHUNK],
                         jnp.where(lane < 2 * POOL_GROUP, w4[POOL_HALO:POOL_HALO + POOL_CHUNK],
                                   jnp.where(lane < 3 * POOL_GROUP, w8[POOL_HALO:POOL_HALO + POOL_CHUNK],
                                             w16[POOL_HALO:POOL_HALO + POOL_CHUNK])))
        t = row + c0
        half_w = w_of_lane // 2
        lo = jnp.maximum(t - half_w, 0)
        hi = jnp.minimum(t + w_of_lane - 1 - half_w, s - 1)
        cnt = (hi - lo + 1).astype(F32)
        dlt = wsum / cnt - u
        y = _dot(dlt.astype(BF16), w_ref[...]) * sc_ref[...]
        o_ref[0, pl.ds(c0, POOL_CHUNK), :] = y.astype(BF16)
        return carry

    lax.fori_loop(0, s // POOL_CHUNK, body, 0)


def _pool(u3, pool_w, pool_scale):
    b, s, width = u3.shape
    wbd = jax.scipy.linalg.block_diag(*[pool_w[g] for g in range(len(POOL_WINDOWS))]).astype(BF16)
    return pl.pallas_call(
        _pool_kernel,
        grid=(b,),
        in_specs=[pl.BlockSpec((1, s, width), lambda bi: (bi, 0, 0)),
                  pl.BlockSpec((width, width), lambda bi: (0, 0)),
                  pl.BlockSpec((1, width), lambda bi: (0, 0))],
        out_specs=pl.BlockSpec((1, s, width), lambda bi: (bi, 0, 0)),
        out_shape=jax.ShapeDtypeStruct((b, s, width), BF16),
        scratch_shapes=[pltpu.VMEM((s + 2 * POOL_PAD, width), F32)],
        compiler_params=_cparams(("parallel",)),
        name="pool",
    )(u3, wbd, pool_scale.reshape(1, width))


SWA_Q = 128
SWA_BAND = SWA_Q + 2 * DIL_HALF


def _swa_kernel(q_ref, k_ref, v_ref, o_ref, l_ref, *, length, chunk):
    c = pl.program_id(2)
    lane = _lane((SWA_Q, LANES))
    rel = (lax.broadcasted_iota(jnp.int32, (SWA_Q, SWA_BAND), 1)
           - lax.broadcasted_iota(jnp.int32, (SWA_Q, SWA_BAND), 0))
    for i in range(chunk // SWA_Q):
        r0 = c * chunk + i * SWA_Q
        l0 = r0 & (length - 1)
        lo = r0 - l0 + jnp.clip(l0 - DIL_HALF, 0, length - SWA_BAND)
        lo = pl.multiple_of(lo, DIL_HALF)
        q2 = _stack_heads(q_ref[0, i * SWA_Q:(i + 1) * SWA_Q, :], lane)
        d = rel + (lo - r0)
        valid = (d >= -DIL_HALF) & (d <= DIL_HALF)
        valid2 = jnp.concatenate([valid, valid], axis=0)
        sc = jnp.where(valid2, _dot_nt(q2, k_ref[0, pl.ds(lo, SWA_BAND), :]), NEG)
        m = jnp.max(sc, axis=-1, keepdims=True)
        e = jnp.exp2(sc - m)
        den = jnp.sum(e, axis=-1, keepdims=True)
        pv = _dot(e.astype(BF16), v_ref[0, pl.ds(lo, SWA_BAND), :]) / den
        lse2 = m + jnp.log2(den)
        o_ref[0, i * SWA_Q:(i + 1) * SWA_Q, :] = jnp.where(lane < HEAD_DIM, pv[0:SWA_Q], pv[SWA_Q:])
        l_ref[0, i * SWA_Q:(i + 1) * SWA_Q, :] = jnp.where(lane < HEAD_DIM, lse2[0:SWA_Q], lse2[SWA_Q:])


def _swa(qkv, length):
    b, s, _ = qkv.shape
    chunk = 1024
    shp = jax.ShapeDtypeStruct((b, s, 2 * LANES), F32)
    return pl.pallas_call(
        functools.partial(_swa_kernel, length=length, chunk=chunk),
        grid=(b, 2, s // chunk),
        in_specs=[pl.BlockSpec((1, chunk, LANES), lambda bi, hp, c: (bi, c, hp)),
                  pl.BlockSpec((1, s, LANES), lambda bi, hp, c: (bi, 0, 2 + hp)),
                  pl.BlockSpec((1, s, LANES), lambda bi, hp, c: (bi, 0, 4 + hp))],
        out_specs=[pl.BlockSpec((1, chunk, LANES), lambda bi, hp, c: (bi, c, hp)),
                   pl.BlockSpec((1, chunk, LANES), lambda bi, hp, c: (bi, c, hp))],
        out_shape=[shp, shp],
        compiler_params=_cparams(("parallel", "parallel", "arbitrary")),
        name=f"swa_l{length}",
    )(qkv, qkv, qkv)


def _dil_merge_kernel(o0_ref, l0_ref, o1_ref, l1_ref, o2_ref, l2_ref, y_ref, so1, sl1, so2, sl2):
    tm = y_ref.shape[1]
    r1 = DIL_PATTERNS[1][1]
    r2 = DIL_PATTERNS[2][1]
    for m in range(r1):
        so1[pl.ds(m, tm // r1, stride=r1), :] = o1_ref[0, m]
        sl1[pl.ds(m, tm // r1, stride=r1), :] = l1_ref[0, m]
    for m in range(r2):
        so2[pl.ds(m, tm // r2, stride=r2), :] = o2_ref[0, m]
        sl2[pl.ds(m, tm // r2, stride=r2), :] = l2_ref[0, m]
    l0, l1, l2 = l0_ref[0], sl1[...], sl2[...]
    mx = jnp.maximum(jnp.maximum(l0, l1), l2)
    e0, e1, e2 = jnp.exp2(l0 - mx), jnp.exp2(l1 - mx), jnp.exp2(l2 - mx)
    den = e0 + e1 + e2
    y = (e0 / den) * o0_ref[0] + (e1 / den) * so1[...] + (e2 / den) * so2[...]
    y_ref[0] = y.astype(BF16)


def _dil_merge(outs):
    (o0, l0), (o1, l1), (o2, l2) = outs
    b, s, w = o0.shape
    tm = 1024
    r1 = DIL_PATTERNS[1][1]
    r2 = DIL_PATTERNS[2][1]
    v1 = lambda a: a.reshape(b, r1, s // r1, w)
    v2 = lambda a: a.reshape(b, r2, s // r2, w)
    nat = pl.BlockSpec((1, tm, LANES), lambda bi, i, hp: (bi, i, hp))
    g1 = pl.BlockSpec((1, r1, tm // r1, LANES), lambda bi, i, hp: (bi, 0, i, hp))
    g2 = pl.BlockSpec((1, r2, tm // r2, LANES), lambda bi, i, hp: (bi, 0, i, hp))
    return pl.pallas_call(
        _dil_merge_kernel,
        grid=(b, s // tm, w // LANES),
        in_specs=[nat, nat, g1, g1, g2, g2],
        out_specs=nat,
        out_shape=jax.ShapeDtypeStruct((b, s, w), BF16),
        scratch_shapes=[pltpu.VMEM((tm, LANES), F32)] * 4,
        compiler_params=_cparams(("parallel", "parallel", "parallel")),
        name="dil_merge",
    )(o0, l0, v1(o1), v1(l1), v2(o2), v2(l2))


def _merge_kernel(ya_ref, yb_ref, yc_ref, yd_ref, wg_ref, bg_ref, wb_ref, wo_ref, h_ref, g_ref, b_ref,
                  o_ref, op_ref, hb_ref, acc_ref, *, alpha):
    n = pl.program_id(1)

    @pl.when(n == 0)
    def _():
        hb_ref[...] = h_ref[...].astype(BF16)

    zg = _dot(hb_ref[...], wg_ref[...]) + bg_ref[0]
    gate = 1.0 / (1.0 + jnp.exp(-zg))
    for k, y_ref in enumerate((ya_ref, yb_ref, yc_ref, yd_ref)):
        @pl.when(n == k)
        def _(y_ref=y_ref, k=k):
            term = gate * _dot(y_ref[...], wb_ref[0])
            if k == 0:
                acc_ref[...] = term
            else:
                acc_ref[...] += term

    @pl.when(n == pl.num_programs(1) - 1)
    def _():
        mix = _dot(acc_ref[...].astype(BF16), wo_ref[...])
        h1 = _layer_norm(alpha * h_ref[...] + mix, g_ref[...], b_ref[...])
        o_ref[...] = h1
        op_ref[...] = _pack_bf16_pairs(h1)


def _merge(ys, wg, b_gate, wb, wo, h2, g, bb, alpha):
    t, d = h2.shape
    bw = ys[0].shape[1]
    nb = len(ys)
    tm = PROJ_TM
    yspec = pl.BlockSpec((tm, bw), lambda i, n: (i, 0))
    vec = pl.BlockSpec((1, d), lambda i, n: (0, 0))
    return pl.pallas_call(
        functools.partial(_merge_kernel, alpha=alpha),
        grid=(t // tm, nb),
        in_specs=[yspec, yspec, yspec, yspec,
                  pl.BlockSpec((d, d), lambda i, n: (0, n)),
                  pl.BlockSpec((1, 1, d), lambda i, n: (n, 0, 0)),
                  pl.BlockSpec((1, bw, d), lambda i, n: (n, 0, 0)),
                  pl.BlockSpec((d, d), lambda i, n: (0, 0)),
                  pl.BlockSpec((tm, d), lambda i, n: (i, 0)),
                  vec, vec],
        out_specs=[pl.BlockSpec((tm, d), lambda i, n: (i, 0)),
                   pl.BlockSpec((tm, d // 2), lambda i, n: (i, 0))],
        out_shape=[jax.ShapeDtypeStruct((t, d), F32), jax.ShapeDtypeStruct((t, d // 2), jnp.int32)],
        scratch_shapes=[pltpu.VMEM((tm, d), BF16), pltpu.VMEM((tm, d), F32)],
        compiler_params=_cparams(("parallel", "arbitrary")),
        name="merge",
    )(*ys, wg, b_gate.reshape(nb, 1, d), wb, wo, h2, g.reshape(1, d), bb.reshape(1, d))


ROUTER_TM = 512


def _router_kernel(h_ref, w_ref, b_ref, o_ref, cnt_ref, carry, before):
    tm = h_ref.shape[0]

    @pl.when(pl.program_id(0) == 0)
    def _():
        carry[...] = jnp.zeros_like(carry)
        before[...] = (lax.broadcasted_iota(jnp.int32, (tm, tm), 0)
                       < lax.broadcasted_iota(jnp.int32, (tm, tm), 1)).astype(BF16)

    logits = lax.dot_general(w_ref[...], h_ref[...], (((1,), (1,)), ((), ())),
                             preferred_element_type=F32,
                             precision=lax.Precision.HIGHEST) + b_ref[...]
    row = lax.broadcasted_iota(jnp.int32, (LANES, tm), 0)
    big = jnp.int32(1 << 20)
    is_g = row < N_GROUPS
    gl = jnp.where(is_g, logits, -jnp.inf)
    gmax = jnp.max(gl, axis=0, keepdims=True)
    gsel = jnp.min(jnp.where(is_g & (gl == gmax), row, big), axis=0, keepdims=True)
    pg = 1.0 / jnp.sum(jnp.exp(gl - gmax), axis=0, keepdims=True)
    e_lo = N_GROUPS + gsel * EXPERTS_PER_GROUP
    in_grp = (row >= e_lo) & (row < e_lo + EXPERTS_PER_GROUP)
    el = jnp.where(in_grp, logits, -jnp.inf)
    v1 = jnp.max(el, axis=0, keepdims=True)
    i1 = jnp.min(jnp.where(in_grp & (el == v1), row, big), axis=0, keepdims=True)
    el2 = jnp.where(row == i1, -jnp.inf, el)
    v2 = jnp.max(el2, axis=0, keepdims=True)
    i2 = jnp.min(jnp.where(in_grp & (row != i1) & (el2 == v2), row, big), axis=0, keepdims=True)
    t2 = jnp.exp(v2 - v1)
    g1 = pg / (1.0 + t2)
    g2 = pg * t2 / (1.0 + t2)
    oh1 = (row == i1)
    oh2 = (row == i2)
    oh1b = oh1.astype(BF16)
    oh2b = oh2.astype(BF16)
    c0 = carry[...]
    c1 = c0 + jnp.sum(oh1b.astype(F32), axis=1, keepdims=True)
    rank1 = jnp.sum(jnp.where(oh1, _dot(oh1b, before[...]) + c0, 0.0), axis=0, keepdims=True)
    rank2 = jnp.sum(jnp.where(oh2, _dot(oh2b, before[...]) + c1, 0.0), axis=0, keepdims=True)
    c2 = c1 + jnp.sum(oh2b.astype(F32), axis=1, keepdims=True)
    carry[...] = c2
    cnt_ref[...] = c2
    e1 = (i1 - N_GROUPS).astype(F32)
    e2 = (i2 - N_GROUPS).astype(F32)
    out_t = jnp.where(row == 0, e1, jnp.where(row == 1, e2, jnp.where(row == 2, g1, jnp.where(
        row == 3, g2, jnp.where(row == 4, rank1, jnp.where(row == 5, rank2, 0.0))))))
    o_ref[...] = out_t.T


def _router(h2, rg_w, rg_b, re_w, re_b):
    t, d = h2.shape
    pad = LANES - N_GROUPS - N_EXPERTS
    w_t = jnp.concatenate([rg_w.T, re_w.T, jnp.zeros((pad, d), F32)], axis=0)
    bias = jnp.concatenate([rg_b, re_b, jnp.zeros((pad,), F32)]).reshape(LANES, 1)
    tm = ROUTER_TM
    return pl.pallas_call(
        _router_kernel,
        grid=(t // tm,),
        in_specs=[pl.BlockSpec((tm, d), lambda i: (i, 0)),
                  pl.BlockSpec((LANES, d), lambda i: (0, 0)),
                  pl.BlockSpec((LANES, 1), lambda i: (0, 0))],
        out_specs=[pl.BlockSpec((tm, LANES), lambda i: (i, 0)),
                   pl.BlockSpec((LANES, 1), lambda i: (0, 0))],
        out_shape=[jax.ShapeDtypeStruct((t, LANES), F32), jax.ShapeDtypeStruct((LANES, 1), F32)],
        scratch_shapes=[pltpu.VMEM((LANES, 1), F32), pltpu.VMEM((tm, tm), BF16)],
        compiler_params=_cparams(("arbitrary",)),
        name="router",
    )(h2, w_t, bias)


def _moe_kernel(be_ref, nv_ref, x_ref, wg_ref, wu_ref, wd_ref, o_ref, wgb, wub, wdb):
    j = pl.program_id(0)
    n_valid = nv_ref[j]
    new_expert = (j == 0) | (be_ref[j] != be_ref[jnp.maximum(j - 1, 0)])

    @pl.when(new_expert)
    def _():
        wgb[...] = wg_ref[0].astype(BF16)
        wub[...] = wu_ref[0].astype(BF16)
        wdb[...] = wd_ref[0].astype(BF16)

    @pl.when(n_valid > 0)
    def _():
        row = lax.broadcasted_iota(jnp.int32, x_ref.shape, 0)
        x = _unpack_bf16_pairs(jnp.where(row < n_valid, x_ref[...], 0)).astype(BF16)
        g = _dot(x, wgb[...])
        u = _dot(x, wub[...])
        hmid = (g * (1.0 / (1.0 + jnp.exp(-g)))) * u
        o_ref[...] = _pack_bf16_pairs(_dot(hmid.astype(BF16), wdb[...]))

    @pl.when(n_valid <= 0)
    def _():
        o_ref[...] = jnp.zeros_like(o_ref)


def _moe_experts(xs, block_e, n_valid, wg, wu, wd, l):
    cap, dp = xs.shape
    d = 2 * dp
    de = wg.shape[3]
    n_blocks = cap // MOE_ROWS
    grid_spec = pltpu.PrefetchScalarGridSpec(
        num_scalar_prefetch=2,
        grid=(n_blocks,),
        in_specs=[pl.BlockSpec((MOE_ROWS, dp), lambda j, be, nv: (j, 0)),
                  pl.BlockSpec((None, 1, d, de), lambda j, be, nv: (l, be[j], 0, 0)),
                  pl.BlockSpec((None, 1, d, de), lambda j, be, nv: (l, be[j], 0, 0)),
                  pl.BlockSpec((None, 1, de, d), lambda j, be, nv: (l, be[j], 0, 0))],
        out_specs=pl.BlockSpec((MOE_ROWS, dp), lambda j, be, nv: (j, 0)),
        scratch_shapes=[pltpu.VMEM((d, de), BF16), pltpu.VMEM((d, de), BF16), pltpu.VMEM((de, d), BF16)],
    )
    return pl.pallas_call(
        _moe_kernel,
        grid_spec=grid_spec,
        out_shape=jax.ShapeDtypeStruct((cap, dp), jnp.int32),
        compiler_params=_cparams(("arbitrary",)),
        name="moe_experts",
    )(block_e, n_valid, xs, wg, wu, wd)


SC_CORES = 2
SC_SUBCORES = 16
SC_WORKERS = SC_CORES * SC_SUBCORES
SC_CHUNK = 64


def _sc_mesh():
    return plsc.VectorSubcoreMesh(core_axis_name="c", subcore_axis_name="s",
                                  num_cores=SC_CORES, num_subcores=SC_SUBCORES)


def _sc_scatter_rows(src, idx0, idx1, n_out):
    t, d = src.shape
    per_w = t // SC_WORKERS
    n_chunks = per_w // SC_CHUNK
    idx_shape = (SC_WORKERS, n_chunks, SC_CHUNK)

    @functools.partial(
        pl.kernel, mesh=_sc_mesh(),
        out_type=jax.ShapeDtypeStruct((n_out, d), src.dtype),
        scratch_types=[pltpu.VMEM((n_chunks, SC_CHUNK), jnp.int32),
                       pltpu.VMEM((n_chunks, SC_CHUNK), jnp.int32),
                       pltpu.VMEM((2, SC_CHUNK, d), src.dtype),
                       pltpu.SemaphoreType.DMA((2,)), pltpu.SemaphoreType.DMA((2,))],
        name="sc_dispatch",
    )
    def k(src_hbm, i0_hbm, i1_hbm, out_hbm, i0_v, i1_v, rows_v, sem_in, sem_out):
        wid = lax.axis_index("s") * SC_CORES + lax.axis_index("c")
        base = wid * per_w
        pltpu.sync_copy(i0_hbm.at[wid], i0_v)
        pltpu.sync_copy(i1_hbm.at[wid], i1_v)

        def load(ci):
            s = ci % 2
            return pltpu.make_async_copy(src_hbm.at[pl.ds(base + ci * SC_CHUNK, SC_CHUNK)],
                                         rows_v.at[s], sem_in.at[s])

        def scatter(ci, idx_v):
            s = ci % 2
            return pltpu.make_async_copy(rows_v.at[s], out_hbm.at[idx_v.at[ci]], sem_out.at[s])

        load(0).start()
        for ci in range(n_chunks):
            load(ci).wait()
            if ci >= 1:
                scatter(ci - 1, i0_v).wait()
                scatter(ci - 1, i1_v).wait()
            if ci + 1 < n_chunks:
                load(ci + 1).start()
            scatter(ci, i0_v).start()
            scatter(ci, i1_v).start()
        scatter(n_chunks - 1, i0_v).wait()
        scatter(n_chunks - 1, i1_v).wait()

    return k(src, idx0.reshape(idx_shape), idx1.reshape(idx_shape))


def _sc_gather_rows(table, idx):
    n = idx.shape[0]
    d = table.shape[1]
    per_w = n // SC_WORKERS
    n_chunks = per_w // SC_CHUNK

    @functools.partial(
        pl.kernel, mesh=_sc_mesh(),
        out_type=jax.ShapeDtypeStruct((n, d), table.dtype),
        scratch_types=[pltpu.VMEM((n_chunks, SC_CHUNK), jnp.int32),
                       pltpu.VMEM((2, SC_CHUNK, d), table.dtype),
                       pltpu.SemaphoreType.DMA((2,)), pltpu.SemaphoreType.DMA((2,))],
        name="sc_collect",
    )
    def k(table_hbm, idx_hbm, out_hbm, idx_v, rows_v, sem_in, sem_out):
        wid = lax.axis_index("s") * SC_CORES + lax.axis_index("c")
        base = wid * per_w
        pltpu.sync_copy(idx_hbm.at[wid], idx_v)

        def gather(ci):
            s = ci % 2
            return pltpu.make_async_copy(table_hbm.at[idx_v.at[ci]], rows_v.at[s], sem_in.at[s])

        def store(ci):
            s = ci % 2
            return pltpu.make_async_copy(rows_v.at[s], out_hbm.at[pl.ds(base + ci * SC_CHUNK, SC_CHUNK)],
                                         sem_out.at[s])

        gather(0).start()
        for ci in range(n_chunks):
            gather(ci).wait()
            if ci >= 1:
                store(ci - 1).wait()
            if ci + 1 < n_chunks:
                gather(ci + 1).start()
            store(ci).start()
        store(n_chunks - 1).wait()

    return k(table, idx.reshape(SC_WORKERS, n_chunks, SC_CHUNK))


def _combine_kernel(h_ref, r_ref, y1_ref, y2_ref, g_ref, b_ref, o_ref, *, alpha):
    g1 = r_ref[:, 2:3]
    g2 = r_ref[:, 3:4]
    ffn = g1 * _unpack_bf16_pairs(y1_ref[...]) + g2 * _unpack_bf16_pairs(y2_ref[...])
    o_ref[...] = _layer_norm(alpha * h_ref[...] + ffn, g_ref[...], b_ref[...])


def _combine(h2, routing, y12, g, bb, alpha):
    t, d = h2.shape
    tm = 512
    nt = t // tm
    row = pl.BlockSpec((tm, d), lambda i: (i, 0))
    vec = pl.BlockSpec((1, d), lambda i: (0, 0))
    return pl.pallas_call(
        functools.partial(_combine_kernel, alpha=alpha),
        grid=(nt,),
        in_specs=[row, pl.BlockSpec((tm, LANES), lambda i: (i, 0)),
                  pl.BlockSpec((tm, d // 2), lambda i: (i, 0)),
                  pl.BlockSpec((tm, d // 2), lambda i: (nt + i, 0)), vec, vec],
        out_specs=row,
        out_shape=jax.ShapeDtypeStruct((t, d), F32),
        compiler_params=_cparams(("parallel",)),
        name="combine_ln",
    )(h2, routing, y12, y12, g.reshape(1, d), bb.reshape(1, d))


def _moe(h2, h2_packed, rg_w, rg_b, re_w, re_b, wg, wu, wd, l, g, bb, alpha):
    t, d = h2.shape
    routing, counts = _router(h2, rg_w, rg_b, re_w, re_b)
    eid = routing[:, 0:2].astype(jnp.int32)
    rank = routing[:, 4:6].astype(jnp.int32)
    cnt = counts[N_GROUPS:N_GROUPS + N_EXPERTS, 0].astype(jnp.int32)
    padded = (cnt + MOE_ROWS - 1) // MOE_ROWS * MOE_ROWS
    pad_end = jnp.cumsum(padded)
    pad_start = pad_end - padded
    experts = jnp.arange(N_EXPERTS, dtype=jnp.int32)
    dest = jnp.sum(jnp.where(eid[..., None] == experts, pad_start, 0), axis=-1) + rank
    n_slots = t * TOP_K
    n_blocks = (n_slots + N_EXPERTS * (MOE_ROWS - 1) + MOE_ROWS - 1) // MOE_ROWS
    cap = n_blocks * MOE_ROWS
    blk_row = jnp.arange(n_blocks, dtype=jnp.int32) * MOE_ROWS
    block_e = jnp.minimum(jnp.sum((pad_end[None, :] <= blk_row[:, None]).astype(jnp.int32), axis=1),
                          N_EXPERTS - 1)
    is_e = block_e[:, None] == experts
    blk_cnt = jnp.sum(jnp.where(is_e, cnt, 0), axis=1)
    blk_start = jnp.sum(jnp.where(is_e, pad_start, 0), axis=1)
    n_valid = jnp.clip(blk_cnt - (blk_row - blk_start), 0, MOE_ROWS).astype(jnp.int32)
    xs = _sc_scatter_rows(h2_packed, dest[:, 0], dest[:, 1], cap)
    ys = _moe_experts(xs, block_e, n_valid, wg, wu, wd, l)
    y12 = _sc_gather_rows(ys, jnp.concatenate([dest[:, 0], dest[:, 1]]))
    return _combine(h2, routing, y12, g, bb, alpha)


def _mixing_layer(h, b, s, l, p, lam_init):
    t = b * s
    alpha = (2 * p['w_in'].shape[0]) ** 0.25
    d = h.shape[1]
    w_in = p['w_in']
    tab_diff = _rope_tables(s, DIFF_QK_DIM)
    tab_dil = _rope_tables(s, HEAD_DIM)
    h3 = h.reshape(b, s, d)
    qkv_na = _proj_qkv(h3, w_in, l, COL_NA, None, 1, 0, HEAD_DIM ** -0.5 * LOG2E)
    y_a = _na_attention(qkv_na, _na_bias_table(p['na_rpb'][l]))
    qkv_diff = _proj_qkv(h3, w_in, l, COL_DIFF, tab_diff, 1, DIFF_QK_DIM // 8, DIFF_QK_DIM ** -0.5 * LOG2E)
    y_b = _diff_attention(qkv_diff, p['diff_lam'][l], p['diff_subln_g'][l], lam_init)
    u = _proj_pool(h, w_in, l, COL_POOL)
    y_c = _pool(u.reshape(b, s, -1), p['pool_w'][l], p['pool_scale'][l])
    dil = []
    for pi, (_, r) in enumerate(DIL_PATTERNS):
        qkv = _proj_qkv(h3, w_in, l, COL_DIL + QKV_W * pi, tab_dil, r, HEAD_DIM // 8,
                        HEAD_DIM ** -0.5 * LOG2E)
        dil.append(_swa(qkv, s // r))
    y_d = _dil_merge(dil)
    ys = [a.reshape(t, -1) for a in (y_a, y_b, y_c, y_d)]
    wg = lax.slice_in_dim(w_in[l], COL_GATE, COL_GATE + len(ys) * d, axis=1).astype(BF16)
    return _merge(ys, wg, p['b_gate'][l], p['w_branch'][l].astype(BF16), p['w_out'][l].astype(BF16), h,
                  p['ln1_g'][l], p['ln1_b'][l], alpha)


def kernel(x, emb_ln_g, emb_ln_b, w_in, b_gate, na_rpb, diff_lam, diff_subln_g, pool_w, pool_scale,
           w_branch, w_out, ln1_g, ln1_b, router_group_w, router_group_b, router_expert_w,
           router_expert_b, expert_w_gate, expert_w_up, expert_w_down, ln2_g, ln2_b):
    b, s, d = x.shape
    depth = w_in.shape[0]
    alpha = (2 * depth) ** 0.25
    p = dict(w_in=w_in, b_gate=b_gate, na_rpb=na_rpb, diff_lam=diff_lam, diff_subln_g=diff_subln_g,
             pool_w=pool_w, pool_scale=pool_scale, w_branch=w_branch, w_out=w_out, ln1_g=ln1_g, ln1_b=ln1_b)
    h = _embed_ln(x.reshape(b * s, d), emb_ln_g, emb_ln_b)
    for l in range(depth):
        lam_init = 0.8 - 0.6 * math.exp(-0.3 * l)
        h, h_packed = _mixing_layer(h, b, s, l, p, lam_init)
        h = _moe(h, h_packed, router_group_w[l], router_group_b[l], router_expert_w[l], router_expert_b[l],
                 expert_w_gate, expert_w_up, expert_w_down, l, ln2_g[l], ln2_b[l], alpha)
    return h.reshape(b, s, d)
```

```python
import functools
import math

import jax
import jax.numpy as jnp
import numpy as np
from jax import lax
from jax.experimental import pallas as pl
from jax.experimental.pallas import tpu as pltpu
from jax.experimental.pallas import tpu_sc as plsc

F32 = jnp.float32
BF16 = jnp.bfloat16

LANES = 128
GRID_W = 64
HEAD_DIM = 64
ROPE_THETA = 500000.0
LN_EPS = 1e-5
NA_ROWS = 8
NA_COLS = 16
DIFF_QK_DIM = 32
POOL_WINDOWS = (2, 4, 8, 16)
POOL_GROUP = 64
DIL_PATTERNS = ((128, 1), (512, 4), (2048, 16))
DIL_HALF = 64
N_GROUPS = 4
EXPERTS_PER_GROUP = 8
N_EXPERTS = N_GROUPS * EXPERTS_PER_GROUP
TOP_K = 2
MOE_ROWS = 256
NEG = -1e30

COL_NA = 0
COL_DIFF = 768
COL_POOL = 1536
COL_DIL = 1792
COL_GATE = 4096

VMEM_LIMIT = 56 * 1024 * 1024


def _cparams(sem, vmem=VMEM_LIMIT, flags=None):
    return pltpu.CompilerParams(dimension_semantics=sem, vmem_limit_bytes=vmem, flags=flags)


def _layer_norm(x, g, b):
    mu = jnp.mean(x, axis=-1, keepdims=True)
    xc = x - mu
    var = jnp.mean(xc * xc, axis=-1, keepdims=True)
    return xc * lax.rsqrt(var + LN_EPS) * g + b


def _dot(a, b):
    return jnp.dot(a, b, preferred_element_type=F32)


def _dot_nt(a, b):
    return lax.dot_general(a, b, (((1,), (1,)), ((), ())), preferred_element_type=F32)


def _lane(shape):
    return lax.broadcasted_iota(jnp.int32, shape, len(shape) - 1)


HI16 = -65536


def _pack_bf16_pairs(x):
    w = x.shape[1] // 2
    hi = lax.bitcast_convert_type(x[:, :w].astype(BF16).astype(F32), jnp.int32)
    lo = lax.bitcast_convert_type(x[:, w:].astype(BF16).astype(F32), jnp.int32)
    return (hi & HI16) | lax.shift_right_logical(lo, 16)


def _unpack_bf16_pairs(p):
    hi = lax.bitcast_convert_type(p & HI16, F32)
    lo = lax.bitcast_convert_type(lax.shift_left(p, 16), F32)
    return jnp.concatenate([hi, lo], axis=1)


def _ln_kernel(x_ref, g_ref, b_ref, o_ref):
    o_ref[...] = _layer_norm(x_ref[...], g_ref[...], b_ref[...])


def _embed_ln(x2, g, b):
    t, d = x2.shape
    tm = 1024
    return pl.pallas_call(
        _ln_kernel,
        grid=(t // tm,),
        in_specs=[pl.BlockSpec((tm, d), lambda i: (i, 0)),
                  pl.BlockSpec((1, d), lambda i: (0, 0)),
                  pl.BlockSpec((1, d), lambda i: (0, 0))],
        out_specs=pl.BlockSpec((tm, d), lambda i: (i, 0)),
        out_shape=jax.ShapeDtypeStruct((t, d), F32),
        compiler_params=_cparams(("parallel",)),
        name="embed_ln",
    )(x2, g.reshape(1, d), b.reshape(1, d))


PROJ_TM = 1024
QKV_W = 6 * LANES
WCOL = 256


QKV_TM = 512
QKV_SEGMENTS = ((COL_NA, 1, 0, HEAD_DIM), (COL_DIFF, 1, DIFF_QK_DIM // 8, DIFF_QK_DIM),
                (COL_DIL, 1, HEAD_DIM // 8, HEAD_DIM), (COL_DIL + QKV_W, 4, HEAD_DIM // 8, HEAD_DIM),
                (COL_DIL + 2 * QKV_W, 16, HEAD_DIM // 8, HEAD_DIM))


def _qkv_epilogue(z, t_ref, o_ref, zs_ref, r, half, q_scale):
    tm = z.shape[0]
    n = tm // r
    for c in range(QKV_W // LANES):
        blk = z[:, c * LANES:(c + 1) * LANES]
        if half and c < 4:
            blk = _rope(blk, t_ref, half)
        if c < 2:
            blk = blk * q_scale
        if r == 1:
            o_ref[0, 0, :, c * LANES:(c + 1) * LANES] = blk.astype(BF16)
        else:
            zs_ref[c] = blk
            for m in range(r):
                o_ref[0, m, :, c * LANES:(c + 1) * LANES] = zs_ref[
                    c, pl.ds(m, n, stride=r), :].astype(BF16)


def _proj_kernel(x_ref, w_ref, td_ref, tl_ref, na_ref, df_ref, d0_ref, d1_ref, d2_ref, u_ref,
                 zs1_ref, zs2_ref):
    xb = x_ref[0].astype(BF16)
    outs = (na_ref, df_ref, d0_ref, d1_ref, d2_ref)
    stage = (None, None, None, zs1_ref, zs2_ref)
    for (col0, r, half, head_w), o_ref, zs_ref in zip(QKV_SEGMENTS, outs, stage):
        z = _dot(xb, w_ref[:, col0:col0 + QKV_W])
        t_ref = td_ref if head_w == DIFF_QK_DIM else tl_ref
        _qkv_epilogue(z, t_ref, o_ref, zs_ref, r, half, head_w ** -0.5 * LOG2E)
    u_ref[0] = _dot(xb, w_ref[:, COL_POOL:COL_POOL + WCOL])


def _proj_branches(h3, w_bf16, tab_diff, tab_dil):
    b, s, d = h3.shape
    tm = QKV_TM
    qkv_spec = lambda r: pl.BlockSpec((1, r, tm // r, QKV_W), lambda bi, i: (bi, 0, i, 0))
    qkv_shape = lambda r: jax.ShapeDtypeStruct((b, r, s // r, QKV_W), BF16)
    tab_spec = pl.BlockSpec((3, tm, LANES), lambda bi, i: (0, i, 0))
    rs = [seg[1] for seg in QKV_SEGMENTS]
    outs = pl.pallas_call(
        _proj_kernel,
        grid=(b, s // tm),
        in_specs=[pl.BlockSpec((1, tm, d), lambda bi, i: (bi, i, 0)),
                  pl.BlockSpec(w_bf16.shape, lambda bi, i: (0, 0)),
                  tab_spec, tab_spec],
        out_specs=[qkv_spec(r) for r in rs] + [pl.BlockSpec((1, tm, WCOL), lambda bi, i: (bi, i, 0))],
        out_shape=[qkv_shape(r) for r in rs] + [jax.ShapeDtypeStruct((b, s, WCOL), F32)],
        scratch_shapes=[pltpu.VMEM((QKV_W // LANES, tm, LANES), F32)] * 2,
        compiler_params=_cparams(("parallel", "parallel")),
        name="proj_branches",
    )(h3, w_bf16, tab_diff, tab_dil)
    return [o.reshape(b, s, QKV_W) for o in outs[:-1]] + [outs[-1]]


def _rope_tables(seq, head_w):
    rot = head_w // 4
    half = rot // 2
    inv_freq = jnp.exp(jnp.arange(half, dtype=F32) * (-2.0 * math.log(ROPE_THETA) / rot))
    ang = jnp.arange(seq, dtype=jnp.int32).astype(F32)[:, None] * inv_freq[None, :]
    cos, sin = jnp.cos(ang), jnp.sin(ang)
    zero = jnp.zeros((seq, head_w - rot), F32)
    zh = jnp.zeros((seq, half), F32)
    t0 = jnp.concatenate([cos, cos, jnp.ones((seq, head_w - rot), F32)], axis=1)
    t1 = jnp.concatenate([-sin, zh, zero], axis=1)
    t2 = jnp.concatenate([zh, sin, zero], axis=1)
    reps = LANES // head_w
    return jnp.stack([jnp.tile(t0, (1, reps)), jnp.tile(t1, (1, reps)), jnp.tile(t2, (1, reps))])


def _rope(x, t_ref, half):
    return (x * t_ref[0] + pltpu.roll(x, LANES - half, 1) * t_ref[1]
            + pltpu.roll(x, half, 1) * t_ref[2])


LOG2E = math.log2(math.e)
LN2 = math.log(2.0)


def _na_bias_table(rpb):
    kr, kc = NA_ROWS, NA_COLS
    n_heads = rpb.shape[0]
    col = np.arange(GRID_W)
    col_start = np.clip(col - kc // 2, 0, GRID_W - kc)
    in_win = (col[None, :] >= col_start[:, None]) & (col[None, :] < col_start[:, None] + kc)
    dc = np.clip(col[None, :] - col[:, None] + kc - 1, 0, 2 * kc - 2)
    case = np.arange(kr)
    dr = np.arange(kr)[None, :] - case[:, None] + NA_ROWS - 1
    sel_r = jnp.asarray((dr[..., None] == np.arange(2 * kr - 1)).astype(np.float32))
    sel_c = jnp.asarray(((dc[..., None] == np.arange(2 * kc - 1)) & in_win[..., None]).astype(np.float32))
    hi = lax.Precision.HIGHEST
    rp = rpb.astype(F32).reshape(n_heads // 2, 2, 2 * kr - 1, 2 * kc - 1) * LOG2E
    t1 = jnp.einsum('phij,aki->pahkj', rp, sel_r, precision=hi)
    bias = jnp.einsum('pahkj,qcj->pahqkc', t1, sel_c, precision=hi)
    bias = bias + jnp.asarray(np.where(in_win, 0.0, NEG).astype(np.float32))[:, None, :]
    return bias.reshape(n_heads // 2, kr, 2 * GRID_W, kr * GRID_W)


def _stack_heads(q, lane):
    zero = jnp.zeros_like(q)
    return jnp.concatenate([jnp.where(lane < HEAD_DIM, q, zero), jnp.where(lane < HEAD_DIM, zero, q)],
                           axis=0)


def _na_kernel(q_ref, k_ref, v_ref, bias_ref, o_ref, *, rows_per_step, n_rows):
    step = pl.program_id(2)
    lane = _lane((GRID_W, LANES))
    kwin = NA_ROWS * GRID_W
    for i in range(rows_per_step):
        r = step * rows_per_step + i
        start = jnp.clip(r - NA_ROWS // 2, 0, n_rows - NA_ROWS)
        case = r - start
        k0 = pl.multiple_of(start * GRID_W, GRID_W)
        q2 = _stack_heads(q_ref[0, i * GRID_W:(i + 1) * GRID_W, :], lane)
        sc = _dot_nt(q2, k_ref[0, pl.ds(k0, kwin), :]) + bias_ref[0, case]
        m = jnp.max(sc, axis=-1, keepdims=True)
        e = jnp.exp2(sc - m)
        den = jnp.sum(e, axis=-1, keepdims=True)
        pv = _dot(e.astype(BF16), v_ref[0, pl.ds(k0, kwin), :]) / den
        o = jnp.where(lane < HEAD_DIM, pv[0:GRID_W], pv[GRID_W:2 * GRID_W])
        o_ref[0, i * GRID_W:(i + 1) * GRID_W, :] = o.astype(BF16)


def _na_attention(qkv, bias):
    b, s, _ = qkv.shape
    n_rows = s // GRID_W
    rps = 16
    tq = rps * GRID_W
    return pl.pallas_call(
        functools.partial(_na_kernel, rows_per_step=rps, n_rows=n_rows),
        grid=(b, 2, n_rows // rps),
        in_specs=[pl.BlockSpec((1, tq, LANES), lambda bi, hp, i: (bi, i, hp)),
                  pl.BlockSpec((1, s, LANES), lambda bi, hp, i: (bi, 0, 2 + hp)),
                  pl.BlockSpec((1, s, LANES), lambda bi, hp, i: (bi, 0, 4 + hp)),
                  pl.BlockSpec((1, NA_ROWS, 2 * GRID_W, NA_ROWS * GRID_W),
                               lambda bi, hp, i: (hp, 0, 0, 0))],
        out_specs=pl.BlockSpec((1, tq, LANES), lambda bi, hp, i: (bi, i, hp)),
        out_shape=jax.ShapeDtypeStruct((b, s, 2 * LANES), BF16),
        compiler_params=_cparams(("parallel", "parallel", "arbitrary")),
        name="na_attn",
    )(qkv, qkv, qkv, bias)


def _diff_kernel(lam_ref, q_ref, k_ref, v_ref, g_ref, o_ref, *, lam_init, rows):
    tq = q_ref.shape[1]
    lane = _lane((rows, LANES))
    dl = lam_ref[...]
    lam = (jnp.exp(jnp.sum(dl[0:1] * dl[1:2], axis=-1, keepdims=True))
           - jnp.exp(jnp.sum(dl[2:3] * dl[3:4], axis=-1, keepdims=True)) + lam_init)
    v = v_ref[0]
    v_lane = _lane(v.shape)
    one = jnp.ones_like(v)
    v_ext = (jnp.where(v_lane < HEAD_DIM, v, one), jnp.where(v_lane < HEAD_DIM, one, v))
    in_h0 = lane < HEAD_DIM
    for r0 in range(0, tq, rows):
        q = q_ref[0, r0:r0 + rows, :]
        zero = jnp.zeros_like(q)
        pv = []
        for lo in range(0, LANES, DIFF_QK_DIM):
            qm = jnp.where((lane >= lo) & (lane < lo + DIFF_QK_DIM), q, zero)
            sc = _dot_nt(qm, k_ref[0])
            e = jnp.exp2(sc - jnp.max(sc, axis=-1, keepdims=True))
            ev = _dot(e.astype(BF16), v_ext[lo // HEAD_DIM])
            pv.append(ev / pltpu.roll(ev, HEAD_DIM, 1))
        o = jnp.where(in_h0, pv[0] - lam * pv[1], pv[2] - lam * pv[3])
        o2 = o * o
        ms0 = jnp.sum(jnp.where(in_h0, o2, 0.0), axis=-1, keepdims=True) / HEAD_DIM
        ms1 = jnp.sum(jnp.where(in_h0, 0.0, o2), axis=-1, keepdims=True) / HEAD_DIM
        ms = jnp.where(in_h0, ms0, ms1)
        o = o * lax.rsqrt(ms + LN_EPS) * g_ref[...] * (1.0 - lam_init)
        o_ref[0, r0:r0 + rows, :] = o.astype(BF16)


def _diff_attention(qkv, diff_lam, subln_g, lam_init):
    b, s, _ = qkv.shape
    tq = 1024
    g2 = jnp.tile(subln_g.reshape(1, HEAD_DIM), (1, 2))
    return pl.pallas_call(
        functools.partial(_diff_kernel, lam_init=lam_init, rows=512),
        grid=(b, 2, s // tq),
        in_specs=[pl.BlockSpec((4, DIFF_QK_DIM), lambda bi, hp, i: (0, 0)),
                  pl.BlockSpec((1, tq, LANES), lambda bi, hp, i: (bi, i, hp)),
                  pl.BlockSpec((1, s, LANES), lambda bi, hp, i: (bi, 0, 2 + hp)),
                  pl.BlockSpec((1, s, LANES), lambda bi, hp, i: (bi, 0, 4 + hp)),
                  pl.BlockSpec((1, LANES), lambda bi, hp, i: (0, 0))],
        out_specs=pl.BlockSpec((1, tq, LANES), lambda bi, hp, i: (bi, i, hp)),
        out_shape=jax.ShapeDtypeStruct((b, s, 2 * LANES), BF16),
        compiler_params=_cparams(("parallel", "parallel", "arbitrary")),
        name="diff_attn",
    )(diff_lam, qkv, qkv, qkv, g2)


POOL_PAD = 16
POOL_CHUNK = 512
POOL_HALO = 8


def _pool_kernel(u_ref, w_ref, sc_ref, o_ref, p_ref):
    s = u_ref.shape[1]
    width = u_ref.shape[2]
    p_ref[0:POOL_PAD, :] = jnp.zeros((POOL_PAD, width), F32)
    p_ref[POOL_PAD + s:POOL_PAD + s + POOL_PAD, :] = jnp.zeros((POOL_PAD, width), F32)
    p_ref[POOL_PAD:POOL_PAD + s, :] = u_ref[0]
    n = POOL_CHUNK + 2 * POOL_HALO
    lane = _lane((POOL_CHUNK, width))
    row = lax.broadcasted_iota(jnp.int32, (POOL_CHUNK, width), 0)
    w_of_lane = jnp.where(lane < POOL_GROUP, 2, jnp.where(lane < 2 * POOL_GROUP, 4,
                          jnp.where(lane < 3 * POOL_GROUP, 8, 16)))

    def body(ci, carry):
        c0 = pl.multiple_of(ci * POOL_CHUNK, POOL_CHUNK)
        x = p_ref[pl.ds(c0 + POOL_PAD - POOL_HALO, n), :]
        w2 = x + pltpu.roll(x, 1, 0)
        w4 = pltpu.roll(w2, 1, 0) + pltpu.roll(w2, n - 1, 0)
        w8 = pltpu.roll(w4, 2, 0) + pltpu.roll(w4, n - 2, 0)
        w16 = pltpu.roll(w8, 4, 0) + pltpu.roll(w8, n - 4, 0)
        u = x[POOL_HALO:POOL_HALO + POOL_CHUNK]
        wsum = jnp.where(lane < POOL_GROUP, w2[POOL_HALO:POOL_HALO + POOL_CHUNK],
                         jnp.where(lane < 2 * POOL_GROUP, w4[POOL_HALO:POOL_HALO + POOL_CHUNK],
                                   jnp.where(lane < 3 * POOL_GROUP, w8[POOL_HALO:POOL_HALO + POOL_CHUNK],
                                             w16[POOL_HALO:POOL_HALO + POOL_CHUNK])))
        t = row + c0
        half_w = w_of_lane // 2
        lo = jnp.maximum(t - half_w, 0)
        hi = jnp.minimum(t + w_of_lane - 1 - half_w, s - 1)
        cnt = (hi - lo + 1).astype(F32)
        dlt = wsum / cnt - u
        y = _dot(dlt.astype(BF16), w_ref[...]) * sc_ref[...]
        o_ref[0, pl.ds(c0, POOL_CHUNK), :] = y.astype(BF16)
        return carry

    lax.fori_loop(0, s // POOL_CHUNK, body, 0)


def _pool(u3, pool_w, pool_scale):
    b, s, width = u3.shape
    wbd = jax.scipy.linalg.block_diag(*[pool_w[g] for g in range(len(POOL_WINDOWS))]).astype(BF16)
    return pl.pallas_call(
        _pool_kernel,
        grid=(b,),
        in_specs=[pl.BlockSpec((1, s, width), lambda bi: (bi, 0, 0)),
                  pl.BlockSpec((width, width), lambda bi: (0, 0)),
                  pl.BlockSpec((1, width), lambda bi: (0, 0))],
        out_specs=pl.BlockSpec((1, s, width), lambda bi: (bi, 0, 0)),
        out_shape=jax.ShapeDtypeStruct((b, s, width), BF16),
        scratch_shapes=[pltpu.VMEM((s + 2 * POOL_PAD, width), F32)],
        compiler_params=_cparams(("parallel",)),
        name="pool",
    )(u3, wbd, pool_scale.reshape(1, width))


SWA_Q = 128
SWA_BAND = SWA_Q + 2 * DIL_HALF


def _swa_kernel(q_ref, k_ref, v_ref, o_ref, l_ref, *, length, chunk):
    c = pl.program_id(2)
    lane = _lane((SWA_Q, LANES))
    rel = (lax.broadcasted_iota(jnp.int32, (SWA_Q, SWA_BAND), 1)
           - lax.broadcasted_iota(jnp.int32, (SWA_Q, SWA_BAND), 0))
    for i in range(chunk // SWA_Q):
        r0 = c * chunk + i * SWA_Q
        l0 = r0 & (length - 1)
        lo = r0 - l0 + jnp.clip(l0 - DIL_HALF, 0, length - SWA_BAND)
        lo = pl.multiple_of(lo, DIL_HALF)
        q2 = _stack_heads(q_ref[0, i * SWA_Q:(i + 1) * SWA_Q, :], lane)
        d = rel + (lo - r0)
        valid = (d >= -DIL_HALF) & (d <= DIL_HALF)
        valid2 = jnp.concatenate([valid, valid], axis=0)
        sc = jnp.where(valid2, _dot_nt(q2, k_ref[0, pl.ds(lo, SWA_BAND), :]), NEG)
        m = jnp.max(sc, axis=-1, keepdims=True)
        e = jnp.exp2(sc - m)
        den = jnp.sum(e, axis=-1, keepdims=True)
        pv = _dot(e.astype(BF16), v_ref[0, pl.ds(lo, SWA_BAND), :]) / den
        lse2 = m + jnp.log2(den)
        o_ref[0, i * SWA_Q:(i + 1) * SWA_Q, :] = jnp.where(lane < HEAD_DIM, pv[0:SWA_Q], pv[SWA_Q:])
        l_ref[0, i * SWA_Q:(i + 1) * SWA_Q, :] = jnp.where(lane < HEAD_DIM, lse2[0:SWA_Q], lse2[SWA_Q:])


def _swa(qkv, length):
    b, s, _ = qkv.shape
    chunk = 1024
    shp = jax.ShapeDtypeStruct((b, s, 2 * LANES), F32)
    return pl.pallas_call(
        functools.partial(_swa_kernel, length=length, chunk=chunk),
        grid=(b, 2, s // chunk),
        in_specs=[pl.BlockSpec((1, chunk, LANES), lambda bi, hp, c: (bi, c, hp)),
                  pl.BlockSpec((1, s, LANES), lambda bi, hp, c: (bi, 0, 2 + hp)),
                  pl.BlockSpec((1, s, LANES), lambda bi, hp, c: (bi, 0, 4 + hp))],
        out_specs=[pl.BlockSpec((1, chunk, LANES), lambda bi, hp, c: (bi, c, hp)),
                   pl.BlockSpec((1, chunk, LANES), lambda bi, hp, c: (bi, c, hp))],
        out_shape=[shp, shp],
        compiler_params=_cparams(("parallel", "parallel", "arbitrary")),
        name=f"swa_l{length}",
    )(qkv, qkv, qkv)


def _dil_merge_kernel(o0_ref, l0_ref, o1_ref, l1_ref, o2_ref, l2_ref, y_ref, so1, sl1, so2, sl2):
    tm = y_ref.shape[1]
    r1 = DIL_PATTERNS[1][1]
    r2 = DIL_PATTERNS[2][1]
    for m in range(r1):
        so1[pl.ds(m, tm // r1, stride=r1), :] = o1_ref[0, m]
        sl1[pl.ds(m, tm // r1, stride=r1), :] = l1_ref[0, m]
    for m in range(r2):
        so2[pl.ds(m, tm // r2, stride=r2), :] = o2_ref[0, m]
        sl2[pl.ds(m, tm // r2, stride=r2), :] = l2_ref[0, m]
    l0, l1, l2 = l0_ref[0], sl1[...], sl2[...]
    mx = jnp.maximum(jnp.maximum(l0, l1), l2)
    e0, e1, e2 = jnp.exp2(l0 - mx), jnp.exp2(l1 - mx), jnp.exp2(l2 - mx)
    den = e0 + e1 + e2
    y = (e0 / den) * o0_ref[0] + (e1 / den) * so1[...] + (e2 / den) * so2[...]
    y_ref[0] = y.astype(BF16)


def _dil_merge(outs):
    (o0, l0), (o1, l1), (o2, l2) = outs
    b, s, w = o0.shape
    tm = 1024
    r1 = DIL_PATTERNS[1][1]
    r2 = DIL_PATTERNS[2][1]
    v1 = lambda a: a.reshape(b, r1, s // r1, w)
    v2 = lambda a: a.reshape(b, r2, s // r2, w)
    nat = pl.BlockSpec((1, tm, LANES), lambda bi, i, hp: (bi, i, hp))
    g1 = pl.BlockSpec((1, r1, tm // r1, LANES), lambda bi, i, hp: (bi, 0, i, hp))
    g2 = pl.BlockSpec((1, r2, tm // r2, LANES), lambda bi, i, hp: (bi, 0, i, hp))
    return pl.pallas_call(
        _dil_merge_kernel,
        grid=(b, s // tm, w // LANES),
        in_specs=[nat, nat, g1, g1, g2, g2],
        out_specs=nat,
        out_shape=jax.ShapeDtypeStruct((b, s, w), BF16),
        scratch_shapes=[pltpu.VMEM((tm, LANES), F32)] * 4,
        compiler_params=_cparams(("parallel", "parallel", "parallel")),
        name="dil_merge",
    )(o0, l0, v1(o1), v1(l1), v2(o2), v2(l2))


def _merge_kernel(ya_ref, yb_ref, yc_ref, yd_ref, wg_ref, bg_ref, wb_ref, wo_ref, h_ref, g_ref, b_ref,
                  o_ref, op_ref, hb_ref, acc_ref, *, alpha):
    n = pl.program_id(1)

    @pl.when(n == 0)
    def _():
        hb_ref[...] = h_ref[...].astype(BF16)

    zg = _dot(hb_ref[...], wg_ref[...]) + bg_ref[0]
    gate = 1.0 / (1.0 + jnp.exp(-zg))
    for k, y_ref in enumerate((ya_ref, yb_ref, yc_ref, yd_ref)):
        @pl.when(n == k)
        def _(y_ref=y_ref, k=k):
            term = gate * _dot(y_ref[...], wb_ref[0])
            if k == 0:
                acc_ref[...] = term
            else:
                acc_ref[...] += term

    @pl.when(n == pl.num_programs(1) - 1)
    def _():
        mix = _dot(acc_ref[...].astype(BF16), wo_ref[...])
        h1 = _layer_norm(alpha * h_ref[...] + mix, g_ref[...], b_ref[...])
        o_ref[...] = h1
        op_ref[...] = _pack_bf16_pairs(h1)


def _merge(ys, wg, b_gate, wb, wo, h2, g, bb, alpha):
    t, d = h2.shape
    bw = ys[0].shape[1]
    nb = len(ys)
    tm = PROJ_TM
    yspec = pl.BlockSpec((tm, bw), lambda i, n: (i, 0))
    vec = pl.BlockSpec((1, d), lambda i, n: (0, 0))
    return pl.pallas_call(
        functools.partial(_merge_kernel, alpha=alpha),
        grid=(t // tm, nb),
        in_specs=[yspec, yspec, yspec, yspec,
                  pl.BlockSpec((d, d), lambda i, n: (0, n)),
                  pl.BlockSpec((1, 1, d), lambda i, n: (n, 0, 0)),
                  pl.BlockSpec((1, bw, d), lambda i, n: (n, 0, 0)),
                  pl.BlockSpec((d, d), lambda i, n: (0, 0)),
                  pl.BlockSpec((tm, d), lambda i, n: (i, 0)),
                  vec, vec],
        out_specs=[pl.BlockSpec((tm, d), lambda i, n: (i, 0)),
                   pl.BlockSpec((tm, d // 2), lambda i, n: (i, 0))],
        out_shape=[jax.ShapeDtypeStruct((t, d), F32), jax.ShapeDtypeStruct((t, d // 2), jnp.int32)],
        scratch_shapes=[pltpu.VMEM((tm, d), BF16), pltpu.VMEM((tm, d), F32)],
        compiler_params=_cparams(("parallel", "arbitrary")),
        name="merge",
    )(*ys, wg, b_gate.reshape(nb, 1, d), wb, wo, h2, g.reshape(1, d), bb.reshape(1, d))


ROUTER_TM = 512


def _router_kernel(h_ref, w_ref, b_ref, o_ref, cnt_ref, carry, before):
    tm = h_ref.shape[0]

    @pl.when(pl.program_id(0) == 0)
    def _():
        carry[...] = jnp.zeros_like(carry)
        before[...] = (lax.broadcasted_iota(jnp.int32, (tm, tm), 0)
                       < lax.broadcasted_iota(jnp.int32, (tm, tm), 1)).astype(BF16)

    logits = lax.dot_general(w_ref[...], h_ref[...], (((1,), (1,)), ((), ())),
                             preferred_element_type=F32,
                             precision=lax.Precision.HIGHEST) + b_ref[...]
    row = lax.broadcasted_iota(jnp.int32, (LANES, tm), 0)
    big = jnp.int32(1 << 20)
    is_g = row < N_GROUPS
    gl = jnp.where(is_g, logits, -jnp.inf)
    gmax = jnp.max(gl, axis=0, keepdims=True)
    gsel = jnp.min(jnp.where(is_g & (gl == gmax), row, big), axis=0, keepdims=True)
    pg = 1.0 / jnp.sum(jnp.exp(gl - gmax), axis=0, keepdims=True)
    e_lo = N_GROUPS + gsel * EXPERTS_PER_GROUP
    in_grp = (row >= e_lo) & (row < e_lo + EXPERTS_PER_GROUP)
    el = jnp.where(in_grp, logits, -jnp.inf)
    v1 = jnp.max(el, axis=0, keepdims=True)
    i1 = jnp.min(jnp.where(in_grp & (el == v1), row, big), axis=0, keepdims=True)
    el2 = jnp.where(row == i1, -jnp.inf, el)
    v2 = jnp.max(el2, axis=0, keepdims=True)
    i2 = jnp.min(jnp.where(in_grp & (row != i1) & (el2 == v2), row, big), axis=0, keepdims=True)
    t2 = jnp.exp(v2 - v1)
    g1 = pg / (1.0 + t2)
    g2 = pg * t2 / (1.0 + t2)
    oh1 = (row == i1)
    oh2 = (row == i2)
    oh1b = oh1.astype(BF16)
    oh2b = oh2.astype(BF16)
    c0 = carry[...]
    c1 = c0 + jnp.sum(oh1b.astype(F32), axis=1, keepdims=True)
    rank1 = jnp.sum(jnp.where(oh1, _dot(oh1b, before[...]) + c0, 0.0), axis=0, keepdims=True)
    rank2 = jnp.sum(jnp.where(oh2, _dot(oh2b, before[...]) + c1, 0.0), axis=0, keepdims=True)
    c2 = c1 + jnp.sum(oh2b.astype(F32), axis=1, keepdims=True)
    carry[...] = c2
    cnt_ref[...] = c2
    e1 = (i1 - N_GROUPS).astype(F32)
    e2 = (i2 - N_GROUPS).astype(F32)
    out_t = jnp.where(row == 0, e1, jnp.where(row == 1, e2, jnp.where(row == 2, g1, jnp.where(
        row == 3, g2, jnp.where(row == 4, rank1, jnp.where(row == 5, rank2, 0.0))))))
    o_ref[...] = out_t.T


def _router(h2, rg_w, rg_b, re_w, re_b):
    t, d = h2.shape
    pad = LANES - N_GROUPS - N_EXPERTS
    w_t = jnp.concatenate([rg_w.T, re_w.T, jnp.zeros((pad, d), F32)], axis=0)
    bias = jnp.concatenate([rg_b, re_b, jnp.zeros((pad,), F32)]).reshape(LANES, 1)
    tm = ROUTER_TM
    return pl.pallas_call(
        _router_kernel,
        grid=(t // tm,),
        in_specs=[pl.BlockSpec((tm, d), lambda i: (i, 0)),
                  pl.BlockSpec((LANES, d), lambda i: (0, 0)),
                  pl.BlockSpec((LANES, 1), lambda i: (0, 0))],
        out_specs=[pl.BlockSpec((tm, LANES), lambda i: (i, 0)),
                   pl.BlockSpec((LANES, 1), lambda i: (0, 0))],
        out_shape=[jax.ShapeDtypeStruct((t, LANES), F32), jax.ShapeDtypeStruct((LANES, 1), F32)],
        scratch_shapes=[pltpu.VMEM((LANES, 1), F32), pltpu.VMEM((tm, tm), BF16)],
        compiler_params=_cparams(("arbitrary",)),
        name="router",
    )(h2, w_t, bias)


def _moe_kernel(be_ref, nv_ref, x_ref, wg_ref, wu_ref, wd_ref, o_ref, wgb, wub, wdb):
    j = pl.program_id(0)
    n_valid = nv_ref[j]
    new_expert = (j == 0) | (be_ref[j] != be_ref[jnp.maximum(j - 1, 0)])

    @pl.when(new_expert)
    def _():
        wgb[...] = wg_ref[0].astype(BF16)
        wub[...] = wu_ref[0].astype(BF16)
        wdb[...] = wd_ref[0].astype(BF16)

    @pl.when(n_valid > 0)
    def _():
        row = lax.broadcasted_iota(jnp.int32, x_ref.shape, 0)
        x = _unpack_bf16_pairs(jnp.where(row < n_valid, x_ref[...], 0)).astype(BF16)
        g = _dot(x, wgb[...])
        u = _dot(x, wub[...])
        hmid = (g * (1.0 / (1.0 + jnp.exp(-g)))) * u
        o_ref[...] = _pack_bf16_pairs(_dot(hmid.astype(BF16), wdb[...]))

    @pl.when(n_valid <= 0)
    def _():
        o_ref[...] = jnp.zeros_like(o_ref)


def _moe_experts(xs, block_e, n_valid, wg, wu, wd, l):
    cap, dp = xs.shape
    d = 2 * dp
    de = wg.shape[3]
    n_blocks = cap // MOE_ROWS
    grid_spec = pltpu.PrefetchScalarGridSpec(
        num_scalar_prefetch=2,
        grid=(n_blocks,),
        in_specs=[pl.BlockSpec((MOE_ROWS, dp), lambda j, be, nv: (j, 0)),
                  pl.BlockSpec((None, 1, d, de), lambda j, be, nv: (l, be[j], 0, 0)),
                  pl.BlockSpec((None, 1, d, de), lambda j, be, nv: (l, be[j], 0, 0)),
                  pl.BlockSpec((None, 1, de, d), lambda j, be, nv: (l, be[j], 0, 0))],
        out_specs=pl.BlockSpec((MOE_ROWS, dp), lambda j, be, nv: (j, 0)),
        scratch_shapes=[pltpu.VMEM((d, de), BF16), pltpu.VMEM((d, de), BF16), pltpu.VMEM((de, d), BF16)],
    )
    return pl.pallas_call(
        _moe_kernel,
        grid_spec=grid_spec,
        out_shape=jax.ShapeDtypeStruct((cap, dp), jnp.int32),
        compiler_params=_cparams(("arbitrary",)),
        name="moe_experts",
    )(block_e, n_valid, xs, wg, wu, wd)


SC_CORES = 2
SC_SUBCORES = 16
SC_WORKERS = SC_CORES * SC_SUBCORES
SC_CHUNK = 64


def _sc_mesh():
    return plsc.VectorSubcoreMesh(core_axis_name="c", subcore_axis_name="s",
                                  num_cores=SC_CORES, num_subcores=SC_SUBCORES)


def _sc_scatter_rows(src, idx0, idx1, n_out):
    t, d = src.shape
    per_w = t // SC_WORKERS
    n_chunks = per_w // SC_CHUNK
    idx_shape = (SC_WORKERS, n_chunks, SC_CHUNK)

    @functools.partial(
        pl.kernel, mesh=_sc_mesh(),
        out_type=jax.ShapeDtypeStruct((n_out, d), src.dtype),
        scratch_types=[pltpu.VMEM((n_chunks, SC_CHUNK), jnp.int32),
                       pltpu.VMEM((n_chunks, SC_CHUNK), jnp.int32),
                       pltpu.VMEM((2, SC_CHUNK, d), src.dtype),
                       pltpu.SemaphoreType.DMA((2,)), pltpu.SemaphoreType.DMA((2,))],
        name="sc_dispatch",
    )
    def k(src_hbm, i0_hbm, i1_hbm, out_hbm, i0_v, i1_v, rows_v, sem_in, sem_out):
        wid = lax.axis_index("s") * SC_CORES + lax.axis_index("c")
        base = wid * per_w
        pltpu.sync_copy(i0_hbm.at[wid], i0_v)
        pltpu.sync_copy(i1_hbm.at[wid], i1_v)

        def load(ci):
            s = ci % 2
            return pltpu.make_async_copy(src_hbm.at[pl.ds(base + ci * SC_CHUNK, SC_CHUNK)],
                                         rows_v.at[s], sem_in.at[s])

        def scatter(ci, idx_v):
            s = ci % 2
            return pltpu.make_async_copy(rows_v.at[s], out_hbm.at[idx_v.at[ci]], sem_out.at[s])

        load(0).start()
        for ci in range(n_chunks):
            load(ci).wait()
            if ci >= 1:
                scatter(ci - 1, i0_v).wait()
                scatter(ci - 1, i1_v).wait()
            if ci + 1 < n_chunks:
                load(ci + 1).start()
            scatter(ci, i0_v).start()
            scatter(ci, i1_v).start()
        scatter(n_chunks - 1, i0_v).wait()
        scatter(n_chunks - 1, i1_v).wait()

    return k(src, idx0.reshape(idx_shape), idx1.reshape(idx_shape))


def _sc_gather_rows(table, idx):
    n = idx.shape[0]
    d = table.shape[1]
    per_w = n // SC_WORKERS
    n_chunks = per_w // SC_CHUNK

    @functools.partial(
        pl.kernel, mesh=_sc_mesh(),
        out_type=jax.ShapeDtypeStruct((n, d), table.dtype),
        scratch_types=[pltpu.VMEM((n_chunks, SC_CHUNK), jnp.int32),
                       pltpu.VMEM((2, SC_CHUNK, d), table.dtype),
                       pltpu.SemaphoreType.DMA((2,)), pltpu.SemaphoreType.DMA((2,))],
        name="sc_collect",
    )
    def k(table_hbm, idx_hbm, out_hbm, idx_v, rows_v, sem_in, sem_out):
        wid = lax.axis_index("s") * SC_CORES + lax.axis_index("c")
        base = wid * per_w
        pltpu.sync_copy(idx_hbm.at[wid], idx_v)

        def gather(ci):
            s = ci % 2
            return pltpu.make_async_copy(table_hbm.at[idx_v.at[ci]], rows_v.at[s], sem_in.at[s])

        def store(ci):
            s = ci % 2
            return pltpu.make_async_copy(rows_v.at[s], out_hbm.at[pl.ds(base + ci * SC_CHUNK, SC_CHUNK)],
                                         sem_out.at[s])

        gather(0).start()
        for ci in range(n_chunks):
            gather(ci).wait()
            if ci >= 1:
                store(ci - 1).wait()
            if ci + 1 < n_chunks:
                gather(ci + 1).start()
            store(ci).start()
        store(n_chunks - 1).wait()

    return k(table, idx.reshape(SC_WORKERS, n_chunks, SC_CHUNK))


def _combine_kernel(h_ref, r_ref, y1_ref, y2_ref, g_ref, b_ref, o_ref, *, alpha):
    g1 = r_ref[:, 2:3]
    g2 = r_ref[:, 3:4]
    ffn = g1 * _unpack_bf16_pairs(y1_ref[...]) + g2 * _unpack_bf16_pairs(y2_ref[...])
    o_ref[...] = _layer_norm(alpha * h_ref[...] + ffn, g_ref[...], b_ref[...])


def _combine(h2, routing, y12, g, bb, alpha):
    t, d = h2.shape
    tm = 512
    nt = t // tm
    row = pl.BlockSpec((tm, d), lambda i: (i, 0))
    vec = pl.BlockSpec((1, d), lambda i: (0, 0))
    return pl.pallas_call(
        functools.partial(_combine_kernel, alpha=alpha),
        grid=(nt,),
        in_specs=[row, pl.BlockSpec((tm, LANES), lambda i: (i, 0)),
                  pl.BlockSpec((tm, d // 2), lambda i: (i, 0)),
                  pl.BlockSpec((tm, d // 2), lambda i: (nt + i, 0)), vec, vec],
        out_specs=row,
        out_shape=jax.ShapeDtypeStruct((t, d), F32),
        compiler_params=_cparams(("parallel",)),
        name="combine_ln",
    )(h2, routing, y12, y12, g.reshape(1, d), bb.reshape(1, d))


def _moe(h2, h2_packed, rg_w, rg_b, re_w, re_b, wg, wu, wd, l, g, bb, alpha):
    t, d = h2.shape
    routing, counts = _router(h2, rg_w, rg_b, re_w, re_b)
    eid = routing[:, 0:2].astype(jnp.int32)
    rank = routing[:, 4:6].astype(jnp.int32)
    cnt = counts[N_GROUPS:N_GROUPS + N_EXPERTS, 0].astype(jnp.int32)
    padded = (cnt + MOE_ROWS - 1) // MOE_ROWS * MOE_ROWS
    pad_end = jnp.cumsum(padded)
    pad_start = pad_end - padded
    experts = jnp.arange(N_EXPERTS, dtype=jnp.int32)
    dest = jnp.sum(jnp.where(eid[..., None] == experts, pad_start, 0), axis=-1) + rank
    n_slots = t * TOP_K
    n_blocks = (n_slots + N_EXPERTS * (MOE_ROWS - 1) + MOE_ROWS - 1) // MOE_ROWS
    cap = n_blocks * MOE_ROWS
    blk_row = jnp.arange(n_blocks, dtype=jnp.int32) * MOE_ROWS
    block_e = jnp.minimum(jnp.sum((pad_end[None, :] <= blk_row[:, None]).astype(jnp.int32), axis=1),
                          N_EXPERTS - 1)
    is_e = block_e[:, None] == experts
    blk_cnt = jnp.sum(jnp.where(is_e, cnt, 0), axis=1)
    blk_start = jnp.sum(jnp.where(is_e, pad_start, 0), axis=1)
    n_valid = jnp.clip(blk_cnt - (blk_row - blk_start), 0, MOE_ROWS).astype(jnp.int32)
    xs = _sc_scatter_rows(h2_packed, dest[:, 0], dest[:, 1], cap)
    ys = _moe_experts(xs, block_e, n_valid, wg, wu, wd, l)
    y12 = _sc_gather_rows(ys, jnp.concatenate([dest[:, 0], dest[:, 1]]))
    return _combine(h2, routing, y12, g, bb, alpha)


def _mixing_layer(h, b, s, l, p, lam_init):
    t = b * s
    alpha = (2 * p['w_in'].shape[0]) ** 0.25
    d = h.shape[1]
    w_in = p['w_in']
    tab_diff = _rope_tables(s, DIFF_QK_DIM)
    tab_dil = _rope_tables(s, HEAD_DIM)
    h3 = h.reshape(b, s, d)
    w_left = lax.slice_in_dim(w_in[l], 0, COL_GATE, axis=1).astype(BF16)
    qkv_na, qkv_diff, qkv_d0, qkv_d1, qkv_d2, u = _proj_branches(h3, w_left, tab_diff, tab_dil)
    y_a = _na_attention(qkv_na, _na_bias_table(p['na_rpb'][l]))
    y_b = _diff_attention(qkv_diff, p['diff_lam'][l], p['diff_subln_g'][l], lam_init)
    y_c = _pool(u, p['pool_w'][l], p['pool_scale'][l])
    y_d = _dil_merge([_swa(qkv, s // r)
                      for qkv, (_, r) in zip((qkv_d0, qkv_d1, qkv_d2), DIL_PATTERNS)])
    ys = [a.reshape(t, -1) for a in (y_a, y_b, y_c, y_d)]
    wg = lax.slice_in_dim(w_in[l], COL_GATE, COL_GATE + len(ys) * d, axis=1).astype(BF16)
    return _merge(ys, wg, p['b_gate'][l], p['w_branch'][l].astype(BF16), p['w_out'][l].astype(BF16), h,
                  p['ln1_g'][l], p['ln1_b'][l], alpha)


def kernel(x, emb_ln_g, emb_ln_b, w_in, b_gate, na_rpb, diff_lam, diff_subln_g, pool_w, pool_scale,
           w_branch, w_out, ln1_g, ln1_b, router_group_w, router_group_b, router_expert_w,
           router_expert_b, expert_w_gate, expert_w_up, expert_w_down, ln2_g, ln2_b):
    b, s, d = x.shape
    depth = w_in.shape[0]
    alpha = (2 * depth) ** 0.25
    p = dict(w_in=w_in, b_gate=b_gate, na_rpb=na_rpb, diff_lam=diff_lam, diff_subln_g=diff_subln_g,
             pool_w=pool_w, pool_scale=pool_scale, w_branch=w_branch, w_out=w_out, ln1_g=ln1_g, ln1_b=ln1_b)
    h = _embed_ln(x.reshape(b * s, d), emb_ln_g, emb_ln_b)
    for l in range(depth):
        lam_init = 0.8 - 0.6 * math.exp(-0.3 * l)
        h, h_packed = _mixing_layer(h, b, s, l, p, lam_init)
        h = _moe(h, h_packed, router_group_w[l], router_group_b[l], router_expert_w[l], router_expert_b[l],
                 expert_w_gate, expert_w_up, expert_w_down, l, ln2_g[l], ln2_b[l], alpha)
    return h.reshape(b, s, d)
```

```python
import functools
import math

import jax
import jax.numpy as jnp
import numpy as np
from jax import lax
from jax.experimental import pallas as pl
from jax.experimental.pallas import tpu as pltpu
from jax.experimental.pallas import tpu_sc as plsc

F32 = jnp.float32
BF16 = jnp.bfloat16

LANES = 128
GRID_W = 64
HEAD_DIM = 64
ROPE_THETA = 500000.0
LN_EPS = 1e-5
NA_ROWS = 8
NA_COLS = 16
DIFF_QK_DIM = 32
POOL_WINDOWS = (2, 4, 8, 16)
POOL_GROUP = 64
DIL_PATTERNS = ((128, 1), (512, 4), (2048, 16))
DIL_HALF = 64
N_GROUPS = 4
EXPERTS_PER_GROUP = 8
N_EXPERTS = N_GROUPS * EXPERTS_PER_GROUP
TOP_K = 2
MOE_ROWS = 256
NEG = -1e30

COL_NA = 0
COL_DIFF = 768
COL_POOL = 1536
COL_DIL = 1792
COL_GATE = 4096

VMEM_LIMIT = 56 * 1024 * 1024


def _cparams(sem, vmem=VMEM_LIMIT, flags=None):
    return pltpu.CompilerParams(dimension_semantics=sem, vmem_limit_bytes=vmem, flags=flags)


def _layer_norm(x, g, b):
    mu = jnp.mean(x, axis=-1, keepdims=True)
    xc = x - mu
    var = jnp.mean(xc * xc, axis=-1, keepdims=True)
    return xc * lax.rsqrt(var + LN_EPS) * g + b


def _dot(a, b):
    return jnp.dot(a, b, preferred_element_type=F32)


def _dot_nt(a, b):
    return lax.dot_general(a, b, (((1,), (1,)), ((), ())), preferred_element_type=F32)


def _lane(shape):
    return lax.broadcasted_iota(jnp.int32, shape, len(shape) - 1)


HI16 = -65536


def _pack_bf16_pairs(x):
    w = x.shape[1] // 2
    hi = lax.bitcast_convert_type(x[:, :w].astype(BF16).astype(F32), jnp.int32)
    lo = lax.bitcast_convert_type(x[:, w:].astype(BF16).astype(F32), jnp.int32)
    return (hi & HI16) | lax.shift_right_logical(lo, 16)


def _unpack_bf16_pairs(p):
    hi = lax.bitcast_convert_type(p & HI16, F32)
    lo = lax.bitcast_convert_type(lax.shift_left(p, 16), F32)
    return jnp.concatenate([hi, lo], axis=1)


def _ln_kernel(x_ref, g_ref, b_ref, o_ref):
    o_ref[...] = _layer_norm(x_ref[...], g_ref[...], b_ref[...])


def _embed_ln(x2, g, b):
    t, d = x2.shape
    tm = 1024
    return pl.pallas_call(
        _ln_kernel,
        grid=(t // tm,),
        in_specs=[pl.BlockSpec((tm, d), lambda i: (i, 0)),
                  pl.BlockSpec((1, d), lambda i: (0, 0)),
                  pl.BlockSpec((1, d), lambda i: (0, 0))],
        out_specs=pl.BlockSpec((tm, d), lambda i: (i, 0)),
        out_shape=jax.ShapeDtypeStruct((t, d), F32),
        compiler_params=_cparams(("parallel",)),
        name="embed_ln",
    )(x2, g.reshape(1, d), b.reshape(1, d))


PROJ_TM = 1024
QKV_W = 6 * LANES
WCOL = 256


QKV_TM = 512
QKV_SEGMENTS = ((COL_NA, 1, 0, HEAD_DIM), (COL_DIFF, 1, DIFF_QK_DIM // 8, DIFF_QK_DIM),
                (COL_DIL, 1, HEAD_DIM // 8, HEAD_DIM), (COL_DIL + QKV_W, 4, HEAD_DIM // 8, HEAD_DIM),
                (COL_DIL + 2 * QKV_W, 16, HEAD_DIM // 8, HEAD_DIM))


def _qkv_epilogue(z, t_ref, o_ref, zs_ref, r, half, q_scale):
    tm = z.shape[0]
    n = tm // r
    for c in range(QKV_W // LANES):
        blk = z[:, c * LANES:(c + 1) * LANES]
        if half and c < 4:
            blk = _rope(blk, t_ref, half)
        if c < 2:
            blk = blk * q_scale
        if r == 1:
            o_ref[0, 0, :, c * LANES:(c + 1) * LANES] = blk.astype(BF16)
        else:
            zs_ref[c] = blk
            for m in range(r):
                o_ref[0, m, :, c * LANES:(c + 1) * LANES] = zs_ref[
                    c, pl.ds(m, n, stride=r), :].astype(BF16)


def _proj_kernel(x_ref, w_ref, td_ref, tl_ref, na_ref, df_ref, d0_ref, d1_ref, d2_ref, u_ref,
                 zs1_ref, zs2_ref):
    xb = x_ref[0].astype(BF16)
    outs = (na_ref, df_ref, d0_ref, d1_ref, d2_ref)
    stage = (None, None, None, zs1_ref, zs2_ref)
    for (col0, r, half, head_w), o_ref, zs_ref in zip(QKV_SEGMENTS, outs, stage):
        z = _dot(xb, w_ref[:, col0:col0 + QKV_W])
        t_ref = td_ref if head_w == DIFF_QK_DIM else tl_ref
        _qkv_epilogue(z, t_ref, o_ref, zs_ref, r, half, head_w ** -0.5 * LOG2E)
    u_ref[0] = _dot(xb, w_ref[:, COL_POOL:COL_POOL + WCOL])


def _proj_branches(h3, w_bf16, tab_diff, tab_dil):
    b, s, d = h3.shape
    tm = QKV_TM
    qkv_spec = lambda r: pl.BlockSpec((1, r, tm // r, QKV_W), lambda bi, i: (bi, 0, i, 0))
    qkv_shape = lambda r: jax.ShapeDtypeStruct((b, r, s // r, QKV_W), BF16)
    tab_spec = pl.BlockSpec((3, tm, LANES), lambda bi, i: (0, i, 0))
    rs = [seg[1] for seg in QKV_SEGMENTS]
    outs = pl.pallas_call(
        _proj_kernel,
        grid=(b, s // tm),
        in_specs=[pl.BlockSpec((1, tm, d), lambda bi, i: (bi, i, 0)),
                  pl.BlockSpec(w_bf16.shape, lambda bi, i: (0, 0)),
                  tab_spec, tab_spec],
        out_specs=[qkv_spec(r) for r in rs] + [pl.BlockSpec((1, tm, WCOL), lambda bi, i: (bi, i, 0))],
        out_shape=[qkv_shape(r) for r in rs] + [jax.ShapeDtypeStruct((b, s, WCOL), F32)],
        scratch_shapes=[pltpu.VMEM((QKV_W // LANES, tm, LANES), F32)] * 2,
        compiler_params=_cparams(("parallel", "parallel")),
        name="proj_branches",
    )(h3, w_bf16, tab_diff, tab_dil)
    return [o.reshape(b, s, QKV_W) for o in outs[:-1]] + [outs[-1]]


def _rope_tables(seq, head_w):
    rot = head_w // 4
    half = rot // 2
    inv_freq = jnp.exp(jnp.arange(half, dtype=F32) * (-2.0 * math.log(ROPE_THETA) / rot))
    ang = jnp.arange(seq, dtype=jnp.int32).astype(F32)[:, None] * inv_freq[None, :]
    cos, sin = jnp.cos(ang), jnp.sin(ang)
    zero = jnp.zeros((seq, head_w - rot), F32)
    zh = jnp.zeros((seq, half), F32)
    t0 = jnp.concatenate([cos, cos, jnp.ones((seq, head_w - rot), F32)], axis=1)
    t1 = jnp.concatenate([-sin, zh, zero], axis=1)
    t2 = jnp.concatenate([zh, sin, zero], axis=1)
    reps = LANES // head_w
    return jnp.stack([jnp.tile(t0, (1, reps)), jnp.tile(t1, (1, reps)), jnp.tile(t2, (1, reps))])


def _rope(x, t_ref, half):
    return (x * t_ref[0] + pltpu.roll(x, LANES - half, 1) * t_ref[1]
            + pltpu.roll(x, half, 1) * t_ref[2])


LOG2E = math.log2(math.e)
LN2 = math.log(2.0)


def _na_bias_table(rpb):
    kr, kc = NA_ROWS, NA_COLS
    n_heads = rpb.shape[0]
    n_rows = GRID_W
    col = np.arange(GRID_W)
    col_start = np.clip(col - kc // 2, 0, GRID_W - kc)
    in_win = (col[None, :] >= col_start[:, None]) & (col[None, :] < col_start[:, None] + kc)
    dc = np.clip(col[None, :] - col[:, None] + kc - 1, 0, 2 * kc - 2)
    sel_r = np.zeros((3, NA_QROWS, NA_KROWS, 2 * kr - 1), np.float32)
    row_ok = np.zeros((3, NA_QROWS, NA_KROWS), bool)
    for case, blk in enumerate((0, 1, n_rows // NA_QROWS - 1)):
        ws = _na_window_start(blk, n_rows)
        for i in range(NA_QROWS):
            r = blk * NA_QROWS + i
            start = min(max(r - kr // 2, 0), n_rows - kr)
            for kk in range(NA_KROWS):
                if start <= ws + kk < start + kr:
                    row_ok[case, i, kk] = True
                    sel_r[case, i, kk, ws + kk - r + kr - 1] = 1.0
    sel_c = ((dc[..., None] == np.arange(2 * kc - 1)) & in_win[..., None]).astype(np.float32)
    hi = lax.Precision.HIGHEST
    rp = rpb.astype(F32).reshape(n_heads // 2, 2, 2 * kr - 1, 2 * kc - 1) * LOG2E
    t1 = jnp.einsum('phaj,cika->pchikj', rp, jnp.asarray(sel_r), precision=hi)
    bias = jnp.einsum('pchikj,qxj->pchiqkx', t1, jnp.asarray(sel_c), precision=hi)
    ok = row_ok[:, None, :, None, :, None] & in_win[None, None, None, :, None, :]
    bias = bias + jnp.asarray(np.where(ok, 0.0, NEG).astype(np.float32))
    return bias.reshape(n_heads // 2, 3, 2 * NA_QROWS * GRID_W, NA_KROWS * GRID_W)


NA_QROWS = 4
NA_KROWS = 12


def _na_window_start(blk, n_rows):
    lo = blk * NA_QROWS - NA_ROWS // 2
    hi = n_rows - NA_KROWS
    if isinstance(blk, int):
        return min(max(lo, 0), hi)
    return jnp.clip(lo, 0, hi)


def _stack_heads(q, lane):
    zero = jnp.zeros_like(q)
    return jnp.concatenate([jnp.where(lane < HEAD_DIM, q, zero), jnp.where(lane < HEAD_DIM, zero, q)],
                           axis=0)


def _na_kernel(q_ref, k_ref, v_ref, bias_ref, o_ref, *, blocks_per_step, n_rows):
    step = pl.program_id(2)
    nq = NA_QROWS * GRID_W
    kwin = NA_KROWS * GRID_W
    lane = _lane((nq, LANES))
    last_blk = n_rows // NA_QROWS - 1
    for i in range(blocks_per_step):
        blk = step * blocks_per_step + i
        case = jnp.where(blk == 0, 0, jnp.where(blk == last_blk, 2, 1))
        k0 = pl.multiple_of(_na_window_start(blk, n_rows) * GRID_W, GRID_W)
        q2 = _stack_heads(q_ref[0, i * nq:(i + 1) * nq, :], lane)
        sc = _dot_nt(q2, k_ref[0, pl.ds(k0, kwin), :]) + bias_ref[0, case]
        m = jnp.max(sc, axis=-1, keepdims=True)
        e = jnp.exp2(sc - m)
        den = jnp.sum(e, axis=-1, keepdims=True)
        pv = _dot(e.astype(BF16), v_ref[0, pl.ds(k0, kwin), :]) / den
        o = jnp.where(lane < HEAD_DIM, pv[0:nq], pv[nq:2 * nq])
        o_ref[0, i * nq:(i + 1) * nq, :] = o.astype(BF16)


def _na_attention(qkv, bias):
    b, s, _ = qkv.shape
    n_rows = s // GRID_W
    bps = 4
    rps = bps * NA_QROWS
    tq = rps * GRID_W
    return pl.pallas_call(
        functools.partial(_na_kernel, blocks_per_step=bps, n_rows=n_rows),
        grid=(b, 2, n_rows // rps),
        in_specs=[pl.BlockSpec((1, tq, LANES), lambda bi, hp, i: (bi, i, hp)),
                  pl.BlockSpec((1, s, LANES), lambda bi, hp, i: (bi, 0, 2 + hp)),
                  pl.BlockSpec((1, s, LANES), lambda bi, hp, i: (bi, 0, 4 + hp)),
                  pl.BlockSpec((1, 3, 2 * NA_QROWS * GRID_W, NA_KROWS * GRID_W),
                               lambda bi, hp, i: (hp, 0, 0, 0))],
        out_specs=pl.BlockSpec((1, tq, LANES), lambda bi, hp, i: (bi, i, hp)),
        out_shape=jax.ShapeDtypeStruct((b, s, 2 * LANES), BF16),
        compiler_params=_cparams(("parallel", "parallel", "arbitrary")),
        name="na_attn",
    )(qkv, qkv, qkv, bias)


def _diff_kernel(lam_ref, q_ref, k_ref, v_ref, g_ref, o_ref, *, lam_init, rows):
    tq = q_ref.shape[1]
    lane = _lane((rows, LANES))
    dl = lam_ref[...]
    lam = (jnp.exp(jnp.sum(dl[0:1] * dl[1:2], axis=-1, keepdims=True))
           - jnp.exp(jnp.sum(dl[2:3] * dl[3:4], axis=-1, keepdims=True)) + lam_init)
    v = v_ref[0]
    v_lane = _lane(v.shape)
    one = jnp.ones_like(v)
    v_ext = (jnp.where(v_lane < HEAD_DIM, v, one), jnp.where(v_lane < HEAD_DIM, one, v))
    in_h0 = lane < HEAD_DIM
    for r0 in range(0, tq, rows):
        q = q_ref[0, r0:r0 + rows, :]
        zero = jnp.zeros_like(q)
        pv = []
        for lo in range(0, LANES, DIFF_QK_DIM):
            qm = jnp.where((lane >= lo) & (lane < lo + DIFF_QK_DIM), q, zero)
            sc = _dot_nt(qm, k_ref[0])
            e = jnp.exp2(sc - jnp.max(sc, axis=-1, keepdims=True))
            ev = _dot(e.astype(BF16), v_ext[lo // HEAD_DIM])
            pv.append(ev / pltpu.roll(ev, HEAD_DIM, 1))
        o = jnp.where(in_h0, pv[0] - lam * pv[1], pv[2] - lam * pv[3])
        o2 = o * o
        ms0 = jnp.sum(jnp.where(in_h0, o2, 0.0), axis=-1, keepdims=True) / HEAD_DIM
        ms1 = jnp.sum(jnp.where(in_h0, 0.0, o2), axis=-1, keepdims=True) / HEAD_DIM
        ms = jnp.where(in_h0, ms0, ms1)
        o = o * lax.rsqrt(ms + LN_EPS) * g_ref[...] * (1.0 - lam_init)
        o_ref[0, r0:r0 + rows, :] = o.astype(BF16)


def _diff_attention(qkv, diff_lam, subln_g, lam_init):
    b, s, _ = qkv.shape
    tq = 1024
    g2 = jnp.tile(subln_g.reshape(1, HEAD_DIM), (1, 2))
    return pl.pallas_call(
        functools.partial(_diff_kernel, lam_init=lam_init, rows=512),
        grid=(b, 2, s // tq),
        in_specs=[pl.BlockSpec((4, DIFF_QK_DIM), lambda bi, hp, i: (0, 0)),
                  pl.BlockSpec((1, tq, LANES), lambda bi, hp, i: (bi, i, hp)),
                  pl.BlockSpec((1, s, LANES), lambda bi, hp, i: (bi, 0, 2 + hp)),
                  pl.BlockSpec((1, s, LANES), lambda bi, hp, i: (bi, 0, 4 + hp)),
                  pl.BlockSpec((1, LANES), lambda bi, hp, i: (0, 0))],
        out_specs=pl.BlockSpec((1, tq, LANES), lambda bi, hp, i: (bi, i, hp)),
        out_shape=jax.ShapeDtypeStruct((b, s, 2 * LANES), BF16),
        compiler_params=_cparams(("parallel", "parallel", "arbitrary")),
        name="diff_attn",
    )(diff_lam, qkv, qkv, qkv, g2)


POOL_PAD = 16
POOL_CHUNK = 512
POOL_HALO = 8


def _pool_kernel(u_ref, w_ref, sc_ref, o_ref, p_ref):
    s = u_ref.shape[1]
    width = u_ref.shape[2]
    p_ref[0:POOL_PAD, :] = jnp.zeros((POOL_PAD, width), F32)
    p_ref[POOL_PAD + s:POOL_PAD + s + POOL_PAD, :] = jnp.zeros((POOL_PAD, width), F32)
    p_ref[POOL_PAD:POOL_PAD + s, :] = u_ref[0]
    n = POOL_CHUNK + 2 * POOL_HALO
    lane = _lane((POOL_CHUNK, width))
    row = lax.broadcasted_iota(jnp.int32, (POOL_CHUNK, width), 0)
    w_of_lane = jnp.where(lane < POOL_GROUP, 2, jnp.where(lane < 2 * POOL_GROUP, 4,
                          jnp.where(lane < 3 * POOL_GROUP, 8, 16)))

    def body(ci, carry):
        c0 = pl.multiple_of(ci * POOL_CHUNK, POOL_CHUNK)
        x = p_ref[pl.ds(c0 + POOL_PAD - POOL_HALO, n), :]
        w2 = x + pltpu.roll(x, 1, 0)
        w4 = pltpu.roll(w2, 1, 0) + pltpu.roll(w2, n - 1, 0)
        w8 = pltpu.roll(w4, 2, 0) + pltpu.roll(w4, n - 2, 0)
        w16 = pltpu.roll(w8, 4, 0) + pltpu.roll(w8, n - 4, 0)
        u = x[POOL_HALO:POOL_HALO + POOL_CHUNK]
        wsum = jnp.where(lane < POOL_GROUP, w2[POOL_HALO:POOL_HALO + POOL_CHUNK],
                         jnp.where(lane < 2 * POOL_GROUP, w4[POOL_HALO:POOL_HALO + POOL_CHUNK],
                                   jnp.where(lane < 3 * POOL_GROUP, w8[POOL_HALO:POOL_HALO + POOL_CHUNK],
                                             w16[POOL_HALO:POOL_HALO + POOL_CHUNK])))
        t = row + c0
        half_w = w_of_lane // 2
        lo = jnp.maximum(t - half_w, 0)
        hi = jnp.minimum(t + w_of_lane - 1 - half_w, s - 1)
        cnt = (hi - lo + 1).astype(F32)
        dlt = wsum / cnt - u
        y = _dot(dlt.astype(BF16), w_ref[...]) * sc_ref[...]
        o_ref[0, pl.ds(c0, POOL_CHUNK), :] = y.astype(BF16)
        return carry

    lax.fori_loop(0, s // POOL_CHUNK, body, 0)


def _pool(u3, pool_w, pool_scale):
    b, s, width = u3.shape
    wbd = jax.scipy.linalg.block_diag(*[pool_w[g] for g in range(len(POOL_WINDOWS))]).astype(BF16)
    return pl.pallas_call(
        _pool_kernel,
        grid=(b,),
        in_specs=[pl.BlockSpec((1, s, width), lambda bi: (bi, 0, 0)),
                  pl.BlockSpec((width, width), lambda bi: (0, 0)),
                  pl.BlockSpec((1, width), lambda bi: (0, 0))],
        out_specs=pl.BlockSpec((1, s, width), lambda bi: (bi, 0, 0)),
        out_shape=jax.ShapeDtypeStruct((b, s, width), BF16),
        scratch_shapes=[pltpu.VMEM((s + 2 * POOL_PAD, width), F32)],
        compiler_params=_cparams(("parallel",)),
        name="pool",
    )(u3, wbd, pool_scale.reshape(1, width))


SWA_Q = 128
SWA_BAND = SWA_Q + 2 * DIL_HALF


def _swa_kernel(q_ref, k_ref, v_ref, o_ref, l_ref, *, length, chunk):
    c = pl.program_id(2)
    lane = _lane((SWA_Q, LANES))
    rel = (lax.broadcasted_iota(jnp.int32, (SWA_Q, SWA_BAND), 1)
           - lax.broadcasted_iota(jnp.int32, (SWA_Q, SWA_BAND), 0))
    for i in range(chunk // SWA_Q):
        r0 = c * chunk + i * SWA_Q
        l0 = r0 & (length - 1)
        lo = r0 - l0 + jnp.clip(l0 - DIL_HALF, 0, length - SWA_BAND)
        lo = pl.multiple_of(lo, DIL_HALF)
        q2 = _stack_heads(q_ref[0, i * SWA_Q:(i + 1) * SWA_Q, :], lane)
        d = rel + (lo - r0)
        valid = (d >= -DIL_HALF) & (d <= DIL_HALF)
        valid2 = jnp.concatenate([valid, valid], axis=0)
        sc = jnp.where(valid2, _dot_nt(q2, k_ref[0, pl.ds(lo, SWA_BAND), :]), NEG)
        m = jnp.max(sc, axis=-1, keepdims=True)
        e = jnp.exp2(sc - m)
        den = jnp.sum(e, axis=-1, keepdims=True)
        pv = _dot(e.astype(BF16), v_ref[0, pl.ds(lo, SWA_BAND), :]) / den
        lse2 = m + jnp.log2(den)
        o_ref[0, i * SWA_Q:(i + 1) * SWA_Q, :] = jnp.where(lane < HEAD_DIM, pv[0:SWA_Q], pv[SWA_Q:])
        l_ref[0, i * SWA_Q:(i + 1) * SWA_Q, :] = jnp.where(lane < HEAD_DIM, lse2[0:SWA_Q], lse2[SWA_Q:])


def _swa(qkv, length):
    b, s, _ = qkv.shape
    chunk = 1024
    shp = jax.ShapeDtypeStruct((b, s, 2 * LANES), F32)
    return pl.pallas_call(
        functools.partial(_swa_kernel, length=length, chunk=chunk),
        grid=(b, 2, s // chunk),
        in_specs=[pl.BlockSpec((1, chunk, LANES), lambda bi, hp, c: (bi, c, hp)),
                  pl.BlockSpec((1, s, LANES), lambda bi, hp, c: (bi, 0, 2 + hp)),
                  pl.BlockSpec((1, s, LANES), lambda bi, hp, c: (bi, 0, 4 + hp))],
        out_specs=[pl.BlockSpec((1, chunk, LANES), lambda bi, hp, c: (bi, c, hp)),
                   pl.BlockSpec((1, chunk, LANES), lambda bi, hp, c: (bi, c, hp))],
        out_shape=[shp, shp],
        compiler_params=_cparams(("parallel", "parallel", "arbitrary")),
        name=f"swa_l{length}",
    )(qkv, qkv, qkv)


def _dil_merge_kernel(o0_ref, l0_ref, o1_ref, l1_ref, o2_ref, l2_ref, y_ref, so1, sl1, so2, sl2):
    tm = y_ref.shape[1]
    r1 = DIL_PATTERNS[1][1]
    r2 = DIL_PATTERNS[2][1]
    for m in range(r1):
        so1[pl.ds(m, tm // r1, stride=r1), :] = o1_ref[0, m]
        sl1[pl.ds(m, tm // r1, stride=r1), :] = l1_ref[0, m]
    for m in range(r2):
        so2[pl.ds(m, tm // r2, stride=r2), :] = o2_ref[0, m]
        sl2[pl.ds(m, tm // r2, stride=r2), :] = l2_ref[0, m]
    l0, l1, l2 = l0_ref[0], sl1[...], sl2[...]
    mx = jnp.maximum(jnp.maximum(l0, l1), l2)
    e0, e1, e2 = jnp.exp2(l0 - mx), jnp.exp2(l1 - mx), jnp.exp2(l2 - mx)
    den = e0 + e1 + e2
    y = (e0 / den) * o0_ref[0] + (e1 / den) * so1[...] + (e2 / den) * so2[...]
    y_ref[0] = y.astype(BF16)


def _dil_merge(outs):
    (o0, l0), (o1, l1), (o2, l2) = outs
    b, s, w = o0.shape
    tm = 1024
    r1 = DIL_PATTERNS[1][1]
    r2 = DIL_PATTERNS[2][1]
    v1 = lambda a: a.reshape(b, r1, s // r1, w)
    v2 = lambda a: a.reshape(b, r2, s // r2, w)
    nat = pl.BlockSpec((1, tm, LANES), lambda bi, i, hp: (bi, i, hp))
    g1 = pl.BlockSpec((1, r1, tm // r1, LANES), lambda bi, i, hp: (bi, 0, i, hp))
    g2 = pl.BlockSpec((1, r2, tm // r2, LANES), lambda bi, i, hp: (bi, 0, i, hp))
    return pl.pallas_call(
        _dil_merge_kernel,
        grid=(b, s // tm, w // LANES),
        in_specs=[nat, nat, g1, g1, g2, g2],
        out_specs=nat,
        out_shape=jax.ShapeDtypeStruct((b, s, w), BF16),
        scratch_shapes=[pltpu.VMEM((tm, LANES), F32)] * 4,
        compiler_params=_cparams(("parallel", "parallel", "parallel")),
        name="dil_merge",
    )(o0, l0, v1(o1), v1(l1), v2(o2), v2(l2))


def _merge_kernel(ya_ref, yb_ref, yc_ref, yd_ref, wg_ref, bg_ref, wb_ref, wo_ref, h_ref, g_ref, b_ref,
                  o_ref, op_ref, *, alpha):
    d = h_ref.shape[1]
    hb = h_ref[...].astype(BF16)
    merged = None
    for n, y_ref in enumerate((ya_ref, yb_ref, yc_ref, yd_ref)):
        zg = _dot(hb, wg_ref[:, n * d:(n + 1) * d]) + bg_ref[:, n * d:(n + 1) * d]
        term = (1.0 / (1.0 + jnp.exp(-zg))) * _dot(y_ref[...], wb_ref[n])
        merged = term if merged is None else merged + term
    mix = _dot(merged.astype(BF16), wo_ref[...])
    h1 = _layer_norm(alpha * h_ref[...] + mix, g_ref[...], b_ref[...])
    o_ref[...] = h1
    op_ref[...] = _pack_bf16_pairs(h1)


def _merge(ys, wg, b_gate, wb, wo, h2, g, bb, alpha):
    t, d = h2.shape
    bw = ys[0].shape[1]
    nb = len(ys)
    tm = QKV_TM
    yspec = pl.BlockSpec((tm, bw), lambda i: (i, 0))
    vec = pl.BlockSpec((1, d), lambda i: (0, 0))
    return pl.pallas_call(
        functools.partial(_merge_kernel, alpha=alpha),
        grid=(t // tm,),
        in_specs=[yspec, yspec, yspec, yspec,
                  pl.BlockSpec((d, nb * d), lambda i: (0, 0)),
                  pl.BlockSpec((1, nb * d), lambda i: (0, 0)),
                  pl.BlockSpec((nb, bw, d), lambda i: (0, 0, 0)),
                  pl.BlockSpec((d, d), lambda i: (0, 0)),
                  pl.BlockSpec((tm, d), lambda i: (i, 0)),
                  vec, vec],
        out_specs=[pl.BlockSpec((tm, d), lambda i: (i, 0)),
                   pl.BlockSpec((tm, d // 2), lambda i: (i, 0))],
        out_shape=[jax.ShapeDtypeStruct((t, d), F32), jax.ShapeDtypeStruct((t, d // 2), jnp.int32)],
        compiler_params=_cparams(("parallel",)),
        name="merge",
    )(*ys, wg, b_gate.reshape(1, nb * d), wb, wo, h2, g.reshape(1, d), bb.reshape(1, d))


ROUTER_TM = 512


def _router_kernel(h_ref, w_ref, b_ref, o_ref, cnt_ref, carry, before):
    tm = h_ref.shape[0]

    @pl.when(pl.program_id(0) == 0)
    def _():
        carry[...] = jnp.zeros_like(carry)
        before[...] = (lax.broadcasted_iota(jnp.int32, (tm, tm), 0)
                       < lax.broadcasted_iota(jnp.int32, (tm, tm), 1)).astype(BF16)

    logits = lax.dot_general(w_ref[...], h_ref[...], (((1,), (1,)), ((), ())),
                             preferred_element_type=F32,
                             precision=lax.Precision.HIGHEST) + b_ref[...]
    row = lax.broadcasted_iota(jnp.int32, (LANES, tm), 0)
    big = jnp.int32(1 << 20)
    is_g = row < N_GROUPS
    gl = jnp.where(is_g, logits, -jnp.inf)
    gmax = jnp.max(gl, axis=0, keepdims=True)
    gsel = jnp.min(jnp.where(is_g & (gl == gmax), row, big), axis=0, keepdims=True)
    pg = 1.0 / jnp.sum(jnp.exp(gl - gmax), axis=0, keepdims=True)
    e_lo = N_GROUPS + gsel * EXPERTS_PER_GROUP
    in_grp = (row >= e_lo) & (row < e_lo + EXPERTS_PER_GROUP)
    el = jnp.where(in_grp, logits, -jnp.inf)
    v1 = jnp.max(el, axis=0, keepdims=True)
    i1 = jnp.min(jnp.where(in_grp & (el == v1), row, big), axis=0, keepdims=True)
    el2 = jnp.where(row == i1, -jnp.inf, el)
    v2 = jnp.max(el2, axis=0, keepdims=True)
    i2 = jnp.min(jnp.where(in_grp & (row != i1) & (el2 == v2), row, big), axis=0, keepdims=True)
    t2 = jnp.exp(v2 - v1)
    g1 = pg / (1.0 + t2)
    g2 = pg * t2 / (1.0 + t2)
    oh1 = (row == i1)
    oh2 = (row == i2)
    oh1b = oh1.astype(BF16)
    oh2b = oh2.astype(BF16)
    c0 = carry[...]
    c1 = c0 + jnp.sum(oh1b.astype(F32), axis=1, keepdims=True)
    rank1 = jnp.sum(jnp.where(oh1, _dot(oh1b, before[...]) + c0, 0.0), axis=0, keepdims=True)
    rank2 = jnp.sum(jnp.where(oh2, _dot(oh2b, before[...]) + c1, 0.0), axis=0, keepdims=True)
    c2 = c1 + jnp.sum(oh2b.astype(F32), axis=1, keepdims=True)
    carry[...] = c2
    cnt_ref[...] = c2
    e1 = (i1 - N_GROUPS).astype(F32)
    e2 = (i2 - N_GROUPS).astype(F32)
    out_t = jnp.where(row == 0, e1, jnp.where(row == 1, e2, jnp.where(row == 2, g1, jnp.where(
        row == 3, g2, jnp.where(row == 4, rank1, jnp.where(row == 5, rank2, 0.0))))))
    o_ref[...] = out_t.T


def _router(h2, rg_w, rg_b, re_w, re_b):
    t, d = h2.shape
    pad = LANES - N_GROUPS - N_EXPERTS
    w_t = jnp.concatenate([rg_w.T, re_w.T, jnp.zeros((pad, d), F32)], axis=0)
    bias = jnp.concatenate([rg_b, re_b, jnp.zeros((pad,), F32)]).reshape(LANES, 1)
    tm = ROUTER_TM
    return pl.pallas_call(
        _router_kernel,
        grid=(t // tm,),
        in_specs=[pl.BlockSpec((tm, d), lambda i: (i, 0)),
                  pl.BlockSpec((LANES, d), lambda i: (0, 0)),
                  pl.BlockSpec((LANES, 1), lambda i: (0, 0))],
        out_specs=[pl.BlockSpec((tm, LANES), lambda i: (i, 0)),
                   pl.BlockSpec((LANES, 1), lambda i: (0, 0))],
        out_shape=[jax.ShapeDtypeStruct((t, LANES), F32), jax.ShapeDtypeStruct((LANES, 1), F32)],
        scratch_shapes=[pltpu.VMEM((LANES, 1), F32), pltpu.VMEM((tm, tm), BF16)],
        compiler_params=_cparams(("arbitrary",)),
        name="router",
    )(h2, w_t, bias)


def _moe_kernel(be_ref, nv_ref, x_ref, wg_ref, wu_ref, wd_ref, o_ref, wgb, wub, wdb):
    j = pl.program_id(0)
    n_valid = nv_ref[j]
    new_expert = (j == 0) | (be_ref[j] != be_ref[jnp.maximum(j - 1, 0)])

    @pl.when(new_expert)
    def _():
        wgb[...] = wg_ref[0].astype(BF16)
        wub[...] = wu_ref[0].astype(BF16)
        wdb[...] = wd_ref[0].astype(BF16)

    @pl.when(n_valid > 0)
    def _():
        row = lax.broadcasted_iota(jnp.int32, x_ref.shape, 0)
        x = _unpack_bf16_pairs(jnp.where(row < n_valid, x_ref[...], 0)).astype(BF16)
        g = _dot(x, wgb[...])
        u = _dot(x, wub[...])
        hmid = (g * (1.0 / (1.0 + jnp.exp(-g)))) * u
        o_ref[...] = _pack_bf16_pairs(_dot(hmid.astype(BF16), wdb[...]))

    @pl.when(n_valid <= 0)
    def _():
        o_ref[...] = jnp.zeros_like(o_ref)


def _moe_experts(xs, block_e, n_valid, wg, wu, wd, l):
    cap, dp = xs.shape
    d = 2 * dp
    de = wg.shape[3]
    n_blocks = cap // MOE_ROWS
    grid_spec = pltpu.PrefetchScalarGridSpec(
        num_scalar_prefetch=2,
        grid=(n_blocks,),
        in_specs=[pl.BlockSpec((MOE_ROWS, dp), lambda j, be, nv: (j, 0)),
                  pl.BlockSpec((None, 1, d, de), lambda j, be, nv: (l, be[j], 0, 0)),
                  pl.BlockSpec((None, 1, d, de), lambda j, be, nv: (l, be[j], 0, 0)),
                  pl.BlockSpec((None, 1, de, d), lambda j, be, nv: (l, be[j], 0, 0))],
        out_specs=pl.BlockSpec((MOE_ROWS, dp), lambda j, be, nv: (j, 0)),
        scratch_shapes=[pltpu.VMEM((d, de), BF16), pltpu.VMEM((d, de), BF16), pltpu.VMEM((de, d), BF16)],
    )
    return pl.pallas_call(
        _moe_kernel,
        grid_spec=grid_spec,
        out_shape=jax.ShapeDtypeStruct((cap, dp), jnp.int32),
        compiler_params=_cparams(("arbitrary",)),
        name="moe_experts",
    )(block_e, n_valid, xs, wg, wu, wd)


SC_CORES = 2
SC_SUBCORES = 16
SC_WORKERS = SC_CORES * SC_SUBCORES
SC_CHUNK = 64


def _sc_mesh():
    return plsc.VectorSubcoreMesh(core_axis_name="c", subcore_axis_name="s",
                                  num_cores=SC_CORES, num_subcores=SC_SUBCORES)


def _sc_scatter_rows(src, idx0, idx1, n_out):
    t, d = src.shape
    per_w = t // SC_WORKERS
    n_chunks = per_w // SC_CHUNK
    idx_shape = (SC_WORKERS, n_chunks, SC_CHUNK)

    @functools.partial(
        pl.kernel, mesh=_sc_mesh(),
        out_type=jax.ShapeDtypeStruct((n_out, d), src.dtype),
        scratch_types=[pltpu.VMEM((n_chunks, SC_CHUNK), jnp.int32),
                       pltpu.VMEM((n_chunks, SC_CHUNK), jnp.int32),
                       pltpu.VMEM((2, SC_CHUNK, d), src.dtype),
                       pltpu.SemaphoreType.DMA((2,)), pltpu.SemaphoreType.DMA((2,))],
        name="sc_dispatch",
    )
    def k(src_hbm, i0_hbm, i1_hbm, out_hbm, i0_v, i1_v, rows_v, sem_in, sem_out):
        wid = lax.axis_index("s") * SC_CORES + lax.axis_index("c")
        base = wid * per_w
        pltpu.sync_copy(i0_hbm.at[wid], i0_v)
        pltpu.sync_copy(i1_hbm.at[wid], i1_v)

        def load(ci):
            s = ci % 2
            return pltpu.make_async_copy(src_hbm.at[pl.ds(base + ci * SC_CHUNK, SC_CHUNK)],
                                         rows_v.at[s], sem_in.at[s])

        def scatter(ci, idx_v):
            s = ci % 2
            return pltpu.make_async_copy(rows_v.at[s], out_hbm.at[idx_v.at[ci]], sem_out.at[s])

        load(0).start()
        for ci in range(n_chunks):
            load(ci).wait()
            if ci >= 1:
                scatter(ci - 1, i0_v).wait()
                scatter(ci - 1, i1_v).wait()
            if ci + 1 < n_chunks:
                load(ci + 1).start()
            scatter(ci, i0_v).start()
            scatter(ci, i1_v).start()
        scatter(n_chunks - 1, i0_v).wait()
        scatter(n_chunks - 1, i1_v).wait()

    return k(src, idx0.reshape(idx_shape), idx1.reshape(idx_shape))


def _sc_gather_rows(table, idx):
    n = idx.shape[0]
    d = table.shape[1]
    per_w = n // SC_WORKERS
    n_chunks = per_w // SC_CHUNK

    @functools.partial(
        pl.kernel, mesh=_sc_mesh(),
        out_type=jax.ShapeDtypeStruct((n, d), table.dtype),
        scratch_types=[pltpu.VMEM((n_chunks, SC_CHUNK), jnp.int32),
                       pltpu.VMEM((2, SC_CHUNK, d), table.dtype),
                       pltpu.SemaphoreType.DMA((2,)), pltpu.SemaphoreType.DMA((2,))],
        name="sc_collect",
    )
    def k(table_hbm, idx_hbm, out_hbm, idx_v, rows_v, sem_in, sem_out):
        wid = lax.axis_index("s") * SC_CORES + lax.axis_index("c")
        base = wid * per_w
        pltpu.sync_copy(idx_hbm.at[wid], idx_v)

        def gather(ci):
            s = ci % 2
            return pltpu.make_async_copy(table_hbm.at[idx_v.at[ci]], rows_v.at[s], sem_in.at[s])

        def store(ci):
            s = ci % 2
            return pltpu.make_async_copy(rows_v.at[s], out_hbm.at[pl.ds(base + ci * SC_CHUNK, SC_CHUNK)],
                                         sem_out.at[s])

        gather(0).start()
        for ci in range(n_chunks):
            gather(ci).wait()
            if ci >= 1:
                store(ci - 1).wait()
            if ci + 1 < n_chunks:
                gather(ci + 1).start()
            store(ci).start()
        store(n_chunks - 1).wait()

    return k(table, idx.reshape(SC_WORKERS, n_chunks, SC_CHUNK))


def _combine_kernel(h_ref, r_ref, y1_ref, y2_ref, g_ref, b_ref, o_ref, *, alpha):
    g1 = r_ref[:, 2:3]
    g2 = r_ref[:, 3:4]
    ffn = g1 * _unpack_bf16_pairs(y1_ref[...]) + g2 * _unpack_bf16_pairs(y2_ref[...])
    o_ref[...] = _layer_norm(alpha * h_ref[...] + ffn, g_ref[...], b_ref[...])


def _combine(h2, routing, y12, g, bb, alpha):
    t, d = h2.shape
    tm = 512
    nt = t // tm
    row = pl.BlockSpec((tm, d), lambda i: (i, 0))
    vec = pl.BlockSpec((1, d), lambda i: (0, 0))
    return pl.pallas_call(
        functools.partial(_combine_kernel, alpha=alpha),
        grid=(nt,),
        in_specs=[row, pl.BlockSpec((tm, LANES), lambda i: (i, 0)),
                  pl.BlockSpec((tm, d // 2), lambda i: (i, 0)),
                  pl.BlockSpec((tm, d // 2), lambda i: (nt + i, 0)), vec, vec],
        out_specs=row,
        out_shape=jax.ShapeDtypeStruct((t, d), F32),
        compiler_params=_cparams(("parallel",)),
        name="combine_ln",
    )(h2, routing, y12, y12, g.reshape(1, d), bb.reshape(1, d))


def _moe(h2, h2_packed, rg_w, rg_b, re_w, re_b, wg, wu, wd, l, g, bb, alpha):
    t, d = h2.shape
    routing, counts = _router(h2, rg_w, rg_b, re_w, re_b)
    eid = routing[:, 0:2].astype(jnp.int32)
    rank = routing[:, 4:6].astype(jnp.int32)
    cnt = counts[N_GROUPS:N_GROUPS + N_EXPERTS, 0].astype(jnp.int32)
    padded = (cnt + MOE_ROWS - 1) // MOE_ROWS * MOE_ROWS
    pad_end = jnp.cumsum(padded)
    pad_start = pad_end - padded
    experts = jnp.arange(N_EXPERTS, dtype=jnp.int32)
    dest = jnp.sum(jnp.where(eid[..., None] == experts, pad_start, 0), axis=-1) + rank
    n_slots = t * TOP_K
    n_blocks = (n_slots + N_EXPERTS * (MOE_ROWS - 1) + MOE_ROWS - 1) // MOE_ROWS
    cap = n_blocks * MOE_ROWS
    blk_row = jnp.arange(n_blocks, dtype=jnp.int32) * MOE_ROWS
    block_e = jnp.minimum(jnp.sum((pad_end[None, :] <= blk_row[:, None]).astype(jnp.int32), axis=1),
                          N_EXPERTS - 1)
    is_e = block_e[:, None] == experts
    blk_cnt = jnp.sum(jnp.where(is_e, cnt, 0), axis=1)
    blk_start = jnp.sum(jnp.where(is_e, pad_start, 0), axis=1)
    n_valid = jnp.clip(blk_cnt - (blk_row - blk_start), 0, MOE_ROWS).astype(jnp.int32)
    xs = _sc_scatter_rows(h2_packed, dest[:, 0], dest[:, 1], cap)
    ys = _moe_experts(xs, block_e, n_valid, wg, wu, wd, l)
    y12 = _sc_gather_rows(ys, jnp.concatenate([dest[:, 0], dest[:, 1]]))
    return _combine(h2, routing, y12, g, bb, alpha)


def _mixing_layer(h, b, s, l, p, lam_init):
    t = b * s
    alpha = (2 * p['w_in'].shape[0]) ** 0.25
    d = h.shape[1]
    w_in = p['w_in']
    tab_diff = _rope_tables(s, DIFF_QK_DIM)
    tab_dil = _rope_tables(s, HEAD_DIM)
    h3 = h.reshape(b, s, d)
    w_left = lax.slice_in_dim(w_in[l], 0, COL_GATE, axis=1).astype(BF16)
    qkv_na, qkv_diff, qkv_d0, qkv_d1, qkv_d2, u = _proj_branches(h3, w_left, tab_diff, tab_dil)
    y_a = _na_attention(qkv_na, _na_bias_table(p['na_rpb'][l]))
    y_b = _diff_attention(qkv_diff, p['diff_lam'][l], p['diff_subln_g'][l], lam_init)
    y_c = _pool(u, p['pool_w'][l], p['pool_scale'][l])
    y_d = _dil_merge([_swa(qkv, s // r)
                      for qkv, (_, r) in zip((qkv_d0, qkv_d1, qkv_d2), DIL_PATTERNS)])
    ys = [a.reshape(t, -1) for a in (y_a, y_b, y_c, y_d)]
    wg = lax.slice_in_dim(w_in[l], COL_GATE, COL_GATE + len(ys) * d, axis=1).astype(BF16)
    return _merge(ys, wg, p['b_gate'][l], p['w_branch'][l].astype(BF16), p['w_out'][l].astype(BF16), h,
                  p['ln1_g'][l], p['ln1_b'][l], alpha)


def kernel(x, emb_ln_g, emb_ln_b, w_in, b_gate, na_rpb, diff_lam, diff_subln_g, pool_w, pool_scale,
           w_branch, w_out, ln1_g, ln1_b, router_group_w, router_group_b, router_expert_w,
           router_expert_b, expert_w_gate, expert_w_up, expert_w_down, ln2_g, ln2_b):
    b, s, d = x.shape
    depth = w_in.shape[0]
    alpha = (2 * depth) ** 0.25
    p = dict(w_in=w_in, b_gate=b_gate, na_rpb=na_rpb, diff_lam=diff_lam, diff_subln_g=diff_subln_g,
             pool_w=pool_w, pool_scale=pool_scale, w_branch=w_branch, w_out=w_out, ln1_g=ln1_g, ln1_b=ln1_b)
    h = _embed_ln(x.reshape(b * s, d), emb_ln_g, emb_ln_b)
    for l in range(depth):
        lam_init = 0.8 - 0.6 * math.exp(-0.3 * l)
        h, h_packed = _mixing_layer(h, b, s, l, p, lam_init)
        h = _moe(h, h_packed, router_group_w[l], router_group_b[l], router_expert_w[l], router_expert_b[l],
                 expert_w_gate, expert_w_up, expert_w_down, l, ln2_g[l], ln2_b[l], alpha)
    return h.reshape(b, s, d)
```

```python
import functools
import math

import jax
import jax.numpy as jnp
import numpy as np
from jax import lax
from jax.experimental import pallas as pl
from jax.experimental.pallas import tpu as pltpu
from jax.experimental.pallas import tpu_sc as plsc

F32 = jnp.float32
BF16 = jnp.bfloat16

LANES = 128
GRID_W = 64
HEAD_DIM = 64
ROPE_THETA = 500000.0
LN_EPS = 1e-5
NA_ROWS = 8
NA_COLS = 16
DIFF_QK_DIM = 32
POOL_WINDOWS = (2, 4, 8, 16)
POOL_GROUP = 64
DIL_PATTERNS = ((128, 1), (512, 4), (2048, 16))
DIL_HALF = 64
N_GROUPS = 4
EXPERTS_PER_GROUP = 8
N_EXPERTS = N_GROUPS * EXPERTS_PER_GROUP
TOP_K = 2
MOE_ROWS = 512
NEG = -1e30

COL_NA = 0
COL_DIFF = 768
COL_POOL = 1536
COL_DIL = 1792
COL_GATE = 4096

VMEM_LIMIT = 56 * 1024 * 1024


def _cparams(sem, vmem=VMEM_LIMIT, flags=None):
    return pltpu.CompilerParams(dimension_semantics=sem, vmem_limit_bytes=vmem, flags=flags)


def _layer_norm(x, g, b):
    mu = jnp.mean(x, axis=-1, keepdims=True)
    xc = x - mu
    var = jnp.mean(xc * xc, axis=-1, keepdims=True)
    return xc * lax.rsqrt(var + LN_EPS) * g + b


def _dot(a, b):
    return jnp.dot(a, b, preferred_element_type=F32)


def _dot_nt(a, b):
    return lax.dot_general(a, b, (((1,), (1,)), ((), ())), preferred_element_type=F32)


def _lane(shape):
    return lax.broadcasted_iota(jnp.int32, shape, len(shape) - 1)


HI16 = -65536


def _pack_bf16_pairs(x):
    w = x.shape[1] // 2
    hi = lax.bitcast_convert_type(x[:, :w].astype(BF16).astype(F32), jnp.int32)
    lo = lax.bitcast_convert_type(x[:, w:].astype(BF16).astype(F32), jnp.int32)
    return (hi & HI16) | lax.shift_right_logical(lo, 16)


def _unpack_bf16_pairs(p):
    hi = lax.bitcast_convert_type(p & HI16, F32)
    lo = lax.bitcast_convert_type(lax.shift_left(p, 16), F32)
    return jnp.concatenate([hi, lo], axis=1)


def _ln_kernel(x_ref, g_ref, b_ref, o_ref):
    o_ref[...] = _layer_norm(x_ref[...], g_ref[...], b_ref[...])


def _embed_ln(x2, g, b):
    t, d = x2.shape
    tm = 1024
    return pl.pallas_call(
        _ln_kernel,
        grid=(t // tm,),
        in_specs=[pl.BlockSpec((tm, d), lambda i: (i, 0)),
                  pl.BlockSpec((1, d), lambda i: (0, 0)),
                  pl.BlockSpec((1, d), lambda i: (0, 0))],
        out_specs=pl.BlockSpec((tm, d), lambda i: (i, 0)),
        out_shape=jax.ShapeDtypeStruct((t, d), F32),
        compiler_params=_cparams(("parallel",)),
        name="embed_ln",
    )(x2, g.reshape(1, d), b.reshape(1, d))


PROJ_TM = 1024
QKV_W = 6 * LANES
WCOL = 256


QKV_TM = 512
QKV_SEGMENTS = ((COL_NA, 1, 0, HEAD_DIM), (COL_DIFF, 1, DIFF_QK_DIM // 8, DIFF_QK_DIM),
                (COL_DIL, 1, HEAD_DIM // 8, HEAD_DIM), (COL_DIL + QKV_W, 4, HEAD_DIM // 8, HEAD_DIM),
                (COL_DIL + 2 * QKV_W, 16, HEAD_DIM // 8, HEAD_DIM))


def _qkv_epilogue(z, t_ref, o_ref, zs_ref, r, half, q_scale):
    tm = z.shape[0]
    n = tm // r
    for c in range(QKV_W // LANES):
        blk = z[:, c * LANES:(c + 1) * LANES]
        if half and c < 4:
            blk = _rope(blk, t_ref, half)
        if c < 2:
            blk = blk * q_scale
        if r == 1:
            o_ref[0, 0, :, c * LANES:(c + 1) * LANES] = blk.astype(BF16)
        else:
            zs_ref[c] = blk
            for m in range(r):
                o_ref[0, m, :, c * LANES:(c + 1) * LANES] = zs_ref[
                    c, pl.ds(m, n, stride=r), :].astype(BF16)


def _proj_kernel(x_ref, w_ref, td_ref, tl_ref, na_ref, df_ref, d0_ref, d1_ref, d2_ref, u_ref,
                 zs1_ref, zs2_ref):
    xb = x_ref[0].astype(BF16)
    outs = (na_ref, df_ref, d0_ref, d1_ref, d2_ref)
    stage = (None, None, None, zs1_ref, zs2_ref)
    for (col0, r, half, head_w), o_ref, zs_ref in zip(QKV_SEGMENTS, outs, stage):
        z = _dot(xb, w_ref[:, col0:col0 + QKV_W])
        t_ref = td_ref if head_w == DIFF_QK_DIM else tl_ref
        _qkv_epilogue(z, t_ref, o_ref, zs_ref, r, half, head_w ** -0.5 * LOG2E)
    u_ref[0] = _dot(xb, w_ref[:, COL_POOL:COL_POOL + WCOL])


def _proj_branches(h3, w_bf16, tab_diff, tab_dil):
    b, s, d = h3.shape
    tm = QKV_TM
    qkv_spec = lambda r: pl.BlockSpec((1, r, tm // r, QKV_W), lambda bi, i: (bi, 0, i, 0))
    qkv_shape = lambda r: jax.ShapeDtypeStruct((b, r, s // r, QKV_W), BF16)
    tab_spec = pl.BlockSpec((3, tm, LANES), lambda bi, i: (0, i, 0))
    rs = [seg[1] for seg in QKV_SEGMENTS]
    outs = pl.pallas_call(
        _proj_kernel,
        grid=(b, s // tm),
        in_specs=[pl.BlockSpec((1, tm, d), lambda bi, i: (bi, i, 0)),
                  pl.BlockSpec(w_bf16.shape, lambda bi, i: (0, 0)),
                  tab_spec, tab_spec],
        out_specs=[qkv_spec(r) for r in rs] + [pl.BlockSpec((1, tm, WCOL), lambda bi, i: (bi, i, 0))],
        out_shape=[qkv_shape(r) for r in rs] + [jax.ShapeDtypeStruct((b, s, WCOL), F32)],
        scratch_shapes=[pltpu.VMEM((QKV_W // LANES, tm, LANES), F32)] * 2,
        compiler_params=_cparams(("parallel", "parallel")),
        name="proj_branches",
    )(h3, w_bf16, tab_diff, tab_dil)
    return [o.reshape(b, s, QKV_W) for o in outs[:-1]] + [outs[-1]]


def _rope_tables(seq, head_w):
    rot = head_w // 4
    half = rot // 2
    inv_freq = jnp.exp(jnp.arange(half, dtype=F32) * (-2.0 * math.log(ROPE_THETA) / rot))
    ang = jnp.arange(seq, dtype=jnp.int32).astype(F32)[:, None] * inv_freq[None, :]
    cos, sin = jnp.cos(ang), jnp.sin(ang)
    zero = jnp.zeros((seq, head_w - rot), F32)
    zh = jnp.zeros((seq, half), F32)
    t0 = jnp.concatenate([cos, cos, jnp.ones((seq, head_w - rot), F32)], axis=1)
    t1 = jnp.concatenate([-sin, zh, zero], axis=1)
    t2 = jnp.concatenate([zh, sin, zero], axis=1)
    reps = LANES // head_w
    return jnp.stack([jnp.tile(t0, (1, reps)), jnp.tile(t1, (1, reps)), jnp.tile(t2, (1, reps))])


def _rope(x, t_ref, half):
    return (x * t_ref[0] + pltpu.roll(x, LANES - half, 1) * t_ref[1]
            + pltpu.roll(x, half, 1) * t_ref[2])


LOG2E = math.log2(math.e)
LN2 = math.log(2.0)


def _na_bias_table(rpb):
    kr, kc = NA_ROWS, NA_COLS
    n_heads = rpb.shape[0]
    n_rows = GRID_W
    col = np.arange(GRID_W)
    col_start = np.clip(col - kc // 2, 0, GRID_W - kc)
    in_win = (col[None, :] >= col_start[:, None]) & (col[None, :] < col_start[:, None] + kc)
    dc = np.clip(col[None, :] - col[:, None] + kc - 1, 0, 2 * kc - 2)
    sel_r = np.zeros((3, NA_QROWS, NA_KROWS, 2 * kr - 1), np.float32)
    row_ok = np.zeros((3, NA_QROWS, NA_KROWS), bool)
    for case, blk in enumerate((0, 1, n_rows // NA_QROWS - 1)):
        ws = _na_window_start(blk, n_rows)
        for i in range(NA_QROWS):
            r = blk * NA_QROWS + i
            start = min(max(r - kr // 2, 0), n_rows - kr)
            for kk in range(NA_KROWS):
                if start <= ws + kk < start + kr:
                    row_ok[case, i, kk] = True
                    sel_r[case, i, kk, ws + kk - r + kr - 1] = 1.0
    sel_c = ((dc[..., None] == np.arange(2 * kc - 1)) & in_win[..., None]).astype(np.float32)
    hi = lax.Precision.HIGHEST
    rp = rpb.astype(F32).reshape(n_heads // 2, 2, 2 * kr - 1, 2 * kc - 1) * LOG2E
    t1 = jnp.einsum('phaj,cika->pchikj', rp, jnp.asarray(sel_r), precision=hi)
    bias = jnp.einsum('pchikj,qxj->pchiqkx', t1, jnp.asarray(sel_c), precision=hi)
    ok = row_ok[:, None, :, None, :, None] & in_win[None, None, None, :, None, :]
    bias = bias + jnp.asarray(np.where(ok, 0.0, NEG).astype(np.float32))
    return bias.reshape(n_heads // 2, 3, 2 * NA_QROWS * GRID_W, NA_KROWS * GRID_W)


NA_QROWS = 4
NA_KROWS = 12


def _na_window_start(blk, n_rows):
    lo = blk * NA_QROWS - NA_ROWS // 2
    hi = n_rows - NA_KROWS
    if isinstance(blk, int):
        return min(max(lo, 0), hi)
    return jnp.clip(lo, 0, hi)


def _stack_heads(q, lane):
    zero = jnp.zeros_like(q)
    return jnp.concatenate([jnp.where(lane < HEAD_DIM, q, zero), jnp.where(lane < HEAD_DIM, zero, q)],
                           axis=0)


def _na_kernel(q_ref, k_ref, v_ref, bias_ref, o_ref, *, blocks_per_step, n_rows):
    step = pl.program_id(2)
    nq = NA_QROWS * GRID_W
    kwin = NA_KROWS * GRID_W
    lane = _lane((nq, LANES))
    last_blk = n_rows // NA_QROWS - 1
    for i in range(blocks_per_step):
        blk = step * blocks_per_step + i
        case = jnp.where(blk == 0, 0, jnp.where(blk == last_blk, 2, 1))
        k0 = pl.multiple_of(_na_window_start(blk, n_rows) * GRID_W, GRID_W)
        q2 = _stack_heads(q_ref[0, i * nq:(i + 1) * nq, :], lane)
        sc = _dot_nt(q2, k_ref[0, pl.ds(k0, kwin), :]) + bias_ref[0, case]
        m = jnp.max(sc, axis=-1, keepdims=True)
        e = jnp.exp2(sc - m)
        den = jnp.sum(e, axis=-1, keepdims=True)
        pv = _dot(e.astype(BF16), v_ref[0, pl.ds(k0, kwin), :]) / den
        o = jnp.where(lane < HEAD_DIM, pv[0:nq], pv[nq:2 * nq])
        o_ref[0, i * nq:(i + 1) * nq, :] = o.astype(BF16)


def _na_attention(qkv, bias):
    b, s, _ = qkv.shape
    n_rows = s // GRID_W
    bps = 4
    rps = bps * NA_QROWS
    tq = rps * GRID_W
    return pl.pallas_call(
        functools.partial(_na_kernel, blocks_per_step=bps, n_rows=n_rows),
        grid=(b, 2, n_rows // rps),
        in_specs=[pl.BlockSpec((1, tq, LANES), lambda bi, hp, i: (bi, i, hp)),
                  pl.BlockSpec((1, s, LANES), lambda bi, hp, i: (bi, 0, 2 + hp)),
                  pl.BlockSpec((1, s, LANES), lambda bi, hp, i: (bi, 0, 4 + hp)),
                  pl.BlockSpec((1, 3, 2 * NA_QROWS * GRID_W, NA_KROWS * GRID_W),
                               lambda bi, hp, i: (hp, 0, 0, 0))],
        out_specs=pl.BlockSpec((1, tq, LANES), lambda bi, hp, i: (bi, i, hp)),
        out_shape=jax.ShapeDtypeStruct((b, s, 2 * LANES), BF16),
        compiler_params=_cparams(("parallel", "parallel", "arbitrary")),
        name="na_attn",
    )(qkv, qkv, qkv, bias)


def _diff_kernel(lam_ref, q_ref, k_ref, v_ref, g_ref, o_ref, *, lam_init, rows):
    tq = q_ref.shape[1]
    lane = _lane((rows, LANES))
    dl = lam_ref[...]
    lam = (jnp.exp(jnp.sum(dl[0:1] * dl[1:2], axis=-1, keepdims=True))
           - jnp.exp(jnp.sum(dl[2:3] * dl[3:4], axis=-1, keepdims=True)) + lam_init)
    v = v_ref[0]
    v_lane = _lane(v.shape)
    one = jnp.ones_like(v)
    v_ext = (jnp.where(v_lane < HEAD_DIM, v, one), jnp.where(v_lane < HEAD_DIM, one, v))
    in_h0 = lane < HEAD_DIM
    for r0 in range(0, tq, rows):
        q = q_ref[0, r0:r0 + rows, :]
        zero = jnp.zeros_like(q)
        pv = []
        for lo in range(0, LANES, DIFF_QK_DIM):
            qm = jnp.where((lane >= lo) & (lane < lo + DIFF_QK_DIM), q, zero)
            sc = _dot_nt(qm, k_ref[0])
            e = jnp.exp2(sc - jnp.max(sc, axis=-1, keepdims=True))
            ev = _dot(e.astype(BF16), v_ext[lo // HEAD_DIM])
            pv.append(ev / pltpu.roll(ev, HEAD_DIM, 1))
        o = jnp.where(in_h0, pv[0] - lam * pv[1], pv[2] - lam * pv[3])
        o2 = o * o
        ms0 = jnp.sum(jnp.where(in_h0, o2, 0.0), axis=-1, keepdims=True) / HEAD_DIM
        ms1 = jnp.sum(jnp.where(in_h0, 0.0, o2), axis=-1, keepdims=True) / HEAD_DIM
        ms = jnp.where(in_h0, ms0, ms1)
        o = o * lax.rsqrt(ms + LN_EPS) * g_ref[...] * (1.0 - lam_init)
        o_ref[0, r0:r0 + rows, :] = o.astype(BF16)


def _diff_attention(qkv, diff_lam, subln_g, lam_init):
    b, s, _ = qkv.shape
    tq = 1024
    g2 = jnp.tile(subln_g.reshape(1, HEAD_DIM), (1, 2))
    return pl.pallas_call(
        functools.partial(_diff_kernel, lam_init=lam_init, rows=512),
        grid=(b, 2, s // tq),
        in_specs=[pl.BlockSpec((4, DIFF_QK_DIM), lambda bi, hp, i: (0, 0)),
                  pl.BlockSpec((1, tq, LANES), lambda bi, hp, i: (bi, i, hp)),
                  pl.BlockSpec((1, s, LANES), lambda bi, hp, i: (bi, 0, 2 + hp)),
                  pl.BlockSpec((1, s, LANES), lambda bi, hp, i: (bi, 0, 4 + hp)),
                  pl.BlockSpec((1, LANES), lambda bi, hp, i: (0, 0))],
        out_specs=pl.BlockSpec((1, tq, LANES), lambda bi, hp, i: (bi, i, hp)),
        out_shape=jax.ShapeDtypeStruct((b, s, 2 * LANES), BF16),
        compiler_params=_cparams(("parallel", "parallel", "arbitrary")),
        name="diff_attn",
    )(diff_lam, qkv, qkv, qkv, g2)


POOL_PAD = 16
POOL_CHUNK = 512
POOL_HALO = 8


def _pool_kernel(u_ref, w_ref, sc_ref, o_ref, p_ref):
    s = u_ref.shape[1]
    width = u_ref.shape[2]
    p_ref[0:POOL_PAD, :] = jnp.zeros((POOL_PAD, width), F32)
    p_ref[POOL_PAD + s:POOL_PAD + s + POOL_PAD, :] = jnp.zeros((POOL_PAD, width), F32)
    p_ref[POOL_PAD:POOL_PAD + s, :] = u_ref[0]
    n = POOL_CHUNK + 2 * POOL_HALO
    lane = _lane((POOL_CHUNK, width))
    row = lax.broadcasted_iota(jnp.int32, (POOL_CHUNK, width), 0)
    w_of_lane = jnp.where(lane < POOL_GROUP, 2, jnp.where(lane < 2 * POOL_GROUP, 4,
                          jnp.where(lane < 3 * POOL_GROUP, 8, 16)))

    def body(ci, carry):
        c0 = pl.multiple_of(ci * POOL_CHUNK, POOL_CHUNK)
        x = p_ref[pl.ds(c0 + POOL_PAD - POOL_HALO, n), :]
        w2 = x + pltpu.roll(x, 1, 0)
        w4 = pltpu.roll(w2, 1, 0) + pltpu.roll(w2, n - 1, 0)
        w8 = pltpu.roll(w4, 2, 0) + pltpu.roll(w4, n - 2, 0)
        w16 = pltpu.roll(w8, 4, 0) + pltpu.roll(w8, n - 4, 0)
        u = x[POOL_HALO:POOL_HALO + POOL_CHUNK]
        wsum = jnp.where(lane < POOL_GROUP, w2[POOL_HALO:POOL_HALO + POOL_CHUNK],
                         jnp.where(lane < 2 * POOL_GROUP, w4[POOL_HALO:POOL_HALO + POOL_CHUNK],
                                   jnp.where(lane < 3 * POOL_GROUP, w8[POOL_HALO:POOL_HALO + POOL_CHUNK],
                                             w16[POOL_HALO:POOL_HALO + POOL_CHUNK])))
        t = row + c0
        half_w = w_of_lane // 2
        lo = jnp.maximum(t - half_w, 0)
        hi = jnp.minimum(t + w_of_lane - 1 - half_w, s - 1)
        cnt = (hi - lo + 1).astype(F32)
        dlt = wsum / cnt - u
        y = _dot(dlt.astype(BF16), w_ref[...]) * sc_ref[...]
        o_ref[0, pl.ds(c0, POOL_CHUNK), :] = y.astype(BF16)
        return carry

    lax.fori_loop(0, s // POOL_CHUNK, body, 0)


def _pool(u3, pool_w, pool_scale):
    b, s, width = u3.shape
    wbd = jax.scipy.linalg.block_diag(*[pool_w[g] for g in range(len(POOL_WINDOWS))]).astype(BF16)
    return pl.pallas_call(
        _pool_kernel,
        grid=(b,),
        in_specs=[pl.BlockSpec((1, s, width), lambda bi: (bi, 0, 0)),
                  pl.BlockSpec((width, width), lambda bi: (0, 0)),
                  pl.BlockSpec((1, width), lambda bi: (0, 0))],
        out_specs=pl.BlockSpec((1, s, width), lambda bi: (bi, 0, 0)),
        out_shape=jax.ShapeDtypeStruct((b, s, width), BF16),
        scratch_shapes=[pltpu.VMEM((s + 2 * POOL_PAD, width), F32)],
        compiler_params=_cparams(("parallel",)),
        name="pool",
    )(u3, wbd, pool_scale.reshape(1, width))


SWA_Q = 128
SWA_BAND = SWA_Q + 2 * DIL_HALF


def _swa_kernel(q_ref, k_ref, v_ref, o_ref, l_ref, *, length, chunk):
    c = pl.program_id(2)
    lane = _lane((SWA_Q, LANES))
    rel = (lax.broadcasted_iota(jnp.int32, (SWA_Q, SWA_BAND), 1)
           - lax.broadcasted_iota(jnp.int32, (SWA_Q, SWA_BAND), 0))
    for i in range(chunk // SWA_Q):
        r0 = c * chunk + i * SWA_Q
        l0 = r0 & (length - 1)
        lo = r0 - l0 + jnp.clip(l0 - DIL_HALF, 0, length - SWA_BAND)
        lo = pl.multiple_of(lo, DIL_HALF)
        q2 = _stack_heads(q_ref[0, i * SWA_Q:(i + 1) * SWA_Q, :], lane)
        d = rel + (lo - r0)
        valid = (d >= -DIL_HALF) & (d <= DIL_HALF)
        valid2 = jnp.concatenate([valid, valid], axis=0)
        sc = jnp.where(valid2, _dot_nt(q2, k_ref[0, pl.ds(lo, SWA_BAND), :]), NEG)
        m = jnp.max(sc, axis=-1, keepdims=True)
        e = jnp.exp2(sc - m)
        den = jnp.sum(e, axis=-1, keepdims=True)
        pv = _dot(e.astype(BF16), v_ref[0, pl.ds(lo, SWA_BAND), :]) / den
        lse2 = m + jnp.log2(den)
        o_ref[0, i * SWA_Q:(i + 1) * SWA_Q, :] = jnp.where(lane < HEAD_DIM, pv[0:SWA_Q], pv[SWA_Q:])
        l_ref[0, i * SWA_Q:(i + 1) * SWA_Q, :] = jnp.where(lane < HEAD_DIM, lse2[0:SWA_Q], lse2[SWA_Q:])


def _swa(qkv, length):
    b, s, _ = qkv.shape
    chunk = 1024
    shp = jax.ShapeDtypeStruct((b, s, 2 * LANES), F32)
    return pl.pallas_call(
        functools.partial(_swa_kernel, length=length, chunk=chunk),
        grid=(b, 2, s // chunk),
        in_specs=[pl.BlockSpec((1, chunk, LANES), lambda bi, hp, c: (bi, c, hp)),
                  pl.BlockSpec((1, s, LANES), lambda bi, hp, c: (bi, 0, 2 + hp)),
                  pl.BlockSpec((1, s, LANES), lambda bi, hp, c: (bi, 0, 4 + hp))],
        out_specs=[pl.BlockSpec((1, chunk, LANES), lambda bi, hp, c: (bi, c, hp)),
                   pl.BlockSpec((1, chunk, LANES), lambda bi, hp, c: (bi, c, hp))],
        out_shape=[shp, shp],
        compiler_params=_cparams(("parallel", "parallel", "arbitrary")),
        name=f"swa_l{length}",
    )(qkv, qkv, qkv)


def _dil_combine(dil_refs, stage_refs, tm):
    (o0_ref, l0_ref), (o1_ref, l1_ref), (o2_ref, l2_ref) = dil_refs
    so1, sl1, so2, sl2 = stage_refs
    r1 = DIL_PATTERNS[1][1]
    r2 = DIL_PATTERNS[2][1]
    halves = []
    for hp in range(2):
        cols = slice(hp * LANES, (hp + 1) * LANES)
        for m in range(r1):
            so1[pl.ds(m, tm // r1, stride=r1), :] = o1_ref[0, m, :, cols]
            sl1[pl.ds(m, tm // r1, stride=r1), :] = l1_ref[0, m, :, cols]
        for m in range(r2):
            so2[pl.ds(m, tm // r2, stride=r2), :] = o2_ref[0, m, :, cols]
            sl2[pl.ds(m, tm // r2, stride=r2), :] = l2_ref[0, m, :, cols]
        l0, l1, l2 = l0_ref[0, 0, :, cols], sl1[...], sl2[...]
        mx = jnp.maximum(jnp.maximum(l0, l1), l2)
        e0, e1, e2 = jnp.exp2(l0 - mx), jnp.exp2(l1 - mx), jnp.exp2(l2 - mx)
        den = e0 + e1 + e2
        halves.append((e0 / den) * o0_ref[0, 0, :, cols] + (e1 / den) * so1[...] + (e2 / den) * so2[...])
    return jnp.concatenate(halves, axis=1)


def _merge_kernel(ya_ref, yb_ref, yc_ref, o0_ref, l0_ref, o1_ref, l1_ref, o2_ref, l2_ref,
                  wg_ref, bg_ref, wb_ref, wo_ref, h_ref, g_ref, b_ref,
                  o_ref, op_ref, so1, sl1, so2, sl2, *, alpha):
    tm, d = h_ref.shape
    hb = h_ref[...].astype(BF16)
    y_d = _dil_combine(((o0_ref, l0_ref), (o1_ref, l1_ref), (o2_ref, l2_ref)),
                       (so1, sl1, so2, sl2), tm).astype(BF16)
    merged = None
    for n, y in enumerate((ya_ref[...], yb_ref[...], yc_ref[...], y_d)):
        zg = _dot(hb, wg_ref[:, n * d:(n + 1) * d]) + bg_ref[:, n * d:(n + 1) * d]
        term = (1.0 / (1.0 + jnp.exp(-zg))) * _dot(y, wb_ref[n])
        merged = term if merged is None else merged + term
    mix = _dot(merged.astype(BF16), wo_ref[...])
    h1 = _layer_norm(alpha * h_ref[...] + mix, g_ref[...], b_ref[...])
    o_ref[...] = h1
    op_ref[...] = _pack_bf16_pairs(h1)


def _merge(ys, dil, wg, b_gate, wb, wo, h2, g, bb, alpha):
    t, d = h2.shape
    b, s, bw = dil[0][0].shape
    nb = len(ys) + 1
    tm = QKV_TM
    nt = s // tm
    yspec = pl.BlockSpec((tm, bw), lambda i: (i, 0))
    vec = pl.BlockSpec((1, d), lambda i: (0, 0))
    dil_args, dil_specs = [], []
    for (o, lse), (_, r) in zip(dil, DIL_PATTERNS):
        spec = pl.BlockSpec((1, r, tm // r, bw), lambda i: (i // nt, 0, i % nt, 0))
        dil_args += [o.reshape(b, r, s // r, bw), lse.reshape(b, r, s // r, bw)]
        dil_specs += [spec, spec]
    return pl.pallas_call(
        functools.partial(_merge_kernel, alpha=alpha),
        grid=(t // tm,),
        in_specs=[yspec, yspec, yspec] + dil_specs + [
                  pl.BlockSpec((d, nb * d), lambda i: (0, 0)),
                  pl.BlockSpec((1, nb * d), lambda i: (0, 0)),
                  pl.BlockSpec((nb, bw, d), lambda i: (0, 0, 0)),
                  pl.BlockSpec((d, d), lambda i: (0, 0)),
                  pl.BlockSpec((tm, d), lambda i: (i, 0)),
                  vec, vec],
        out_specs=[pl.BlockSpec((tm, d), lambda i: (i, 0)),
                   pl.BlockSpec((tm, d // 2), lambda i: (i, 0))],
        out_shape=[jax.ShapeDtypeStruct((t, d), F32), jax.ShapeDtypeStruct((t, d // 2), jnp.int32)],
        scratch_shapes=[pltpu.VMEM((tm, LANES), F32)] * 4,
        compiler_params=_cparams(("parallel",)),
        name="merge",
    )(*ys, *dil_args, wg, b_gate.reshape(1, nb * d), wb, wo, h2, g.reshape(1, d), bb.reshape(1, d))


ROUTER_TM = 512


def _router_kernel(h_ref, w_ref, b_ref, o_ref, cnt_ref, carry, before):
    tm = h_ref.shape[0]

    @pl.when(pl.program_id(0) == 0)
    def _():
        carry[...] = jnp.zeros_like(carry)
        before[...] = (lax.broadcasted_iota(jnp.int32, (tm, tm), 0)
                       < lax.broadcasted_iota(jnp.int32, (tm, tm), 1)).astype(BF16)

    logits = lax.dot_general(w_ref[...], h_ref[...], (((1,), (1,)), ((), ())),
                             preferred_element_type=F32,
                             precision=lax.Precision.HIGHEST) + b_ref[...]
    row = lax.broadcasted_iota(jnp.int32, (LANES, tm), 0)
    big = jnp.int32(1 << 20)
    is_g = row < N_GROUPS
    gl = jnp.where(is_g, logits, -jnp.inf)
    gmax = jnp.max(gl, axis=0, keepdims=True)
    gsel = jnp.min(jnp.where(is_g & (gl == gmax), row, big), axis=0, keepdims=True)
    pg = 1.0 / jnp.sum(jnp.exp(gl - gmax), axis=0, keepdims=True)
    e_lo = N_GROUPS + gsel * EXPERTS_PER_GROUP
    in_grp = (row >= e_lo) & (row < e_lo + EXPERTS_PER_GROUP)
    el = jnp.where(in_grp, logits, -jnp.inf)
    v1 = jnp.max(el, axis=0, keepdims=True)
    i1 = jnp.min(jnp.where(in_grp & (el == v1), row, big), axis=0, keepdims=True)
    el2 = jnp.where(row == i1, -jnp.inf, el)
    v2 = jnp.max(el2, axis=0, keepdims=True)
    i2 = jnp.min(jnp.where(in_grp & (row != i1) & (el2 == v2), row, big), axis=0, keepdims=True)
    t2 = jnp.exp(v2 - v1)
    g1 = pg / (1.0 + t2)
    g2 = pg * t2 / (1.0 + t2)
    oh1 = (row == i1)
    oh2 = (row == i2)
    oh1b = oh1.astype(BF16)
    oh2b = oh2.astype(BF16)
    c0 = carry[...]
    c1 = c0 + jnp.sum(oh1b.astype(F32), axis=1, keepdims=True)
    rank1 = jnp.sum(jnp.where(oh1, _dot(oh1b, before[...]) + c0, 0.0), axis=0, keepdims=True)
    rank2 = jnp.sum(jnp.where(oh2, _dot(oh2b, before[...]) + c1, 0.0), axis=0, keepdims=True)
    c2 = c1 + jnp.sum(oh2b.astype(F32), axis=1, keepdims=True)
    carry[...] = c2
    cnt_ref[...] = c2
    e1 = (i1 - N_GROUPS).astype(F32)
    e2 = (i2 - N_GROUPS).astype(F32)
    out_t = jnp.where(row == 0, e1, jnp.where(row == 1, e2, jnp.where(row == 2, g1, jnp.where(
        row == 3, g2, jnp.where(row == 4, rank1, jnp.where(row == 5, rank2, 0.0))))))
    o_ref[...] = out_t.T


def _router(h2, rg_w, rg_b, re_w, re_b):
    t, d = h2.shape
    pad = LANES - N_GROUPS - N_EXPERTS
    w_t = jnp.concatenate([rg_w.T, re_w.T, jnp.zeros((pad, d), F32)], axis=0)
    bias = jnp.concatenate([rg_b, re_b, jnp.zeros((pad,), F32)]).reshape(LANES, 1)
    tm = ROUTER_TM
    return pl.pallas_call(
        _router_kernel,
        grid=(t // tm,),
        in_specs=[pl.BlockSpec((tm, d), lambda i: (i, 0)),
                  pl.BlockSpec((LANES, d), lambda i: (0, 0)),
                  pl.BlockSpec((LANES, 1), lambda i: (0, 0))],
        out_specs=[pl.BlockSpec((tm, LANES), lambda i: (i, 0)),
                   pl.BlockSpec((LANES, 1), lambda i: (0, 0))],
        out_shape=[jax.ShapeDtypeStruct((t, LANES), F32), jax.ShapeDtypeStruct((LANES, 1), F32)],
        scratch_shapes=[pltpu.VMEM((LANES, 1), F32), pltpu.VMEM((tm, tm), BF16)],
        compiler_params=_cparams(("arbitrary",)),
        name="router",
    )(h2, w_t, bias)


def _moe_kernel(be_ref, nv_ref, x_ref, wg_ref, wu_ref, wd_ref, o_ref, wgb, wub, wdb):
    j = pl.program_id(0)
    n_valid = nv_ref[j]
    new_expert = (j == 0) | (be_ref[j] != be_ref[jnp.maximum(j - 1, 0)])

    @pl.when(new_expert)
    def _():
        wgb[...] = wg_ref[0].astype(BF16)
        wub[...] = wu_ref[0].astype(BF16)
        wdb[...] = wd_ref[0].astype(BF16)

    @pl.when(n_valid > 0)
    def _():
        row = lax.broadcasted_iota(jnp.int32, x_ref.shape, 0)
        x = _unpack_bf16_pairs(jnp.where(row < n_valid, x_ref[...], 0)).astype(BF16)
        g = _dot(x, wgb[...])
        u = _dot(x, wub[...])
        hmid = (g * (1.0 / (1.0 + jnp.exp(-g)))) * u
        o_ref[...] = _pack_bf16_pairs(_dot(hmid.astype(BF16), wdb[...]))

    @pl.when(n_valid <= 0)
    def _():
        o_ref[...] = jnp.zeros_like(o_ref)


def _moe_experts(xs, block_e, n_valid, wg, wu, wd, l):
    cap, dp = xs.shape
    d = 2 * dp
    de = wg.shape[3]
    n_blocks = cap // MOE_ROWS
    grid_spec = pltpu.PrefetchScalarGridSpec(
        num_scalar_prefetch=2,
        grid=(n_blocks,),
        in_specs=[pl.BlockSpec((MOE_ROWS, dp), lambda j, be, nv: (j, 0)),
                  pl.BlockSpec((None, 1, d, de), lambda j, be, nv: (l, be[j], 0, 0)),
                  pl.BlockSpec((None, 1, d, de), lambda j, be, nv: (l, be[j], 0, 0)),
                  pl.BlockSpec((None, 1, de, d), lambda j, be, nv: (l, be[j], 0, 0))],
        out_specs=pl.BlockSpec((MOE_ROWS, dp), lambda j, be, nv: (j, 0)),
        scratch_shapes=[pltpu.VMEM((d, de), BF16), pltpu.VMEM((d, de), BF16), pltpu.VMEM((de, d), BF16)],
    )
    return pl.pallas_call(
        _moe_kernel,
        grid_spec=grid_spec,
        out_shape=jax.ShapeDtypeStruct((cap, dp), jnp.int32),
        compiler_params=_cparams(("arbitrary",)),
        name="moe_experts",
    )(block_e, n_valid, xs, wg, wu, wd)


SC_CORES = 2
SC_SUBCORES = 16
SC_WORKERS = SC_CORES * SC_SUBCORES
SC_CHUNK = 64


def _sc_mesh():
    return plsc.VectorSubcoreMesh(core_axis_name="c", subcore_axis_name="s",
                                  num_cores=SC_CORES, num_subcores=SC_SUBCORES)


def _sc_scatter_rows(src, idx0, idx1, n_out):
    t, d = src.shape
    per_w = t // SC_WORKERS
    n_chunks = per_w // SC_CHUNK
    idx_shape = (SC_WORKERS, n_chunks, SC_CHUNK)

    @functools.partial(
        pl.kernel, mesh=_sc_mesh(),
        out_type=jax.ShapeDtypeStruct((n_out, d), src.dtype),
        scratch_types=[pltpu.VMEM((n_chunks, SC_CHUNK), jnp.int32),
                       pltpu.VMEM((n_chunks, SC_CHUNK), jnp.int32),
                       pltpu.VMEM((2, SC_CHUNK, d), src.dtype),
                       pltpu.SemaphoreType.DMA((2,)), pltpu.SemaphoreType.DMA((2,))],
        name="sc_dispatch",
    )
    def k(src_hbm, i0_hbm, i1_hbm, out_hbm, i0_v, i1_v, rows_v, sem_in, sem_out):
        wid = lax.axis_index("s") * SC_CORES + lax.axis_index("c")
        base = wid * per_w
        pltpu.sync_copy(i0_hbm.at[wid], i0_v)
        pltpu.sync_copy(i1_hbm.at[wid], i1_v)

        def load(ci):
            s = ci % 2
            return pltpu.make_async_copy(src_hbm.at[pl.ds(base + ci * SC_CHUNK, SC_CHUNK)],
                                         rows_v.at[s], sem_in.at[s])

        def scatter(ci, idx_v):
            s = ci % 2
            return pltpu.make_async_copy(rows_v.at[s], out_hbm.at[idx_v.at[ci]], sem_out.at[s])

        load(0).start()
        for ci in range(n_chunks):
            load(ci).wait()
            if ci >= 1:
                scatter(ci - 1, i0_v).wait()
                scatter(ci - 1, i1_v).wait()
            if ci + 1 < n_chunks:
                load(ci + 1).start()
            scatter(ci, i0_v).start()
            scatter(ci, i1_v).start()
        scatter(n_chunks - 1, i0_v).wait()
        scatter(n_chunks - 1, i1_v).wait()

    return k(src, idx0.reshape(idx_shape), idx1.reshape(idx_shape))


def _sc_gather_rows(table, idx):
    n = idx.shape[0]
    d = table.shape[1]
    per_w = n // SC_WORKERS
    n_chunks = per_w // SC_CHUNK

    @functools.partial(
        pl.kernel, mesh=_sc_mesh(),
        out_type=jax.ShapeDtypeStruct((n, d), table.dtype),
        scratch_types=[pltpu.VMEM((n_chunks, SC_CHUNK), jnp.int32),
                       pltpu.VMEM((2, SC_CHUNK, d), table.dtype),
                       pltpu.SemaphoreType.DMA((2,)), pltpu.SemaphoreType.DMA((2,))],
        name="sc_collect",
    )
    def k(table_hbm, idx_hbm, out_hbm, idx_v, rows_v, sem_in, sem_out):
        wid = lax.axis_index("s") * SC_CORES + lax.axis_index("c")
        base = wid * per_w
        pltpu.sync_copy(idx_hbm.at[wid], idx_v)

        def gather(ci):
            s = ci % 2
            return pltpu.make_async_copy(table_hbm.at[idx_v.at[ci]], rows_v.at[s], sem_in.at[s])

        def store(ci):
            s = ci % 2
            return pltpu.make_async_copy(rows_v.at[s], out_hbm.at[pl.ds(base + ci * SC_CHUNK, SC_CHUNK)],
                                         sem_out.at[s])

        gather(0).start()
        for ci in range(n_chunks):
            gather(ci).wait()
            if ci >= 1:
                store(ci - 1).wait()
            if ci + 1 < n_chunks:
                gather(ci + 1).start()
            store(ci).start()
        store(n_chunks - 1).wait()

    return k(table, idx.reshape(SC_WORKERS, n_chunks, SC_CHUNK))


def _combine_kernel(h_ref, r_ref, y1_ref, y2_ref, g_ref, b_ref, o_ref, *, alpha):
    g1 = r_ref[:, 2:3]
    g2 = r_ref[:, 3:4]
    ffn = g1 * _unpack_bf16_pairs(y1_ref[...]) + g2 * _unpack_bf16_pairs(y2_ref[...])
    o_ref[...] = _layer_norm(alpha * h_ref[...] + ffn, g_ref[...], b_ref[...])


def _combine(h2, routing, y12, g, bb, alpha):
    t, d = h2.shape
    tm = 512
    nt = t // tm
    row = pl.BlockSpec((tm, d), lambda i: (i, 0))
    vec = pl.BlockSpec((1, d), lambda i: (0, 0))
    return pl.pallas_call(
        functools.partial(_combine_kernel, alpha=alpha),
        grid=(nt,),
        in_specs=[row, pl.BlockSpec((tm, LANES), lambda i: (i, 0)),
                  pl.BlockSpec((tm, d // 2), lambda i: (i, 0)),
                  pl.BlockSpec((tm, d // 2), lambda i: (nt + i, 0)), vec, vec],
        out_specs=row,
        out_shape=jax.ShapeDtypeStruct((t, d), F32),
        compiler_params=_cparams(("parallel",)),
        name="combine_ln",
    )(h2, routing, y12, y12, g.reshape(1, d), bb.reshape(1, d))


def _moe(h2, h2_packed, rg_w, rg_b, re_w, re_b, wg, wu, wd, l, g, bb, alpha):
    t, d = h2.shape
    routing, counts = _router(h2, rg_w, rg_b, re_w, re_b)
    eid = routing[:, 0:2].astype(jnp.int32)
    rank = routing[:, 4:6].astype(jnp.int32)
    cnt = counts[N_GROUPS:N_GROUPS + N_EXPERTS, 0].astype(jnp.int32)
    padded = (cnt + MOE_ROWS - 1) // MOE_ROWS * MOE_ROWS
    pad_end = jnp.cumsum(padded)
    pad_start = pad_end - padded
    experts = jnp.arange(N_EXPERTS, dtype=jnp.int32)
    dest = jnp.sum(jnp.where(eid[..., None] == experts, pad_start, 0), axis=-1) + rank
    n_slots = t * TOP_K
    n_blocks = (n_slots + N_EXPERTS * (MOE_ROWS - 1) + MOE_ROWS - 1) // MOE_ROWS
    cap = n_blocks * MOE_ROWS
    blk_row = jnp.arange(n_blocks, dtype=jnp.int32) * MOE_ROWS
    block_e = jnp.minimum(jnp.sum((pad_end[None, :] <= blk_row[:, None]).astype(jnp.int32), axis=1),
                          N_EXPERTS - 1)
    is_e = block_e[:, None] == experts
    blk_cnt = jnp.sum(jnp.where(is_e, cnt, 0), axis=1)
    blk_start = jnp.sum(jnp.where(is_e, pad_start, 0), axis=1)
    n_valid = jnp.clip(blk_cnt - (blk_row - blk_start), 0, MOE_ROWS).astype(jnp.int32)
    xs = _sc_scatter_rows(h2_packed, dest[:, 0], dest[:, 1], cap)
    ys = _moe_experts(xs, block_e, n_valid, wg, wu, wd, l)
    y12 = _sc_gather_rows(ys, jnp.concatenate([dest[:, 0], dest[:, 1]]))
    return _combine(h2, routing, y12, g, bb, alpha)


def _mixing_layer(h, b, s, l, p, lam_init):
    t = b * s
    alpha = (2 * p['w_in'].shape[0]) ** 0.25
    d = h.shape[1]
    w_in = p['w_in']
    tab_diff = _rope_tables(s, DIFF_QK_DIM)
    tab_dil = _rope_tables(s, HEAD_DIM)
    h3 = h.reshape(b, s, d)
    w_left = lax.slice_in_dim(w_in[l], 0, COL_GATE, axis=1).astype(BF16)
    qkv_na, qkv_diff, qkv_d0, qkv_d1, qkv_d2, u = _proj_branches(h3, w_left, tab_diff, tab_dil)
    y_a = _na_attention(qkv_na, _na_bias_table(p['na_rpb'][l]))
    y_b = _diff_attention(qkv_diff, p['diff_lam'][l], p['diff_subln_g'][l], lam_init)
    y_c = _pool(u, p['pool_w'][l], p['pool_scale'][l])
    dil = [_swa(qkv, s // r) for qkv, (_, r) in zip((qkv_d0, qkv_d1, qkv_d2), DIL_PATTERNS)]
    ys = [a.reshape(t, -1) for a in (y_a, y_b, y_c)]
    wg = lax.slice_in_dim(w_in[l], COL_GATE, COL_GATE + (len(ys) + 1) * d, axis=1).astype(BF16)
    return _merge(ys, dil, wg, p['b_gate'][l], p['w_branch'][l].astype(BF16), p['w_out'][l].astype(BF16), h,
                  p['ln1_g'][l], p['ln1_b'][l], alpha)


def kernel(x, emb_ln_g, emb_ln_b, w_in, b_gate, na_rpb, diff_lam, diff_subln_g, pool_w, pool_scale,
           w_branch, w_out, ln1_g, ln1_b, router_group_w, router_group_b, router_expert_w,
           router_expert_b, expert_w_gate, expert_w_up, expert_w_down, ln2_g, ln2_b):
    b, s, d = x.shape
    depth = w_in.shape[0]
    alpha = (2 * depth) ** 0.25
    p = dict(w_in=w_in, b_gate=b_gate, na_rpb=na_rpb, diff_lam=diff_lam, diff_subln_g=diff_subln_g,
             pool_w=pool_w, pool_scale=pool_scale, w_branch=w_branch, w_out=w_out, ln1_g=ln1_g, ln1_b=ln1_b)
    h = _embed_ln(x.reshape(b * s, d), emb_ln_g, emb_ln_b)
    for l in range(depth):
        lam_init = 0.8 - 0.6 * math.exp(-0.3 * l)
        h, h_packed = _mixing_layer(h, b, s, l, p, lam_init)
        h = _moe(h, h_packed, router_group_w[l], router_group_b[l], router_expert_w[l], router_expert_b[l],
                 expert_w_gate, expert_w_up, expert_w_down, l, ln2_g[l], ln2_b[l], alpha)
    return h.reshape(b, s, d)
```

```python
import functools
import math

import jax
import jax.numpy as jnp
import numpy as np
from jax import lax
from jax.experimental import pallas as pl
from jax.experimental.pallas import tpu as pltpu
from jax.experimental.pallas import tpu_sc as plsc

F32 = jnp.float32
BF16 = jnp.bfloat16

LANES = 128
GRID_W = 64
HEAD_DIM = 64
ROPE_THETA = 500000.0
LN_EPS = 1e-5
NA_ROWS = 8
NA_COLS = 16
DIFF_QK_DIM = 32
POOL_WINDOWS = (2, 4, 8, 16)
POOL_GROUP = 64
DIL_PATTERNS = ((128, 1), (512, 4), (2048, 16))
DIL_HALF = 64
N_GROUPS = 4
EXPERTS_PER_GROUP = 8
N_EXPERTS = N_GROUPS * EXPERTS_PER_GROUP
TOP_K = 2
MOE_ROWS = 512
NEG = -1e30

COL_NA = 0
COL_DIFF = 768
COL_POOL = 1536
COL_DIL = 1792
COL_GATE = 4096

VMEM_LIMIT = 56 * 1024 * 1024


def _cparams(sem, vmem=VMEM_LIMIT, flags=None):
    return pltpu.CompilerParams(dimension_semantics=sem, vmem_limit_bytes=vmem, flags=flags)


def _layer_norm(x, g, b):
    mu = jnp.mean(x, axis=-1, keepdims=True)
    xc = x - mu
    var = jnp.mean(xc * xc, axis=-1, keepdims=True)
    return xc * lax.rsqrt(var + LN_EPS) * g + b


def _dot(a, b):
    return jnp.dot(a, b, preferred_element_type=F32)


def _dot_nt(a, b):
    return lax.dot_general(a, b, (((1,), (1,)), ((), ())), preferred_element_type=F32)


def _lane(shape):
    return lax.broadcasted_iota(jnp.int32, shape, len(shape) - 1)


HI16 = -65536


def _pack_bf16_pairs(x):
    w = x.shape[1] // 2
    hi = lax.bitcast_convert_type(x[:, :w].astype(BF16).astype(F32), jnp.int32)
    lo = lax.bitcast_convert_type(x[:, w:].astype(BF16).astype(F32), jnp.int32)
    return (hi & HI16) | lax.shift_right_logical(lo, 16)


def _unpack_bf16_pairs(p):
    hi = lax.bitcast_convert_type(p & HI16, F32)
    lo = lax.bitcast_convert_type(lax.shift_left(p, 16), F32)
    return jnp.concatenate([hi, lo], axis=1)


def _ln_kernel(x_ref, g_ref, b_ref, o_ref):
    o_ref[...] = _layer_norm(x_ref[...], g_ref[...], b_ref[...])


def _embed_ln(x2, g, b):
    t, d = x2.shape
    tm = 1024
    return pl.pallas_call(
        _ln_kernel,
        grid=(t // tm,),
        in_specs=[pl.BlockSpec((tm, d), lambda i: (i, 0)),
                  pl.BlockSpec((1, d), lambda i: (0, 0)),
                  pl.BlockSpec((1, d), lambda i: (0, 0))],
        out_specs=pl.BlockSpec((tm, d), lambda i: (i, 0)),
        out_shape=jax.ShapeDtypeStruct((t, d), F32),
        compiler_params=_cparams(("parallel",)),
        name="embed_ln",
    )(x2, g.reshape(1, d), b.reshape(1, d))


PROJ_TM = 1024
QKV_W = 6 * LANES
WCOL = 256


QKV_TM = 512
QKV_SEGMENTS = ((COL_NA, 1, 0, HEAD_DIM), (COL_DIFF, 1, DIFF_QK_DIM // 8, DIFF_QK_DIM),
                (COL_DIL, 1, HEAD_DIM // 8, HEAD_DIM), (COL_DIL + QKV_W, 4, HEAD_DIM // 8, HEAD_DIM),
                (COL_DIL + 2 * QKV_W, 16, HEAD_DIM // 8, HEAD_DIM))


def _qkv_epilogue(z, t_ref, o_ref, zs_ref, r, half, q_scale):
    tm = z.shape[0]
    n = tm // r
    for c in range(QKV_W // LANES):
        blk = z[:, c * LANES:(c + 1) * LANES]
        if half and c < 4:
            blk = _rope(blk, t_ref, half)
        if c < 2:
            blk = blk * q_scale
        if r == 1:
            o_ref[0, 0, :, c * LANES:(c + 1) * LANES] = blk.astype(BF16)
        else:
            zs_ref[c] = blk
            for m in range(r):
                o_ref[0, m, :, c * LANES:(c + 1) * LANES] = zs_ref[
                    c, pl.ds(m, n, stride=r), :].astype(BF16)


def _proj_kernel(x_ref, w_ref, td_ref, tl_ref, na_ref, df_ref, d0_ref, d1_ref, d2_ref, u_ref,
                 zs1_ref, zs2_ref):
    xb = x_ref[0].astype(BF16)
    outs = (na_ref, df_ref, d0_ref, d1_ref, d2_ref)
    stage = (None, None, None, zs1_ref, zs2_ref)
    for (col0, r, half, head_w), o_ref, zs_ref in zip(QKV_SEGMENTS, outs, stage):
        z = _dot(xb, w_ref[:, col0:col0 + QKV_W])
        t_ref = td_ref if head_w == DIFF_QK_DIM else tl_ref
        _qkv_epilogue(z, t_ref, o_ref, zs_ref, r, half, head_w ** -0.5 * LOG2E)
    u_ref[0] = _dot(xb, w_ref[:, COL_POOL:COL_POOL + WCOL])


def _proj_branches(h3, w_bf16, tab_diff, tab_dil):
    b, s, d = h3.shape
    tm = QKV_TM
    qkv_spec = lambda r: pl.BlockSpec((1, r, tm // r, QKV_W), lambda bi, i: (bi, 0, i, 0))
    qkv_shape = lambda r: jax.ShapeDtypeStruct((b, r, s // r, QKV_W), BF16)
    tab_spec = pl.BlockSpec((3, tm, LANES), lambda bi, i: (0, i, 0))
    rs = [seg[1] for seg in QKV_SEGMENTS]
    outs = pl.pallas_call(
        _proj_kernel,
        grid=(b, s // tm),
        in_specs=[pl.BlockSpec((1, tm, d), lambda bi, i: (bi, i, 0)),
                  pl.BlockSpec(w_bf16.shape, lambda bi, i: (0, 0)),
                  tab_spec, tab_spec],
        out_specs=[qkv_spec(r) for r in rs] + [pl.BlockSpec((1, tm, WCOL), lambda bi, i: (bi, i, 0))],
        out_shape=[qkv_shape(r) for r in rs] + [jax.ShapeDtypeStruct((b, s, WCOL), F32)],
        scratch_shapes=[pltpu.VMEM((QKV_W // LANES, tm, LANES), F32)] * 2,
        compiler_params=_cparams(("parallel", "parallel")),
        name="proj_branches",
    )(h3, w_bf16, tab_diff, tab_dil)
    return [o.reshape(b, s, QKV_W) for o in outs[:-1]] + [outs[-1]]


def _rope_tables(seq, head_w):
    rot = head_w // 4
    half = rot // 2
    inv_freq = jnp.exp(jnp.arange(half, dtype=F32) * (-2.0 * math.log(ROPE_THETA) / rot))
    ang = jnp.arange(seq, dtype=jnp.int32).astype(F32)[:, None] * inv_freq[None, :]
    cos, sin = jnp.cos(ang), jnp.sin(ang)
    zero = jnp.zeros((seq, head_w - rot), F32)
    zh = jnp.zeros((seq, half), F32)
    t0 = jnp.concatenate([cos, cos, jnp.ones((seq, head_w - rot), F32)], axis=1)
    t1 = jnp.concatenate([-sin, zh, zero], axis=1)
    t2 = jnp.concatenate([zh, sin, zero], axis=1)
    reps = LANES // head_w
    return jnp.stack([jnp.tile(t0, (1, reps)), jnp.tile(t1, (1, reps)), jnp.tile(t2, (1, reps))])


def _rope(x, t_ref, half):
    return (x * t_ref[0] + pltpu.roll(x, LANES - half, 1) * t_ref[1]
            + pltpu.roll(x, half, 1) * t_ref[2])


LOG2E = math.log2(math.e)
LN2 = math.log(2.0)


def _na_bias_table(rpb):
    kr, kc = NA_ROWS, NA_COLS
    n_heads = rpb.shape[0]
    col = np.arange(GRID_W)
    col_start = np.clip(col - kc // 2, 0, GRID_W - kc)
    in_win = (col[None, :] >= col_start[:, None]) & (col[None, :] < col_start[:, None] + kc)
    dc = np.clip(col[None, :] - col[:, None] + kc - 1, 0, 2 * kc - 2)
    sel_c = ((dc[..., None] == np.arange(2 * kc - 1)) & in_win[..., None]).astype(np.float32)
    rp = rpb.astype(F32).reshape(n_heads // 2, 2, 2 * kr - 1, 2 * kc - 1) * LOG2E
    tiles = jnp.einsum('phaj,qxj->phaqx', rp, jnp.asarray(sel_c), precision=lax.Precision.HIGHEST)
    tiles = tiles + jnp.asarray(np.where(in_win, 0.0, NEG).astype(np.float32))
    tiles = jnp.concatenate([tiles, jnp.full_like(tiles[:, :, :1], NEG)], axis=2)
    return jnp.concatenate([tiles, tiles], axis=-1)


NA_QROWS = 4
NA_KROWS = 12
NA_MASKED = 2 * NA_ROWS - 1


def _na_row_offsets(n_rows):
    table = []
    for blk in (0, 1, n_rows // NA_QROWS - 1):
        ws = _na_window_start(blk, n_rows)
        rows = []
        for i in range(NA_QROWS):
            r = blk * NA_QROWS + i
            start = min(max(r - NA_ROWS // 2, 0), n_rows - NA_ROWS)
            rows.append([ws + kk - r + NA_ROWS - 1 if start <= ws + kk < start + NA_ROWS else NA_MASKED
                         for kk in range(NA_KROWS)])
        table.append(rows)
    return table


def _na_window_start(blk, n_rows):
    lo = blk * NA_QROWS - NA_ROWS // 2
    hi = n_rows - NA_KROWS
    if isinstance(blk, int):
        return min(max(lo, 0), hi)
    return jnp.clip(lo, 0, hi)


def _stack_heads(q, lane):
    zero = jnp.zeros_like(q)
    return jnp.concatenate([jnp.where(lane < HEAD_DIM, q, zero), jnp.where(lane < HEAD_DIM, zero, q)],
                           axis=0)


def _na_kernel(q_ref, k_ref, v_ref, tile_ref, o_ref, bias_ref, *, blocks_per_step, n_rows):
    step = pl.program_id(2)
    nq = NA_QROWS * GRID_W
    kwin = NA_KROWS * GRID_W
    lane = _lane((nq, LANES))
    last_blk = n_rows // NA_QROWS - 1

    @pl.when(step == 0)
    def _():
        low = _lane((GRID_W, LANES)) < GRID_W
        for case, rows in enumerate(_na_row_offsets(n_rows)):
            for hh in range(2):
                for i, offs in enumerate(rows):
                    r0 = (hh * NA_QROWS + i) * GRID_W
                    for kp in range(NA_KROWS // 2):
                        bias_ref[case, r0:r0 + GRID_W, kp * LANES:(kp + 1) * LANES] = jnp.where(
                            low, tile_ref[0, hh, offs[2 * kp]], tile_ref[0, hh, offs[2 * kp + 1]])

    for i in range(blocks_per_step):
        blk = step * blocks_per_step + i
        case = jnp.where(blk == 0, 0, jnp.where(blk == last_blk, 2, 1))
        k0 = pl.multiple_of(_na_window_start(blk, n_rows) * GRID_W, GRID_W)
        q2 = _stack_heads(q_ref[0, i * nq:(i + 1) * nq, :], lane)
        sc = _dot_nt(q2, k_ref[0, pl.ds(k0, kwin), :]) + bias_ref[case]
        m = jnp.max(sc, axis=-1, keepdims=True)
        e = jnp.exp2(sc - m)
        den = jnp.sum(e, axis=-1, keepdims=True)
        pv = _dot(e.astype(BF16), v_ref[0, pl.ds(k0, kwin), :]) / den
        o = jnp.where(lane < HEAD_DIM, pv[0:nq], pv[nq:2 * nq])
        o_ref[0, i * nq:(i + 1) * nq, :] = o.astype(BF16)


def _na_attention(qkv, tiles):
    b, s, _ = qkv.shape
    n_rows = s // GRID_W
    bps = 4
    rps = bps * NA_QROWS
    tq = rps * GRID_W
    return pl.pallas_call(
        functools.partial(_na_kernel, blocks_per_step=bps, n_rows=n_rows),
        grid=(b, 2, n_rows // rps),
        in_specs=[pl.BlockSpec((1, tq, LANES), lambda bi, hp, i: (bi, i, hp)),
                  pl.BlockSpec((1, s, LANES), lambda bi, hp, i: (bi, 0, 2 + hp)),
                  pl.BlockSpec((1, s, LANES), lambda bi, hp, i: (bi, 0, 4 + hp)),
                  pl.BlockSpec((1, 2, NA_MASKED + 1, GRID_W, LANES), lambda bi, hp, i: (hp, 0, 0, 0, 0))],
        out_specs=pl.BlockSpec((1, tq, LANES), lambda bi, hp, i: (bi, i, hp)),
        out_shape=jax.ShapeDtypeStruct((b, s, 2 * LANES), BF16),
        scratch_shapes=[pltpu.VMEM((3, 2 * NA_QROWS * GRID_W, NA_KROWS * GRID_W), F32)],
        compiler_params=_cparams(("parallel", "parallel", "arbitrary")),
        name="na_attn",
    )(qkv, qkv, qkv, tiles)


def _diff_kernel(lam_ref, q_ref, k_ref, v_ref, g_ref, o_ref, *, lam_init, rows):
    tq = q_ref.shape[1]
    lane = _lane((rows, LANES))
    dl = lam_ref[...]
    lam = (jnp.exp(jnp.sum(dl[0:1] * dl[1:2], axis=-1, keepdims=True))
           - jnp.exp(jnp.sum(dl[2:3] * dl[3:4], axis=-1, keepdims=True)) + lam_init)
    v = v_ref[0]
    v_lane = _lane(v.shape)
    one = jnp.ones_like(v)
    v_ext = (jnp.where(v_lane < HEAD_DIM, v, one), jnp.where(v_lane < HEAD_DIM, one, v))
    in_h0 = lane < HEAD_DIM
    for r0 in range(0, tq, rows):
        q = q_ref[0, r0:r0 + rows, :]
        zero = jnp.zeros_like(q)
        pv = []
        for lo in range(0, LANES, DIFF_QK_DIM):
            qm = jnp.where((lane >= lo) & (lane < lo + DIFF_QK_DIM), q, zero)
            sc = _dot_nt(qm, k_ref[0])
            e = jnp.exp2(sc - jnp.max(sc, axis=-1, keepdims=True))
            ev = _dot(e.astype(BF16), v_ext[lo // HEAD_DIM])
            pv.append(ev / pltpu.roll(ev, HEAD_DIM, 1))
        o = jnp.where(in_h0, pv[0] - lam * pv[1], pv[2] - lam * pv[3])
        o2 = o * o
        ms0 = jnp.sum(jnp.where(in_h0, o2, 0.0), axis=-1, keepdims=True) / HEAD_DIM
        ms1 = jnp.sum(jnp.where(in_h0, 0.0, o2), axis=-1, keepdims=True) / HEAD_DIM
        ms = jnp.where(in_h0, ms0, ms1)
        o = o * lax.rsqrt(ms + LN_EPS) * g_ref[...] * (1.0 - lam_init)
        o_ref[0, r0:r0 + rows, :] = o.astype(BF16)


def _diff_attention(qkv, diff_lam, subln_g, lam_init):
    b, s, _ = qkv.shape
    tq = 1024
    g2 = jnp.tile(subln_g.reshape(1, HEAD_DIM), (1, 2))
    return pl.pallas_call(
        functools.partial(_diff_kernel, lam_init=lam_init, rows=512),
        grid=(b, 2, s // tq),
        in_specs=[pl.BlockSpec((4, DIFF_QK_DIM), lambda bi, hp, i: (0, 0)),
                  pl.BlockSpec((1, tq, LANES), lambda bi, hp, i: (bi, i, hp)),
                  pl.BlockSpec((1, s, LANES), lambda bi, hp, i: (bi, 0, 2 + hp)),
                  pl.BlockSpec((1, s, LANES), lambda bi, hp, i: (bi, 0, 4 + hp)),
                  pl.BlockSpec((1, LANES), lambda bi, hp, i: (0, 0))],
        out_specs=pl.BlockSpec((1, tq, LANES), lambda bi, hp, i: (bi, i, hp)),
        out_shape=jax.ShapeDtypeStruct((b, s, 2 * LANES), BF16),
        compiler_params=_cparams(("parallel", "parallel", "arbitrary")),
        name="diff_attn",
    )(diff_lam, qkv, qkv, qkv, g2)


POOL_PAD = 16
POOL_CHUNK = 512
POOL_HALO = 8


def _pool_kernel(u_ref, w_ref, sc_ref, o_ref, p_ref):
    s = u_ref.shape[1]
    width = u_ref.shape[2]
    p_ref[0:POOL_PAD, :] = jnp.zeros((POOL_PAD, width), F32)
    p_ref[POOL_PAD + s:POOL_PAD + s + POOL_PAD, :] = jnp.zeros((POOL_PAD, width), F32)
    p_ref[POOL_PAD:POOL_PAD + s, :] = u_ref[0]
    n = POOL_CHUNK + 2 * POOL_HALO
    lane = _lane((POOL_CHUNK, width))
    row = lax.broadcasted_iota(jnp.int32, (POOL_CHUNK, width), 0)
    w_of_lane = jnp.where(lane < POOL_GROUP, 2, jnp.where(lane < 2 * POOL_GROUP, 4,
                          jnp.where(lane < 3 * POOL_GROUP, 8, 16)))

    def body(ci, carry):
        c0 = pl.multiple_of(ci * POOL_CHUNK, POOL_CHUNK)
        x = p_ref[pl.ds(c0 + POOL_PAD - POOL_HALO, n), :]
        w2 = x + pltpu.roll(x, 1, 0)
        w4 = pltpu.roll(w2, 1, 0) + pltpu.roll(w2, n - 1, 0)
        w8 = pltpu.roll(w4, 2, 0) + pltpu.roll(w4, n - 2, 0)
        w16 = pltpu.roll(w8, 4, 0) + pltpu.roll(w8, n - 4, 0)
        u = x[POOL_HALO:POOL_HALO + POOL_CHUNK]
        wsum = jnp.where(lane < POOL_GROUP, w2[POOL_HALO:POOL_HALO + POOL_CHUNK],
                         jnp.where(lane < 2 * POOL_GROUP, w4[POOL_HALO:POOL_HALO + POOL_CHUNK],
                                   jnp.where(lane < 3 * POOL_GROUP, w8[POOL_HALO:POOL_HALO + POOL_CHUNK],
                                             w16[POOL_HALO:POOL_HALO + POOL_CHUNK])))
        t = row + c0
        half_w = w_of_lane // 2
        lo = jnp.maximum(t - half_w, 0)
        hi = jnp.minimum(t + w_of_lane - 1 - half_w, s - 1)
        cnt = (hi - lo + 1).astype(F32)
        dlt = wsum / cnt - u
        y = _dot(dlt.astype(BF16), w_ref[...]) * sc_ref[...]
        o_ref[0, pl.ds(c0, POOL_CHUNK), :] = y.astype(BF16)
        return carry

    lax.fori_loop(0, s // POOL_CHUNK, body, 0)


def _pool(u3, pool_w, pool_scale):
    b, s, width = u3.shape
    wbd = jax.scipy.linalg.block_diag(*[pool_w[g] for g in range(len(POOL_WINDOWS))]).astype(BF16)
    return pl.pallas_call(
        _pool_kernel,
        grid=(b,),
        in_specs=[pl.BlockSpec((1, s, width), lambda bi: (bi, 0, 0)),
                  pl.BlockSpec((width, width), lambda bi: (0, 0)),
                  pl.BlockSpec((1, width), lambda bi: (0, 0))],
        out_specs=pl.BlockSpec((1, s, width), lambda bi: (bi, 0, 0)),
        out_shape=jax.ShapeDtypeStruct((b, s, width), BF16),
        scratch_shapes=[pltpu.VMEM((s + 2 * POOL_PAD, width), F32)],
        compiler_params=_cparams(("parallel",)),
        name="pool",
    )(u3, wbd, pool_scale.reshape(1, width))


SWA_Q = 128
SWA_BAND = SWA_Q + 2 * DIL_HALF


def _swa_kernel(q_ref, k_ref, v_ref, o_ref, l_ref, *, length):
    lane = _lane((SWA_Q, LANES))
    rel = (lax.broadcasted_iota(jnp.int32, (2 * SWA_Q, SWA_BAND), 1)
           - (lax.broadcasted_iota(jnp.int32, (2 * SWA_Q, SWA_BAND), 0) & (SWA_Q - 1)))
    masks = {}
    for i in range(q_ref.shape[1] // SWA_Q):
        r0 = i * SWA_Q
        l0 = r0 % length
        lo = r0 - l0 + min(max(l0 - DIL_HALF, 0), length - SWA_BAND)
        if lo - r0 not in masks:
            d = rel + (lo - r0)
            masks[lo - r0] = jnp.where((d >= -DIL_HALF) & (d <= DIL_HALF), 0.0, NEG)
        q2 = _stack_heads(q_ref[0, r0:r0 + SWA_Q, :], lane)
        sc = _dot_nt(q2, k_ref[0, lo:lo + SWA_BAND, :]) + masks[lo - r0]
        m = jnp.max(sc, axis=-1, keepdims=True)
        e = jnp.exp2(sc - m)
        den = jnp.sum(e, axis=-1, keepdims=True)
        pv = _dot(e.astype(BF16), v_ref[0, lo:lo + SWA_BAND, :]) / den
        lse2 = m + jnp.log2(den)
        o_ref[0, i * SWA_Q:(i + 1) * SWA_Q, :] = jnp.where(lane < HEAD_DIM, pv[0:SWA_Q], pv[SWA_Q:])
        l_ref[0, i * SWA_Q:(i + 1) * SWA_Q, :] = jnp.where(lane < HEAD_DIM, lse2[0:SWA_Q], lse2[SWA_Q:])


def _swa(qkv, length):
    b, s, _ = qkv.shape
    shp = jax.ShapeDtypeStruct((b, s, 2 * LANES), F32)
    seq = lambda col: pl.BlockSpec((1, s, LANES), lambda bi, hp: (bi, 0, col + hp))
    return pl.pallas_call(
        functools.partial(_swa_kernel, length=length),
        grid=(b, 2),
        in_specs=[seq(0), seq(2), seq(4)],
        out_specs=[seq(0), seq(0)],
        out_shape=[shp, shp],
        compiler_params=_cparams(("parallel", "parallel")),
        name=f"swa_l{length}",
    )(qkv, qkv, qkv)


def _dil_combine(dil_refs, stage_refs, tm):
    (o0_ref, l0_ref), (o1_ref, l1_ref), (o2_ref, l2_ref) = dil_refs
    so1, sl1, so2, sl2 = stage_refs
    r1 = DIL_PATTERNS[1][1]
    r2 = DIL_PATTERNS[2][1]
    halves = []
    for hp in range(2):
        cols = slice(hp * LANES, (hp + 1) * LANES)
        for m in range(r1):
            so1[pl.ds(m, tm // r1, stride=r1), :] = o1_ref[0, m, :, cols]
            sl1[pl.ds(m, tm // r1, stride=r1), :] = l1_ref[0, m, :, cols]
        for m in range(r2):
            so2[pl.ds(m, tm // r2, stride=r2), :] = o2_ref[0, m, :, cols]
            sl2[pl.ds(m, tm // r2, stride=r2), :] = l2_ref[0, m, :, cols]
        l0, l1, l2 = l0_ref[0, 0, :, cols], sl1[...], sl2[...]
        mx = jnp.maximum(jnp.maximum(l0, l1), l2)
        e0, e1, e2 = jnp.exp2(l0 - mx), jnp.exp2(l1 - mx), jnp.exp2(l2 - mx)
        den = e0 + e1 + e2
        halves.append((e0 / den) * o0_ref[0, 0, :, cols] + (e1 / den) * so1[...] + (e2 / den) * so2[...])
    return jnp.concatenate(halves, axis=1)


def _merge_kernel(ya_ref, yb_ref, yc_ref, o0_ref, l0_ref, o1_ref, l1_ref, o2_ref, l2_ref,
                  wg_ref, bg_ref, wb_ref, wo_ref, h_ref, g_ref, b_ref,
                  o_ref, op_ref, so1, sl1, so2, sl2, *, alpha):
    tm, d = h_ref.shape
    hb = h_ref[...].astype(BF16)
    y_d = _dil_combine(((o0_ref, l0_ref), (o1_ref, l1_ref), (o2_ref, l2_ref)),
                       (so1, sl1, so2, sl2), tm).astype(BF16)
    merged = None
    for n, y in enumerate((ya_ref[...], yb_ref[...], yc_ref[...], y_d)):
        zg = _dot(hb, wg_ref[:, n * d:(n + 1) * d]) + bg_ref[:, n * d:(n + 1) * d]
        term = (1.0 / (1.0 + jnp.exp(-zg))) * _dot(y, wb_ref[n])
        merged = term if merged is None else merged + term
    mix = _dot(merged.astype(BF16), wo_ref[...])
    h1 = _layer_norm(alpha * h_ref[...] + mix, g_ref[...], b_ref[...])
    o_ref[...] = h1
    op_ref[...] = _pack_bf16_pairs(h1)


def _merge(ys, dil, wg, b_gate, wb, wo, h2, g, bb, alpha):
    t, d = h2.shape
    b, s, bw = dil[0][0].shape
    nb = len(ys) + 1
    tm = QKV_TM
    nt = s // tm
    yspec = pl.BlockSpec((tm, bw), lambda i: (i, 0))
    vec = pl.BlockSpec((1, d), lambda i: (0, 0))
    dil_args, dil_specs = [], []
    for (o, lse), (_, r) in zip(dil, DIL_PATTERNS):
        spec = pl.BlockSpec((1, r, tm // r, bw), lambda i: (i // nt, 0, i % nt, 0))
        dil_args += [o.reshape(b, r, s // r, bw), lse.reshape(b, r, s // r, bw)]
        dil_specs += [spec, spec]
    return pl.pallas_call(
        functools.partial(_merge_kernel, alpha=alpha),
        grid=(t // tm,),
        in_specs=[yspec, yspec, yspec] + dil_specs + [
                  pl.BlockSpec((d, nb * d), lambda i: (0, 0)),
                  pl.BlockSpec((1, nb * d), lambda i: (0, 0)),
                  pl.BlockSpec((nb, bw, d), lambda i: (0, 0, 0)),
                  pl.BlockSpec((d, d), lambda i: (0, 0)),
                  pl.BlockSpec((tm, d), lambda i: (i, 0)),
                  vec, vec],
        out_specs=[pl.BlockSpec((tm, d), lambda i: (i, 0)),
                   pl.BlockSpec((tm, d // 2), lambda i: (i, 0))],
        out_shape=[jax.ShapeDtypeStruct((t, d), F32), jax.ShapeDtypeStruct((t, d // 2), jnp.int32)],
        scratch_shapes=[pltpu.VMEM((tm, LANES), F32)] * 4,
        compiler_params=_cparams(("parallel",)),
        name="merge",
    )(*ys, *dil_args, wg, b_gate.reshape(1, nb * d), wb, wo, h2, g.reshape(1, d), bb.reshape(1, d))


ROUTER_TM = 512


def _router_kernel(h_ref, w_ref, b_ref, o_ref, cnt_ref, carry, before):
    tm = h_ref.shape[0]

    @pl.when(pl.program_id(0) == 0)
    def _():
        carry[...] = jnp.zeros_like(carry)
        before[...] = (lax.broadcasted_iota(jnp.int32, (tm, tm), 0)
                       < lax.broadcasted_iota(jnp.int32, (tm, tm), 1)).astype(BF16)

    logits = lax.dot_general(w_ref[...], h_ref[...], (((1,), (1,)), ((), ())),
                             preferred_element_type=F32,
                             precision=lax.Precision.HIGHEST) + b_ref[...]
    row = lax.broadcasted_iota(jnp.int32, (LANES, tm), 0)
    big = jnp.int32(1 << 20)
    is_g = row < N_GROUPS
    gl = jnp.where(is_g, logits, -jnp.inf)
    gmax = jnp.max(gl, axis=0, keepdims=True)
    gsel = jnp.min(jnp.where(is_g & (gl == gmax), row, big), axis=0, keepdims=True)
    pg = 1.0 / jnp.sum(jnp.exp(gl - gmax), axis=0, keepdims=True)
    e_lo = N_GROUPS + gsel * EXPERTS_PER_GROUP
    in_grp = (row >= e_lo) & (row < e_lo + EXPERTS_PER_GROUP)
    el = jnp.where(in_grp, logits, -jnp.inf)
    v1 = jnp.max(el, axis=0, keepdims=True)
    i1 = jnp.min(jnp.where(in_grp & (el == v1), row, big), axis=0, keepdims=True)
    el2 = jnp.where(row == i1, -jnp.inf, el)
    v2 = jnp.max(el2, axis=0, keepdims=True)
    i2 = jnp.min(jnp.where(in_grp & (row != i1) & (el2 == v2), row, big), axis=0, keepdims=True)
    t2 = jnp.exp(v2 - v1)
    g1 = pg / (1.0 + t2)
    g2 = pg * t2 / (1.0 + t2)
    oh1 = (row == i1)
    oh2 = (row == i2)
    oh1b = oh1.astype(BF16)
    oh2b = oh2.astype(BF16)
    c0 = carry[...]
    c1 = c0 + jnp.sum(oh1b.astype(F32), axis=1, keepdims=True)
    rank1 = jnp.sum(jnp.where(oh1, _dot(oh1b, before[...]) + c0, 0.0), axis=0, keepdims=True)
    rank2 = jnp.sum(jnp.where(oh2, _dot(oh2b, before[...]) + c1, 0.0), axis=0, keepdims=True)
    c2 = c1 + jnp.sum(oh2b.astype(F32), axis=1, keepdims=True)
    carry[...] = c2
    cnt_ref[...] = c2
    e1 = (i1 - N_GROUPS).astype(F32)
    e2 = (i2 - N_GROUPS).astype(F32)
    out_t = jnp.where(row == 0, e1, jnp.where(row == 1, e2, jnp.where(row == 2, g1, jnp.where(
        row == 3, g2, jnp.where(row == 4, rank1, jnp.where(row == 5, rank2, 0.0))))))
    o_ref[...] = out_t.T


def _router(h2, rg_w, rg_b, re_w, re_b):
    t, d = h2.shape
    pad = LANES - N_GROUPS - N_EXPERTS
    w_t = jnp.concatenate([rg_w.T, re_w.T, jnp.zeros((pad, d), F32)], axis=0)
    bias = jnp.concatenate([rg_b, re_b, jnp.zeros((pad,), F32)]).reshape(LANES, 1)
    tm = ROUTER_TM
    return pl.pallas_call(
        _router_kernel,
        grid=(t // tm,),
        in_specs=[pl.BlockSpec((tm, d), lambda i: (i, 0)),
                  pl.BlockSpec((LANES, d), lambda i: (0, 0)),
                  pl.BlockSpec((LANES, 1), lambda i: (0, 0))],
        out_specs=[pl.BlockSpec((tm, LANES), lambda i: (i, 0)),
                   pl.BlockSpec((LANES, 1), lambda i: (0, 0))],
        out_shape=[jax.ShapeDtypeStruct((t, LANES), F32), jax.ShapeDtypeStruct((LANES, 1), F32)],
        scratch_shapes=[pltpu.VMEM((LANES, 1), F32), pltpu.VMEM((tm, tm), BF16)],
        compiler_params=_cparams(("arbitrary",)),
        name="router",
    )(h2, w_t, bias)


def _moe_kernel(be_ref, nv_ref, x_ref, wg_ref, wu_ref, wd_ref, o_ref, wgb, wub, wdb):
    j = pl.program_id(0)
    n_valid = nv_ref[j]
    new_expert = (j == 0) | (be_ref[j] != be_ref[jnp.maximum(j - 1, 0)])

    @pl.when(new_expert)
    def _():
        wgb[...] = wg_ref[0].astype(BF16)
        wub[...] = wu_ref[0].astype(BF16)
        wdb[...] = wd_ref[0].astype(BF16)

    @pl.when(n_valid > 0)
    def _():
        row = lax.broadcasted_iota(jnp.int32, x_ref.shape, 0)
        x = _unpack_bf16_pairs(jnp.where(row < n_valid, x_ref[...], 0)).astype(BF16)
        g = _dot(x, wgb[...])
        u = _dot(x, wub[...])
        hmid = (g * (1.0 / (1.0 + jnp.exp(-g)))) * u
        o_ref[...] = _pack_bf16_pairs(_dot(hmid.astype(BF16), wdb[...]))

    @pl.when(n_valid <= 0)
    def _():
        o_ref[...] = jnp.zeros_like(o_ref)


def _moe_experts(xs, block_e, n_valid, wg, wu, wd, l):
    cap, dp = xs.shape
    d = 2 * dp
    de = wg.shape[3]
    n_blocks = cap // MOE_ROWS
    grid_spec = pltpu.PrefetchScalarGridSpec(
        num_scalar_prefetch=2,
        grid=(n_blocks,),
        in_specs=[pl.BlockSpec((MOE_ROWS, dp), lambda j, be, nv: (j, 0)),
                  pl.BlockSpec((None, 1, d, de), lambda j, be, nv: (l, be[j], 0, 0)),
                  pl.BlockSpec((None, 1, d, de), lambda j, be, nv: (l, be[j], 0, 0)),
                  pl.BlockSpec((None, 1, de, d), lambda j, be, nv: (l, be[j], 0, 0))],
        out_specs=pl.BlockSpec((MOE_ROWS, dp), lambda j, be, nv: (j, 0)),
        scratch_shapes=[pltpu.VMEM((d, de), BF16), pltpu.VMEM((d, de), BF16), pltpu.VMEM((de, d), BF16)],
    )
    return pl.pallas_call(
        _moe_kernel,
        grid_spec=grid_spec,
        out_shape=jax.ShapeDtypeStruct((cap, dp), jnp.int32),
        compiler_params=_cparams(("arbitrary",)),
        name="moe_experts",
    )(block_e, n_valid, xs, wg, wu, wd)


SC_CORES = 2
SC_SUBCORES = 16
SC_WORKERS = SC_CORES * SC_SUBCORES
SC_CHUNK = 64


def _sc_mesh():
    return plsc.VectorSubcoreMesh(core_axis_name="c", subcore_axis_name="s",
                                  num_cores=SC_CORES, num_subcores=SC_SUBCORES)


def _sc_scatter_rows(src, idx0, idx1, n_out):
    t, d = src.shape
    per_w = t // SC_WORKERS
    n_chunks = per_w // SC_CHUNK
    idx_shape = (SC_WORKERS, n_chunks, SC_CHUNK)

    @functools.partial(
        pl.kernel, mesh=_sc_mesh(),
        out_type=jax.ShapeDtypeStruct((n_out, d), src.dtype),
        scratch_types=[pltpu.VMEM((n_chunks, SC_CHUNK), jnp.int32),
                       pltpu.VMEM((n_chunks, SC_CHUNK), jnp.int32),
                       pltpu.VMEM((2, SC_CHUNK, d), src.dtype),
                       pltpu.SemaphoreType.DMA((2,)), pltpu.SemaphoreType.DMA((2,))],
        name="sc_dispatch",
    )
    def k(src_hbm, i0_hbm, i1_hbm, out_hbm, i0_v, i1_v, rows_v, sem_in, sem_out):
        wid = lax.axis_index("s") * SC_CORES + lax.axis_index("c")
        base = wid * per_w
        pltpu.sync_copy(i0_hbm.at[wid], i0_v)
        pltpu.sync_copy(i1_hbm.at[wid], i1_v)

        def load(ci):
            s = ci % 2
            return pltpu.make_async_copy(src_hbm.at[pl.ds(base + ci * SC_CHUNK, SC_CHUNK)],
                                         rows_v.at[s], sem_in.at[s])

        def scatter(ci, idx_v):
            s = ci % 2
            return pltpu.make_async_copy(rows_v.at[s], out_hbm.at[idx_v.at[ci]], sem_out.at[s])

        load(0).start()
        for ci in range(n_chunks):
            load(ci).wait()
            if ci >= 1:
                scatter(ci - 1, i0_v).wait()
                scatter(ci - 1, i1_v).wait()
            if ci + 1 < n_chunks:
                load(ci + 1).start()
            scatter(ci, i0_v).start()
            scatter(ci, i1_v).start()
        scatter(n_chunks - 1, i0_v).wait()
        scatter(n_chunks - 1, i1_v).wait()

    return k(src, idx0.reshape(idx_shape), idx1.reshape(idx_shape))


def _sc_gather_rows(table, idx):
    n = idx.shape[0]
    d = table.shape[1]
    per_w = n // SC_WORKERS
    n_chunks = per_w // SC_CHUNK

    @functools.partial(
        pl.kernel, mesh=_sc_mesh(),
        out_type=jax.ShapeDtypeStruct((n, d), table.dtype),
        scratch_types=[pltpu.VMEM((n_chunks, SC_CHUNK), jnp.int32),
                       pltpu.VMEM((2, SC_CHUNK, d), table.dtype),
                       pltpu.SemaphoreType.DMA((2,)), pltpu.SemaphoreType.DMA((2,))],
        name="sc_collect",
    )
    def k(table_hbm, idx_hbm, out_hbm, idx_v, rows_v, sem_in, sem_out):
        wid = lax.axis_index("s") * SC_CORES + lax.axis_index("c")
        base = wid * per_w
        pltpu.sync_copy(idx_hbm.at[wid], idx_v)

        def gather(ci):
            s = ci % 2
            return pltpu.make_async_copy(table_hbm.at[idx_v.at[ci]], rows_v.at[s], sem_in.at[s])

        def store(ci):
            s = ci % 2
            return pltpu.make_async_copy(rows_v.at[s], out_hbm.at[pl.ds(base + ci * SC_CHUNK, SC_CHUNK)],
                                         sem_out.at[s])

        gather(0).start()
        for ci in range(n_chunks):
            gather(ci).wait()
            if ci >= 1:
                store(ci - 1).wait()
            if ci + 1 < n_chunks:
                gather(ci + 1).start()
            store(ci).start()
        store(n_chunks - 1).wait()

    return k(table, idx.reshape(SC_WORKERS, n_chunks, SC_CHUNK))


def _combine_kernel(h_ref, r_ref, y1_ref, y2_ref, g_ref, b_ref, o_ref, *, alpha):
    g1 = r_ref[:, 2:3]
    g2 = r_ref[:, 3:4]
    ffn = g1 * _unpack_bf16_pairs(y1_ref[...]) + g2 * _unpack_bf16_pairs(y2_ref[...])
    o_ref[...] = _layer_norm(alpha * h_ref[...] + ffn, g_ref[...], b_ref[...])


def _combine(h2, routing, y12, g, bb, alpha):
    t, d = h2.shape
    tm = 512
    nt = t // tm
    row = pl.BlockSpec((tm, d), lambda i: (i, 0))
    vec = pl.BlockSpec((1, d), lambda i: (0, 0))
    return pl.pallas_call(
        functools.partial(_combine_kernel, alpha=alpha),
        grid=(nt,),
        in_specs=[row, pl.BlockSpec((tm, LANES), lambda i: (i, 0)),
                  pl.BlockSpec((tm, d // 2), lambda i: (i, 0)),
                  pl.BlockSpec((tm, d // 2), lambda i: (nt + i, 0)), vec, vec],
        out_specs=row,
        out_shape=jax.ShapeDtypeStruct((t, d), F32),
        compiler_params=_cparams(("parallel",)),
        name="combine_ln",
    )(h2, routing, y12, y12, g.reshape(1, d), bb.reshape(1, d))


def _moe(h2, h2_packed, rg_w, rg_b, re_w, re_b, wg, wu, wd, l, g, bb, alpha):
    t, d = h2.shape
    routing, counts = _router(h2, rg_w, rg_b, re_w, re_b)
    eid = routing[:, 0:2].astype(jnp.int32)
    rank = routing[:, 4:6].astype(jnp.int32)
    cnt = counts[N_GROUPS:N_GROUPS + N_EXPERTS, 0].astype(jnp.int32)
    padded = (cnt + MOE_ROWS - 1) // MOE_ROWS * MOE_ROWS
    pad_end = jnp.cumsum(padded)
    pad_start = pad_end - padded
    experts = jnp.arange(N_EXPERTS, dtype=jnp.int32)
    dest = jnp.sum(jnp.where(eid[..., None] == experts, pad_start, 0), axis=-1) + rank
    n_slots = t * TOP_K
    n_blocks = (n_slots + N_EXPERTS * (MOE_ROWS - 1) + MOE_ROWS - 1) // MOE_ROWS
    cap = n_blocks * MOE_ROWS
    blk_row = jnp.arange(n_blocks, dtype=jnp.int32) * MOE_ROWS
    block_e = jnp.minimum(jnp.sum((pad_end[None, :] <= blk_row[:, None]).astype(jnp.int32), axis=1),
                          N_EXPERTS - 1)
    is_e = block_e[:, None] == experts
    blk_cnt = jnp.sum(jnp.where(is_e, cnt, 0), axis=1)
    blk_start = jnp.sum(jnp.where(is_e, pad_start, 0), axis=1)
    n_valid = jnp.clip(blk_cnt - (blk_row - blk_start), 0, MOE_ROWS).astype(jnp.int32)
    xs = _sc_scatter_rows(h2_packed, dest[:, 0], dest[:, 1], cap)
    ys = _moe_experts(xs, block_e, n_valid, wg, wu, wd, l)
    y12 = _sc_gather_rows(ys, jnp.concatenate([dest[:, 0], dest[:, 1]]))
    return _combine(h2, routing, y12, g, bb, alpha)


def _mixing_layer(h, b, s, l, p, lam_init):
    t = b * s
    alpha = (2 * p['w_in'].shape[0]) ** 0.25
    d = h.shape[1]
    w_in = p['w_in']
    tab_diff = _rope_tables(s, DIFF_QK_DIM)
    tab_dil = _rope_tables(s, HEAD_DIM)
    h3 = h.reshape(b, s, d)
    w_left = lax.slice_in_dim(w_in[l], 0, COL_GATE, axis=1).astype(BF16)
    qkv_na, qkv_diff, qkv_d0, qkv_d1, qkv_d2, u = _proj_branches(h3, w_left, tab_diff, tab_dil)
    y_a = _na_attention(qkv_na, _na_bias_table(p['na_rpb'][l]))
    y_b = _diff_attention(qkv_diff, p['diff_lam'][l], p['diff_subln_g'][l], lam_init)
    y_c = _pool(u, p['pool_w'][l], p['pool_scale'][l])
    dil = [_swa(qkv, s // r) for qkv, (_, r) in zip((qkv_d0, qkv_d1, qkv_d2), DIL_PATTERNS)]
    ys = [a.reshape(t, -1) for a in (y_a, y_b, y_c)]
    wg = lax.slice_in_dim(w_in[l], COL_GATE, COL_GATE + (len(ys) + 1) * d, axis=1).astype(BF16)
    return _merge(ys, dil, wg, p['b_gate'][l], p['w_branch'][l].astype(BF16), p['w_out'][l].astype(BF16), h,
                  p['ln1_g'][l], p['ln1_b'][l], alpha)


def kernel(x, emb_ln_g, emb_ln_b, w_in, b_gate, na_rpb, diff_lam, diff_subln_g, pool_w, pool_scale,
           w_branch, w_out, ln1_g, ln1_b, router_group_w, router_group_b, router_expert_w,
           router_expert_b, expert_w_gate, expert_w_up, expert_w_down, ln2_g, ln2_b):
    b, s, d = x.shape
    depth = w_in.shape[0]
    alpha = (2 * depth) ** 0.25
    p = dict(w_in=w_in, b_gate=b_gate, na_rpb=na_rpb, diff_lam=diff_lam, diff_subln_g=diff_subln_g,
             pool_w=pool_w, pool_scale=pool_scale, w_branch=w_branch, w_out=w_out, ln1_g=ln1_g, ln1_b=ln1_b)
    h = _embed_ln(x.reshape(b * s, d), emb_ln_g, emb_ln_b)
    for l in range(depth):
        lam_init = 0.8 - 0.6 * math.exp(-0.3 * l)
        h, h_packed = _mixing_layer(h, b, s, l, p, lam_init)
        h = _moe(h, h_packed, router_group_w[l], router_group_b[l], router_expert_w[l], router_expert_b[l],
                 expert_w_gate, expert_w_up, expert_w_down, l, ln2_g[l], ln2_b[l], alpha)
    return h.reshape(b, s, d)
```

```python
import functools
import math

import jax
import jax.numpy as jnp
import numpy as np
from jax import lax
from jax.experimental import pallas as pl
from jax.experimental.pallas import tpu as pltpu
from jax.experimental.pallas import tpu_sc as plsc

F32 = jnp.float32
BF16 = jnp.bfloat16

LANES = 128
GRID_W = 64
HEAD_DIM = 64
ROPE_THETA = 500000.0
LN_EPS = 1e-5
NA_ROWS = 8
NA_COLS = 16
DIFF_QK_DIM = 32
POOL_WINDOWS = (2, 4, 8, 16)
POOL_GROUP = 64
DIL_PATTERNS = ((128, 1), (512, 4), (2048, 16))
DIL_HALF = 64
N_GROUPS = 4
EXPERTS_PER_GROUP = 8
N_EXPERTS = N_GROUPS * EXPERTS_PER_GROUP
TOP_K = 2
MOE_ROWS = 512
NEG = -1e30

COL_NA = 0
COL_DIFF = 768
COL_POOL = 1536
COL_DIL = 1792
COL_GATE = 4096

VMEM_LIMIT = 56 * 1024 * 1024


def _cparams(sem, vmem=VMEM_LIMIT, flags=None):
    return pltpu.CompilerParams(dimension_semantics=sem, vmem_limit_bytes=vmem, flags=flags)


def _layer_norm(x, g, b):
    mu = jnp.mean(x, axis=-1, keepdims=True)
    xc = x - mu
    var = jnp.mean(xc * xc, axis=-1, keepdims=True)
    return xc * lax.rsqrt(var + LN_EPS) * g + b


def _dot(a, b):
    return jnp.dot(a, b, preferred_element_type=F32)


def _dot_nt(a, b):
    return lax.dot_general(a, b, (((1,), (1,)), ((), ())), preferred_element_type=F32)


def _lane(shape):
    return lax.broadcasted_iota(jnp.int32, shape, len(shape) - 1)


HI16 = -65536


def _pack_bf16_pairs(x):
    w = x.shape[1] // 2
    hi = lax.bitcast_convert_type(x[:, :w].astype(BF16).astype(F32), jnp.int32)
    lo = lax.bitcast_convert_type(x[:, w:].astype(BF16).astype(F32), jnp.int32)
    return (hi & HI16) | lax.shift_right_logical(lo, 16)


def _unpack_bf16_pairs(p):
    hi = lax.bitcast_convert_type(p & HI16, F32)
    lo = lax.bitcast_convert_type(lax.shift_left(p, 16), F32)
    return jnp.concatenate([hi, lo], axis=1)


def _ln_kernel(x_ref, g_ref, b_ref, o_ref):
    o_ref[...] = _layer_norm(x_ref[...], g_ref[...], b_ref[...])


def _embed_ln(x2, g, b):
    t, d = x2.shape
    tm = 1024
    return pl.pallas_call(
        _ln_kernel,
        grid=(t // tm,),
        in_specs=[pl.BlockSpec((tm, d), lambda i: (i, 0)),
                  pl.BlockSpec((1, d), lambda i: (0, 0)),
                  pl.BlockSpec((1, d), lambda i: (0, 0))],
        out_specs=pl.BlockSpec((tm, d), lambda i: (i, 0)),
        out_shape=jax.ShapeDtypeStruct((t, d), F32),
        compiler_params=_cparams(("parallel",)),
        name="embed_ln",
    )(x2, g.reshape(1, d), b.reshape(1, d))


PROJ_TM = 1024
QKV_W = 6 * LANES
WCOL = 256


QKV_TM = 512
QKV_SEGMENTS = ((COL_NA, 1, 0, HEAD_DIM), (COL_DIFF, 1, DIFF_QK_DIM // 8, DIFF_QK_DIM),
                (COL_DIL, 1, HEAD_DIM // 8, HEAD_DIM), (COL_DIL + QKV_W, 4, HEAD_DIM // 8, HEAD_DIM),
                (COL_DIL + 2 * QKV_W, 16, HEAD_DIM // 8, HEAD_DIM))


def _qkv_epilogue(z, t_ref, o_ref, zs_ref, r, half, q_scale):
    tm = z.shape[0]
    n = tm // r
    for c in range(QKV_W // LANES):
        blk = z[:, c * LANES:(c + 1) * LANES]
        if half and c < 4:
            blk = _rope(blk, t_ref, half)
        if c < 2:
            blk = blk * q_scale
        if r == 1:
            o_ref[0, 0, :, c * LANES:(c + 1) * LANES] = blk.astype(BF16)
        else:
            zs_ref[c] = blk
            for m in range(r):
                o_ref[0, m, :, c * LANES:(c + 1) * LANES] = zs_ref[
                    c, pl.ds(m, n, stride=r), :].astype(BF16)


def _proj_kernel(x_ref, w_ref, td_ref, tl_ref, na_ref, df_ref, d0_ref, d1_ref, d2_ref, u_ref,
                 zs1_ref, zs2_ref):
    xb = x_ref[0].astype(BF16)
    outs = (na_ref, df_ref, d0_ref, d1_ref, d2_ref)
    stage = (None, None, None, zs1_ref, zs2_ref)
    for (col0, r, half, head_w), o_ref, zs_ref in zip(QKV_SEGMENTS, outs, stage):
        z = _dot(xb, w_ref[:, col0:col0 + QKV_W])
        t_ref = td_ref if head_w == DIFF_QK_DIM else tl_ref
        _qkv_epilogue(z, t_ref, o_ref, zs_ref, r, half, head_w ** -0.5 * LOG2E)
    u_ref[0] = _dot(xb, w_ref[:, COL_POOL:COL_POOL + WCOL])


def _proj_branches(h3, w_bf16, tab_diff, tab_dil):
    b, s, d = h3.shape
    tm = QKV_TM
    qkv_spec = lambda r: pl.BlockSpec((1, r, tm // r, QKV_W), lambda bi, i: (bi, 0, i, 0))
    qkv_shape = lambda r: jax.ShapeDtypeStruct((b, r, s // r, QKV_W), BF16)
    tab_spec = pl.BlockSpec((3, tm, LANES), lambda bi, i: (0, i, 0))
    rs = [seg[1] for seg in QKV_SEGMENTS]
    outs = pl.pallas_call(
        _proj_kernel,
        grid=(b, s // tm),
        in_specs=[pl.BlockSpec((1, tm, d), lambda bi, i: (bi, i, 0)),
                  pl.BlockSpec(w_bf16.shape, lambda bi, i: (0, 0)),
                  tab_spec, tab_spec],
        out_specs=[qkv_spec(r) for r in rs] + [pl.BlockSpec((1, tm, WCOL), lambda bi, i: (bi, i, 0))],
        out_shape=[qkv_shape(r) for r in rs] + [jax.ShapeDtypeStruct((b, s, WCOL), F32)],
        scratch_shapes=[pltpu.VMEM((QKV_W // LANES, tm, LANES), F32)] * 2,
        compiler_params=_cparams(("parallel", "parallel")),
        name="proj_branches",
    )(h3, w_bf16, tab_diff, tab_dil)
    return [o.reshape(b, s, QKV_W) for o in outs[:-1]] + [outs[-1]]


def _rope_tables(seq, head_w):
    rot = head_w // 4
    half = rot // 2
    inv_freq = jnp.exp(jnp.arange(half, dtype=F32) * (-2.0 * math.log(ROPE_THETA) / rot))
    ang = jnp.arange(seq, dtype=jnp.int32).astype(F32)[:, None] * inv_freq[None, :]
    cos, sin = jnp.cos(ang), jnp.sin(ang)
    zero = jnp.zeros((seq, head_w - rot), F32)
    zh = jnp.zeros((seq, half), F32)
    t0 = jnp.concatenate([cos, cos, jnp.ones((seq, head_w - rot), F32)], axis=1)
    t1 = jnp.concatenate([-sin, zh, zero], axis=1)
    t2 = jnp.concatenate([zh, sin, zero], axis=1)
    reps = LANES // head_w
    return jnp.stack([jnp.tile(t0, (1, reps)), jnp.tile(t1, (1, reps)), jnp.tile(t2, (1, reps))])


def _rope(x, t_ref, half):
    return (x * t_ref[0] + pltpu.roll(x, LANES - half, 1) * t_ref[1]
            + pltpu.roll(x, half, 1) * t_ref[2])


LOG2E = math.log2(math.e)
LN2 = math.log(2.0)


def _na_bias_table(rpb):
    kr, kc = NA_ROWS, NA_COLS
    n_heads = rpb.shape[0]
    col = np.arange(GRID_W)
    col_start = np.clip(col - kc // 2, 0, GRID_W - kc)
    in_win = (col[None, :] >= col_start[:, None]) & (col[None, :] < col_start[:, None] + kc)
    dc = np.clip(col[None, :] - col[:, None] + kc - 1, 0, 2 * kc - 2)
    sel_c = ((dc[..., None] == np.arange(2 * kc - 1)) & in_win[..., None]).astype(np.float32)
    rp = rpb.astype(F32).reshape(n_heads // 2, 2, 2 * kr - 1, 2 * kc - 1) * LOG2E
    tiles = jnp.einsum('phaj,qxj->phaqx', rp, jnp.asarray(sel_c), precision=lax.Precision.HIGHEST)
    tiles = tiles + jnp.asarray(np.where(in_win, 0.0, NEG).astype(np.float32))
    tiles = jnp.concatenate([tiles, jnp.full_like(tiles[:, :, :1], NEG)], axis=2)
    return jnp.concatenate([tiles, tiles], axis=-1)


NA_QROWS = 4
NA_KROWS = 12
NA_MASKED = 2 * NA_ROWS - 1


def _na_row_offsets(n_rows):
    table = []
    for blk in (0, 1, n_rows // NA_QROWS - 1):
        ws = _na_window_start(blk, n_rows)
        rows = []
        for i in range(NA_QROWS):
            r = blk * NA_QROWS + i
            start = min(max(r - NA_ROWS // 2, 0), n_rows - NA_ROWS)
            rows.append([ws + kk - r + NA_ROWS - 1 if start <= ws + kk < start + NA_ROWS else NA_MASKED
                         for kk in range(NA_KROWS)])
        table.append(rows)
    return table


def _na_window_start(blk, n_rows):
    lo = blk * NA_QROWS - NA_ROWS // 2
    hi = n_rows - NA_KROWS
    if isinstance(blk, int):
        return min(max(lo, 0), hi)
    return jnp.clip(lo, 0, hi)


def _stack_heads(q, lane):
    zero = jnp.zeros_like(q)
    return jnp.concatenate([jnp.where(lane < HEAD_DIM, q, zero), jnp.where(lane < HEAD_DIM, zero, q)],
                           axis=0)


def _na_kernel(q_ref, k_ref, v_ref, tile_ref, o_ref, bias_ref, *, blocks_per_step, n_rows):
    step = pl.program_id(2)
    nq = NA_QROWS * GRID_W
    kwin = NA_KROWS * GRID_W
    lane = _lane((nq, LANES))
    last_blk = n_rows // NA_QROWS - 1

    @pl.when(step == 0)
    def _():
        low = _lane((GRID_W, LANES)) < GRID_W
        for case, rows in enumerate(_na_row_offsets(n_rows)):
            for hh in range(2):
                for i, offs in enumerate(rows):
                    r0 = (hh * NA_QROWS + i) * GRID_W
                    for kp in range(NA_KROWS // 2):
                        bias_ref[case, r0:r0 + GRID_W, kp * LANES:(kp + 1) * LANES] = jnp.where(
                            low, tile_ref[0, hh, offs[2 * kp]], tile_ref[0, hh, offs[2 * kp + 1]])

    for i in range(blocks_per_step):
        blk = step * blocks_per_step + i
        case = jnp.where(blk == 0, 0, jnp.where(blk == last_blk, 2, 1))
        k0 = pl.multiple_of(_na_window_start(blk, n_rows) * GRID_W, GRID_W)
        q2 = _stack_heads(q_ref[0, i * nq:(i + 1) * nq, :], lane)
        sc = _dot_nt(q2, k_ref[0, pl.ds(k0, kwin), :]) + bias_ref[case]
        m = jnp.max(sc, axis=-1, keepdims=True)
        e = jnp.exp2(sc - m)
        den = jnp.sum(e, axis=-1, keepdims=True)
        pv = _dot(e.astype(BF16), v_ref[0, pl.ds(k0, kwin), :]) / den
        o = jnp.where(lane < HEAD_DIM, pv[0:nq], pv[nq:2 * nq])
        o_ref[0, i * nq:(i + 1) * nq, :] = o.astype(BF16)


def _na_attention(qkv, tiles):
    b, s, _ = qkv.shape
    n_rows = s // GRID_W
    bps = 4
    rps = bps * NA_QROWS
    tq = rps * GRID_W
    return pl.pallas_call(
        functools.partial(_na_kernel, blocks_per_step=bps, n_rows=n_rows),
        grid=(b, 2, n_rows // rps),
        in_specs=[pl.BlockSpec((1, tq, LANES), lambda bi, hp, i: (bi, i, hp)),
                  pl.BlockSpec((1, s, LANES), lambda bi, hp, i: (bi, 0, 2 + hp)),
                  pl.BlockSpec((1, s, LANES), lambda bi, hp, i: (bi, 0, 4 + hp)),
                  pl.BlockSpec((1, 2, NA_MASKED + 1, GRID_W, LANES), lambda bi, hp, i: (hp, 0, 0, 0, 0))],
        out_specs=pl.BlockSpec((1, tq, LANES), lambda bi, hp, i: (bi, i, hp)),
        out_shape=jax.ShapeDtypeStruct((b, s, 2 * LANES), BF16),
        scratch_shapes=[pltpu.VMEM((3, 2 * NA_QROWS * GRID_W, NA_KROWS * GRID_W), F32)],
        compiler_params=_cparams(("parallel", "parallel", "arbitrary")),
        name="na_attn",
    )(qkv, qkv, qkv, tiles)


def _diff_kernel(lam_ref, q_ref, k_ref, v_ref, g_ref, o_ref, *, lam_init, rows):
    tq = q_ref.shape[1]
    lane = _lane((rows, LANES))
    dl = lam_ref[...]
    lam = (jnp.exp(jnp.sum(dl[0:1] * dl[1:2], axis=-1, keepdims=True))
           - jnp.exp(jnp.sum(dl[2:3] * dl[3:4], axis=-1, keepdims=True)) + lam_init)
    v = v_ref[0]
    v_lane = _lane(v.shape)
    one = jnp.ones_like(v)
    v_ext = (jnp.where(v_lane < HEAD_DIM, v, one), jnp.where(v_lane < HEAD_DIM, one, v))
    in_h0 = lane < HEAD_DIM
    for r0 in range(0, tq, rows):
        q = q_ref[0, r0:r0 + rows, :]
        zero = jnp.zeros_like(q)
        pv = []
        for lo in range(0, LANES, DIFF_QK_DIM):
            qm = jnp.where((lane >= lo) & (lane < lo + DIFF_QK_DIM), q, zero)
            sc = _dot_nt(qm, k_ref[0])
            e = jnp.exp2(sc - jnp.max(sc, axis=-1, keepdims=True))
            ev = _dot(e.astype(BF16), v_ext[lo // HEAD_DIM])
            pv.append(ev / pltpu.roll(ev, HEAD_DIM, 1))
        o = jnp.where(in_h0, pv[0] - lam * pv[1], pv[2] - lam * pv[3])
        o2 = o * o
        ms0 = jnp.sum(jnp.where(in_h0, o2, 0.0), axis=-1, keepdims=True) / HEAD_DIM
        ms1 = jnp.sum(jnp.where(in_h0, 0.0, o2), axis=-1, keepdims=True) / HEAD_DIM
        ms = jnp.where(in_h0, ms0, ms1)
        o = o * lax.rsqrt(ms + LN_EPS) * g_ref[...] * (1.0 - lam_init)
        o_ref[0, r0:r0 + rows, :] = o.astype(BF16)


def _diff_attention(qkv, diff_lam, subln_g, lam_init):
    b, s, _ = qkv.shape
    tq = 1024
    g2 = jnp.tile(subln_g.reshape(1, HEAD_DIM), (1, 2))
    return pl.pallas_call(
        functools.partial(_diff_kernel, lam_init=lam_init, rows=512),
        grid=(b, 2, s // tq),
        in_specs=[pl.BlockSpec((4, DIFF_QK_DIM), lambda bi, hp, i: (0, 0)),
                  pl.BlockSpec((1, tq, LANES), lambda bi, hp, i: (bi, i, hp)),
                  pl.BlockSpec((1, s, LANES), lambda bi, hp, i: (bi, 0, 2 + hp)),
                  pl.BlockSpec((1, s, LANES), lambda bi, hp, i: (bi, 0, 4 + hp)),
                  pl.BlockSpec((1, LANES), lambda bi, hp, i: (0, 0))],
        out_specs=pl.BlockSpec((1, tq, LANES), lambda bi, hp, i: (bi, i, hp)),
        out_shape=jax.ShapeDtypeStruct((b, s, 2 * LANES), BF16),
        compiler_params=_cparams(("parallel", "parallel", "arbitrary")),
        name="diff_attn",
    )(diff_lam, qkv, qkv, qkv, g2)


POOL_PAD = 16
POOL_CHUNK = 512
POOL_HALO = 8


def _pool_kernel(u_ref, w_ref, sc_ref, o_ref, p_ref):
    s = u_ref.shape[1]
    width = u_ref.shape[2]
    p_ref[0:POOL_PAD, :] = jnp.zeros((POOL_PAD, width), F32)
    p_ref[POOL_PAD + s:POOL_PAD + s + POOL_PAD, :] = jnp.zeros((POOL_PAD, width), F32)
    p_ref[POOL_PAD:POOL_PAD + s, :] = u_ref[0]
    n = POOL_CHUNK + 2 * POOL_HALO
    lane = _lane((POOL_CHUNK, width))
    row = lax.broadcasted_iota(jnp.int32, (POOL_CHUNK, width), 0)
    w_of_lane = jnp.where(lane < POOL_GROUP, 2, jnp.where(lane < 2 * POOL_GROUP, 4,
                          jnp.where(lane < 3 * POOL_GROUP, 8, 16)))

    def body(ci, carry):
        c0 = pl.multiple_of(ci * POOL_CHUNK, POOL_CHUNK)
        x = p_ref[pl.ds(c0 + POOL_PAD - POOL_HALO, n), :]
        w2 = x + pltpu.roll(x, 1, 0)
        w4 = pltpu.roll(w2, 1, 0) + pltpu.roll(w2, n - 1, 0)
        w8 = pltpu.roll(w4, 2, 0) + pltpu.roll(w4, n - 2, 0)
        w16 = pltpu.roll(w8, 4, 0) + pltpu.roll(w8, n - 4, 0)
        u = x[POOL_HALO:POOL_HALO + POOL_CHUNK]
        wsum = jnp.where(lane < POOL_GROUP, w2[POOL_HALO:POOL_HALO + POOL_CHUNK],
                         jnp.where(lane < 2 * POOL_GROUP, w4[POOL_HALO:POOL_HALO + POOL_CHUNK],
                                   jnp.where(lane < 3 * POOL_GROUP, w8[POOL_HALO:POOL_HALO + POOL_CHUNK],
                                             w16[POOL_HALO:POOL_HALO + POOL_CHUNK])))
        t = row + c0
        half_w = w_of_lane // 2
        lo = jnp.maximum(t - half_w, 0)
        hi = jnp.minimum(t + w_of_lane - 1 - half_w, s - 1)
        cnt = (hi - lo + 1).astype(F32)
        dlt = wsum / cnt - u
        y = _dot(dlt.astype(BF16), w_ref[...]) * sc_ref[...]
        o_ref[0, pl.ds(c0, POOL_CHUNK), :] = y.astype(BF16)
        return carry

    lax.fori_loop(0, s // POOL_CHUNK, body, 0)


def _pool(u3, pool_w, pool_scale):
    b, s, width = u3.shape
    wbd = jax.scipy.linalg.block_diag(*[pool_w[g] for g in range(len(POOL_WINDOWS))]).astype(BF16)
    return pl.pallas_call(
        _pool_kernel,
        grid=(b,),
        in_specs=[pl.BlockSpec((1, s, width), lambda bi: (bi, 0, 0)),
                  pl.BlockSpec((width, width), lambda bi: (0, 0)),
                  pl.BlockSpec((1, width), lambda bi: (0, 0))],
        out_specs=pl.BlockSpec((1, s, width), lambda bi: (bi, 0, 0)),
        out_shape=jax.ShapeDtypeStruct((b, s, width), BF16),
        scratch_shapes=[pltpu.VMEM((s + 2 * POOL_PAD, width), F32)],
        compiler_params=_cparams(("parallel",)),
        name="pool",
    )(u3, wbd, pool_scale.reshape(1, width))


SWA_Q = 128
SWA_BAND = SWA_Q + 2 * DIL_HALF


def _swa_kernel(q_ref, k_ref, v_ref, o_ref, l_ref, *, length):
    lane = _lane((SWA_Q, LANES))
    rel = (lax.broadcasted_iota(jnp.int32, (2 * SWA_Q, SWA_BAND), 1)
           - (lax.broadcasted_iota(jnp.int32, (2 * SWA_Q, SWA_BAND), 0) & (SWA_Q - 1)))
    masks = {}
    for i in range(q_ref.shape[1] // SWA_Q):
        r0 = i * SWA_Q
        l0 = r0 % length
        lo = r0 - l0 + min(max(l0 - DIL_HALF, 0), length - SWA_BAND)
        if lo - r0 not in masks:
            d = rel + (lo - r0)
            masks[lo - r0] = jnp.where((d >= -DIL_HALF) & (d <= DIL_HALF), 0.0, NEG)
        q2 = _stack_heads(q_ref[0, r0:r0 + SWA_Q, :], lane)
        sc = _dot_nt(q2, k_ref[0, lo:lo + SWA_BAND, :]) + masks[lo - r0]
        m = jnp.max(sc, axis=-1, keepdims=True)
        e = jnp.exp2(sc - m)
        den = jnp.sum(e, axis=-1, keepdims=True)
        pv = _dot(e.astype(BF16), v_ref[0, lo:lo + SWA_BAND, :]) / den
        lse2 = m + jnp.log2(den)
        o_ref[0, i * SWA_Q:(i + 1) * SWA_Q, :] = jnp.where(lane < HEAD_DIM, pv[0:SWA_Q], pv[SWA_Q:])
        l_ref[0, i * SWA_Q:(i + 1) * SWA_Q, :] = jnp.where(lane < HEAD_DIM, lse2[0:SWA_Q], lse2[SWA_Q:])


def _swa(qkv, length):
    b, s, _ = qkv.shape
    shp = jax.ShapeDtypeStruct((b, s, 2 * LANES), F32)
    seq = lambda col: pl.BlockSpec((1, s, LANES), lambda bi, hp: (bi, 0, col + hp))
    return pl.pallas_call(
        functools.partial(_swa_kernel, length=length),
        grid=(b, 2),
        in_specs=[seq(0), seq(2), seq(4)],
        out_specs=[seq(0), seq(0)],
        out_shape=[shp, shp],
        compiler_params=_cparams(("parallel", "parallel")),
        name=f"swa_l{length}",
    )(qkv, qkv, qkv)


def _dil_combine(dil_refs, stage_refs, tm):
    (o0_ref, l0_ref), (o1_ref, l1_ref), (o2_ref, l2_ref) = dil_refs
    so1, sl1, so2, sl2 = stage_refs
    r1 = DIL_PATTERNS[1][1]
    r2 = DIL_PATTERNS[2][1]
    halves = []
    for hp in range(2):
        cols = slice(hp * LANES, (hp + 1) * LANES)
        for m in range(r1):
            so1[pl.ds(m, tm // r1, stride=r1), :] = o1_ref[0, m, :, cols]
            sl1[pl.ds(m, tm // r1, stride=r1), :] = l1_ref[0, m, :, cols]
        for m in range(r2):
            so2[pl.ds(m, tm // r2, stride=r2), :] = o2_ref[0, m, :, cols]
            sl2[pl.ds(m, tm // r2, stride=r2), :] = l2_ref[0, m, :, cols]
        l0, l1, l2 = l0_ref[0, 0, :, cols], sl1[...], sl2[...]
        mx = jnp.maximum(jnp.maximum(l0, l1), l2)
        e0, e1, e2 = jnp.exp2(l0 - mx), jnp.exp2(l1 - mx), jnp.exp2(l2 - mx)
        den = e0 + e1 + e2
        halves.append((e0 / den) * o0_ref[0, 0, :, cols] + (e1 / den) * so1[...] + (e2 / den) * so2[...])
    return jnp.concatenate(halves, axis=1)


def _merge_kernel(ya_ref, yb_ref, yc_ref, o0_ref, l0_ref, o1_ref, l1_ref, o2_ref, l2_ref,
                  wg_ref, bg_ref, wb_ref, wo_ref, h_ref, g_ref, b_ref,
                  o_ref, op_ref, so1, sl1, so2, sl2, *, alpha):
    tm, d = h_ref.shape
    hb = h_ref[...].astype(BF16)
    y_d = _dil_combine(((o0_ref, l0_ref), (o1_ref, l1_ref), (o2_ref, l2_ref)),
                       (so1, sl1, so2, sl2), tm).astype(BF16)
    merged = None
    for n, y in enumerate((ya_ref[...], yb_ref[...], yc_ref[...], y_d)):
        zg = _dot(hb, wg_ref[:, n * d:(n + 1) * d]) + bg_ref[:, n * d:(n + 1) * d]
        term = (1.0 / (1.0 + jnp.exp(-zg))) * _dot(y, wb_ref[n])
        merged = term if merged is None else merged + term
    mix = _dot(merged.astype(BF16), wo_ref[...])
    h1 = _layer_norm(alpha * h_ref[...] + mix, g_ref[...], b_ref[...])
    o_ref[...] = h1
    op_ref[...] = _pack_bf16_pairs(h1)


def _merge(ys, dil, wg, b_gate, wb, wo, h2, g, bb, alpha):
    t, d = h2.shape
    b, s, bw = dil[0][0].shape
    nb = len(ys) + 1
    tm = QKV_TM
    nt = s // tm
    yspec = pl.BlockSpec((tm, bw), lambda i: (i, 0))
    vec = pl.BlockSpec((1, d), lambda i: (0, 0))
    dil_args, dil_specs = [], []
    for (o, lse), (_, r) in zip(dil, DIL_PATTERNS):
        spec = pl.BlockSpec((1, r, tm // r, bw), lambda i: (i // nt, 0, i % nt, 0))
        dil_args += [o.reshape(b, r, s // r, bw), lse.reshape(b, r, s // r, bw)]
        dil_specs += [spec, spec]
    return pl.pallas_call(
        functools.partial(_merge_kernel, alpha=alpha),
        grid=(t // tm,),
        in_specs=[yspec, yspec, yspec] + dil_specs + [
                  pl.BlockSpec((d, nb * d), lambda i: (0, 0)),
                  pl.BlockSpec((1, nb * d), lambda i: (0, 0)),
                  pl.BlockSpec((nb, bw, d), lambda i: (0, 0, 0)),
                  pl.BlockSpec((d, d), lambda i: (0, 0)),
                  pl.BlockSpec((tm, d), lambda i: (i, 0)),
                  vec, vec],
        out_specs=[pl.BlockSpec((tm, d), lambda i: (i, 0)),
                   pl.BlockSpec((tm, d // 2), lambda i: (i, 0))],
        out_shape=[jax.ShapeDtypeStruct((t, d), F32), jax.ShapeDtypeStruct((t, d // 2), jnp.int32)],
        scratch_shapes=[pltpu.VMEM((tm, LANES), F32)] * 4,
        compiler_params=_cparams(("parallel",)),
        name="merge",
    )(*ys, *dil_args, wg, b_gate.reshape(1, nb * d), wb, wo, h2, g.reshape(1, d), bb.reshape(1, d))


ROUTER_TM = 512


def _router_kernel(h_ref, w_ref, b_ref, o_ref, cnt_ref, carry, before):
    tm = h_ref.shape[0]

    @pl.when(pl.program_id(0) == 0)
    def _():
        carry[...] = jnp.zeros_like(carry)
        before[...] = (lax.broadcasted_iota(jnp.int32, (tm, tm), 0)
                       < lax.broadcasted_iota(jnp.int32, (tm, tm), 1)).astype(BF16)

    logits = lax.dot_general(w_ref[...], h_ref[...], (((1,), (1,)), ((), ())),
                             preferred_element_type=F32,
                             precision=lax.Precision.HIGHEST) + b_ref[...]
    row = lax.broadcasted_iota(jnp.int32, (LANES, tm), 0)
    big = jnp.int32(1 << 20)
    is_g = row < N_GROUPS
    gl = jnp.where(is_g, logits, -jnp.inf)
    gmax = jnp.max(gl, axis=0, keepdims=True)
    gsel = jnp.min(jnp.where(is_g & (gl == gmax), row, big), axis=0, keepdims=True)
    pg = 1.0 / jnp.sum(jnp.exp(gl - gmax), axis=0, keepdims=True)
    e_lo = N_GROUPS + gsel * EXPERTS_PER_GROUP
    in_grp = (row >= e_lo) & (row < e_lo + EXPERTS_PER_GROUP)
    el = jnp.where(in_grp, logits, -jnp.inf)
    v1 = jnp.max(el, axis=0, keepdims=True)
    i1 = jnp.min(jnp.where(in_grp & (el == v1), row, big), axis=0, keepdims=True)
    el2 = jnp.where(row == i1, -jnp.inf, el)
    v2 = jnp.max(el2, axis=0, keepdims=True)
    i2 = jnp.min(jnp.where(in_grp & (row != i1) & (el2 == v2), row, big), axis=0, keepdims=True)
    t2 = jnp.exp(v2 - v1)
    g1 = pg / (1.0 + t2)
    g2 = pg * t2 / (1.0 + t2)
    oh1 = (row == i1)
    oh2 = (row == i2)
    oh1b = oh1.astype(BF16)
    oh2b = oh2.astype(BF16)
    c0 = carry[...]
    c1 = c0 + jnp.sum(oh1b.astype(F32), axis=1, keepdims=True)
    rank1 = jnp.sum(jnp.where(oh1, _dot(oh1b, before[...]) + c0, 0.0), axis=0, keepdims=True)
    rank2 = jnp.sum(jnp.where(oh2, _dot(oh2b, before[...]) + c1, 0.0), axis=0, keepdims=True)
    c2 = c1 + jnp.sum(oh2b.astype(F32), axis=1, keepdims=True)
    carry[...] = c2
    cnt_ref[...] = c2
    e1 = (i1 - N_GROUPS).astype(F32)
    e2 = (i2 - N_GROUPS).astype(F32)
    out_t = jnp.where(row == 0, e1, jnp.where(row == 1, e2, jnp.where(row == 2, g1, jnp.where(
        row == 3, g2, jnp.where(row == 4, rank1, jnp.where(row == 5, rank2, 0.0))))))
    o_ref[...] = out_t.T


def _router(h2, rg_w, rg_b, re_w, re_b):
    t, d = h2.shape
    pad = LANES - N_GROUPS - N_EXPERTS
    w_t = jnp.concatenate([rg_w.T, re_w.T, jnp.zeros((pad, d), F32)], axis=0)
    bias = jnp.concatenate([rg_b, re_b, jnp.zeros((pad,), F32)]).reshape(LANES, 1)
    tm = ROUTER_TM
    return pl.pallas_call(
        _router_kernel,
        grid=(t // tm,),
        in_specs=[pl.BlockSpec((tm, d), lambda i: (i, 0)),
                  pl.BlockSpec((LANES, d), lambda i: (0, 0)),
                  pl.BlockSpec((LANES, 1), lambda i: (0, 0))],
        out_specs=[pl.BlockSpec((tm, LANES), lambda i: (i, 0)),
                   pl.BlockSpec((LANES, 1), lambda i: (0, 0))],
        out_shape=[jax.ShapeDtypeStruct((t, LANES), F32), jax.ShapeDtypeStruct((LANES, 1), F32)],
        scratch_shapes=[pltpu.VMEM((LANES, 1), F32), pltpu.VMEM((tm, tm), BF16)],
        compiler_params=_cparams(("arbitrary",)),
        name="router",
    )(h2, w_t, bias)


MOE_FIRST, MOE_SLOT, MOE_HAS_NEXT = 1, 2, 4


def _moe_kernel(be_ref, nv_ref, fl_ref, nx_ref, x_ref, wg_hbm, wu_hbm, wd_hbm, o_ref,
                wgb, wub, wdb, stage_g, stage_u, stage_d, sem, *, layer):
    j = pl.program_id(0)
    n_valid = nv_ref[j]
    flags = fl_ref[j]
    slot = (flags // MOE_SLOT) % 2

    def weight_copies(e, s):
        return [pltpu.make_async_copy(w_hbm.at[layer, e], stage.at[s], sem.at[s, k])
                for k, (w_hbm, stage) in enumerate(((wg_hbm, stage_g), (wu_hbm, stage_u),
                                                    (wd_hbm, stage_d)))]

    @pl.when(flags % 2 == MOE_FIRST)
    def _():
        @pl.when(j == 0)
        def _():
            for cp in weight_copies(be_ref[j], slot):
                cp.start()

        for cp in weight_copies(be_ref[j], slot):
            cp.wait()

        @pl.when((flags // MOE_HAS_NEXT) % 2 == 1)
        def _():
            for cp in weight_copies(nx_ref[j], 1 - slot):
                cp.start()

        wgb[...] = stage_g[slot].astype(BF16)
        wub[...] = stage_u[slot].astype(BF16)
        wdb[...] = stage_d[slot].astype(BF16)

    @pl.when(n_valid > 0)
    def _():
        row = lax.broadcasted_iota(jnp.int32, x_ref.shape, 0)
        x = _unpack_bf16_pairs(jnp.where(row < n_valid, x_ref[...], 0)).astype(BF16)
        g = _dot(x, wgb[...])
        u = _dot(x, wub[...])
        hmid = (g * (1.0 / (1.0 + jnp.exp(-g)))) * u
        o_ref[...] = _pack_bf16_pairs(_dot(hmid.astype(BF16), wdb[...]))

    @pl.when(n_valid <= 0)
    def _():
        o_ref[...] = jnp.zeros_like(o_ref)


def _moe_experts(xs, block_e, n_valid, flags, next_e, wg, wu, wd, l):
    cap, dp = xs.shape
    d = 2 * dp
    de = wg.shape[3]
    n_blocks = cap // MOE_ROWS
    hbm = pl.BlockSpec(memory_space=pl.ANY)
    grid_spec = pltpu.PrefetchScalarGridSpec(
        num_scalar_prefetch=4,
        grid=(n_blocks,),
        in_specs=[pl.BlockSpec((MOE_ROWS, dp), lambda j, *_: (j, 0)), hbm, hbm, hbm],
        out_specs=pl.BlockSpec((MOE_ROWS, dp), lambda j, *_: (j, 0)),
        scratch_shapes=[pltpu.VMEM((d, de), BF16), pltpu.VMEM((d, de), BF16), pltpu.VMEM((de, d), BF16),
                        pltpu.VMEM((2, d, de), F32), pltpu.VMEM((2, d, de), F32),
                        pltpu.VMEM((2, de, d), F32), pltpu.SemaphoreType.DMA((2, 3))],
    )
    return pl.pallas_call(
        functools.partial(_moe_kernel, layer=l),
        grid_spec=grid_spec,
        out_shape=jax.ShapeDtypeStruct((cap, dp), jnp.int32),
        compiler_params=_cparams(("arbitrary",)),
        name="moe_experts",
    )(block_e, n_valid, flags, next_e, xs, wg, wu, wd)


SC_CORES = 2
SC_SUBCORES = 16
SC_WORKERS = SC_CORES * SC_SUBCORES
SC_CHUNK = 64


def _sc_mesh():
    return plsc.VectorSubcoreMesh(core_axis_name="c", subcore_axis_name="s",
                                  num_cores=SC_CORES, num_subcores=SC_SUBCORES)


def _sc_scatter_rows(src, idx0, idx1, n_out):
    t, d = src.shape
    per_w = t // SC_WORKERS
    n_chunks = per_w // SC_CHUNK
    idx_shape = (SC_WORKERS, n_chunks, SC_CHUNK)

    @functools.partial(
        pl.kernel, mesh=_sc_mesh(),
        out_type=jax.ShapeDtypeStruct((n_out, d), src.dtype),
        scratch_types=[pltpu.VMEM((n_chunks, SC_CHUNK), jnp.int32),
                       pltpu.VMEM((n_chunks, SC_CHUNK), jnp.int32),
                       pltpu.VMEM((2, SC_CHUNK, d), src.dtype),
                       pltpu.SemaphoreType.DMA((2,)), pltpu.SemaphoreType.DMA((2,))],
        name="sc_dispatch",
    )
    def k(src_hbm, i0_hbm, i1_hbm, out_hbm, i0_v, i1_v, rows_v, sem_in, sem_out):
        wid = lax.axis_index("s") * SC_CORES + lax.axis_index("c")
        base = wid * per_w
        pltpu.sync_copy(i0_hbm.at[wid], i0_v)
        pltpu.sync_copy(i1_hbm.at[wid], i1_v)

        def load(ci):
            s = ci % 2
            return pltpu.make_async_copy(src_hbm.at[pl.ds(base + ci * SC_CHUNK, SC_CHUNK)],
                                         rows_v.at[s], sem_in.at[s])

        def scatter(ci, idx_v):
            s = ci % 2
            return pltpu.make_async_copy(rows_v.at[s], out_hbm.at[idx_v.at[ci]], sem_out.at[s])

        load(0).start()
        for ci in range(n_chunks):
            load(ci).wait()
            if ci >= 1:
                scatter(ci - 1, i0_v).wait()
                scatter(ci - 1, i1_v).wait()
            if ci + 1 < n_chunks:
                load(ci + 1).start()
            scatter(ci, i0_v).start()
            scatter(ci, i1_v).start()
        scatter(n_chunks - 1, i0_v).wait()
        scatter(n_chunks - 1, i1_v).wait()

    return k(src, idx0.reshape(idx_shape), idx1.reshape(idx_shape))


def _sc_gather_rows(table, idx):
    n = idx.shape[0]
    d = table.shape[1]
    per_w = n // SC_WORKERS
    n_chunks = per_w // SC_CHUNK

    @functools.partial(
        pl.kernel, mesh=_sc_mesh(),
        out_type=jax.ShapeDtypeStruct((n, d), table.dtype),
        scratch_types=[pltpu.VMEM((n_chunks, SC_CHUNK), jnp.int32),
                       pltpu.VMEM((2, SC_CHUNK, d), table.dtype),
                       pltpu.SemaphoreType.DMA((2,)), pltpu.SemaphoreType.DMA((2,))],
        name="sc_collect",
    )
    def k(table_hbm, idx_hbm, out_hbm, idx_v, rows_v, sem_in, sem_out):
        wid = lax.axis_index("s") * SC_CORES + lax.axis_index("c")
        base = wid * per_w
        pltpu.sync_copy(idx_hbm.at[wid], idx_v)

        def gather(ci):
            s = ci % 2
            return pltpu.make_async_copy(table_hbm.at[idx_v.at[ci]], rows_v.at[s], sem_in.at[s])

        def store(ci):
            s = ci % 2
            return pltpu.make_async_copy(rows_v.at[s], out_hbm.at[pl.ds(base + ci * SC_CHUNK, SC_CHUNK)],
                                         sem_out.at[s])

        gather(0).start()
        for ci in range(n_chunks):
            gather(ci).wait()
            if ci >= 1:
                store(ci - 1).wait()
            if ci + 1 < n_chunks:
                gather(ci + 1).start()
            store(ci).start()
        store(n_chunks - 1).wait()

    return k(table, idx.reshape(SC_WORKERS, n_chunks, SC_CHUNK))


def _combine_kernel(h_ref, r_ref, y1_ref, y2_ref, g_ref, b_ref, o_ref, *, alpha):
    g1 = r_ref[:, 2:3]
    g2 = r_ref[:, 3:4]
    ffn = g1 * _unpack_bf16_pairs(y1_ref[...]) + g2 * _unpack_bf16_pairs(y2_ref[...])
    o_ref[...] = _layer_norm(alpha * h_ref[...] + ffn, g_ref[...], b_ref[...])


def _combine(h2, routing, y12, g, bb, alpha):
    t, d = h2.shape
    tm = 512
    nt = t // tm
    row = pl.BlockSpec((tm, d), lambda i: (i, 0))
    vec = pl.BlockSpec((1, d), lambda i: (0, 0))
    return pl.pallas_call(
        functools.partial(_combine_kernel, alpha=alpha),
        grid=(nt,),
        in_specs=[row, pl.BlockSpec((tm, LANES), lambda i: (i, 0)),
                  pl.BlockSpec((tm, d // 2), lambda i: (i, 0)),
                  pl.BlockSpec((tm, d // 2), lambda i: (nt + i, 0)), vec, vec],
        out_specs=row,
        out_shape=jax.ShapeDtypeStruct((t, d), F32),
        compiler_params=_cparams(("parallel",)),
        name="combine_ln",
    )(h2, routing, y12, y12, g.reshape(1, d), bb.reshape(1, d))


def _moe(h2, h2_packed, rg_w, rg_b, re_w, re_b, wg, wu, wd, l, g, bb, alpha):
    t, d = h2.shape
    routing, counts = _router(h2, rg_w, rg_b, re_w, re_b)
    eid = routing[:, 0:2].astype(jnp.int32)
    rank = routing[:, 4:6].astype(jnp.int32)
    cnt = counts[N_GROUPS:N_GROUPS + N_EXPERTS, 0].astype(jnp.int32)
    padded = (cnt + MOE_ROWS - 1) // MOE_ROWS * MOE_ROWS
    pad_end = jnp.cumsum(padded)
    pad_start = pad_end - padded
    experts = jnp.arange(N_EXPERTS, dtype=jnp.int32)
    dest = jnp.sum(jnp.where(eid[..., None] == experts, pad_start, 0), axis=-1) + rank
    n_slots = t * TOP_K
    n_blocks = (n_slots + N_EXPERTS * (MOE_ROWS - 1) + MOE_ROWS - 1) // MOE_ROWS
    cap = n_blocks * MOE_ROWS
    blk_row = jnp.arange(n_blocks, dtype=jnp.int32) * MOE_ROWS
    block_e = jnp.minimum(jnp.sum((pad_end[None, :] <= blk_row[:, None]).astype(jnp.int32), axis=1),
                          N_EXPERTS - 1)
    is_e = block_e[:, None] == experts
    blk_cnt = jnp.sum(jnp.where(is_e, cnt, 0), axis=1)
    blk_start = jnp.sum(jnp.where(is_e, pad_start, 0), axis=1)
    n_valid = jnp.clip(blk_cnt - (blk_row - blk_start), 0, MOE_ROWS).astype(jnp.int32)
    used = cnt > 0
    later = (experts[None, :] > experts[:, None]) & used[None, :]
    next_used = jnp.min(jnp.where(later, experts[None, :], N_EXPERTS), axis=1)
    rank = jnp.cumsum(used.astype(jnp.int32)) - 1
    blk_first = (n_valid > 0) & (blk_row == blk_start)
    blk_next = jnp.sum(jnp.where(is_e, next_used, 0), axis=1)
    blk_slot = jnp.sum(jnp.where(is_e, rank, 0), axis=1) % 2
    flags = (blk_first.astype(jnp.int32) * MOE_FIRST + blk_slot * MOE_SLOT
             + (blk_next < N_EXPERTS).astype(jnp.int32) * MOE_HAS_NEXT).astype(jnp.int32)
    next_e = jnp.minimum(blk_next, N_EXPERTS - 1).astype(jnp.int32)
    xs = _sc_scatter_rows(h2_packed, dest[:, 0], dest[:, 1], cap)
    ys = _moe_experts(xs, block_e, n_valid, flags, next_e, wg, wu, wd, l)
    y12 = _sc_gather_rows(ys, jnp.concatenate([dest[:, 0], dest[:, 1]]))
    return _combine(h2, routing, y12, g, bb, alpha)


def _mixing_layer(h, b, s, l, p, lam_init):
    t = b * s
    alpha = (2 * p['w_in'].shape[0]) ** 0.25
    d = h.shape[1]
    w_in = p['w_in']
    tab_diff = _rope_tables(s, DIFF_QK_DIM)
    tab_dil = _rope_tables(s, HEAD_DIM)
    h3 = h.reshape(b, s, d)
    w_left = lax.slice_in_dim(w_in[l], 0, COL_GATE, axis=1).astype(BF16)
    qkv_na, qkv_diff, qkv_d0, qkv_d1, qkv_d2, u = _proj_branches(h3, w_left, tab_diff, tab_dil)
    y_a = _na_attention(qkv_na, _na_bias_table(p['na_rpb'][l]))
    y_b = _diff_attention(qkv_diff, p['diff_lam'][l], p['diff_subln_g'][l], lam_init)
    y_c = _pool(u, p['pool_w'][l], p['pool_scale'][l])
    dil = [_swa(qkv, s // r) for qkv, (_, r) in zip((qkv_d0, qkv_d1, qkv_d2), DIL_PATTERNS)]
    ys = [a.reshape(t, -1) for a in (y_a, y_b, y_c)]
    wg = lax.slice_in_dim(w_in[l], COL_GATE, COL_GATE + (len(ys) + 1) * d, axis=1).astype(BF16)
    return _merge(ys, dil, wg, p['b_gate'][l], p['w_branch'][l].astype(BF16), p['w_out'][l].astype(BF16), h,
                  p['ln1_g'][l], p['ln1_b'][l], alpha)


def kernel(x, emb_ln_g, emb_ln_b, w_in, b_gate, na_rpb, diff_lam, diff_subln_g, pool_w, pool_scale,
           w_branch, w_out, ln1_g, ln1_b, router_group_w, router_group_b, router_expert_w,
           router_expert_b, expert_w_gate, expert_w_up, expert_w_down, ln2_g, ln2_b):
    b, s, d = x.shape
    depth = w_in.shape[0]
    alpha = (2 * depth) ** 0.25
    p = dict(w_in=w_in, b_gate=b_gate, na_rpb=na_rpb, diff_lam=diff_lam, diff_subln_g=diff_subln_g,
             pool_w=pool_w, pool_scale=pool_scale, w_branch=w_branch, w_out=w_out, ln1_g=ln1_g, ln1_b=ln1_b)
    h = _embed_ln(x.reshape(b * s, d), emb_ln_g, emb_ln_b)
    for l in range(depth):
        lam_init = 0.8 - 0.6 * math.exp(-0.3 * l)
        h, h_packed = _mixing_layer(h, b, s, l, p, lam_init)
        h = _moe(h, h_packed, router_group_w[l], router_group_b[l], router_expert_w[l], router_expert_b[l],
                 expert_w_gate, expert_w_up, expert_w_down, l, ln2_g[l], ln2_b[l], alpha)
    return h.reshape(b, s, d)
```

```python
import functools
import math

import jax
import jax.numpy as jnp
import numpy as np
from jax import lax
from jax.experimental import pallas as pl
from jax.experimental.pallas import tpu as pltpu
from jax.experimental.pallas import tpu_sc as plsc

F32 = jnp.float32
BF16 = jnp.bfloat16

LANES = 128
GRID_W = 64
HEAD_DIM = 64
ROPE_THETA = 500000.0
LN_EPS = 1e-5
NA_ROWS = 8
NA_COLS = 16
DIFF_QK_DIM = 32
POOL_WINDOWS = (2, 4, 8, 16)
POOL_GROUP = 64
DIL_PATTERNS = ((128, 1), (512, 4), (2048, 16))
DIL_HALF = 64
N_GROUPS = 4
EXPERTS_PER_GROUP = 8
N_EXPERTS = N_GROUPS * EXPERTS_PER_GROUP
TOP_K = 2
MOE_ROWS = 512
NEG = -1e30

COL_NA = 0
COL_DIFF = 768
COL_POOL = 1536
COL_DIL = 1792
COL_GATE = 4096

VMEM_LIMIT = 56 * 1024 * 1024


def _cparams(sem, vmem=VMEM_LIMIT, flags=None):
    return pltpu.CompilerParams(dimension_semantics=sem, vmem_limit_bytes=vmem, flags=flags)


def _layer_norm(x, g, b):
    mu = jnp.mean(x, axis=-1, keepdims=True)
    xc = x - mu
    var = jnp.mean(xc * xc, axis=-1, keepdims=True)
    return xc * lax.rsqrt(var + LN_EPS) * g + b


def _sigmoid(x):
    return 0.5 * jnp.tanh(0.5 * x) + 0.5


def _dot(a, b):
    return jnp.dot(a, b, preferred_element_type=F32)


def _dot_nt(a, b):
    return lax.dot_general(a, b, (((1,), (1,)), ((), ())), preferred_element_type=F32)


def _lane(shape):
    return lax.broadcasted_iota(jnp.int32, shape, len(shape) - 1)


HI16 = -65536


def _pack_bf16_pairs(x):
    w = x.shape[1] // 2
    hi = lax.bitcast_convert_type(x[:, :w].astype(BF16).astype(F32), jnp.int32)
    lo = lax.bitcast_convert_type(x[:, w:].astype(BF16).astype(F32), jnp.int32)
    return (hi & HI16) | lax.shift_right_logical(lo, 16)


def _unpack_bf16_pairs(p):
    hi = lax.bitcast_convert_type(p & HI16, F32)
    lo = lax.bitcast_convert_type(lax.shift_left(p, 16), F32)
    return jnp.concatenate([hi, lo], axis=1)


def _ln_kernel(x_ref, g_ref, b_ref, o_ref):
    o_ref[...] = _layer_norm(x_ref[...], g_ref[...], b_ref[...])


def _embed_ln(x2, g, b):
    t, d = x2.shape
    tm = 1024
    return pl.pallas_call(
        _ln_kernel,
        grid=(t // tm,),
        in_specs=[pl.BlockSpec((tm, d), lambda i: (i, 0)),
                  pl.BlockSpec((1, d), lambda i: (0, 0)),
                  pl.BlockSpec((1, d), lambda i: (0, 0))],
        out_specs=pl.BlockSpec((tm, d), lambda i: (i, 0)),
        out_shape=jax.ShapeDtypeStruct((t, d), F32),
        compiler_params=_cparams(("parallel",)),
        name="embed_ln",
    )(x2, g.reshape(1, d), b.reshape(1, d))


PROJ_TM = 1024
QKV_W = 6 * LANES
WCOL = 256


QKV_TM = 512
QKV_SEGMENTS = ((COL_NA, 1, 0, HEAD_DIM), (COL_DIFF, 1, DIFF_QK_DIM // 8, DIFF_QK_DIM),
                (COL_DIL, 1, HEAD_DIM // 8, HEAD_DIM), (COL_DIL + QKV_W, 4, HEAD_DIM // 8, HEAD_DIM),
                (COL_DIL + 2 * QKV_W, 16, HEAD_DIM // 8, HEAD_DIM))


def _qkv_epilogue(z, t_ref, o_ref, zs_ref, r, half, q_scale):
    tm = z.shape[0]
    n = tm // r
    for c in range(QKV_W // LANES):
        blk = z[:, c * LANES:(c + 1) * LANES]
        if half and c < 4:
            blk = _rope(blk, t_ref, half)
        if c < 2:
            blk = blk * q_scale
        if r == 1:
            o_ref[0, 0, :, c * LANES:(c + 1) * LANES] = blk.astype(BF16)
        else:
            zs_ref[c] = blk
            for m in range(r):
                o_ref[0, m, :, c * LANES:(c + 1) * LANES] = zs_ref[
                    c, pl.ds(m, n, stride=r), :].astype(BF16)


def _proj_kernel(x_ref, w_ref, td_ref, tl_ref, na_ref, df_ref, d0_ref, d1_ref, d2_ref, u_ref,
                 zs1_ref, zs2_ref):
    xb = x_ref[0].astype(BF16)
    outs = (na_ref, df_ref, d0_ref, d1_ref, d2_ref)
    stage = (None, None, None, zs1_ref, zs2_ref)
    for (col0, r, half, head_w), o_ref, zs_ref in zip(QKV_SEGMENTS, outs, stage):
        z = _dot(xb, w_ref[:, col0:col0 + QKV_W])
        t_ref = td_ref if head_w == DIFF_QK_DIM else tl_ref
        _qkv_epilogue(z, t_ref, o_ref, zs_ref, r, half, head_w ** -0.5 * LOG2E)
    u_ref[0] = _dot(xb, w_ref[:, COL_POOL:COL_POOL + WCOL])


def _proj_branches(h3, w_bf16, tab_diff, tab_dil):
    b, s, d = h3.shape
    tm = QKV_TM
    qkv_spec = lambda r: pl.BlockSpec((1, r, tm // r, QKV_W), lambda bi, i: (bi, 0, i, 0))
    qkv_shape = lambda r: jax.ShapeDtypeStruct((b, r, s // r, QKV_W), BF16)
    tab_spec = pl.BlockSpec((3, tm, LANES), lambda bi, i: (0, i, 0))
    rs = [seg[1] for seg in QKV_SEGMENTS]
    outs = pl.pallas_call(
        _proj_kernel,
        grid=(b, s // tm),
        in_specs=[pl.BlockSpec((1, tm, d), lambda bi, i: (bi, i, 0)),
                  pl.BlockSpec(w_bf16.shape, lambda bi, i: (0, 0)),
                  tab_spec, tab_spec],
        out_specs=[qkv_spec(r) for r in rs] + [pl.BlockSpec((1, tm, WCOL), lambda bi, i: (bi, i, 0))],
        out_shape=[qkv_shape(r) for r in rs] + [jax.ShapeDtypeStruct((b, s, WCOL), F32)],
        scratch_shapes=[pltpu.VMEM((QKV_W // LANES, tm, LANES), F32)] * 2,
        compiler_params=_cparams(("parallel", "parallel")),
        name="proj_branches",
    )(h3, w_bf16, tab_diff, tab_dil)
    return [o.reshape(b, s, QKV_W) for o in outs[:-1]] + [outs[-1]]


def _rope_tables(seq, head_w):
    rot = head_w // 4
    half = rot // 2
    inv_freq = jnp.exp(jnp.arange(half, dtype=F32) * (-2.0 * math.log(ROPE_THETA) / rot))
    ang = jnp.arange(seq, dtype=jnp.int32).astype(F32)[:, None] * inv_freq[None, :]
    cos, sin = jnp.cos(ang), jnp.sin(ang)
    zero = jnp.zeros((seq, head_w - rot), F32)
    zh = jnp.zeros((seq, half), F32)
    t0 = jnp.concatenate([cos, cos, jnp.ones((seq, head_w - rot), F32)], axis=1)
    t1 = jnp.concatenate([-sin, zh, zero], axis=1)
    t2 = jnp.concatenate([zh, sin, zero], axis=1)
    reps = LANES // head_w
    return jnp.stack([jnp.tile(t0, (1, reps)), jnp.tile(t1, (1, reps)), jnp.tile(t2, (1, reps))])


def _rope(x, t_ref, half):
    return (x * t_ref[0] + pltpu.roll(x, LANES - half, 1) * t_ref[1]
            + pltpu.roll(x, half, 1) * t_ref[2])


LOG2E = math.log2(math.e)
LN2 = math.log(2.0)


def _na_bias_table(rpb):
    kr, kc = NA_ROWS, NA_COLS
    n_heads = rpb.shape[0]
    col = np.arange(GRID_W)
    col_start = np.clip(col - kc // 2, 0, GRID_W - kc)
    in_win = (col[None, :] >= col_start[:, None]) & (col[None, :] < col_start[:, None] + kc)
    dc = np.clip(col[None, :] - col[:, None] + kc - 1, 0, 2 * kc - 2)
    sel_c = ((dc[..., None] == np.arange(2 * kc - 1)) & in_win[..., None]).astype(np.float32)
    rp = rpb.astype(F32).reshape(n_heads // 2, 2, 2 * kr - 1, 2 * kc - 1) * LOG2E
    tiles = jnp.einsum('phaj,qxj->phaqx', rp, jnp.asarray(sel_c), precision=lax.Precision.HIGHEST)
    tiles = tiles + jnp.asarray(np.where(in_win, 0.0, NEG).astype(np.float32))
    tiles = jnp.concatenate([tiles, jnp.full_like(tiles[:, :, :1], NEG)], axis=2)
    return jnp.concatenate([tiles, tiles], axis=-1)


NA_QROWS = 4
NA_KROWS = 12
NA_MASKED = 2 * NA_ROWS - 1


def _na_row_offsets(n_rows):
    table = []
    for blk in (0, 1, n_rows // NA_QROWS - 1):
        ws = _na_window_start(blk, n_rows)
        rows = []
        for i in range(NA_QROWS):
            r = blk * NA_QROWS + i
            start = min(max(r - NA_ROWS // 2, 0), n_rows - NA_ROWS)
            rows.append([ws + kk - r + NA_ROWS - 1 if start <= ws + kk < start + NA_ROWS else NA_MASKED
                         for kk in range(NA_KROWS)])
        table.append(rows)
    return table


def _na_window_start(blk, n_rows):
    lo = blk * NA_QROWS - NA_ROWS // 2
    hi = n_rows - NA_KROWS
    if isinstance(blk, int):
        return min(max(lo, 0), hi)
    return jnp.clip(lo, 0, hi)


def _stack_heads(q, lane):
    zero = jnp.zeros_like(q)
    return jnp.concatenate([jnp.where(lane < HEAD_DIM, q, zero), jnp.where(lane < HEAD_DIM, zero, q)],
                           axis=0)


def _na_kernel(q_ref, k_ref, v_ref, tile_ref, o_ref, bias_ref, *, blocks_per_step, n_rows):
    step = pl.program_id(2)
    nq = NA_QROWS * GRID_W
    kwin = NA_KROWS * GRID_W
    lane = _lane((nq, LANES))
    last_blk = n_rows // NA_QROWS - 1

    @pl.when(step == 0)
    def _():
        low = _lane((GRID_W, LANES)) < GRID_W
        for case, rows in enumerate(_na_row_offsets(n_rows)):
            for hh in range(2):
                for i, offs in enumerate(rows):
                    r0 = (hh * NA_QROWS + i) * GRID_W
                    for kp in range(NA_KROWS // 2):
                        bias_ref[case, r0:r0 + GRID_W, kp * LANES:(kp + 1) * LANES] = jnp.where(
                            low, tile_ref[0, hh, offs[2 * kp]], tile_ref[0, hh, offs[2 * kp + 1]])

    for i in range(blocks_per_step):
        blk = step * blocks_per_step + i
        case = jnp.where(blk == 0, 0, jnp.where(blk == last_blk, 2, 1))
        k0 = pl.multiple_of(_na_window_start(blk, n_rows) * GRID_W, GRID_W)
        q2 = _stack_heads(q_ref[0, i * nq:(i + 1) * nq, :], lane)
        sc = _dot_nt(q2, k_ref[0, pl.ds(k0, kwin), :]) + bias_ref[case]
        m = jnp.max(sc, axis=-1, keepdims=True)
        e = jnp.exp2(sc - m)
        den = jnp.sum(e, axis=-1, keepdims=True)
        pv = _dot(e.astype(BF16), v_ref[0, pl.ds(k0, kwin), :]) / den
        o = jnp.where(lane < HEAD_DIM, pv[0:nq], pv[nq:2 * nq])
        o_ref[0, i * nq:(i + 1) * nq, :] = o.astype(BF16)


def _na_attention(qkv, tiles):
    b, s, _ = qkv.shape
    n_rows = s // GRID_W
    bps = 4
    rps = bps * NA_QROWS
    tq = rps * GRID_W
    return pl.pallas_call(
        functools.partial(_na_kernel, blocks_per_step=bps, n_rows=n_rows),
        grid=(b, 2, n_rows // rps),
        in_specs=[pl.BlockSpec((1, tq, LANES), lambda bi, hp, i: (bi, i, hp)),
                  pl.BlockSpec((1, s, LANES), lambda bi, hp, i: (bi, 0, 2 + hp)),
                  pl.BlockSpec((1, s, LANES), lambda bi, hp, i: (bi, 0, 4 + hp)),
                  pl.BlockSpec((1, 2, NA_MASKED + 1, GRID_W, LANES), lambda bi, hp, i: (hp, 0, 0, 0, 0))],
        out_specs=pl.BlockSpec((1, tq, LANES), lambda bi, hp, i: (bi, i, hp)),
        out_shape=jax.ShapeDtypeStruct((b, s, 2 * LANES), BF16),
        scratch_shapes=[pltpu.VMEM((3, 2 * NA_QROWS * GRID_W, NA_KROWS * GRID_W), F32)],
        compiler_params=_cparams(("parallel", "parallel", "arbitrary")),
        name="na_attn",
    )(qkv, qkv, qkv, tiles)


def _diff_kernel(lam_ref, q_ref, k_ref, v_ref, g_ref, o_ref, *, lam_init, rows):
    tq = q_ref.shape[1]
    lane = _lane((rows, LANES))
    dl = lam_ref[...]
    lam = (jnp.exp(jnp.sum(dl[0:1] * dl[1:2], axis=-1, keepdims=True))
           - jnp.exp(jnp.sum(dl[2:3] * dl[3:4], axis=-1, keepdims=True)) + lam_init)
    v = v_ref[0]
    v_lane = _lane(v.shape)
    one = jnp.ones_like(v)
    v_ext = (jnp.where(v_lane < HEAD_DIM, v, one), jnp.where(v_lane < HEAD_DIM, one, v))
    in_h0 = lane < HEAD_DIM
    for r0 in range(0, tq, rows):
        q = q_ref[0, r0:r0 + rows, :]
        zero = jnp.zeros_like(q)
        pv = []
        for lo in range(0, LANES, DIFF_QK_DIM):
            qm = jnp.where((lane >= lo) & (lane < lo + DIFF_QK_DIM), q, zero)
            sc = _dot_nt(qm, k_ref[0])
            e = jnp.exp2(sc - jnp.max(sc, axis=-1, keepdims=True))
            ev = _dot(e.astype(BF16), v_ext[lo // HEAD_DIM])
            pv.append(ev / pltpu.roll(ev, HEAD_DIM, 1))
        o = jnp.where(in_h0, pv[0] - lam * pv[1], pv[2] - lam * pv[3])
        o2 = o * o
        ms0 = jnp.sum(jnp.where(in_h0, o2, 0.0), axis=-1, keepdims=True) / HEAD_DIM
        ms1 = jnp.sum(jnp.where(in_h0, 0.0, o2), axis=-1, keepdims=True) / HEAD_DIM
        ms = jnp.where(in_h0, ms0, ms1)
        o = o * lax.rsqrt(ms + LN_EPS) * g_ref[...] * (1.0 - lam_init)
        o_ref[0, r0:r0 + rows, :] = o.astype(BF16)


def _diff_attention(qkv, diff_lam, subln_g, lam_init):
    b, s, _ = qkv.shape
    tq = 1024
    g2 = jnp.tile(subln_g.reshape(1, HEAD_DIM), (1, 2))
    return pl.pallas_call(
        functools.partial(_diff_kernel, lam_init=lam_init, rows=512),
        grid=(b, 2, s // tq),
        in_specs=[pl.BlockSpec((4, DIFF_QK_DIM), lambda bi, hp, i: (0, 0)),
                  pl.BlockSpec((1, tq, LANES), lambda bi, hp, i: (bi, i, hp)),
                  pl.BlockSpec((1, s, LANES), lambda bi, hp, i: (bi, 0, 2 + hp)),
                  pl.BlockSpec((1, s, LANES), lambda bi, hp, i: (bi, 0, 4 + hp)),
                  pl.BlockSpec((1, LANES), lambda bi, hp, i: (0, 0))],
        out_specs=pl.BlockSpec((1, tq, LANES), lambda bi, hp, i: (bi, i, hp)),
        out_shape=jax.ShapeDtypeStruct((b, s, 2 * LANES), BF16),
        compiler_params=_cparams(("parallel", "parallel", "arbitrary")),
        name="diff_attn",
    )(diff_lam, qkv, qkv, qkv, g2)


POOL_PAD = 16
POOL_CHUNK = 512
POOL_HALO = 8


def _pool_kernel(u_ref, w_ref, sc_ref, o_ref, p_ref):
    s = u_ref.shape[1]
    width = u_ref.shape[2]
    p_ref[0:POOL_PAD, :] = jnp.zeros((POOL_PAD, width), F32)
    p_ref[POOL_PAD + s:POOL_PAD + s + POOL_PAD, :] = jnp.zeros((POOL_PAD, width), F32)
    p_ref[POOL_PAD:POOL_PAD + s, :] = u_ref[0]
    n = POOL_CHUNK + 2 * POOL_HALO
    lane = _lane((POOL_CHUNK, width))
    row = lax.broadcasted_iota(jnp.int32, (POOL_CHUNK, width), 0)
    w_of_lane = jnp.where(lane < POOL_GROUP, 2, jnp.where(lane < 2 * POOL_GROUP, 4,
                          jnp.where(lane < 3 * POOL_GROUP, 8, 16)))

    def body(ci, carry):
        c0 = pl.multiple_of(ci * POOL_CHUNK, POOL_CHUNK)
        x = p_ref[pl.ds(c0 + POOL_PAD - POOL_HALO, n), :]
        w2 = x + pltpu.roll(x, 1, 0)
        w4 = pltpu.roll(w2, 1, 0) + pltpu.roll(w2, n - 1, 0)
        w8 = pltpu.roll(w4, 2, 0) + pltpu.roll(w4, n - 2, 0)
        w16 = pltpu.roll(w8, 4, 0) + pltpu.roll(w8, n - 4, 0)
        u = x[POOL_HALO:POOL_HALO + POOL_CHUNK]
        wsum = jnp.where(lane < POOL_GROUP, w2[POOL_HALO:POOL_HALO + POOL_CHUNK],
                         jnp.where(lane < 2 * POOL_GROUP, w4[POOL_HALO:POOL_HALO + POOL_CHUNK],
                                   jnp.where(lane < 3 * POOL_GROUP, w8[POOL_HALO:POOL_HALO + POOL_CHUNK],
                                             w16[POOL_HALO:POOL_HALO + POOL_CHUNK])))
        t = row + c0
        half_w = w_of_lane // 2
        lo = jnp.maximum(t - half_w, 0)
        hi = jnp.minimum(t + w_of_lane - 1 - half_w, s - 1)
        cnt = (hi - lo + 1).astype(F32)
        dlt = wsum / cnt - u
        y = _dot(dlt.astype(BF16), w_ref[...]) * sc_ref[...]
        o_ref[0, pl.ds(c0, POOL_CHUNK), :] = y.astype(BF16)
        return carry

    lax.fori_loop(0, s // POOL_CHUNK, body, 0)


def _pool(u3, pool_w, pool_scale):
    b, s, width = u3.shape
    wbd = jax.scipy.linalg.block_diag(*[pool_w[g] for g in range(len(POOL_WINDOWS))]).astype(BF16)
    return pl.pallas_call(
        _pool_kernel,
        grid=(b,),
        in_specs=[pl.BlockSpec((1, s, width), lambda bi: (bi, 0, 0)),
                  pl.BlockSpec((width, width), lambda bi: (0, 0)),
                  pl.BlockSpec((1, width), lambda bi: (0, 0))],
        out_specs=pl.BlockSpec((1, s, width), lambda bi: (bi, 0, 0)),
        out_shape=jax.ShapeDtypeStruct((b, s, width), BF16),
        scratch_shapes=[pltpu.VMEM((s + 2 * POOL_PAD, width), F32)],
        compiler_params=_cparams(("parallel",)),
        name="pool",
    )(u3, wbd, pool_scale.reshape(1, width))


SWA_Q = 128
SWA_BAND = SWA_Q + 2 * DIL_HALF


def _swa_kernel(q_ref, k_ref, v_ref, o_ref, l_ref, *, length):
    lane = _lane((SWA_Q, LANES))
    rel = (lax.broadcasted_iota(jnp.int32, (2 * SWA_Q, SWA_BAND), 1)
           - (lax.broadcasted_iota(jnp.int32, (2 * SWA_Q, SWA_BAND), 0) & (SWA_Q - 1)))
    masks = {}
    for i in range(q_ref.shape[1] // SWA_Q):
        r0 = i * SWA_Q
        l0 = r0 % length
        lo = r0 - l0 + min(max(l0 - DIL_HALF, 0), length - SWA_BAND)
        if lo - r0 not in masks:
            d = rel + (lo - r0)
            masks[lo - r0] = jnp.where((d >= -DIL_HALF) & (d <= DIL_HALF), 0.0, NEG)
        q2 = _stack_heads(q_ref[0, r0:r0 + SWA_Q, :], lane)
        sc = _dot_nt(q2, k_ref[0, lo:lo + SWA_BAND, :]) + masks[lo - r0]
        m = jnp.max(sc, axis=-1, keepdims=True)
        e = jnp.exp2(sc - m)
        den = jnp.sum(e, axis=-1, keepdims=True)
        pv = _dot(e.astype(BF16), v_ref[0, lo:lo + SWA_BAND, :]) / den
        lse2 = m + jnp.log2(den)
        o_ref[0, i * SWA_Q:(i + 1) * SWA_Q, :] = jnp.where(lane < HEAD_DIM, pv[0:SWA_Q], pv[SWA_Q:])
        l_ref[0, i * SWA_Q:(i + 1) * SWA_Q, :] = jnp.where(lane < HEAD_DIM, lse2[0:SWA_Q], lse2[SWA_Q:])


def _swa(qkv, length):
    b, s, _ = qkv.shape
    shp = jax.ShapeDtypeStruct((b, s, 2 * LANES), F32)
    seq = lambda col: pl.BlockSpec((1, s, LANES), lambda bi, hp: (bi, 0, col + hp))
    return pl.pallas_call(
        functools.partial(_swa_kernel, length=length),
        grid=(b, 2),
        in_specs=[seq(0), seq(2), seq(4)],
        out_specs=[seq(0), seq(0)],
        out_shape=[shp, shp],
        compiler_params=_cparams(("parallel", "parallel")),
        name=f"swa_l{length}",
    )(qkv, qkv, qkv)


def _dil_combine(dil_refs, stage_refs, tm):
    (o0_ref, l0_ref), (o1_ref, l1_ref), (o2_ref, l2_ref) = dil_refs
    so1, sl1, so2, sl2 = stage_refs
    r1 = DIL_PATTERNS[1][1]
    r2 = DIL_PATTERNS[2][1]
    halves = []
    for hp in range(2):
        cols = slice(hp * LANES, (hp + 1) * LANES)
        for m in range(r1):
            so1[pl.ds(m, tm // r1, stride=r1), :] = o1_ref[0, m, :, cols]
            sl1[pl.ds(m, tm // r1, stride=r1), :] = l1_ref[0, m, :, cols]
        for m in range(r2):
            so2[pl.ds(m, tm // r2, stride=r2), :] = o2_ref[0, m, :, cols]
            sl2[pl.ds(m, tm // r2, stride=r2), :] = l2_ref[0, m, :, cols]
        l0, l1, l2 = l0_ref[0, 0, :, cols], sl1[...], sl2[...]
        mx = jnp.maximum(jnp.maximum(l0, l1), l2)
        e0, e1, e2 = jnp.exp2(l0 - mx), jnp.exp2(l1 - mx), jnp.exp2(l2 - mx)
        den = e0 + e1 + e2
        halves.append((e0 / den) * o0_ref[0, 0, :, cols] + (e1 / den) * so1[...] + (e2 / den) * so2[...])
    return jnp.concatenate(halves, axis=1)


def _merge_kernel(ya_ref, yb_ref, yc_ref, o0_ref, l0_ref, o1_ref, l1_ref, o2_ref, l2_ref,
                  wg_ref, bg_ref, wb_ref, wo_ref, h_ref, g_ref, b_ref, wr_ref, br_ref,
                  o_ref, op_ref, r_ref, cnt_ref, so1, sl1, so2, sl2, carry, before, *, alpha):
    tm, d = h_ref.shape
    hb = h_ref[...].astype(BF16)
    y_d = _dil_combine(((o0_ref, l0_ref), (o1_ref, l1_ref), (o2_ref, l2_ref)),
                       (so1, sl1, so2, sl2), tm).astype(BF16)
    merged = None
    for n, y in enumerate((ya_ref[...], yb_ref[...], yc_ref[...], y_d)):
        zg = _dot(hb, wg_ref[:, n * d:(n + 1) * d]) + bg_ref[:, n * d:(n + 1) * d]
        term = _sigmoid(zg) * _dot(y, wb_ref[n])
        merged = term if merged is None else merged + term
    mix = _dot(merged.astype(BF16), wo_ref[...])
    h1 = _layer_norm(alpha * h_ref[...] + mix, g_ref[...], b_ref[...])
    o_ref[...] = h1
    op_ref[...] = _pack_bf16_pairs(h1)
    _route(h1, wr_ref, br_ref, r_ref, cnt_ref, carry, before)


def _merge(ys, dil, wg, b_gate, wb, wo, h2, g, bb, router, alpha):
    t, d = h2.shape
    w_route, b_route = router
    b, s, bw = dil[0][0].shape
    nb = len(ys) + 1
    tm = QKV_TM
    nt = s // tm
    yspec = pl.BlockSpec((tm, bw), lambda i: (i, 0))
    vec = pl.BlockSpec((1, d), lambda i: (0, 0))
    dil_args, dil_specs = [], []
    for (o, lse), (_, r) in zip(dil, DIL_PATTERNS):
        spec = pl.BlockSpec((1, r, tm // r, bw), lambda i: (i // nt, 0, i % nt, 0))
        dil_args += [o.reshape(b, r, s // r, bw), lse.reshape(b, r, s // r, bw)]
        dil_specs += [spec, spec]
    return pl.pallas_call(
        functools.partial(_merge_kernel, alpha=alpha),
        grid=(t // tm,),
        in_specs=[yspec, yspec, yspec] + dil_specs + [
                  pl.BlockSpec((d, nb * d), lambda i: (0, 0)),
                  pl.BlockSpec((1, nb * d), lambda i: (0, 0)),
                  pl.BlockSpec((nb, bw, d), lambda i: (0, 0, 0)),
                  pl.BlockSpec((d, d), lambda i: (0, 0)),
                  pl.BlockSpec((tm, d), lambda i: (i, 0)),
                  vec, vec,
                  pl.BlockSpec((LANES, d), lambda i: (0, 0)),
                  pl.BlockSpec((LANES, 1), lambda i: (0, 0))],
        out_specs=[pl.BlockSpec((tm, d), lambda i: (i, 0)),
                   pl.BlockSpec((tm, d // 2), lambda i: (i, 0)),
                   pl.BlockSpec((tm, LANES), lambda i: (i, 0)),
                   pl.BlockSpec((LANES, 1), lambda i: (0, 0))],
        out_shape=[jax.ShapeDtypeStruct((t, d), F32), jax.ShapeDtypeStruct((t, d // 2), jnp.int32),
                   jax.ShapeDtypeStruct((t, LANES), F32), jax.ShapeDtypeStruct((LANES, 1), F32)],
        scratch_shapes=[pltpu.VMEM((tm, LANES), F32)] * 4 + [pltpu.VMEM((LANES, 1), F32),
                                                               pltpu.VMEM((tm, tm), BF16)],
        compiler_params=_cparams(("arbitrary",)),
        name="merge",
    )(*ys, *dil_args, wg, b_gate.reshape(1, nb * d), wb, wo, h2, g.reshape(1, d), bb.reshape(1, d),
      w_route, b_route)


def _route(h, w_ref, b_ref, o_ref, cnt_ref, carry, before):
    tm = h.shape[0]

    @pl.when(pl.program_id(0) == 0)
    def _():
        carry[...] = jnp.zeros_like(carry)
        before[...] = (lax.broadcasted_iota(jnp.int32, (tm, tm), 0)
                       < lax.broadcasted_iota(jnp.int32, (tm, tm), 1)).astype(BF16)

    logits = lax.dot_general(w_ref[...], h, (((1,), (1,)), ((), ())),
                             preferred_element_type=F32,
                             precision=lax.Precision.HIGHEST) + b_ref[...]
    row = lax.broadcasted_iota(jnp.int32, (LANES, tm), 0)
    big = jnp.int32(1 << 20)
    is_g = row < N_GROUPS
    gl = jnp.where(is_g, logits, -jnp.inf)
    gmax = jnp.max(gl, axis=0, keepdims=True)
    gsel = jnp.min(jnp.where(is_g & (gl == gmax), row, big), axis=0, keepdims=True)
    pg = 1.0 / jnp.sum(jnp.exp(gl - gmax), axis=0, keepdims=True)
    e_lo = N_GROUPS + gsel * EXPERTS_PER_GROUP
    in_grp = (row >= e_lo) & (row < e_lo + EXPERTS_PER_GROUP)
    el = jnp.where(in_grp, logits, -jnp.inf)
    v1 = jnp.max(el, axis=0, keepdims=True)
    i1 = jnp.min(jnp.where(in_grp & (el == v1), row, big), axis=0, keepdims=True)
    el2 = jnp.where(row == i1, -jnp.inf, el)
    v2 = jnp.max(el2, axis=0, keepdims=True)
    i2 = jnp.min(jnp.where(in_grp & (row != i1) & (el2 == v2), row, big), axis=0, keepdims=True)
    t2 = jnp.exp(v2 - v1)
    g1 = pg / (1.0 + t2)
    g2 = pg * t2 / (1.0 + t2)
    oh1 = (row == i1)
    oh2 = (row == i2)
    oh1b = oh1.astype(BF16)
    oh2b = oh2.astype(BF16)
    c0 = carry[...]
    c1 = c0 + jnp.sum(oh1b.astype(F32), axis=1, keepdims=True)
    rank1 = jnp.sum(jnp.where(oh1, _dot(oh1b, before[...]) + c0, 0.0), axis=0, keepdims=True)
    rank2 = jnp.sum(jnp.where(oh2, _dot(oh2b, before[...]) + c1, 0.0), axis=0, keepdims=True)
    c2 = c1 + jnp.sum(oh2b.astype(F32), axis=1, keepdims=True)
    carry[...] = c2
    cnt_ref[...] = c2
    e1 = (i1 - N_GROUPS).astype(F32)
    e2 = (i2 - N_GROUPS).astype(F32)
    out_t = jnp.where(row == 0, e1, jnp.where(row == 1, e2, jnp.where(row == 2, g1, jnp.where(
        row == 3, g2, jnp.where(row == 4, rank1, jnp.where(row == 5, rank2, 0.0))))))
    o_ref[...] = out_t.T


def _router_params(rg_w, rg_b, re_w, re_b):
    d = rg_w.shape[0]
    pad = LANES - N_GROUPS - N_EXPERTS
    w_t = jnp.concatenate([rg_w.T, re_w.T, jnp.zeros((pad, d), F32)], axis=0)
    bias = jnp.concatenate([rg_b, re_b, jnp.zeros((pad,), F32)]).reshape(LANES, 1)
    return w_t, bias


MOE_FIRST, MOE_SLOT, MOE_HAS_NEXT = 1, 2, 4


def _moe_kernel(be_ref, nv_ref, fl_ref, nx_ref, x_ref, wg_hbm, wu_hbm, wd_hbm, o_ref,
                wgb, wub, wdb, stage_g, stage_u, stage_d, sem, *, layer):
    j = pl.program_id(0)
    n_valid = nv_ref[j]
    flags = fl_ref[j]
    slot = (flags // MOE_SLOT) % 2

    def weight_copies(e, s):
        return [pltpu.make_async_copy(w_hbm.at[layer, e], stage.at[s], sem.at[s, k])
                for k, (w_hbm, stage) in enumerate(((wg_hbm, stage_g), (wu_hbm, stage_u),
                                                    (wd_hbm, stage_d)))]

    @pl.when(flags % 2 == MOE_FIRST)
    def _():
        @pl.when(j == 0)
        def _():
            for cp in weight_copies(be_ref[j], slot):
                cp.start()

        for cp in weight_copies(be_ref[j], slot):
            cp.wait()

        @pl.when((flags // MOE_HAS_NEXT) % 2 == 1)
        def _():
            for cp in weight_copies(nx_ref[j], 1 - slot):
                cp.start()

        wgb[...] = stage_g[slot].astype(BF16)
        wub[...] = stage_u[slot].astype(BF16)
        wdb[...] = stage_d[slot].astype(BF16)

    @pl.when(n_valid > 0)
    def _():
        row = lax.broadcasted_iota(jnp.int32, x_ref.shape, 0)
        x = _unpack_bf16_pairs(jnp.where(row < n_valid, x_ref[...], 0)).astype(BF16)
        g = _dot(x, wgb[...])
        u = _dot(x, wub[...])
        hmid = (g * _sigmoid(g)) * u
        o_ref[...] = _pack_bf16_pairs(_dot(hmid.astype(BF16), wdb[...]))

    @pl.when(n_valid <= 0)
    def _():
        o_ref[...] = jnp.zeros_like(o_ref)


def _moe_experts(xs, block_e, n_valid, flags, next_e, wg, wu, wd, l):
    cap, dp = xs.shape
    d = 2 * dp
    de = wg.shape[3]
    n_blocks = cap // MOE_ROWS
    hbm = pl.BlockSpec(memory_space=pl.ANY)
    grid_spec = pltpu.PrefetchScalarGridSpec(
        num_scalar_prefetch=4,
        grid=(n_blocks,),
        in_specs=[pl.BlockSpec((MOE_ROWS, dp), lambda j, *_: (j, 0)), hbm, hbm, hbm],
        out_specs=pl.BlockSpec((MOE_ROWS, dp), lambda j, *_: (j, 0)),
        scratch_shapes=[pltpu.VMEM((d, de), BF16), pltpu.VMEM((d, de), BF16), pltpu.VMEM((de, d), BF16),
                        pltpu.VMEM((2, d, de), F32), pltpu.VMEM((2, d, de), F32),
                        pltpu.VMEM((2, de, d), F32), pltpu.SemaphoreType.DMA((2, 3))],
    )
    return pl.pallas_call(
        functools.partial(_moe_kernel, layer=l),
        grid_spec=grid_spec,
        out_shape=jax.ShapeDtypeStruct((cap, dp), jnp.int32),
        compiler_params=_cparams(("arbitrary",)),
        name="moe_experts",
    )(block_e, n_valid, flags, next_e, xs, wg, wu, wd)


SC_CORES = 2
SC_SUBCORES = 16
SC_WORKERS = SC_CORES * SC_SUBCORES
SC_CHUNK = 64


def _sc_mesh():
    return plsc.VectorSubcoreMesh(core_axis_name="c", subcore_axis_name="s",
                                  num_cores=SC_CORES, num_subcores=SC_SUBCORES)


def _sc_scatter_rows(src, idx0, idx1, n_out):
    t, d = src.shape
    per_w = t // SC_WORKERS
    n_chunks = per_w // SC_CHUNK
    idx_shape = (SC_WORKERS, n_chunks, SC_CHUNK)

    @functools.partial(
        pl.kernel, mesh=_sc_mesh(),
        out_type=jax.ShapeDtypeStruct((n_out, d), src.dtype),
        scratch_types=[pltpu.VMEM((n_chunks, SC_CHUNK), jnp.int32),
                       pltpu.VMEM((n_chunks, SC_CHUNK), jnp.int32),
                       pltpu.VMEM((2, SC_CHUNK, d), src.dtype),
                       pltpu.SemaphoreType.DMA((2,)), pltpu.SemaphoreType.DMA((2,))],
        name="sc_dispatch",
    )
    def k(src_hbm, i0_hbm, i1_hbm, out_hbm, i0_v, i1_v, rows_v, sem_in, sem_out):
        wid = lax.axis_index("s") * SC_CORES + lax.axis_index("c")
        base = wid * per_w
        pltpu.sync_copy(i0_hbm.at[wid], i0_v)
        pltpu.sync_copy(i1_hbm.at[wid], i1_v)

        def load(ci):
            s = ci % 2
            return pltpu.make_async_copy(src_hbm.at[pl.ds(base + ci * SC_CHUNK, SC_CHUNK)],
                                         rows_v.at[s], sem_in.at[s])

        def scatter(ci, idx_v):
            s = ci % 2
            return pltpu.make_async_copy(rows_v.at[s], out_hbm.at[idx_v.at[ci]], sem_out.at[s])

        load(0).start()
        for ci in range(n_chunks):
            load(ci).wait()
            if ci >= 1:
                scatter(ci - 1, i0_v).wait()
                scatter(ci - 1, i1_v).wait()
            if ci + 1 < n_chunks:
                load(ci + 1).start()
            scatter(ci, i0_v).start()
            scatter(ci, i1_v).start()
        scatter(n_chunks - 1, i0_v).wait()
        scatter(n_chunks - 1, i1_v).wait()

    return k(src, idx0.reshape(idx_shape), idx1.reshape(idx_shape))


def _sc_gather_rows(table, idx):
    n = idx.shape[0]
    d = table.shape[1]
    per_w = n // SC_WORKERS
    n_chunks = per_w // SC_CHUNK

    @functools.partial(
        pl.kernel, mesh=_sc_mesh(),
        out_type=jax.ShapeDtypeStruct((n, d), table.dtype),
        scratch_types=[pltpu.VMEM((n_chunks, SC_CHUNK), jnp.int32),
                       pltpu.VMEM((2, SC_CHUNK, d), table.dtype),
                       pltpu.SemaphoreType.DMA((2,)), pltpu.SemaphoreType.DMA((2,))],
        name="sc_collect",
    )
    def k(table_hbm, idx_hbm, out_hbm, idx_v, rows_v, sem_in, sem_out):
        wid = lax.axis_index("s") * SC_CORES + lax.axis_index("c")
        base = wid * per_w
        pltpu.sync_copy(idx_hbm.at[wid], idx_v)

        def gather(ci):
            s = ci % 2
            return pltpu.make_async_copy(table_hbm.at[idx_v.at[ci]], rows_v.at[s], sem_in.at[s])

        def store(ci):
            s = ci % 2
            return pltpu.make_async_copy(rows_v.at[s], out_hbm.at[pl.ds(base + ci * SC_CHUNK, SC_CHUNK)],
                                         sem_out.at[s])

        gather(0).start()
        for ci in range(n_chunks):
            gather(ci).wait()
            if ci >= 1:
                store(ci - 1).wait()
            if ci + 1 < n_chunks:
                gather(ci + 1).start()
            store(ci).start()
        store(n_chunks - 1).wait()

    return k(table, idx.reshape(SC_WORKERS, n_chunks, SC_CHUNK))


def _combine_kernel(h_ref, r_ref, y1_ref, y2_ref, g_ref, b_ref, o_ref, *, alpha):
    g1 = r_ref[:, 2:3]
    g2 = r_ref[:, 3:4]
    ffn = g1 * _unpack_bf16_pairs(y1_ref[...]) + g2 * _unpack_bf16_pairs(y2_ref[...])
    o_ref[...] = _layer_norm(alpha * h_ref[...] + ffn, g_ref[...], b_ref[...])


def _combine(h2, routing, y12, g, bb, alpha):
    t, d = h2.shape
    tm = 512
    nt = t // tm
    row = pl.BlockSpec((tm, d), lambda i: (i, 0))
    vec = pl.BlockSpec((1, d), lambda i: (0, 0))
    return pl.pallas_call(
        functools.partial(_combine_kernel, alpha=alpha),
        grid=(nt,),
        in_specs=[row, pl.BlockSpec((tm, LANES), lambda i: (i, 0)),
                  pl.BlockSpec((tm, d // 2), lambda i: (i, 0)),
                  pl.BlockSpec((tm, d // 2), lambda i: (nt + i, 0)), vec, vec],
        out_specs=row,
        out_shape=jax.ShapeDtypeStruct((t, d), F32),
        compiler_params=_cparams(("parallel",)),
        name="combine_ln",
    )(h2, routing, y12, y12, g.reshape(1, d), bb.reshape(1, d))


def _moe(h2, h2_packed, routing, counts, wg, wu, wd, l, g, bb, alpha):
    t, d = h2.shape
    eid = routing[:, 0:2].astype(jnp.int32)
    rank = routing[:, 4:6].astype(jnp.int32)
    cnt = counts[N_GROUPS:N_GROUPS + N_EXPERTS, 0].astype(jnp.int32)
    padded = (cnt + MOE_ROWS - 1) // MOE_ROWS * MOE_ROWS
    pad_end = jnp.cumsum(padded)
    pad_start = pad_end - padded
    experts = jnp.arange(N_EXPERTS, dtype=jnp.int32)
    dest = jnp.sum(jnp.where(eid[..., None] == experts, pad_start, 0), axis=-1) + rank
    n_slots = t * TOP_K
    n_blocks = (n_slots + N_EXPERTS * (MOE_ROWS - 1) + MOE_ROWS - 1) // MOE_ROWS
    cap = n_blocks * MOE_ROWS
    blk_row = jnp.arange(n_blocks, dtype=jnp.int32) * MOE_ROWS
    block_e = jnp.minimum(jnp.sum((pad_end[None, :] <= blk_row[:, None]).astype(jnp.int32), axis=1),
                          N_EXPERTS - 1)
    is_e = block_e[:, None] == experts
    blk_cnt = jnp.sum(jnp.where(is_e, cnt, 0), axis=1)
    blk_start = jnp.sum(jnp.where(is_e, pad_start, 0), axis=1)
    n_valid = jnp.clip(blk_cnt - (blk_row - blk_start), 0, MOE_ROWS).astype(jnp.int32)
    used = cnt > 0
    later = (experts[None, :] > experts[:, None]) & used[None, :]
    next_used = jnp.min(jnp.where(later, experts[None, :], N_EXPERTS), axis=1)
    rank = jnp.cumsum(used.astype(jnp.int32)) - 1
    blk_first = (n_valid > 0) & (blk_row == blk_start)
    blk_next = jnp.sum(jnp.where(is_e, next_used, 0), axis=1)
    blk_slot = jnp.sum(jnp.where(is_e, rank, 0), axis=1) % 2
    flags = (blk_first.astype(jnp.int32) * MOE_FIRST + blk_slot * MOE_SLOT
             + (blk_next < N_EXPERTS).astype(jnp.int32) * MOE_HAS_NEXT).astype(jnp.int32)
    next_e = jnp.minimum(blk_next, N_EXPERTS - 1).astype(jnp.int32)
    xs = _sc_scatter_rows(h2_packed, dest[:, 0], dest[:, 1], cap)
    ys = _moe_experts(xs, block_e, n_valid, flags, next_e, wg, wu, wd, l)
    y12 = _sc_gather_rows(ys, jnp.concatenate([dest[:, 0], dest[:, 1]]))
    return _combine(h2, routing, y12, g, bb, alpha)


def _mixing_layer(h, b, s, l, p, lam_init):
    t = b * s
    alpha = (2 * p['w_in'].shape[0]) ** 0.25
    d = h.shape[1]
    w_in = p['w_in']
    tab_diff = _rope_tables(s, DIFF_QK_DIM)
    tab_dil = _rope_tables(s, HEAD_DIM)
    h3 = h.reshape(b, s, d)
    w_left = lax.slice_in_dim(w_in[l], 0, COL_GATE, axis=1).astype(BF16)
    qkv_na, qkv_diff, qkv_d0, qkv_d1, qkv_d2, u = _proj_branches(h3, w_left, tab_diff, tab_dil)
    y_a = _na_attention(qkv_na, _na_bias_table(p['na_rpb'][l]))
    y_b = _diff_attention(qkv_diff, p['diff_lam'][l], p['diff_subln_g'][l], lam_init)
    y_c = _pool(u, p['pool_w'][l], p['pool_scale'][l])
    dil = [_swa(qkv, s // r) for qkv, (_, r) in zip((qkv_d0, qkv_d1, qkv_d2), DIL_PATTERNS)]
    ys = [a.reshape(t, -1) for a in (y_a, y_b, y_c)]
    wg = lax.slice_in_dim(w_in[l], COL_GATE, COL_GATE + (len(ys) + 1) * d, axis=1).astype(BF16)
    router = _router_params(p['router_group_w'][l], p['router_group_b'][l], p['router_expert_w'][l],
                            p['router_expert_b'][l])
    return _merge(ys, dil, wg, p['b_gate'][l], p['w_branch'][l].astype(BF16), p['w_out'][l].astype(BF16), h,
                  p['ln1_g'][l], p['ln1_b'][l], router, alpha)


def kernel(x, emb_ln_g, emb_ln_b, w_in, b_gate, na_rpb, diff_lam, diff_subln_g, pool_w, pool_scale,
           w_branch, w_out, ln1_g, ln1_b, router_group_w, router_group_b, router_expert_w,
           router_expert_b, expert_w_gate, expert_w_up, expert_w_down, ln2_g, ln2_b):
    b, s, d = x.shape
    depth = w_in.shape[0]
    alpha = (2 * depth) ** 0.25
    p = dict(w_in=w_in, b_gate=b_gate, na_rpb=na_rpb, diff_lam=diff_lam, diff_subln_g=diff_subln_g,
             pool_w=pool_w, pool_scale=pool_scale, w_branch=w_branch, w_out=w_out, ln1_g=ln1_g, ln1_b=ln1_b,
             router_group_w=router_group_w, router_group_b=router_group_b,
             router_expert_w=router_expert_w, router_expert_b=router_expert_b)
    h = _embed_ln(x.reshape(b * s, d), emb_ln_g, emb_ln_b)
    for l in range(depth):
        lam_init = 0.8 - 0.6 * math.exp(-0.3 * l)
        h, h_packed, routing, counts = _mixing_layer(h, b, s, l, p, lam_init)
        h = _moe(h, h_packed, routing, counts, expert_w_gate, expert_w_up, expert_w_down, l,
                 ln2_g[l], ln2_b[l], alpha)
    return h.reshape(b, s, d)
```

```python
import functools
import math

import jax
import jax.numpy as jnp
import numpy as np
from jax import lax
from jax.experimental import pallas as pl
from jax.experimental.pallas import tpu as pltpu
from jax.experimental.pallas import tpu_sc as plsc

F32 = jnp.float32
BF16 = jnp.bfloat16

LANES = 128
GRID_W = 64
HEAD_DIM = 64
ROPE_THETA = 500000.0
LN_EPS = 1e-5
NA_ROWS = 8
NA_COLS = 16
DIFF_QK_DIM = 32
POOL_WINDOWS = (2, 4, 8, 16)
POOL_GROUP = 64
DIL_PATTERNS = ((128, 1), (512, 4), (2048, 16))
DIL_HALF = 64
N_GROUPS = 4
EXPERTS_PER_GROUP = 8
N_EXPERTS = N_GROUPS * EXPERTS_PER_GROUP
TOP_K = 2
MOE_ROWS = 512
NEG = -1e30

COL_NA = 0
COL_DIFF = 768
COL_POOL = 1536
COL_DIL = 1792
COL_GATE = 4096

VMEM_LIMIT = 56 * 1024 * 1024


def _cparams(sem, vmem=VMEM_LIMIT, flags=None):
    return pltpu.CompilerParams(dimension_semantics=sem, vmem_limit_bytes=vmem, flags=flags)


def _layer_norm(x, g, b):
    mu = jnp.mean(x, axis=-1, keepdims=True)
    xc = x - mu
    var = jnp.mean(xc * xc, axis=-1, keepdims=True)
    return xc * lax.rsqrt(var + LN_EPS) * g + b


def _sigmoid(x):
    return 0.5 * jnp.tanh(0.5 * x) + 0.5


def _dot(a, b):
    return jnp.dot(a, b, preferred_element_type=F32)


def _dot_nt(a, b):
    return lax.dot_general(a, b, (((1,), (1,)), ((), ())), preferred_element_type=F32)


def _lane(shape):
    return lax.broadcasted_iota(jnp.int32, shape, len(shape) - 1)


HI16 = -65536


def _pack_bf16_pairs(x):
    w = x.shape[1] // 2
    hi = lax.bitcast_convert_type(x[:, :w].astype(BF16).astype(F32), jnp.int32)
    lo = lax.bitcast_convert_type(x[:, w:].astype(BF16).astype(F32), jnp.int32)
    return (hi & HI16) | lax.shift_right_logical(lo, 16)


def _unpack_bf16_pairs(p):
    hi = lax.bitcast_convert_type(p & HI16, F32)
    lo = lax.bitcast_convert_type(lax.shift_left(p, 16), F32)
    return jnp.concatenate([hi, lo], axis=1)


QKV_W = 6 * LANES
WCOL = 256


QKV_TM = 512
QKV_SEGMENTS = ((COL_NA, 1, 0, HEAD_DIM), (COL_DIFF, 1, DIFF_QK_DIM // 8, DIFF_QK_DIM),
                (COL_DIL, 1, HEAD_DIM // 8, HEAD_DIM), (COL_DIL + QKV_W, 4, HEAD_DIM // 8, HEAD_DIM),
                (COL_DIL + 2 * QKV_W, 16, HEAD_DIM // 8, HEAD_DIM))


def _qkv_epilogue(z, t_ref, o_ref, zs_ref, r, half, q_scale):
    tm = z.shape[0]
    n = tm // r
    for c in range(QKV_W // LANES):
        blk = z[:, c * LANES:(c + 1) * LANES]
        if half and c < 4:
            blk = _rope(blk, t_ref, half)
        if c < 2:
            blk = blk * q_scale
        if r == 1:
            o_ref[0, 0, :, c * LANES:(c + 1) * LANES] = blk.astype(BF16)
        else:
            zs_ref[c] = blk
            for m in range(r):
                o_ref[0, m, :, c * LANES:(c + 1) * LANES] = zs_ref[
                    c, pl.ds(m, n, stride=r), :].astype(BF16)


def _proj_kernel(*refs, norm_input):
    if norm_input:
        x_ref, g_ref, b_ref, w_ref, td_ref, tl_ref, h_ref = refs[:7]
        refs = refs[7:]
    else:
        x_ref, w_ref, td_ref, tl_ref = refs[:4]
        refs = refs[4:]
    na_ref, df_ref, d0_ref, d1_ref, d2_ref, u_ref, wb_ref, zs1_ref, zs2_ref = refs

    @pl.when((pl.program_id(0) == 0) & (pl.program_id(1) == 0))
    def _():
        wb_ref[...] = w_ref[...].astype(BF16)

    x = x_ref[0]
    if norm_input:
        x = _layer_norm(x, g_ref[...], b_ref[...])
        h_ref[0] = x
    xb = x.astype(BF16)
    outs = (na_ref, df_ref, d0_ref, d1_ref, d2_ref)
    stage = (None, None, None, zs1_ref, zs2_ref)
    for (col0, r, half, head_w), o_ref, zs_ref in zip(QKV_SEGMENTS, outs, stage):
        z = _dot(xb, wb_ref[:, col0:col0 + QKV_W])
        t_ref = td_ref if head_w == DIFF_QK_DIM else tl_ref
        _qkv_epilogue(z, t_ref, o_ref, zs_ref, r, half, head_w ** -0.5 * LOG2E)
    u_ref[0] = _dot(xb, wb_ref[:, COL_POOL:COL_POOL + WCOL])


def _proj_branches(x3, w_in, l, tab_diff, tab_dil, norm=None):
    b, s, d = x3.shape
    tm = QKV_TM
    qkv_spec = lambda r: pl.BlockSpec((1, r, tm // r, QKV_W), lambda bi, i: (bi, 0, i, 0))
    qkv_shape = lambda r: jax.ShapeDtypeStruct((b, r, s // r, QKV_W), BF16)
    tab_spec = pl.BlockSpec((3, tm, LANES), lambda bi, i: (0, i, 0))
    row_spec = pl.BlockSpec((1, tm, d), lambda bi, i: (bi, i, 0))
    vec = pl.BlockSpec((1, d), lambda bi, i: (0, 0))
    w_spec = pl.BlockSpec((None, d, COL_GATE), lambda bi, i: (l, 0, 0), pipeline_mode=pl.Buffered(1))
    rs = [seg[1] for seg in QKV_SEGMENTS]
    in_specs = [row_spec] + ([vec, vec] if norm else []) + [w_spec, tab_spec, tab_spec]
    args = [x3] + ([a.reshape(1, d) for a in norm] if norm else []) + [w_in, tab_diff, tab_dil]
    outs = pl.pallas_call(
        functools.partial(_proj_kernel, norm_input=bool(norm)),
        grid=(b, s // tm),
        in_specs=in_specs,
        out_specs=([row_spec] if norm else []) + [qkv_spec(r) for r in rs] + [
            pl.BlockSpec((1, tm, WCOL), lambda bi, i: (bi, i, 0))],
        out_shape=([jax.ShapeDtypeStruct((b, s, d), F32)] if norm else []) + [
            qkv_shape(r) for r in rs] + [jax.ShapeDtypeStruct((b, s, WCOL), F32)],
        scratch_shapes=[pltpu.VMEM((d, COL_GATE), BF16)] + [pltpu.VMEM((QKV_W // LANES, tm, LANES), F32)] * 2,
        compiler_params=_cparams(("arbitrary", "arbitrary")),
        name="proj_branches",
    )(*args)
    n_qkv = len(rs)
    head = list(outs[:-n_qkv - 1])
    return head + [o.reshape(b, s, QKV_W) for o in outs[-n_qkv - 1:-1]] + [outs[-1]]


def _rope_tables(seq, head_w):
    rot = head_w // 4
    half = rot // 2
    inv_freq = jnp.exp(jnp.arange(half, dtype=F32) * (-2.0 * math.log(ROPE_THETA) / rot))
    ang = jnp.arange(seq, dtype=jnp.int32).astype(F32)[:, None] * inv_freq[None, :]
    cos, sin = jnp.cos(ang), jnp.sin(ang)
    zero = jnp.zeros((seq, head_w - rot), F32)
    zh = jnp.zeros((seq, half), F32)
    t0 = jnp.concatenate([cos, cos, jnp.ones((seq, head_w - rot), F32)], axis=1)
    t1 = jnp.concatenate([-sin, zh, zero], axis=1)
    t2 = jnp.concatenate([zh, sin, zero], axis=1)
    reps = LANES // head_w
    return jnp.stack([jnp.tile(t0, (1, reps)), jnp.tile(t1, (1, reps)), jnp.tile(t2, (1, reps))])


def _rope(x, t_ref, half):
    return (x * t_ref[0] + pltpu.roll(x, LANES - half, 1) * t_ref[1]
            + pltpu.roll(x, half, 1) * t_ref[2])


LOG2E = math.log2(math.e)
LN2 = math.log(2.0)


def _na_bias_table(rpb):
    kr, kc = NA_ROWS, NA_COLS
    n_heads = rpb.shape[0]
    col = np.arange(GRID_W)
    col_start = np.clip(col - kc // 2, 0, GRID_W - kc)
    in_win = (col[None, :] >= col_start[:, None]) & (col[None, :] < col_start[:, None] + kc)
    dc = np.clip(col[None, :] - col[:, None] + kc - 1, 0, 2 * kc - 2)
    sel_c = ((dc[..., None] == np.arange(2 * kc - 1)) & in_win[..., None]).astype(np.float32)
    rp = rpb.astype(F32).reshape(n_heads // 2, 2, 2 * kr - 1, 2 * kc - 1) * LOG2E
    tiles = jnp.einsum('phaj,qxj->phaqx', rp, jnp.asarray(sel_c), precision=lax.Precision.HIGHEST)
    tiles = tiles + jnp.asarray(np.where(in_win, 0.0, NEG).astype(np.float32))
    tiles = jnp.concatenate([tiles, jnp.full_like(tiles[:, :, :1], NEG)], axis=2)
    return jnp.concatenate([tiles, tiles], axis=-1)


NA_QROWS = 4
NA_KROWS = 12
NA_MASKED = 2 * NA_ROWS - 1


def _na_row_offsets(n_rows):
    table = []
    for blk in (0, 1, n_rows // NA_QROWS - 1):
        ws = _na_window_start(blk, n_rows)
        rows = []
        for i in range(NA_QROWS):
            r = blk * NA_QROWS + i
            start = min(max(r - NA_ROWS // 2, 0), n_rows - NA_ROWS)
            rows.append([ws + kk - r + NA_ROWS - 1 if start <= ws + kk < start + NA_ROWS else NA_MASKED
                         for kk in range(NA_KROWS)])
        table.append(rows)
    return table


def _na_window_start(blk, n_rows):
    lo = blk * NA_QROWS - NA_ROWS // 2
    hi = n_rows - NA_KROWS
    if isinstance(blk, int):
        return min(max(lo, 0), hi)
    return jnp.clip(lo, 0, hi)


def _stack_heads(q, lane):
    zero = jnp.zeros_like(q)
    return jnp.concatenate([jnp.where(lane < HEAD_DIM, q, zero), jnp.where(lane < HEAD_DIM, zero, q)],
                           axis=0)


def _na_kernel(q_ref, k_ref, v_ref, tile_ref, o_ref, bias_ref, *, blocks_per_step, n_rows):
    step = pl.program_id(2)
    nq = NA_QROWS * GRID_W
    kwin = NA_KROWS * GRID_W
    lane = _lane((nq, LANES))
    last_blk = n_rows // NA_QROWS - 1

    @pl.when(step == 0)
    def _():
        low = _lane((GRID_W, LANES)) < GRID_W
        for case, rows in enumerate(_na_row_offsets(n_rows)):
            for hh in range(2):
                for i, offs in enumerate(rows):
                    r0 = (hh * NA_QROWS + i) * GRID_W
                    for kp in range(NA_KROWS // 2):
                        bias_ref[case, r0:r0 + GRID_W, kp * LANES:(kp + 1) * LANES] = jnp.where(
                            low, tile_ref[0, hh, offs[2 * kp]], tile_ref[0, hh, offs[2 * kp + 1]])

    for i in range(blocks_per_step):
        blk = step * blocks_per_step + i
        case = jnp.where(blk == 0, 0, jnp.where(blk == last_blk, 2, 1))
        k0 = pl.multiple_of(_na_window_start(blk, n_rows) * GRID_W, GRID_W)
        q2 = _stack_heads(q_ref[0, i * nq:(i + 1) * nq, :], lane)
        sc = _dot_nt(q2, k_ref[0, pl.ds(k0, kwin), :]) + bias_ref[case]
        m = jnp.max(sc, axis=-1, keepdims=True)
        e = jnp.exp2(sc - m)
        den = jnp.sum(e, axis=-1, keepdims=True)
        pv = _dot(e.astype(BF16), v_ref[0, pl.ds(k0, kwin), :]) / den
        o = jnp.where(lane < HEAD_DIM, pv[0:nq], pv[nq:2 * nq])
        o_ref[0, i * nq:(i + 1) * nq, :] = o.astype(BF16)


def _na_attention(qkv, tiles):
    b, s, _ = qkv.shape
    n_rows = s // GRID_W
    bps = 4
    rps = bps * NA_QROWS
    tq = rps * GRID_W
    return pl.pallas_call(
        functools.partial(_na_kernel, blocks_per_step=bps, n_rows=n_rows),
        grid=(b, 2, n_rows // rps),
        in_specs=[pl.BlockSpec((1, tq, LANES), lambda bi, hp, i: (bi, i, hp)),
                  pl.BlockSpec((1, s, LANES), lambda bi, hp, i: (bi, 0, 2 + hp)),
                  pl.BlockSpec((1, s, LANES), lambda bi, hp, i: (bi, 0, 4 + hp)),
                  pl.BlockSpec((1, 2, NA_MASKED + 1, GRID_W, LANES), lambda bi, hp, i: (hp, 0, 0, 0, 0))],
        out_specs=pl.BlockSpec((1, tq, LANES), lambda bi, hp, i: (bi, i, hp)),
        out_shape=jax.ShapeDtypeStruct((b, s, 2 * LANES), BF16),
        scratch_shapes=[pltpu.VMEM((3, 2 * NA_QROWS * GRID_W, NA_KROWS * GRID_W), F32)],
        compiler_params=_cparams(("parallel", "parallel", "arbitrary")),
        name="na_attn",
    )(qkv, qkv, qkv, tiles)


def _diff_kernel(lam_ref, q_ref, k_ref, v_ref, g_ref, o_ref, *, lam_init, rows):
    tq = q_ref.shape[1]
    lane = _lane((rows, LANES))
    dl = lam_ref[...]
    lam = (jnp.exp(jnp.sum(dl[0:1] * dl[1:2], axis=-1, keepdims=True))
           - jnp.exp(jnp.sum(dl[2:3] * dl[3:4], axis=-1, keepdims=True)) + lam_init)
    v = v_ref[0]
    v_lane = _lane(v.shape)
    one = jnp.ones_like(v)
    v_ext = (jnp.where(v_lane < HEAD_DIM, v, one), jnp.where(v_lane < HEAD_DIM, one, v))
    in_h0 = lane < HEAD_DIM
    for r0 in range(0, tq, rows):
        q = q_ref[0, r0:r0 + rows, :]
        zero = jnp.zeros_like(q)
        pv = []
        for lo in range(0, LANES, DIFF_QK_DIM):
            qm = jnp.where((lane >= lo) & (lane < lo + DIFF_QK_DIM), q, zero)
            sc = _dot_nt(qm, k_ref[0])
            e = jnp.exp2(sc - jnp.max(sc, axis=-1, keepdims=True))
            ev = _dot(e.astype(BF16), v_ext[lo // HEAD_DIM])
            pv.append(ev / pltpu.roll(ev, HEAD_DIM, 1))
        o = jnp.where(in_h0, pv[0] - lam * pv[1], pv[2] - lam * pv[3])
        o2 = o * o
        ms0 = jnp.sum(jnp.where(in_h0, o2, 0.0), axis=-1, keepdims=True) / HEAD_DIM
        ms1 = jnp.sum(jnp.where(in_h0, 0.0, o2), axis=-1, keepdims=True) / HEAD_DIM
        ms = jnp.where(in_h0, ms0, ms1)
        o = o * lax.rsqrt(ms + LN_EPS) * g_ref[...] * (1.0 - lam_init)
        o_ref[0, r0:r0 + rows, :] = o.astype(BF16)


def _diff_attention(qkv, diff_lam, subln_g, lam_init):
    b, s, _ = qkv.shape
    tq = 1024
    g2 = jnp.tile(subln_g.reshape(1, HEAD_DIM), (1, 2))
    return pl.pallas_call(
        functools.partial(_diff_kernel, lam_init=lam_init, rows=512),
        grid=(b, 2, s // tq),
        in_specs=[pl.BlockSpec((4, DIFF_QK_DIM), lambda bi, hp, i: (0, 0)),
                  pl.BlockSpec((1, tq, LANES), lambda bi, hp, i: (bi, i, hp)),
                  pl.BlockSpec((1, s, LANES), lambda bi, hp, i: (bi, 0, 2 + hp)),
                  pl.BlockSpec((1, s, LANES), lambda bi, hp, i: (bi, 0, 4 + hp)),
                  pl.BlockSpec((1, LANES), lambda bi, hp, i: (0, 0))],
        out_specs=pl.BlockSpec((1, tq, LANES), lambda bi, hp, i: (bi, i, hp)),
        out_shape=jax.ShapeDtypeStruct((b, s, 2 * LANES), BF16),
        compiler_params=_cparams(("parallel", "parallel", "arbitrary")),
        name="diff_attn",
    )(diff_lam, qkv, qkv, qkv, g2)


POOL_PAD = 16
POOL_CHUNK = 512
POOL_HALO = 8


def _pool_kernel(u_ref, w_ref, sc_ref, o_ref, p_ref):
    s = u_ref.shape[1]
    width = u_ref.shape[2]
    p_ref[0:POOL_PAD, :] = jnp.zeros((POOL_PAD, width), F32)
    p_ref[POOL_PAD + s:POOL_PAD + s + POOL_PAD, :] = jnp.zeros((POOL_PAD, width), F32)
    p_ref[POOL_PAD:POOL_PAD + s, :] = u_ref[0]
    n = POOL_CHUNK + 2 * POOL_HALO
    lane = _lane((POOL_CHUNK, width))
    row = lax.broadcasted_iota(jnp.int32, (POOL_CHUNK, width), 0)
    w_of_lane = jnp.where(lane < POOL_GROUP, 2, jnp.where(lane < 2 * POOL_GROUP, 4,
                          jnp.where(lane < 3 * POOL_GROUP, 8, 16)))

    def body(ci, carry):
        c0 = pl.multiple_of(ci * POOL_CHUNK, POOL_CHUNK)
        x = p_ref[pl.ds(c0 + POOL_PAD - POOL_HALO, n), :]
        w2 = x + pltpu.roll(x, 1, 0)
        w4 = pltpu.roll(w2, 1, 0) + pltpu.roll(w2, n - 1, 0)
        w8 = pltpu.roll(w4, 2, 0) + pltpu.roll(w4, n - 2, 0)
        w16 = pltpu.roll(w8, 4, 0) + pltpu.roll(w8, n - 4, 0)
        u = x[POOL_HALO:POOL_HALO + POOL_CHUNK]
        wsum = jnp.where(lane < POOL_GROUP, w2[POOL_HALO:POOL_HALO + POOL_CHUNK],
                         jnp.where(lane < 2 * POOL_GROUP, w4[POOL_HALO:POOL_HALO + POOL_CHUNK],
                                   jnp.where(lane < 3 * POOL_GROUP, w8[POOL_HALO:POOL_HALO + POOL_CHUNK],
                                             w16[POOL_HALO:POOL_HALO + POOL_CHUNK])))
        t = row + c0
        half_w = w_of_lane // 2
        lo = jnp.maximum(t - half_w, 0)
        hi = jnp.minimum(t + w_of_lane - 1 - half_w, s - 1)
        cnt = (hi - lo + 1).astype(F32)
        dlt = wsum / cnt - u
        y = _dot(dlt.astype(BF16), w_ref[...]) * sc_ref[...]
        o_ref[0, pl.ds(c0, POOL_CHUNK), :] = y.astype(BF16)
        return carry

    lax.fori_loop(0, s // POOL_CHUNK, body, 0)


def _pool(u3, pool_w, pool_scale):
    b, s, width = u3.shape
    wbd = jax.scipy.linalg.block_diag(*[pool_w[g] for g in range(len(POOL_WINDOWS))]).astype(BF16)
    return pl.pallas_call(
        _pool_kernel,
        grid=(b,),
        in_specs=[pl.BlockSpec((1, s, width), lambda bi: (bi, 0, 0)),
                  pl.BlockSpec((width, width), lambda bi: (0, 0)),
                  pl.BlockSpec((1, width), lambda bi: (0, 0))],
        out_specs=pl.BlockSpec((1, s, width), lambda bi: (bi, 0, 0)),
        out_shape=jax.ShapeDtypeStruct((b, s, width), BF16),
        scratch_shapes=[pltpu.VMEM((s + 2 * POOL_PAD, width), F32)],
        compiler_params=_cparams(("parallel",)),
        name="pool",
    )(u3, wbd, pool_scale.reshape(1, width))


SWA_Q = 128
SWA_BAND = SWA_Q + 2 * DIL_HALF


def _swa_kernel(q_ref, k_ref, v_ref, o_ref, l_ref, *, length):
    lane = _lane((SWA_Q, LANES))
    rel = (lax.broadcasted_iota(jnp.int32, (2 * SWA_Q, SWA_BAND), 1)
           - (lax.broadcasted_iota(jnp.int32, (2 * SWA_Q, SWA_BAND), 0) & (SWA_Q - 1)))
    masks = {}
    for i in range(q_ref.shape[1] // SWA_Q):
        r0 = i * SWA_Q
        l0 = r0 % length
        lo = r0 - l0 + min(max(l0 - DIL_HALF, 0), length - SWA_BAND)
        if lo - r0 not in masks:
            d = rel + (lo - r0)
            masks[lo - r0] = jnp.where((d >= -DIL_HALF) & (d <= DIL_HALF), 0.0, NEG)
        q2 = _stack_heads(q_ref[0, r0:r0 + SWA_Q, :], lane)
        sc = _dot_nt(q2, k_ref[0, lo:lo + SWA_BAND, :]) + masks[lo - r0]
        m = jnp.max(sc, axis=-1, keepdims=True)
        e = jnp.exp2(sc - m)
        den = jnp.sum(e, axis=-1, keepdims=True)
        pv = _dot(e.astype(BF16), v_ref[0, lo:lo + SWA_BAND, :]) / den
        lse2 = m + jnp.log2(den)
        o_ref[0, i * SWA_Q:(i + 1) * SWA_Q, :] = jnp.where(lane < HEAD_DIM, pv[0:SWA_Q], pv[SWA_Q:])
        l_ref[0, i * SWA_Q:(i + 1) * SWA_Q, :] = jnp.where(lane < HEAD_DIM, lse2[0:SWA_Q], lse2[SWA_Q:])


def _swa(qkv, length):
    b, s, _ = qkv.shape
    shp = jax.ShapeDtypeStruct((b, s, 2 * LANES), F32)
    seq = lambda col: pl.BlockSpec((1, s, LANES), lambda bi, hp: (bi, 0, col + hp))
    return pl.pallas_call(
        functools.partial(_swa_kernel, length=length),
        grid=(b, 2),
        in_specs=[seq(0), seq(2), seq(4)],
        out_specs=[seq(0), seq(0)],
        out_shape=[shp, shp],
        compiler_params=_cparams(("parallel", "parallel")),
        name=f"swa_l{length}",
    )(qkv, qkv, qkv)


def _dil_combine(dil_refs, stage_refs, tm):
    (o0_ref, l0_ref), (o1_ref, l1_ref), (o2_ref, l2_ref) = dil_refs
    so1, sl1, so2, sl2 = stage_refs
    r1 = DIL_PATTERNS[1][1]
    r2 = DIL_PATTERNS[2][1]
    halves = []
    for hp in range(2):
        cols = slice(hp * LANES, (hp + 1) * LANES)
        for m in range(r1):
            so1[pl.ds(m, tm // r1, stride=r1), :] = o1_ref[0, m, :, cols]
            sl1[pl.ds(m, tm // r1, stride=r1), :] = l1_ref[0, m, :, cols]
        for m in range(r2):
            so2[pl.ds(m, tm // r2, stride=r2), :] = o2_ref[0, m, :, cols]
            sl2[pl.ds(m, tm // r2, stride=r2), :] = l2_ref[0, m, :, cols]
        l0, l1, l2 = l0_ref[0, 0, :, cols], sl1[...], sl2[...]
        mx = jnp.maximum(jnp.maximum(l0, l1), l2)
        e0, e1, e2 = jnp.exp2(l0 - mx), jnp.exp2(l1 - mx), jnp.exp2(l2 - mx)
        den = e0 + e1 + e2
        halves.append((e0 / den) * o0_ref[0, 0, :, cols] + (e1 / den) * so1[...] + (e2 / den) * so2[...])
    return jnp.concatenate(halves, axis=1)


def _merge_kernel(ya_ref, yb_ref, yc_ref, o0_ref, l0_ref, o1_ref, l1_ref, o2_ref, l2_ref,
                  wg_ref, bg_ref, wb_ref, wo_ref, h_ref, g_ref, b_ref, wr_ref, br_ref,
                  o_ref, op_ref, r_ref, cnt_ref, so1, sl1, so2, sl2, carry, before, *, alpha):
    tm, d = h_ref.shape

    @pl.when(pl.program_id(0) == 0)
    def _():
        _route_init(carry, before)

    hb = h_ref[...].astype(BF16)
    y_d = _dil_combine(((o0_ref, l0_ref), (o1_ref, l1_ref), (o2_ref, l2_ref)),
                       (so1, sl1, so2, sl2), tm).astype(BF16)
    merged = None
    for n, y in enumerate((ya_ref[...], yb_ref[...], yc_ref[...], y_d)):
        zg = _dot(hb, wg_ref[:, n * d:(n + 1) * d]) + bg_ref[:, n * d:(n + 1) * d]
        term = _sigmoid(zg) * _dot(y, wb_ref[n])
        merged = term if merged is None else merged + term
    mix = _dot(merged.astype(BF16), wo_ref[...])
    h1 = _layer_norm(alpha * h_ref[...] + mix, g_ref[...], b_ref[...])
    o_ref[...] = h1
    op_ref[...] = _pack_bf16_pairs(h1)
    _route_finish(_route_select(h1, wr_ref, br_ref), r_ref, cnt_ref, carry, before)


def _merge(ys, dil, wg, b_gate, wb, wo, h2, g, bb, router, alpha):
    t, d = h2.shape
    w_route, b_route = router
    b, s, bw = dil[0][0].shape
    nb = len(ys) + 1
    tm = QKV_TM
    nt = s // tm
    yspec = pl.BlockSpec((tm, bw), lambda i: (i, 0))
    vec = pl.BlockSpec((1, d), lambda i: (0, 0))
    dil_args, dil_specs = [], []
    for (o, lse), (_, r) in zip(dil, DIL_PATTERNS):
        spec = pl.BlockSpec((1, r, tm // r, bw), lambda i: (i // nt, 0, i % nt, 0))
        dil_args += [o.reshape(b, r, s // r, bw), lse.reshape(b, r, s // r, bw)]
        dil_specs += [spec, spec]
    return pl.pallas_call(
        functools.partial(_merge_kernel, alpha=alpha),
        grid=(t // tm,),
        in_specs=[yspec, yspec, yspec] + dil_specs + [
                  pl.BlockSpec((d, nb * d), lambda i: (0, 0)),
                  pl.BlockSpec((1, nb * d), lambda i: (0, 0)),
                  pl.BlockSpec((nb, bw, d), lambda i: (0, 0, 0)),
                  pl.BlockSpec((d, d), lambda i: (0, 0)),
                  pl.BlockSpec((tm, d), lambda i: (i, 0)),
                  vec, vec,
                  pl.BlockSpec((LANES, d), lambda i: (0, 0)),
                  pl.BlockSpec((LANES, 1), lambda i: (0, 0))],
        out_specs=[pl.BlockSpec((tm, d), lambda i: (i, 0)),
                   pl.BlockSpec((tm, d // 2), lambda i: (i, 0)),
                   pl.BlockSpec((tm, LANES), lambda i: (i, 0)),
                   pl.BlockSpec((LANES, 1), lambda i: (0, 0))],
        out_shape=[jax.ShapeDtypeStruct((t, d), F32), jax.ShapeDtypeStruct((t, d // 2), jnp.int32),
                   jax.ShapeDtypeStruct((t, LANES), F32), jax.ShapeDtypeStruct((LANES, 1), F32)],
        scratch_shapes=[pltpu.VMEM((tm, LANES), F32)] * 4 + [pltpu.VMEM((LANES, 1), F32),
                                                               pltpu.VMEM((tm, tm), BF16)],
        compiler_params=_cparams(("arbitrary",)),
        name="merge",
    )(*ys, *dil_args, wg, b_gate.reshape(1, nb * d), wb, wo, h2, g.reshape(1, d), bb.reshape(1, d),
      w_route, b_route)


def _route_init(carry, before):
    tm = before.shape[0]
    carry[...] = jnp.zeros_like(carry)
    before[...] = (lax.broadcasted_iota(jnp.int32, (tm, tm), 0)
                   < lax.broadcasted_iota(jnp.int32, (tm, tm), 1)).astype(BF16)


def _route_select(h, w_ref, b_ref):
    tm = h.shape[0]
    logits = lax.dot_general(w_ref[...], h, (((1,), (1,)), ((), ())),
                             preferred_element_type=F32,
                             precision=lax.Precision.HIGHEST) + b_ref[...]
    row = lax.broadcasted_iota(jnp.int32, (LANES, tm), 0)
    big = jnp.int32(1 << 20)
    is_g = row < N_GROUPS
    gl = jnp.where(is_g, logits, -jnp.inf)
    gmax = jnp.max(gl, axis=0, keepdims=True)
    gsel = jnp.min(jnp.where(is_g & (gl == gmax), row, big), axis=0, keepdims=True)
    pg = 1.0 / jnp.sum(jnp.exp(gl - gmax), axis=0, keepdims=True)
    e_lo = N_GROUPS + gsel * EXPERTS_PER_GROUP
    in_grp = (row >= e_lo) & (row < e_lo + EXPERTS_PER_GROUP)
    el = jnp.where(in_grp, logits, -jnp.inf)
    v1 = jnp.max(el, axis=0, keepdims=True)
    i1 = jnp.min(jnp.where(in_grp & (el == v1), row, big), axis=0, keepdims=True)
    el2 = jnp.where(row == i1, -jnp.inf, el)
    v2 = jnp.max(el2, axis=0, keepdims=True)
    i2 = jnp.min(jnp.where(in_grp & (row != i1) & (el2 == v2), row, big), axis=0, keepdims=True)
    t2 = jnp.exp(v2 - v1)
    g1 = pg / (1.0 + t2)
    g2 = pg * t2 / (1.0 + t2)
    return dict(row=row, i1=i1, i2=i2, g1=g1, g2=g2,
                oh1=row == i1, oh2=row == i2)


def _route_finish(sel, o_ref, cnt_ref, carry, before):
    row, i1, i2, g1, g2, oh1, oh2 = (sel[k] for k in ('row', 'i1', 'i2', 'g1', 'g2', 'oh1', 'oh2'))
    oh1b = oh1.astype(BF16)
    oh2b = oh2.astype(BF16)
    c0 = carry[...]
    c1 = c0 + jnp.sum(oh1b.astype(F32), axis=1, keepdims=True)
    rank1 = jnp.sum(jnp.where(oh1, _dot(oh1b, before[...]) + c0, 0.0), axis=0, keepdims=True)
    rank2 = jnp.sum(jnp.where(oh2, _dot(oh2b, before[...]) + c1, 0.0), axis=0, keepdims=True)
    c2 = c1 + jnp.sum(oh2b.astype(F32), axis=1, keepdims=True)
    carry[...] = c2
    cnt_ref[...] = c2
    e1 = (i1 - N_GROUPS).astype(F32)
    e2 = (i2 - N_GROUPS).astype(F32)
    out_t = jnp.where(row == 0, e1, jnp.where(row == 1, e2, jnp.where(row == 2, g1, jnp.where(
        row == 3, g2, jnp.where(row == 4, rank1, jnp.where(row == 5, rank2, 0.0))))))
    o_ref[...] = out_t.T


def _router_params(rg_w, rg_b, re_w, re_b):
    d = rg_w.shape[0]
    pad = LANES - N_GROUPS - N_EXPERTS
    w_t = jnp.concatenate([rg_w.T, re_w.T, jnp.zeros((pad, d), F32)], axis=0)
    bias = jnp.concatenate([rg_b, re_b, jnp.zeros((pad,), F32)]).reshape(LANES, 1)
    return w_t, bias


MOE_FIRST, MOE_SLOT, MOE_HAS_NEXT = 1, 2, 4


def _moe_kernel(be_ref, nv_ref, fl_ref, nx_ref, x_ref, wg_hbm, wu_hbm, wd_hbm, o_ref,
                wgb, wub, wdb, stage_g, stage_u, stage_d, sem, *, layer):
    j = pl.program_id(0)
    n_valid = nv_ref[j]
    flags = fl_ref[j]
    slot = (flags // MOE_SLOT) % 2

    def weight_copies(e, s):
        return [pltpu.make_async_copy(w_hbm.at[layer, e], stage.at[s], sem.at[s, k])
                for k, (w_hbm, stage) in enumerate(((wg_hbm, stage_g), (wu_hbm, stage_u),
                                                    (wd_hbm, stage_d)))]

    @pl.when(flags % 2 == MOE_FIRST)
    def _():
        @pl.when(j == 0)
        def _():
            for cp in weight_copies(be_ref[j], slot):
                cp.start()

        for cp in weight_copies(be_ref[j], slot):
            cp.wait()

        @pl.when((flags // MOE_HAS_NEXT) % 2 == 1)
        def _():
            for cp in weight_copies(nx_ref[j], 1 - slot):
                cp.start()

        wgb[...] = stage_g[slot].astype(BF16)
        wub[...] = stage_u[slot].astype(BF16)
        wdb[...] = stage_d[slot].astype(BF16)

    @pl.when(n_valid > 0)
    def _():
        row = lax.broadcasted_iota(jnp.int32, x_ref.shape, 0)
        x = _unpack_bf16_pairs(jnp.where(row < n_valid, x_ref[...], 0)).astype(BF16)
        g = _dot(x, wgb[...])
        u = _dot(x, wub[...])
        hmid = (g * _sigmoid(g)) * u
        o_ref[...] = _pack_bf16_pairs(_dot(hmid.astype(BF16), wdb[...]))

    @pl.when(n_valid <= 0)
    def _():
        o_ref[...] = jnp.zeros_like(o_ref)


def _moe_experts(xs, block_e, n_valid, flags, next_e, wg, wu, wd, l):
    cap, dp = xs.shape
    d = 2 * dp
    de = wg.shape[3]
    n_blocks = cap // MOE_ROWS
    hbm = pl.BlockSpec(memory_space=pl.ANY)
    grid_spec = pltpu.PrefetchScalarGridSpec(
        num_scalar_prefetch=4,
        grid=(n_blocks,),
        in_specs=[pl.BlockSpec((MOE_ROWS, dp), lambda j, *_: (j, 0)), hbm, hbm, hbm],
        out_specs=pl.BlockSpec((MOE_ROWS, dp), lambda j, *_: (j, 0)),
        scratch_shapes=[pltpu.VMEM((d, de), BF16), pltpu.VMEM((d, de), BF16), pltpu.VMEM((de, d), BF16),
                        pltpu.VMEM((2, d, de), F32), pltpu.VMEM((2, d, de), F32),
                        pltpu.VMEM((2, de, d), F32), pltpu.SemaphoreType.DMA((2, 3))],
    )
    return pl.pallas_call(
        functools.partial(_moe_kernel, layer=l),
        grid_spec=grid_spec,
        out_shape=jax.ShapeDtypeStruct((cap, dp), jnp.int32),
        compiler_params=_cparams(("arbitrary",)),
        name="moe_experts",
    )(block_e, n_valid, flags, next_e, xs, wg, wu, wd)


SC_CORES = 2
SC_SUBCORES = 16
SC_WORKERS = SC_CORES * SC_SUBCORES
SC_CHUNK = 64


def _sc_mesh():
    return plsc.VectorSubcoreMesh(core_axis_name="c", subcore_axis_name="s",
                                  num_cores=SC_CORES, num_subcores=SC_SUBCORES)


def _sc_scatter_rows(src, idx0, idx1, n_out):
    t, d = src.shape
    per_w = t // SC_WORKERS
    n_chunks = per_w // SC_CHUNK
    idx_shape = (SC_WORKERS, n_chunks, SC_CHUNK)

    @functools.partial(
        pl.kernel, mesh=_sc_mesh(),
        out_type=jax.ShapeDtypeStruct((n_out, d), src.dtype),
        scratch_types=[pltpu.VMEM((n_chunks, SC_CHUNK), jnp.int32),
                       pltpu.VMEM((n_chunks, SC_CHUNK), jnp.int32),
                       pltpu.VMEM((2, SC_CHUNK, d), src.dtype),
                       pltpu.SemaphoreType.DMA((2,)), pltpu.SemaphoreType.DMA((2,))],
        name="sc_dispatch",
    )
    def k(src_hbm, i0_hbm, i1_hbm, out_hbm, i0_v, i1_v, rows_v, sem_in, sem_out):
        wid = lax.axis_index("s") * SC_CORES + lax.axis_index("c")
        base = wid * per_w
        pltpu.sync_copy(i0_hbm.at[wid], i0_v)
        pltpu.sync_copy(i1_hbm.at[wid], i1_v)

        def load(ci):
            s = ci % 2
            return pltpu.make_async_copy(src_hbm.at[pl.ds(base + ci * SC_CHUNK, SC_CHUNK)],
                                         rows_v.at[s], sem_in.at[s])

        def scatter(ci, idx_v):
            s = ci % 2
            return pltpu.make_async_copy(rows_v.at[s], out_hbm.at[idx_v.at[ci]], sem_out.at[s])

        load(0).start()
        for ci in range(n_chunks):
            load(ci).wait()
            if ci >= 1:
                scatter(ci - 1, i0_v).wait()
                scatter(ci - 1, i1_v).wait()
            if ci + 1 < n_chunks:
                load(ci + 1).start()
            scatter(ci, i0_v).start()
            scatter(ci, i1_v).start()
        scatter(n_chunks - 1, i0_v).wait()
        scatter(n_chunks - 1, i1_v).wait()

    return k(src, idx0.reshape(idx_shape), idx1.reshape(idx_shape))


def _sc_gather_rows(table, idx):
    n = idx.shape[0]
    d = table.shape[1]
    per_w = n // SC_WORKERS
    n_chunks = per_w // SC_CHUNK

    @functools.partial(
        pl.kernel, mesh=_sc_mesh(),
        out_type=jax.ShapeDtypeStruct((n, d), table.dtype),
        scratch_types=[pltpu.VMEM((n_chunks, SC_CHUNK), jnp.int32),
                       pltpu.VMEM((2, SC_CHUNK, d), table.dtype),
                       pltpu.SemaphoreType.DMA((2,)), pltpu.SemaphoreType.DMA((2,))],
        name="sc_collect",
    )
    def k(table_hbm, idx_hbm, out_hbm, idx_v, rows_v, sem_in, sem_out):
        wid = lax.axis_index("s") * SC_CORES + lax.axis_index("c")
        base = wid * per_w
        pltpu.sync_copy(idx_hbm.at[wid], idx_v)

        def gather(ci):
            s = ci % 2
            return pltpu.make_async_copy(table_hbm.at[idx_v.at[ci]], rows_v.at[s], sem_in.at[s])

        def store(ci):
            s = ci % 2
            return pltpu.make_async_copy(rows_v.at[s], out_hbm.at[pl.ds(base + ci * SC_CHUNK, SC_CHUNK)],
                                         sem_out.at[s])

        gather(0).start()
        for ci in range(n_chunks):
            gather(ci).wait()
            if ci >= 1:
                store(ci - 1).wait()
            if ci + 1 < n_chunks:
                gather(ci + 1).start()
            store(ci).start()
        store(n_chunks - 1).wait()

    return k(table, idx.reshape(SC_WORKERS, n_chunks, SC_CHUNK))


def _combine_kernel(h_ref, r_ref, y1_ref, y2_ref, g_ref, b_ref, o_ref, *, alpha):
    g1 = r_ref[:, 2:3]
    g2 = r_ref[:, 3:4]
    ffn = g1 * _unpack_bf16_pairs(y1_ref[...]) + g2 * _unpack_bf16_pairs(y2_ref[...])
    o_ref[...] = _layer_norm(alpha * h_ref[...] + ffn, g_ref[...], b_ref[...])


def _combine(h2, routing, y12, g, bb, alpha):
    t, d = h2.shape
    tm = 512
    nt = t // tm
    row = pl.BlockSpec((tm, d), lambda i: (i, 0))
    vec = pl.BlockSpec((1, d), lambda i: (0, 0))
    return pl.pallas_call(
        functools.partial(_combine_kernel, alpha=alpha),
        grid=(nt,),
        in_specs=[row, pl.BlockSpec((tm, LANES), lambda i: (i, 0)),
                  pl.BlockSpec((tm, d // 2), lambda i: (i, 0)),
                  pl.BlockSpec((tm, d // 2), lambda i: (nt + i, 0)), vec, vec],
        out_specs=row,
        out_shape=jax.ShapeDtypeStruct((t, d), F32),
        compiler_params=_cparams(("parallel",)),
        name="combine_ln",
    )(h2, routing, y12, y12, g.reshape(1, d), bb.reshape(1, d))


def _moe(h2, h2_packed, routing, counts, wg, wu, wd, l, g, bb, alpha):
    t, d = h2.shape
    eid = routing[:, 0:2].astype(jnp.int32)
    rank = routing[:, 4:6].astype(jnp.int32)
    cnt = counts[N_GROUPS:N_GROUPS + N_EXPERTS, 0].astype(jnp.int32)
    padded = (cnt + MOE_ROWS - 1) // MOE_ROWS * MOE_ROWS
    pad_end = jnp.cumsum(padded)
    pad_start = pad_end - padded
    experts = jnp.arange(N_EXPERTS, dtype=jnp.int32)
    dest = jnp.sum(jnp.where(eid[..., None] == experts, pad_start, 0), axis=-1) + rank
    n_slots = t * TOP_K
    n_blocks = (n_slots + N_EXPERTS * (MOE_ROWS - 1) + MOE_ROWS - 1) // MOE_ROWS
    cap = n_blocks * MOE_ROWS
    blk_row = jnp.arange(n_blocks, dtype=jnp.int32) * MOE_ROWS
    block_e = jnp.minimum(jnp.sum((pad_end[None, :] <= blk_row[:, None]).astype(jnp.int32), axis=1),
                          N_EXPERTS - 1)
    is_e = block_e[:, None] == experts
    blk_cnt = jnp.sum(jnp.where(is_e, cnt, 0), axis=1)
    blk_start = jnp.sum(jnp.where(is_e, pad_start, 0), axis=1)
    n_valid = jnp.clip(blk_cnt - (blk_row - blk_start), 0, MOE_ROWS).astype(jnp.int32)
    used = cnt > 0
    later = (experts[None, :] > experts[:, None]) & used[None, :]
    next_used = jnp.min(jnp.where(later, experts[None, :], N_EXPERTS), axis=1)
    rank = jnp.cumsum(used.astype(jnp.int32)) - 1
    blk_first = (n_valid > 0) & (blk_row == blk_start)
    blk_next = jnp.sum(jnp.where(is_e, next_used, 0), axis=1)
    blk_slot = jnp.sum(jnp.where(is_e, rank, 0), axis=1) % 2
    flags = (blk_first.astype(jnp.int32) * MOE_FIRST + blk_slot * MOE_SLOT
             + (blk_next < N_EXPERTS).astype(jnp.int32) * MOE_HAS_NEXT).astype(jnp.int32)
    next_e = jnp.minimum(blk_next, N_EXPERTS - 1).astype(jnp.int32)
    xs = _sc_scatter_rows(h2_packed, dest[:, 0], dest[:, 1], cap)
    ys = _moe_experts(xs, block_e, n_valid, flags, next_e, wg, wu, wd, l)
    y12 = _sc_gather_rows(ys, jnp.concatenate([dest[:, 0], dest[:, 1]]))
    return _combine(h2, routing, y12, g, bb, alpha)


def _mixing_layer(h, b, s, l, p, lam_init, input_norm=None):
    t = b * s
    alpha = (2 * p['w_in'].shape[0]) ** 0.25
    d = h.shape[1]
    w_in = p['w_in']
    tab_diff = _rope_tables(s, DIFF_QK_DIM)
    tab_dil = _rope_tables(s, HEAD_DIM)
    proj = _proj_branches(h.reshape(b, s, d), w_in, l, tab_diff, tab_dil, input_norm)
    if input_norm:
        h = proj.pop(0).reshape(t, d)
    qkv_na, qkv_diff, qkv_d0, qkv_d1, qkv_d2, u = proj
    y_a = _na_attention(qkv_na, _na_bias_table(p['na_rpb'][l]))
    y_b = _diff_attention(qkv_diff, p['diff_lam'][l], p['diff_subln_g'][l], lam_init)
    y_c = _pool(u, p['pool_w'][l], p['pool_scale'][l])
    dil = [_swa(qkv, s // r) for qkv, (_, r) in zip((qkv_d0, qkv_d1, qkv_d2), DIL_PATTERNS)]
    ys = [a.reshape(t, -1) for a in (y_a, y_b, y_c)]
    wg = lax.slice_in_dim(w_in[l], COL_GATE, COL_GATE + (len(ys) + 1) * d, axis=1).astype(BF16)
    router = _router_params(p['router_group_w'][l], p['router_group_b'][l], p['router_expert_w'][l],
                            p['router_expert_b'][l])
    return _merge(ys, dil, wg, p['b_gate'][l], p['w_branch'][l].astype(BF16), p['w_out'][l].astype(BF16), h,
                  p['ln1_g'][l], p['ln1_b'][l], router, alpha)


def kernel(x, emb_ln_g, emb_ln_b, w_in, b_gate, na_rpb, diff_lam, diff_subln_g, pool_w, pool_scale,
           w_branch, w_out, ln1_g, ln1_b, router_group_w, router_group_b, router_expert_w,
           router_expert_b, expert_w_gate, expert_w_up, expert_w_down, ln2_g, ln2_b):
    b, s, d = x.shape
    depth = w_in.shape[0]
    alpha = (2 * depth) ** 0.25
    p = dict(w_in=w_in, b_gate=b_gate, na_rpb=na_rpb, diff_lam=diff_lam, diff_subln_g=diff_subln_g,
             pool_w=pool_w, pool_scale=pool_scale, w_branch=w_branch, w_out=w_out, ln1_g=ln1_g, ln1_b=ln1_b,
             router_group_w=router_group_w, router_group_b=router_group_b,
             router_expert_w=router_expert_w, router_expert_b=router_expert_b)
    h = x.reshape(b * s, d)
    for l in range(depth):
        lam_init = 0.8 - 0.6 * math.exp(-0.3 * l)
        h, h_packed, routing, counts = _mixing_layer(h, b, s, l, p, lam_init,
                                                     (emb_ln_g, emb_ln_b) if l == 0 else None)
        h = _moe(h, h_packed, routing, counts, expert_w_gate, expert_w_up, expert_w_down, l,
                 ln2_g[l], ln2_b[l], alpha)
    return h.reshape(b, s, d)
```

```python
import functools
import math

import jax
import jax.numpy as jnp
import numpy as np
from jax import lax
from jax.experimental import pallas as pl
from jax.experimental.pallas import tpu as pltpu
from jax.experimental.pallas import tpu_sc as plsc

F32 = jnp.float32
BF16 = jnp.bfloat16

LANES = 128
GRID_W = 64
HEAD_DIM = 64
ROPE_THETA = 500000.0
LN_EPS = 1e-5
NA_ROWS = 8
NA_COLS = 16
DIFF_QK_DIM = 32
POOL_WINDOWS = (2, 4, 8, 16)
POOL_GROUP = 64
DIL_PATTERNS = ((128, 1), (512, 4), (2048, 16))
DIL_HALF = 64
N_GROUPS = 4
EXPERTS_PER_GROUP = 8
N_EXPERTS = N_GROUPS * EXPERTS_PER_GROUP
TOP_K = 2
MOE_ROWS = 512
NEG = -1e30

COL_NA = 0
COL_DIFF = 768
COL_POOL = 1536
COL_DIL = 1792
COL_GATE = 4096

V7X_VMEM_BYTES = 64 * 1024 * 1024
VMEM_LIMIT = V7X_VMEM_BYTES - 8 * 1024 * 1024


def _cparams(sem):
    return pltpu.CompilerParams(dimension_semantics=sem, vmem_limit_bytes=VMEM_LIMIT)


def _layer_norm(x, g, b):
    mu = jnp.mean(x, axis=-1, keepdims=True)
    xc = x - mu
    var = jnp.mean(xc * xc, axis=-1, keepdims=True)
    return xc * lax.rsqrt(var + LN_EPS) * g + b


def _sigmoid(x):
    return 0.5 * jnp.tanh(0.5 * x) + 0.5


def _dot(a, b):
    return jnp.dot(a, b, preferred_element_type=F32)


def _dot_nt(a, b):
    return lax.dot_general(a, b, (((1,), (1,)), ((), ())), preferred_element_type=F32)


def _lane(shape):
    return lax.broadcasted_iota(jnp.int32, shape, len(shape) - 1)


HI16 = -65536


def _pack_bf16_pairs(x):
    w = x.shape[1] // 2
    hi = lax.bitcast_convert_type(x[:, :w].astype(BF16).astype(F32), jnp.int32)
    lo = lax.bitcast_convert_type(x[:, w:].astype(BF16).astype(F32), jnp.int32)
    return (hi & HI16) | lax.shift_right_logical(lo, 16)


def _unpack_bf16_pairs(p):
    hi = lax.bitcast_convert_type(p & HI16, F32)
    lo = lax.bitcast_convert_type(lax.shift_left(p, 16), F32)
    return jnp.concatenate([hi, lo], axis=1)


QKV_W = 6 * LANES
WCOL = 256


QKV_TM = 512
QKV_SEGMENTS = ((COL_NA, 1, 0, HEAD_DIM), (COL_DIFF, 1, DIFF_QK_DIM // 8, DIFF_QK_DIM),
                (COL_DIL, 1, HEAD_DIM // 8, HEAD_DIM), (COL_DIL + QKV_W, 4, HEAD_DIM // 8, HEAD_DIM),
                (COL_DIL + 2 * QKV_W, 16, HEAD_DIM // 8, HEAD_DIM))


def _qkv_epilogue(z, t_ref, o_ref, zs_ref, r, half, q_scale):
    tm = z.shape[0]
    n = tm // r
    for c in range(QKV_W // LANES):
        blk = z[:, c * LANES:(c + 1) * LANES]
        if half and c < 4:
            blk = _rope(blk, t_ref, half)
        if c < 2:
            blk = blk * q_scale
        if r == 1:
            o_ref[0, 0, :, c * LANES:(c + 1) * LANES] = blk.astype(BF16)
        else:
            zs_ref[c] = blk
            for m in range(r):
                o_ref[0, m, :, c * LANES:(c + 1) * LANES] = zs_ref[
                    c, pl.ds(m, n, stride=r), :].astype(BF16)


def _proj_kernel(*refs, norm_input):
    if norm_input:
        x_ref, g_ref, b_ref, w_ref, td_ref, tl_ref, h_ref = refs[:7]
        refs = refs[7:]
    else:
        x_ref, w_ref, td_ref, tl_ref = refs[:4]
        refs = refs[4:]
    na_ref, df_ref, d0_ref, d1_ref, d2_ref, u_ref, wb_ref, zs1_ref, zs2_ref = refs

    @pl.when((pl.program_id(0) == 0) & (pl.program_id(1) == 0))
    def _():
        wb_ref[...] = w_ref[...].astype(BF16)

    x = x_ref[0]
    if norm_input:
        x = _layer_norm(x, g_ref[...], b_ref[...])
        h_ref[0] = x
    xb = x.astype(BF16)
    outs = (na_ref, df_ref, d0_ref, d1_ref, d2_ref)
    stage = (None, None, None, zs1_ref, zs2_ref)
    for (col0, r, half, head_w), o_ref, zs_ref in zip(QKV_SEGMENTS, outs, stage):
        z = _dot(xb, wb_ref[:, col0:col0 + QKV_W])
        t_ref = td_ref if head_w == DIFF_QK_DIM else tl_ref
        _qkv_epilogue(z, t_ref, o_ref, zs_ref, r, half, head_w ** -0.5 * LOG2E)
    u_ref[0] = _dot(xb, wb_ref[:, COL_POOL:COL_POOL + WCOL])


def _proj_branches(x3, w_in, l, tab_diff, tab_dil, norm=None):
    b, s, d = x3.shape
    tm = QKV_TM
    qkv_spec = lambda r: pl.BlockSpec((1, r, tm // r, QKV_W), lambda bi, i: (bi, 0, i, 0))
    qkv_shape = lambda r: jax.ShapeDtypeStruct((b, r, s // r, QKV_W), BF16)
    tab_spec = pl.BlockSpec((3, tm, LANES), lambda bi, i: (0, i, 0))
    row_spec = pl.BlockSpec((1, tm, d), lambda bi, i: (bi, i, 0))
    vec = pl.BlockSpec((1, d), lambda bi, i: (0, 0))
    w_spec = pl.BlockSpec((None, d, COL_GATE), lambda bi, i: (l, 0, 0), pipeline_mode=pl.Buffered(1))
    rs = [seg[1] for seg in QKV_SEGMENTS]
    in_specs = [row_spec] + ([vec, vec] if norm else []) + [w_spec, tab_spec, tab_spec]
    args = [x3] + ([a.reshape(1, d) for a in norm] if norm else []) + [w_in, tab_diff, tab_dil]
    outs = pl.pallas_call(
        functools.partial(_proj_kernel, norm_input=bool(norm)),
        grid=(b, s // tm),
        in_specs=in_specs,
        out_specs=([row_spec] if norm else []) + [qkv_spec(r) for r in rs] + [
            pl.BlockSpec((1, tm, WCOL), lambda bi, i: (bi, i, 0))],
        out_shape=([jax.ShapeDtypeStruct((b, s, d), F32)] if norm else []) + [
            qkv_shape(r) for r in rs] + [jax.ShapeDtypeStruct((b, s, WCOL), F32)],
        scratch_shapes=[pltpu.VMEM((d, COL_GATE), BF16)] + [pltpu.VMEM((QKV_W // LANES, tm, LANES), F32)] * 2,
        compiler_params=_cparams(("arbitrary", "arbitrary")),
        name="proj_branches",
    )(*args)
    n_qkv = len(rs)
    head = list(outs[:-n_qkv - 1])
    return head + [o.reshape(b, s, QKV_W) for o in outs[-n_qkv - 1:-1]] + [outs[-1]]


def _rope_tables(seq, head_w):
    rot = head_w // 4
    half = rot // 2
    inv_freq = jnp.exp(jnp.arange(half, dtype=F32) * (-2.0 * math.log(ROPE_THETA) / rot))
    ang = jnp.arange(seq, dtype=jnp.int32).astype(F32)[:, None] * inv_freq[None, :]
    cos, sin = jnp.cos(ang), jnp.sin(ang)
    zero = jnp.zeros((seq, head_w - rot), F32)
    zh = jnp.zeros((seq, half), F32)
    t0 = jnp.concatenate([cos, cos, jnp.ones((seq, head_w - rot), F32)], axis=1)
    t1 = jnp.concatenate([-sin, zh, zero], axis=1)
    t2 = jnp.concatenate([zh, sin, zero], axis=1)
    reps = LANES // head_w
    return jnp.stack([jnp.tile(t0, (1, reps)), jnp.tile(t1, (1, reps)), jnp.tile(t2, (1, reps))])


def _rope(x, t_ref, half):
    return (x * t_ref[0] + pltpu.roll(x, LANES - half, 1) * t_ref[1]
            + pltpu.roll(x, half, 1) * t_ref[2])


LOG2E = math.log2(math.e)


def _na_bias_table(rpb):
    kr, kc = NA_ROWS, NA_COLS
    n_heads = rpb.shape[0]
    col = np.arange(GRID_W)
    col_start = np.clip(col - kc // 2, 0, GRID_W - kc)
    in_win = (col[None, :] >= col_start[:, None]) & (col[None, :] < col_start[:, None] + kc)
    dc = np.clip(col[None, :] - col[:, None] + kc - 1, 0, 2 * kc - 2)
    sel_c = ((dc[..., None] == np.arange(2 * kc - 1)) & in_win[..., None]).astype(np.float32)
    rp = rpb.astype(F32).reshape(n_heads // 2, 2, 2 * kr - 1, 2 * kc - 1) * LOG2E
    tiles = jnp.einsum('phaj,qxj->phaqx', rp, jnp.asarray(sel_c), precision=lax.Precision.HIGHEST)
    tiles = tiles + jnp.asarray(np.where(in_win, 0.0, NEG).astype(np.float32))
    tiles = jnp.concatenate([tiles, jnp.full_like(tiles[:, :, :1], NEG)], axis=2)
    return jnp.concatenate([tiles, tiles], axis=-1)


NA_QROWS = 4
NA_KROWS = 12
NA_MASKED = 2 * NA_ROWS - 1


def _na_row_offsets(n_rows):
    table = []
    for blk in (0, 1, n_rows // NA_QROWS - 1):
        ws = _na_window_start(blk, n_rows)
        rows = []
        for i in range(NA_QROWS):
            r = blk * NA_QROWS + i
            start = min(max(r - NA_ROWS // 2, 0), n_rows - NA_ROWS)
            rows.append([ws + kk - r + NA_ROWS - 1 if start <= ws + kk < start + NA_ROWS else NA_MASKED
                         for kk in range(NA_KROWS)])
        table.append(rows)
    return table


def _na_window_start(blk, n_rows):
    lo = blk * NA_QROWS - NA_ROWS // 2
    hi = n_rows - NA_KROWS
    if isinstance(blk, int):
        return min(max(lo, 0), hi)
    return jnp.clip(lo, 0, hi)


def _stack_heads(q, lane):
    zero = jnp.zeros_like(q)
    return jnp.concatenate([jnp.where(lane < HEAD_DIM, q, zero), jnp.where(lane < HEAD_DIM, zero, q)],
                           axis=0)


def _na_kernel(q_ref, k_ref, v_ref, tile_ref, o_ref, bias_ref, *, blocks_per_step, n_rows):
    step = pl.program_id(2)
    nq = NA_QROWS * GRID_W
    kwin = NA_KROWS * GRID_W
    lane = _lane((nq, LANES))
    last_blk = n_rows // NA_QROWS - 1

    @pl.when(step == 0)
    def _():
        low = _lane((GRID_W, LANES)) < GRID_W
        for case, rows in enumerate(_na_row_offsets(n_rows)):
            for hh in range(2):
                for i, offs in enumerate(rows):
                    r0 = (hh * NA_QROWS + i) * GRID_W
                    for kp in range(NA_KROWS // 2):
                        bias_ref[case, r0:r0 + GRID_W, kp * LANES:(kp + 1) * LANES] = jnp.where(
                            low, tile_ref[0, hh, offs[2 * kp]], tile_ref[0, hh, offs[2 * kp + 1]])

    for i in range(blocks_per_step):
        blk = step * blocks_per_step + i
        case = jnp.where(blk == 0, 0, jnp.where(blk == last_blk, 2, 1))
        k0 = pl.multiple_of(_na_window_start(blk, n_rows) * GRID_W, GRID_W)
        q2 = _stack_heads(q_ref[0, i * nq:(i + 1) * nq, :], lane)
        sc = _dot_nt(q2, k_ref[0, pl.ds(k0, kwin), :]) + bias_ref[case]
        m = jnp.max(sc, axis=-1, keepdims=True)
        e = jnp.exp2(sc - m)
        den = jnp.sum(e, axis=-1, keepdims=True)
        pv = _dot(e.astype(BF16), v_ref[0, pl.ds(k0, kwin), :]) / den
        o = jnp.where(lane < HEAD_DIM, pv[0:nq], pv[nq:2 * nq])
        o_ref[0, i * nq:(i + 1) * nq, :] = o.astype(BF16)


def _na_attention(qkv, tiles):
    b, s, _ = qkv.shape
    n_rows = s // GRID_W
    bps = 4
    rps = bps * NA_QROWS
    tq = rps * GRID_W
    return pl.pallas_call(
        functools.partial(_na_kernel, blocks_per_step=bps, n_rows=n_rows),
        grid=(b, 2, n_rows // rps),
        in_specs=[pl.BlockSpec((1, tq, LANES), lambda bi, hp, i: (bi, i, hp)),
                  pl.BlockSpec((1, s, LANES), lambda bi, hp, i: (bi, 0, 2 + hp)),
                  pl.BlockSpec((1, s, LANES), lambda bi, hp, i: (bi, 0, 4 + hp)),
                  pl.BlockSpec((1, 2, NA_MASKED + 1, GRID_W, LANES), lambda bi, hp, i: (hp, 0, 0, 0, 0))],
        out_specs=pl.BlockSpec((1, tq, LANES), lambda bi, hp, i: (bi, i, hp)),
        out_shape=jax.ShapeDtypeStruct((b, s, 2 * LANES), BF16),
        scratch_shapes=[pltpu.VMEM((3, 2 * NA_QROWS * GRID_W, NA_KROWS * GRID_W), F32)],
        compiler_params=_cparams(("parallel", "parallel", "arbitrary")),
        name="na_attn",
    )(qkv, qkv, qkv, tiles)


def _diff_kernel(lam_ref, q_ref, k_ref, v_ref, g_ref, o_ref, *, lam_init, rows):
    tq = q_ref.shape[1]
    lane = _lane((rows, LANES))
    dl = lam_ref[...]
    lam = (jnp.exp(jnp.sum(dl[0:1] * dl[1:2], axis=-1, keepdims=True))
           - jnp.exp(jnp.sum(dl[2:3] * dl[3:4], axis=-1, keepdims=True)) + lam_init)
    v = v_ref[0]
    v_lane = _lane(v.shape)
    one = jnp.ones_like(v)
    v_ext = (jnp.where(v_lane < HEAD_DIM, v, one), jnp.where(v_lane < HEAD_DIM, one, v))
    in_h0 = lane < HEAD_DIM
    for r0 in range(0, tq, rows):
        q = q_ref[0, r0:r0 + rows, :]
        zero = jnp.zeros_like(q)
        pv = []
        for lo in range(0, LANES, DIFF_QK_DIM):
            qm = jnp.where((lane >= lo) & (lane < lo + DIFF_QK_DIM), q, zero)
            sc = _dot_nt(qm, k_ref[0])
            e = jnp.exp2(sc - jnp.max(sc, axis=-1, keepdims=True))
            ev = _dot(e.astype(BF16), v_ext[lo // HEAD_DIM])
            pv.append(ev / pltpu.roll(ev, HEAD_DIM, 1))
        o = jnp.where(in_h0, pv[0] - lam * pv[1], pv[2] - lam * pv[3])
        o2 = o * o
        ms0 = jnp.sum(jnp.where(in_h0, o2, 0.0), axis=-1, keepdims=True) / HEAD_DIM
        ms1 = jnp.sum(jnp.where(in_h0, 0.0, o2), axis=-1, keepdims=True) / HEAD_DIM
        ms = jnp.where(in_h0, ms0, ms1)
        o = o * lax.rsqrt(ms + LN_EPS) * g_ref[...] * (1.0 - lam_init)
        o_ref[0, r0:r0 + rows, :] = o.astype(BF16)


def _diff_attention(qkv, diff_lam, subln_g, lam_init):
    b, s, _ = qkv.shape
    tq = 1024
    g2 = jnp.tile(subln_g.reshape(1, HEAD_DIM), (1, 2))
    return pl.pallas_call(
        functools.partial(_diff_kernel, lam_init=lam_init, rows=512),
        grid=(b, 2, s // tq),
        in_specs=[pl.BlockSpec((4, DIFF_QK_DIM), lambda bi, hp, i: (0, 0)),
                  pl.BlockSpec((1, tq, LANES), lambda bi, hp, i: (bi, i, hp)),
                  pl.BlockSpec((1, s, LANES), lambda bi, hp, i: (bi, 0, 2 + hp)),
                  pl.BlockSpec((1, s, LANES), lambda bi, hp, i: (bi, 0, 4 + hp)),
                  pl.BlockSpec((1, LANES), lambda bi, hp, i: (0, 0))],
        out_specs=pl.BlockSpec((1, tq, LANES), lambda bi, hp, i: (bi, i, hp)),
        out_shape=jax.ShapeDtypeStruct((b, s, 2 * LANES), BF16),
        compiler_params=_cparams(("parallel", "parallel", "arbitrary")),
        name="diff_attn",
    )(diff_lam, qkv, qkv, qkv, g2)


POOL_PAD = 16
POOL_CHUNK = 512
POOL_HALO = 8


def _pool_kernel(u_ref, w_ref, sc_ref, o_ref, p_ref):
    s = u_ref.shape[1]
    width = u_ref.shape[2]
    p_ref[0:POOL_PAD, :] = jnp.zeros((POOL_PAD, width), F32)
    p_ref[POOL_PAD + s:POOL_PAD + s + POOL_PAD, :] = jnp.zeros((POOL_PAD, width), F32)
    p_ref[POOL_PAD:POOL_PAD + s, :] = u_ref[0]
    n = POOL_CHUNK + 2 * POOL_HALO
    lane = _lane((POOL_CHUNK, width))
    row = lax.broadcasted_iota(jnp.int32, (POOL_CHUNK, width), 0)
    w_of_lane = jnp.where(lane < POOL_GROUP, 2, jnp.where(lane < 2 * POOL_GROUP, 4,
                          jnp.where(lane < 3 * POOL_GROUP, 8, 16)))

    def body(ci, carry):
        c0 = pl.multiple_of(ci * POOL_CHUNK, POOL_CHUNK)
        x = p_ref[pl.ds(c0 + POOL_PAD - POOL_HALO, n), :]
        w2 = x + pltpu.roll(x, 1, 0)
        w4 = pltpu.roll(w2, 1, 0) + pltpu.roll(w2, n - 1, 0)
        w8 = pltpu.roll(w4, 2, 0) + pltpu.roll(w4, n - 2, 0)
        w16 = pltpu.roll(w8, 4, 0) + pltpu.roll(w8, n - 4, 0)
        u = x[POOL_HALO:POOL_HALO + POOL_CHUNK]
        wsum = jnp.where(lane < POOL_GROUP, w2[POOL_HALO:POOL_HALO + POOL_CHUNK],
                         jnp.where(lane < 2 * POOL_GROUP, w4[POOL_HALO:POOL_HALO + POOL_CHUNK],
                                   jnp.where(lane < 3 * POOL_GROUP, w8[POOL_HALO:POOL_HALO + POOL_CHUNK],
                                             w16[POOL_HALO:POOL_HALO + POOL_CHUNK])))
        t = row + c0
        half_w = w_of_lane // 2
        lo = jnp.maximum(t - half_w, 0)
        hi = jnp.minimum(t + w_of_lane - 1 - half_w, s - 1)
        cnt = (hi - lo + 1).astype(F32)
        dlt = wsum / cnt - u
        y = _dot(dlt.astype(BF16), w_ref[...]) * sc_ref[...]
        o_ref[0, pl.ds(c0, POOL_CHUNK), :] = y.astype(BF16)
        return carry

    lax.fori_loop(0, s // POOL_CHUNK, body, 0)


def _pool(u3, pool_w, pool_scale):
    b, s, width = u3.shape
    wbd = jax.scipy.linalg.block_diag(*[pool_w[g] for g in range(len(POOL_WINDOWS))]).astype(BF16)
    return pl.pallas_call(
        _pool_kernel,
        grid=(b,),
        in_specs=[pl.BlockSpec((1, s, width), lambda bi: (bi, 0, 0)),
                  pl.BlockSpec((width, width), lambda bi: (0, 0)),
                  pl.BlockSpec((1, width), lambda bi: (0, 0))],
        out_specs=pl.BlockSpec((1, s, width), lambda bi: (bi, 0, 0)),
        out_shape=jax.ShapeDtypeStruct((b, s, width), BF16),
        scratch_shapes=[pltpu.VMEM((s + 2 * POOL_PAD, width), F32)],
        compiler_params=_cparams(("parallel",)),
        name="pool",
    )(u3, wbd, pool_scale.reshape(1, width))


SWA_Q = 128
SWA_BAND = SWA_Q + 2 * DIL_HALF


def _swa_kernel(q_ref, k_ref, v_ref, o_ref, l_ref, *, length):
    lane = _lane((SWA_Q, LANES))
    rel = (lax.broadcasted_iota(jnp.int32, (2 * SWA_Q, SWA_BAND), 1)
           - (lax.broadcasted_iota(jnp.int32, (2 * SWA_Q, SWA_BAND), 0) & (SWA_Q - 1)))
    masks = {}
    for i in range(q_ref.shape[1] // SWA_Q):
        r0 = i * SWA_Q
        l0 = r0 % length
        lo = r0 - l0 + min(max(l0 - DIL_HALF, 0), length - SWA_BAND)
        if lo - r0 not in masks:
            d = rel + (lo - r0)
            masks[lo - r0] = jnp.where((d >= -DIL_HALF) & (d <= DIL_HALF), 0.0, NEG)
        q2 = _stack_heads(q_ref[0, r0:r0 + SWA_Q, :], lane)
        sc = _dot_nt(q2, k_ref[0, lo:lo + SWA_BAND, :]) + masks[lo - r0]
        m = jnp.max(sc, axis=-1, keepdims=True)
        e = jnp.exp2(sc - m)
        den = jnp.sum(e, axis=-1, keepdims=True)
        pv = _dot(e.astype(BF16), v_ref[0, lo:lo + SWA_BAND, :]) / den
        lse2 = m + jnp.log2(den)
        o_ref[0, i * SWA_Q:(i + 1) * SWA_Q, :] = jnp.where(lane < HEAD_DIM, pv[0:SWA_Q], pv[SWA_Q:])
        l_ref[0, i * SWA_Q:(i + 1) * SWA_Q, :] = jnp.where(lane < HEAD_DIM, lse2[0:SWA_Q], lse2[SWA_Q:])


def _swa(qkv, length):
    b, s, _ = qkv.shape
    shp = jax.ShapeDtypeStruct((b, s, 2 * LANES), F32)
    seq = lambda col: pl.BlockSpec((1, s, LANES), lambda bi, hp: (bi, 0, col + hp))
    return pl.pallas_call(
        functools.partial(_swa_kernel, length=length),
        grid=(b, 2),
        in_specs=[seq(0), seq(2), seq(4)],
        out_specs=[seq(0), seq(0)],
        out_shape=[shp, shp],
        compiler_params=_cparams(("parallel", "parallel")),
        name=f"swa_l{length}",
    )(qkv, qkv, qkv)


def _dil_combine(dil_refs, stage_refs, tm):
    (o0_ref, l0_ref), (o1_ref, l1_ref), (o2_ref, l2_ref) = dil_refs
    so1, sl1, so2, sl2 = stage_refs
    r1 = DIL_PATTERNS[1][1]
    r2 = DIL_PATTERNS[2][1]
    halves = []
    for hp in range(2):
        cols = slice(hp * LANES, (hp + 1) * LANES)
        for m in range(r1):
            so1[pl.ds(m, tm // r1, stride=r1), :] = o1_ref[0, m, :, cols]
            sl1[pl.ds(m, tm // r1, stride=r1), :] = l1_ref[0, m, :, cols]
        for m in range(r2):
            so2[pl.ds(m, tm // r2, stride=r2), :] = o2_ref[0, m, :, cols]
            sl2[pl.ds(m, tm // r2, stride=r2), :] = l2_ref[0, m, :, cols]
        l0, l1, l2 = l0_ref[0, 0, :, cols], sl1[...], sl2[...]
        mx = jnp.maximum(jnp.maximum(l0, l1), l2)
        e0, e1, e2 = jnp.exp2(l0 - mx), jnp.exp2(l1 - mx), jnp.exp2(l2 - mx)
        den = e0 + e1 + e2
        halves.append((e0 / den) * o0_ref[0, 0, :, cols] + (e1 / den) * so1[...] + (e2 / den) * so2[...])
    return jnp.concatenate(halves, axis=1)


def _merge_kernel(ya_ref, yb_ref, yc_ref, o0_ref, l0_ref, o1_ref, l1_ref, o2_ref, l2_ref,
                  wg_ref, bg_ref, wb_ref, wo_ref, h_ref, g_ref, b_ref, wr_ref, br_ref,
                  o_ref, op_ref, r_ref, cnt_ref, so1, sl1, so2, sl2, carry, before, *, alpha):
    tm, d = h_ref.shape

    @pl.when(pl.program_id(0) == 0)
    def _():
        _route_init(carry, before)

    hb = h_ref[...].astype(BF16)
    y_d = _dil_combine(((o0_ref, l0_ref), (o1_ref, l1_ref), (o2_ref, l2_ref)),
                       (so1, sl1, so2, sl2), tm).astype(BF16)
    merged = None
    for n, y in enumerate((ya_ref[...], yb_ref[...], yc_ref[...], y_d)):
        zg = _dot(hb, wg_ref[:, n * d:(n + 1) * d]) + bg_ref[:, n * d:(n + 1) * d]
        term = _sigmoid(zg) * _dot(y, wb_ref[n])
        merged = term if merged is None else merged + term
    mix = _dot(merged.astype(BF16), wo_ref[...])
    h1 = _layer_norm(alpha * h_ref[...] + mix, g_ref[...], b_ref[...])
    o_ref[...] = h1
    op_ref[...] = _pack_bf16_pairs(h1)
    _route_finish(_route_select(h1, wr_ref, br_ref), r_ref, cnt_ref, carry, before)


def _merge(ys, dil, wg, b_gate, wb, wo, h2, g, bb, router, alpha):
    t, d = h2.shape
    w_route, b_route = router
    b, s, bw = dil[0][0].shape
    nb = len(ys) + 1
    tm = QKV_TM
    nt = s // tm
    yspec = pl.BlockSpec((tm, bw), lambda i: (i, 0))
    vec = pl.BlockSpec((1, d), lambda i: (0, 0))
    dil_args, dil_specs = [], []
    for (o, lse), (_, r) in zip(dil, DIL_PATTERNS):
        spec = pl.BlockSpec((1, r, tm // r, bw), lambda i: (i // nt, 0, i % nt, 0))
        dil_args += [o.reshape(b, r, s // r, bw), lse.reshape(b, r, s // r, bw)]
        dil_specs += [spec, spec]
    return pl.pallas_call(
        functools.partial(_merge_kernel, alpha=alpha),
        grid=(t // tm,),
        in_specs=[yspec, yspec, yspec] + dil_specs + [
                  pl.BlockSpec((d, nb * d), lambda i: (0, 0)),
                  pl.BlockSpec((1, nb * d), lambda i: (0, 0)),
                  pl.BlockSpec((nb, bw, d), lambda i: (0, 0, 0)),
                  pl.BlockSpec((d, d), lambda i: (0, 0)),
                  pl.BlockSpec((tm, d), lambda i: (i, 0)),
                  vec, vec,
                  pl.BlockSpec((LANES, d), lambda i: (0, 0)),
                  pl.BlockSpec((LANES, 1), lambda i: (0, 0))],
        out_specs=[pl.BlockSpec((tm, d), lambda i: (i, 0)),
                   pl.BlockSpec((tm, d // 2), lambda i: (i, 0)),
                   pl.BlockSpec((tm, LANES), lambda i: (i, 0)),
                   pl.BlockSpec((LANES, 1), lambda i: (0, 0))],
        out_shape=[jax.ShapeDtypeStruct((t, d), F32), jax.ShapeDtypeStruct((t, d // 2), jnp.int32),
                   jax.ShapeDtypeStruct((t, LANES), F32), jax.ShapeDtypeStruct((LANES, 1), F32)],
        scratch_shapes=[pltpu.VMEM((tm, LANES), F32)] * 4 + [pltpu.VMEM((LANES, 1), F32),
                                                               pltpu.VMEM((tm, tm), BF16)],
        compiler_params=_cparams(("arbitrary",)),
        name="merge",
    )(*ys, *dil_args, wg, b_gate.reshape(1, nb * d), wb, wo, h2, g.reshape(1, d), bb.reshape(1, d),
      w_route, b_route)


def _route_init(carry, before):
    tm = before.shape[0]
    carry[...] = jnp.zeros_like(carry)
    before[...] = (lax.broadcasted_iota(jnp.int32, (tm, tm), 0)
                   < lax.broadcasted_iota(jnp.int32, (tm, tm), 1)).astype(BF16)


def _route_select(h, w_ref, b_ref):
    tm = h.shape[0]
    logits = lax.dot_general(w_ref[...], h, (((1,), (1,)), ((), ())),
                             preferred_element_type=F32,
                             precision=lax.Precision.HIGHEST) + b_ref[...]
    row = lax.broadcasted_iota(jnp.int32, (LANES, tm), 0)
    big = jnp.int32(1 << 20)
    is_g = row < N_GROUPS
    gl = jnp.where(is_g, logits, -jnp.inf)
    gmax = jnp.max(gl, axis=0, keepdims=True)
    gsel = jnp.min(jnp.where(is_g & (gl == gmax), row, big), axis=0, keepdims=True)
    pg = 1.0 / jnp.sum(jnp.exp(gl - gmax), axis=0, keepdims=True)
    e_lo = N_GROUPS + gsel * EXPERTS_PER_GROUP
    in_grp = (row >= e_lo) & (row < e_lo + EXPERTS_PER_GROUP)
    el = jnp.where(in_grp, logits, -jnp.inf)
    v1 = jnp.max(el, axis=0, keepdims=True)
    i1 = jnp.min(jnp.where(in_grp & (el == v1), row, big), axis=0, keepdims=True)
    el2 = jnp.where(row == i1, -jnp.inf, el)
    v2 = jnp.max(el2, axis=0, keepdims=True)
    i2 = jnp.min(jnp.where(in_grp & (row != i1) & (el2 == v2), row, big), axis=0, keepdims=True)
    t2 = jnp.exp(v2 - v1)
    g1 = pg / (1.0 + t2)
    g2 = pg * t2 / (1.0 + t2)
    return dict(row=row, i1=i1, i2=i2, g1=g1, g2=g2,
                oh1=row == i1, oh2=row == i2)


def _route_finish(sel, o_ref, cnt_ref, carry, before):
    row, i1, i2, g1, g2, oh1, oh2 = (sel[k] for k in ('row', 'i1', 'i2', 'g1', 'g2', 'oh1', 'oh2'))
    oh1b = oh1.astype(BF16)
    oh2b = oh2.astype(BF16)
    c0 = carry[...]
    c1 = c0 + jnp.sum(oh1b.astype(F32), axis=1, keepdims=True)
    rank1 = jnp.sum(jnp.where(oh1, _dot(oh1b, before[...]) + c0, 0.0), axis=0, keepdims=True)
    rank2 = jnp.sum(jnp.where(oh2, _dot(oh2b, before[...]) + c1, 0.0), axis=0, keepdims=True)
    c2 = c1 + jnp.sum(oh2b.astype(F32), axis=1, keepdims=True)
    carry[...] = c2
    cnt_ref[...] = c2
    e1 = (i1 - N_GROUPS).astype(F32)
    e2 = (i2 - N_GROUPS).astype(F32)
    out_t = jnp.where(row == 0, e1, jnp.where(row == 1, e2, jnp.where(row == 2, g1, jnp.where(
        row == 3, g2, jnp.where(row == 4, rank1, jnp.where(row == 5, rank2, 0.0))))))
    o_ref[...] = out_t.T


def _router_params(rg_w, rg_b, re_w, re_b):
    d = rg_w.shape[0]
    pad = LANES - N_GROUPS - N_EXPERTS
    w_t = jnp.concatenate([rg_w.T, re_w.T, jnp.zeros((pad, d), F32)], axis=0)
    bias = jnp.concatenate([rg_b, re_b, jnp.zeros((pad,), F32)]).reshape(LANES, 1)
    return w_t, bias


MOE_FIRST, MOE_SLOT, MOE_HAS_NEXT = 1, 2, 4


def _moe_kernel(be_ref, nv_ref, fl_ref, nx_ref, x_ref, wg_hbm, wu_hbm, wd_hbm, o_ref,
                wgb, wub, wdb, stage_g, stage_u, stage_d, sem, *, layer):
    j = pl.program_id(0)
    n_valid = nv_ref[j]
    flags = fl_ref[j]
    slot = (flags // MOE_SLOT) % 2

    def weight_copies(e, s):
        return [pltpu.make_async_copy(w_hbm.at[layer, e], stage.at[s], sem.at[s, k])
                for k, (w_hbm, stage) in enumerate(((wg_hbm, stage_g), (wu_hbm, stage_u),
                                                    (wd_hbm, stage_d)))]

    @pl.when(flags % 2 == MOE_FIRST)
    def _():
        @pl.when(j == 0)
        def _():
            for cp in weight_copies(be_ref[j], slot):
                cp.start()

        for cp in weight_copies(be_ref[j], slot):
            cp.wait()

        @pl.when((flags // MOE_HAS_NEXT) % 2 == 1)
        def _():
            for cp in weight_copies(nx_ref[j], 1 - slot):
                cp.start()

        wgb[...] = stage_g[slot].astype(BF16)
        wub[...] = stage_u[slot].astype(BF16)
        wdb[...] = stage_d[slot].astype(BF16)

    @pl.when(n_valid > 0)
    def _():
        row = lax.broadcasted_iota(jnp.int32, x_ref.shape, 0)
        x = _unpack_bf16_pairs(jnp.where(row < n_valid, x_ref[...], 0)).astype(BF16)
        g = _dot(x, wgb[...])
        u = _dot(x, wub[...])
        hmid = (g * _sigmoid(g)) * u
        o_ref[...] = _pack_bf16_pairs(_dot(hmid.astype(BF16), wdb[...]))

    @pl.when(n_valid <= 0)
    def _():
        o_ref[...] = jnp.zeros_like(o_ref)


def _moe_experts(xs, block_e, n_valid, flags, next_e, wg, wu, wd, l):
    cap, dp = xs.shape
    d = 2 * dp
    de = wg.shape[3]
    n_blocks = cap // MOE_ROWS
    hbm = pl.BlockSpec(memory_space=pl.ANY)
    grid_spec = pltpu.PrefetchScalarGridSpec(
        num_scalar_prefetch=4,
        grid=(n_blocks,),
        in_specs=[pl.BlockSpec((MOE_ROWS, dp), lambda j, *_: (j, 0)), hbm, hbm, hbm],
        out_specs=pl.BlockSpec((MOE_ROWS, dp), lambda j, *_: (j, 0)),
        scratch_shapes=[pltpu.VMEM((d, de), BF16), pltpu.VMEM((d, de), BF16), pltpu.VMEM((de, d), BF16),
                        pltpu.VMEM((2, d, de), F32), pltpu.VMEM((2, d, de), F32),
                        pltpu.VMEM((2, de, d), F32), pltpu.SemaphoreType.DMA((2, 3))],
    )
    return pl.pallas_call(
        functools.partial(_moe_kernel, layer=l),
        grid_spec=grid_spec,
        out_shape=jax.ShapeDtypeStruct((cap, dp), jnp.int32),
        compiler_params=_cparams(("arbitrary",)),
        name="moe_experts",
    )(block_e, n_valid, flags, next_e, xs, wg, wu, wd)


SC_CORES = 2
SC_SUBCORES = 16
SC_WORKERS = SC_CORES * SC_SUBCORES
SC_CHUNK = 64


def _sc_mesh():
    return plsc.VectorSubcoreMesh(core_axis_name="c", subcore_axis_name="s",
                                  num_cores=SC_CORES, num_subcores=SC_SUBCORES)


def _sc_scatter_rows(src, idx0, idx1, n_out):
    t, d = src.shape
    per_w = t // SC_WORKERS
    n_chunks = per_w // SC_CHUNK
    idx_shape = (SC_WORKERS, n_chunks, SC_CHUNK)

    @functools.partial(
        pl.kernel, mesh=_sc_mesh(),
        out_type=jax.ShapeDtypeStruct((n_out, d), src.dtype),
        scratch_types=[pltpu.VMEM((n_chunks, SC_CHUNK), jnp.int32),
                       pltpu.VMEM((n_chunks, SC_CHUNK), jnp.int32),
                       pltpu.VMEM((2, SC_CHUNK, d), src.dtype),
                       pltpu.SemaphoreType.DMA((2,)), pltpu.SemaphoreType.DMA((2,))],
        name="sc_dispatch",
    )
    def k(src_hbm, i0_hbm, i1_hbm, out_hbm, i0_v, i1_v, rows_v, sem_in, sem_out):
        wid = lax.axis_index("s") * SC_CORES + lax.axis_index("c")
        base = wid * per_w
        pltpu.sync_copy(i0_hbm.at[wid], i0_v)
        pltpu.sync_copy(i1_hbm.at[wid], i1_v)

        def load(ci):
            s = ci % 2
            return pltpu.make_async_copy(src_hbm.at[pl.ds(base + ci * SC_CHUNK, SC_CHUNK)],
                                         rows_v.at[s], sem_in.at[s])

        def scatter(ci, idx_v):
            s = ci % 2
            return pltpu.make_async_copy(rows_v.at[s], out_hbm.at[idx_v.at[ci]], sem_out.at[s])

        load(0).start()
        for ci in range(n_chunks):
            load(ci).wait()
            if ci >= 1:
                scatter(ci - 1, i0_v).wait()
                scatter(ci - 1, i1_v).wait()
            if ci + 1 < n_chunks:
                load(ci + 1).start()
            scatter(ci, i0_v).start()
            scatter(ci, i1_v).start()
        scatter(n_chunks - 1, i0_v).wait()
        scatter(n_chunks - 1, i1_v).wait()

    return k(src, idx0.reshape(idx_shape), idx1.reshape(idx_shape))


def _sc_gather_rows(table, idx):
    n = idx.shape[0]
    d = table.shape[1]
    per_w = n // SC_WORKERS
    n_chunks = per_w // SC_CHUNK

    @functools.partial(
        pl.kernel, mesh=_sc_mesh(),
        out_type=jax.ShapeDtypeStruct((n, d), table.dtype),
        scratch_types=[pltpu.VMEM((n_chunks, SC_CHUNK), jnp.int32),
                       pltpu.VMEM((2, SC_CHUNK, d), table.dtype),
                       pltpu.SemaphoreType.DMA((2,)), pltpu.SemaphoreType.DMA((2,))],
        name="sc_collect",
    )
    def k(table_hbm, idx_hbm, out_hbm, idx_v, rows_v, sem_in, sem_out):
        wid = lax.axis_index("s") * SC_CORES + lax.axis_index("c")
        base = wid * per_w
        pltpu.sync_copy(idx_hbm.at[wid], idx_v)

        def gather(ci):
            s = ci % 2
            return pltpu.make_async_copy(table_hbm.at[idx_v.at[ci]], rows_v.at[s], sem_in.at[s])

        def store(ci):
            s = ci % 2
            return pltpu.make_async_copy(rows_v.at[s], out_hbm.at[pl.ds(base + ci * SC_CHUNK, SC_CHUNK)],
                                         sem_out.at[s])

        gather(0).start()
        for ci in range(n_chunks):
            gather(ci).wait()
            if ci >= 1:
                store(ci - 1).wait()
            if ci + 1 < n_chunks:
                gather(ci + 1).start()
            store(ci).start()
        store(n_chunks - 1).wait()

    return k(table, idx.reshape(SC_WORKERS, n_chunks, SC_CHUNK))


def _combine_kernel(h_ref, r_ref, y1_ref, y2_ref, g_ref, b_ref, o_ref, *, alpha):
    g1 = r_ref[:, 2:3]
    g2 = r_ref[:, 3:4]
    ffn = g1 * _unpack_bf16_pairs(y1_ref[...]) + g2 * _unpack_bf16_pairs(y2_ref[...])
    o_ref[...] = _layer_norm(alpha * h_ref[...] + ffn, g_ref[...], b_ref[...])


def _combine(h2, routing, y12, g, bb, alpha):
    t, d = h2.shape
    tm = 512
    nt = t // tm
    row = pl.BlockSpec((tm, d), lambda i: (i, 0))
    vec = pl.BlockSpec((1, d), lambda i: (0, 0))
    return pl.pallas_call(
        functools.partial(_combine_kernel, alpha=alpha),
        grid=(nt,),
        in_specs=[row, pl.BlockSpec((tm, LANES), lambda i: (i, 0)),
                  pl.BlockSpec((tm, d // 2), lambda i: (i, 0)),
                  pl.BlockSpec((tm, d // 2), lambda i: (nt + i, 0)), vec, vec],
        out_specs=row,
        out_shape=jax.ShapeDtypeStruct((t, d), F32),
        compiler_params=_cparams(("parallel",)),
        name="combine_ln",
    )(h2, routing, y12, y12, g.reshape(1, d), bb.reshape(1, d))


def _moe(h2, h2_packed, routing, counts, wg, wu, wd, l, g, bb, alpha):
    t, d = h2.shape
    eid = routing[:, 0:2].astype(jnp.int32)
    rank = routing[:, 4:6].astype(jnp.int32)
    cnt = counts[N_GROUPS:N_GROUPS + N_EXPERTS, 0].astype(jnp.int32)
    padded = (cnt + MOE_ROWS - 1) // MOE_ROWS * MOE_ROWS
    pad_end = jnp.cumsum(padded)
    pad_start = pad_end - padded
    experts = jnp.arange(N_EXPERTS, dtype=jnp.int32)
    dest = jnp.sum(jnp.where(eid[..., None] == experts, pad_start, 0), axis=-1) + rank
    n_slots = t * TOP_K
    n_blocks = (n_slots + N_EXPERTS * (MOE_ROWS - 1) + MOE_ROWS - 1) // MOE_ROWS
    cap = n_blocks * MOE_ROWS
    blk_row = jnp.arange(n_blocks, dtype=jnp.int32) * MOE_ROWS
    block_e = jnp.minimum(jnp.sum((pad_end[None, :] <= blk_row[:, None]).astype(jnp.int32), axis=1),
                          N_EXPERTS - 1)
    is_e = block_e[:, None] == experts
    blk_cnt = jnp.sum(jnp.where(is_e, cnt, 0), axis=1)
    blk_start = jnp.sum(jnp.where(is_e, pad_start, 0), axis=1)
    n_valid = jnp.clip(blk_cnt - (blk_row - blk_start), 0, MOE_ROWS).astype(jnp.int32)
    used = cnt > 0
    later = (experts[None, :] > experts[:, None]) & used[None, :]
    next_used = jnp.min(jnp.where(later, experts[None, :], N_EXPERTS), axis=1)
    rank = jnp.cumsum(used.astype(jnp.int32)) - 1
    blk_first = (n_valid > 0) & (blk_row == blk_start)
    blk_next = jnp.sum(jnp.where(is_e, next_used, 0), axis=1)
    blk_slot = jnp.sum(jnp.where(is_e, rank, 0), axis=1) % 2
    flags = (blk_first.astype(jnp.int32) * MOE_FIRST + blk_slot * MOE_SLOT
             + (blk_next < N_EXPERTS).astype(jnp.int32) * MOE_HAS_NEXT).astype(jnp.int32)
    next_e = jnp.minimum(blk_next, N_EXPERTS - 1).astype(jnp.int32)
    xs = _sc_scatter_rows(h2_packed, dest[:, 0], dest[:, 1], cap)
    ys = _moe_experts(xs, block_e, n_valid, flags, next_e, wg, wu, wd, l)
    y12 = _sc_gather_rows(ys, jnp.concatenate([dest[:, 0], dest[:, 1]]))
    return _combine(h2, routing, y12, g, bb, alpha)


def _mixing_layer(h, b, s, l, p, lam_init, input_norm=None):
    t = b * s
    alpha = (2 * p['w_in'].shape[0]) ** 0.25
    d = h.shape[1]
    w_in = p['w_in']
    tab_diff = _rope_tables(s, DIFF_QK_DIM)
    tab_dil = _rope_tables(s, HEAD_DIM)
    proj = _proj_branches(h.reshape(b, s, d), w_in, l, tab_diff, tab_dil, input_norm)
    if input_norm:
        h = proj.pop(0).reshape(t, d)
    qkv_na, qkv_diff, qkv_d0, qkv_d1, qkv_d2, u = proj
    y_a = _na_attention(qkv_na, _na_bias_table(p['na_rpb'][l]))
    y_b = _diff_attention(qkv_diff, p['diff_lam'][l], p['diff_subln_g'][l], lam_init)
    y_c = _pool(u, p['pool_w'][l], p['pool_scale'][l])
    dil = [_swa(qkv, s // r) for qkv, (_, r) in zip((qkv_d0, qkv_d1, qkv_d2), DIL_PATTERNS)]
    ys = [a.reshape(t, -1) for a in (y_a, y_b, y_c)]
    wg = lax.slice_in_dim(w_in[l], COL_GATE, COL_GATE + (len(ys) + 1) * d, axis=1).astype(BF16)
    router = _router_params(p['router_group_w'][l], p['router_group_b'][l], p['router_expert_w'][l],
                            p['router_expert_b'][l])
    return _merge(ys, dil, wg, p['b_gate'][l], p['w_branch'][l].astype(BF16), p['w_out'][l].astype(BF16), h,
                  p['ln1_g'][l], p['ln1_b'][l], router, alpha)


def kernel(x, emb_ln_g, emb_ln_b, w_in, b_gate, na_rpb, diff_lam, diff_subln_g, pool_w, pool_scale,
           w_branch, w_out, ln1_g, ln1_b, router_group_w, router_group_b, router_expert_w,
           router_expert_b, expert_w_gate, expert_w_up, expert_w_down, ln2_g, ln2_b):
    b, s, d = x.shape
    depth = w_in.shape[0]
    alpha = (2 * depth) ** 0.25
    p = dict(w_in=w_in, b_gate=b_gate, na_rpb=na_rpb, diff_lam=diff_lam, diff_subln_g=diff_subln_g,
             pool_w=pool_w, pool_scale=pool_scale, w_branch=w_branch, w_out=w_out, ln1_g=ln1_g, ln1_b=ln1_b,
             router_group_w=router_group_w, router_group_b=router_group_b,
             router_expert_w=router_expert_w, router_expert_b=router_expert_b)
    h = x.reshape(b * s, d)
    for l in range(depth):
        lam_init = 0.8 - 0.6 * math.exp(-0.3 * l)
        h, h_packed, routing, counts = _mixing_layer(h, b, s, l, p, lam_init,
                                                     (emb_ln_g, emb_ln_b) if l == 0 else None)
        h = _moe(h, h_packed, routing, counts, expert_w_gate, expert_w_up, expert_w_down, l,
                 ln2_g[l], ln2_b[l], alpha)
    return h.reshape(b, s, d)
```

```python
import functools
import math

import jax
import jax.numpy as jnp
import numpy as np
from jax import lax
from jax.experimental import pallas as pl
from jax.experimental.pallas import tpu as pltpu
from jax.experimental.pallas import tpu_sc as plsc

F32 = jnp.float32
BF16 = jnp.bfloat16

LANES = 128
GRID_W = 64
HEAD_DIM = 64
ROPE_THETA = 500000.0
LN_EPS = 1e-5
NA_ROWS = 8
NA_COLS = 16
DIFF_QK_DIM = 32
POOL_WINDOWS = (2, 4, 8, 16)
POOL_GROUP = 64
DIL_PATTERNS = ((128, 1), (512, 4), (2048, 16))
DIL_HALF = 64
N_GROUPS = 4
EXPERTS_PER_GROUP = 8
N_EXPERTS = N_GROUPS * EXPERTS_PER_GROUP
TOP_K = 2
MOE_ROWS = 256
NEG = -1e30

COL_NA = 0
COL_DIFF = 768
COL_POOL = 1536
COL_DIL = 1792
COL_GATE = 4096

V7X_VMEM_BYTES = 64 * 1024 * 1024
VMEM_LIMIT = V7X_VMEM_BYTES - 8 * 1024 * 1024


def _cparams(sem):
    return pltpu.CompilerParams(dimension_semantics=sem, vmem_limit_bytes=VMEM_LIMIT)


def _layer_norm(x, g, b):
    mu = jnp.mean(x, axis=-1, keepdims=True)
    xc = x - mu
    var = jnp.mean(xc * xc, axis=-1, keepdims=True)
    return xc * lax.rsqrt(var + LN_EPS) * g + b


def _sigmoid(x):
    return 0.5 * jnp.tanh(0.5 * x) + 0.5


def _dot(a, b):
    return jnp.dot(a, b, preferred_element_type=F32)


def _dot_nt(a, b):
    return lax.dot_general(a, b, (((1,), (1,)), ((), ())), preferred_element_type=F32)


def _lane(shape):
    return lax.broadcasted_iota(jnp.int32, shape, len(shape) - 1)


HI16 = -65536


def _pack_bf16_pairs(x):
    w = x.shape[1] // 2
    hi = lax.bitcast_convert_type(x[:, :w].astype(BF16).astype(F32), jnp.int32)
    lo = lax.bitcast_convert_type(x[:, w:].astype(BF16).astype(F32), jnp.int32)
    return (hi & HI16) | lax.shift_right_logical(lo, 16)


def _unpack_bf16_pairs(p):
    hi = lax.bitcast_convert_type(p & HI16, F32)
    lo = lax.bitcast_convert_type(lax.shift_left(p, 16), F32)
    return jnp.concatenate([hi, lo], axis=1)


QKV_W = 6 * LANES
WCOL = 256


QKV_TM = 512
QKV_SEGMENTS = ((COL_NA, 1, 0, HEAD_DIM), (COL_DIFF, 1, DIFF_QK_DIM // 8, DIFF_QK_DIM),
                (COL_DIL, 1, HEAD_DIM // 8, HEAD_DIM), (COL_DIL + QKV_W, 4, HEAD_DIM // 8, HEAD_DIM),
                (COL_DIL + 2 * QKV_W, 16, HEAD_DIM // 8, HEAD_DIM))


def _qkv_epilogue(z, t_ref, o_ref, zs_ref, r, half, q_scale):
    tm = z.shape[0]
    n = tm // r
    for c in range(QKV_W // LANES):
        blk = z[:, c * LANES:(c + 1) * LANES]
        if half and c < 4:
            blk = _rope(blk, t_ref, half)
        if c < 2:
            blk = blk * q_scale
        if r == 1:
            o_ref[0, 0, :, c * LANES:(c + 1) * LANES] = blk.astype(BF16)
        else:
            zs_ref[c] = blk
            for m in range(r):
                o_ref[0, m, :, c * LANES:(c + 1) * LANES] = zs_ref[
                    c, pl.ds(m, n, stride=r), :].astype(BF16)


def _proj_kernel(*refs, norm_input):
    if norm_input:
        x_ref, g_ref, b_ref, w_ref, td_ref, tl_ref, h_ref = refs[:7]
        refs = refs[7:]
    else:
        x_ref, w_ref, td_ref, tl_ref = refs[:4]
        refs = refs[4:]
    na_ref, df_ref, d0_ref, d1_ref, d2_ref, u_ref, wb_ref, zs1_ref, zs2_ref = refs

    @pl.when((pl.program_id(0) == 0) & (pl.program_id(1) == 0))
    def _():
        wb_ref[...] = w_ref[...].astype(BF16)

    x = x_ref[0]
    if norm_input:
        x = _layer_norm(x, g_ref[...], b_ref[...])
        h_ref[0] = x
    xb = x.astype(BF16)
    outs = (na_ref, df_ref, d0_ref, d1_ref, d2_ref)
    stage = (None, None, None, zs1_ref, zs2_ref)
    for (col0, r, half, head_w), o_ref, zs_ref in zip(QKV_SEGMENTS, outs, stage):
        z = _dot(xb, wb_ref[:, col0:col0 + QKV_W])
        t_ref = td_ref if head_w == DIFF_QK_DIM else tl_ref
        _qkv_epilogue(z, t_ref, o_ref, zs_ref, r, half, head_w ** -0.5 * LOG2E)
    u_ref[0] = _dot(xb, wb_ref[:, COL_POOL:COL_POOL + WCOL])


def _proj_branches(x3, w_in, l, tab_diff, tab_dil, norm=None):
    b, s, d = x3.shape
    tm = QKV_TM
    qkv_spec = lambda r: pl.BlockSpec((1, r, tm // r, QKV_W), lambda bi, i: (bi, 0, i, 0))
    qkv_shape = lambda r: jax.ShapeDtypeStruct((b, r, s // r, QKV_W), BF16)
    tab_spec = pl.BlockSpec((3, tm, LANES), lambda bi, i: (0, i, 0))
    row_spec = pl.BlockSpec((1, tm, d), lambda bi, i: (bi, i, 0))
    vec = pl.BlockSpec((1, d), lambda bi, i: (0, 0))
    w_spec = pl.BlockSpec((None, d, COL_GATE), lambda bi, i: (l, 0, 0), pipeline_mode=pl.Buffered(1))
    rs = [seg[1] for seg in QKV_SEGMENTS]
    in_specs = [row_spec] + ([vec, vec] if norm else []) + [w_spec, tab_spec, tab_spec]
    args = [x3] + ([a.reshape(1, d) for a in norm] if norm else []) + [w_in, tab_diff, tab_dil]
    outs = pl.pallas_call(
        functools.partial(_proj_kernel, norm_input=bool(norm)),
        grid=(b, s // tm),
        in_specs=in_specs,
        out_specs=([row_spec] if norm else []) + [qkv_spec(r) for r in rs] + [
            pl.BlockSpec((1, tm, WCOL), lambda bi, i: (bi, i, 0))],
        out_shape=([jax.ShapeDtypeStruct((b, s, d), F32)] if norm else []) + [
            qkv_shape(r) for r in rs] + [jax.ShapeDtypeStruct((b, s, WCOL), F32)],
        scratch_shapes=[pltpu.VMEM((d, COL_GATE), BF16)] + [pltpu.VMEM((QKV_W // LANES, tm, LANES), F32)] * 2,
        compiler_params=_cparams(("arbitrary", "arbitrary")),
        name="proj_branches",
    )(*args)
    n_qkv = len(rs)
    head = list(outs[:-n_qkv - 1])
    return head + [o.reshape(b, s, QKV_W) for o in outs[-n_qkv - 1:-1]] + [outs[-1]]


def _rope_tables(seq, head_w):
    rot = head_w // 4
    half = rot // 2
    inv_freq = jnp.exp(jnp.arange(half, dtype=F32) * (-2.0 * math.log(ROPE_THETA) / rot))
    ang = jnp.arange(seq, dtype=jnp.int32).astype(F32)[:, None] * inv_freq[None, :]
    cos, sin = jnp.cos(ang), jnp.sin(ang)
    zero = jnp.zeros((seq, head_w - rot), F32)
    zh = jnp.zeros((seq, half), F32)
    t0 = jnp.concatenate([cos, cos, jnp.ones((seq, head_w - rot), F32)], axis=1)
    t1 = jnp.concatenate([-sin, zh, zero], axis=1)
    t2 = jnp.concatenate([zh, sin, zero], axis=1)
    reps = LANES // head_w
    return jnp.stack([jnp.tile(t0, (1, reps)), jnp.tile(t1, (1, reps)), jnp.tile(t2, (1, reps))])


def _rope(x, t_ref, half):
    return (x * t_ref[0] + pltpu.roll(x, LANES - half, 1) * t_ref[1]
            + pltpu.roll(x, half, 1) * t_ref[2])


LOG2E = math.log2(math.e)


def _na_bias_table(rpb):
    kr, kc = NA_ROWS, NA_COLS
    n_heads = rpb.shape[0]
    col = np.arange(GRID_W)
    col_start = np.clip(col - kc // 2, 0, GRID_W - kc)
    in_win = (col[None, :] >= col_start[:, None]) & (col[None, :] < col_start[:, None] + kc)
    dc = np.clip(col[None, :] - col[:, None] + kc - 1, 0, 2 * kc - 2)
    sel_c = ((dc[..., None] == np.arange(2 * kc - 1)) & in_win[..., None]).astype(np.float32)
    rp = rpb.astype(F32).reshape(n_heads // 2, 2, 2 * kr - 1, 2 * kc - 1) * LOG2E
    tiles = jnp.einsum('phaj,qxj->phaqx', rp, jnp.asarray(sel_c), precision=lax.Precision.HIGHEST)
    tiles = tiles + jnp.asarray(np.where(in_win, 0.0, NEG).astype(np.float32))
    tiles = jnp.concatenate([tiles, jnp.full_like(tiles[:, :, :1], NEG)], axis=2)
    return jnp.concatenate([tiles, tiles], axis=-1)


NA_QROWS = 4
NA_KROWS = 12
NA_MASKED = 2 * NA_ROWS - 1


def _na_row_offsets(n_rows):
    table = []
    for blk in (0, 1, n_rows // NA_QROWS - 1):
        ws = _na_window_start(blk, n_rows)
        rows = []
        for i in range(NA_QROWS):
            r = blk * NA_QROWS + i
            start = min(max(r - NA_ROWS // 2, 0), n_rows - NA_ROWS)
            rows.append([ws + kk - r + NA_ROWS - 1 if start <= ws + kk < start + NA_ROWS else NA_MASKED
                         for kk in range(NA_KROWS)])
        table.append(rows)
    return table


def _na_window_start(blk, n_rows):
    lo = blk * NA_QROWS - NA_ROWS // 2
    hi = n_rows - NA_KROWS
    if isinstance(blk, int):
        return min(max(lo, 0), hi)
    return jnp.clip(lo, 0, hi)


def _stack_heads(q, lane):
    zero = jnp.zeros_like(q)
    return jnp.concatenate([jnp.where(lane < HEAD_DIM, q, zero), jnp.where(lane < HEAD_DIM, zero, q)],
                           axis=0)


def _na_kernel(q_ref, k_ref, v_ref, tile_ref, o_ref, bias_ref, *, blocks_per_step, n_rows):
    step = pl.program_id(2)
    nq = NA_QROWS * GRID_W
    kwin = NA_KROWS * GRID_W
    lane = _lane((nq, LANES))
    last_blk = n_rows // NA_QROWS - 1

    @pl.when(step == 0)
    def _():
        low = _lane((GRID_W, LANES)) < GRID_W
        for case, rows in enumerate(_na_row_offsets(n_rows)):
            for hh in range(2):
                for i, offs in enumerate(rows):
                    r0 = (hh * NA_QROWS + i) * GRID_W
                    for kp in range(NA_KROWS // 2):
                        bias_ref[case, r0:r0 + GRID_W, kp * LANES:(kp + 1) * LANES] = jnp.where(
                            low, tile_ref[0, hh, offs[2 * kp]], tile_ref[0, hh, offs[2 * kp + 1]])

    for i in range(blocks_per_step):
        blk = step * blocks_per_step + i
        case = jnp.where(blk == 0, 0, jnp.where(blk == last_blk, 2, 1))
        k0 = pl.multiple_of(_na_window_start(blk, n_rows) * GRID_W, GRID_W)
        q2 = _stack_heads(q_ref[0, i * nq:(i + 1) * nq, :], lane)
        sc = _dot_nt(q2, k_ref[0, pl.ds(k0, kwin), :]) + bias_ref[case]
        m = jnp.max(sc, axis=-1, keepdims=True)
        e = jnp.exp2(sc - m)
        den = jnp.sum(e, axis=-1, keepdims=True)
        pv = _dot(e.astype(BF16), v_ref[0, pl.ds(k0, kwin), :]) / den
        o = jnp.where(lane < HEAD_DIM, pv[0:nq], pv[nq:2 * nq])
        o_ref[0, i * nq:(i + 1) * nq, :] = o.astype(BF16)


def _na_attention(qkv, tiles):
    b, s, _ = qkv.shape
    n_rows = s // GRID_W
    bps = 8
    rps = bps * NA_QROWS
    tq = rps * GRID_W
    return pl.pallas_call(
        functools.partial(_na_kernel, blocks_per_step=bps, n_rows=n_rows),
        grid=(b, 2, n_rows // rps),
        in_specs=[pl.BlockSpec((1, tq, LANES), lambda bi, hp, i: (bi, i, hp)),
                  pl.BlockSpec((1, s, LANES), lambda bi, hp, i: (bi, 0, 2 + hp)),
                  pl.BlockSpec((1, s, LANES), lambda bi, hp, i: (bi, 0, 4 + hp)),
                  pl.BlockSpec((1, 2, NA_MASKED + 1, GRID_W, LANES), lambda bi, hp, i: (hp, 0, 0, 0, 0))],
        out_specs=pl.BlockSpec((1, tq, LANES), lambda bi, hp, i: (bi, i, hp)),
        out_shape=jax.ShapeDtypeStruct((b, s, 2 * LANES), BF16),
        scratch_shapes=[pltpu.VMEM((3, 2 * NA_QROWS * GRID_W, NA_KROWS * GRID_W), F32)],
        compiler_params=_cparams(("parallel", "parallel", "arbitrary")),
        name="na_attn",
    )(qkv, qkv, qkv, tiles)


def _diff_kernel(lam_ref, q_ref, k_ref, v_ref, g_ref, o_ref, *, lam_init, rows):
    tq = q_ref.shape[1]
    lane = _lane((rows, LANES))
    dl = lam_ref[...]
    lam = (jnp.exp(jnp.sum(dl[0:1] * dl[1:2], axis=-1, keepdims=True))
           - jnp.exp(jnp.sum(dl[2:3] * dl[3:4], axis=-1, keepdims=True)) + lam_init)
    v = v_ref[0]
    v_lane = _lane(v.shape)
    one = jnp.ones_like(v)
    v_ext = (jnp.where(v_lane < HEAD_DIM, v, one), jnp.where(v_lane < HEAD_DIM, one, v))
    in_h0 = lane < HEAD_DIM
    for r0 in range(0, tq, rows):
        q = q_ref[0, r0:r0 + rows, :]
        zero = jnp.zeros_like(q)
        pv = []
        for lo in range(0, LANES, DIFF_QK_DIM):
            qm = jnp.where((lane >= lo) & (lane < lo + DIFF_QK_DIM), q, zero)
            sc = _dot_nt(qm, k_ref[0])
            e = jnp.exp2(sc - jnp.max(sc, axis=-1, keepdims=True))
            ev = _dot(e.astype(BF16), v_ext[lo // HEAD_DIM])
            pv.append(ev / pltpu.roll(ev, HEAD_DIM, 1))
        o = jnp.where(in_h0, pv[0] - lam * pv[1], pv[2] - lam * pv[3])
        o2 = o * o
        ms0 = jnp.sum(jnp.where(in_h0, o2, 0.0), axis=-1, keepdims=True) / HEAD_DIM
        ms1 = jnp.sum(jnp.where(in_h0, 0.0, o2), axis=-1, keepdims=True) / HEAD_DIM
        ms = jnp.where(in_h0, ms0, ms1)
        o = o * lax.rsqrt(ms + LN_EPS) * g_ref[...] * (1.0 - lam_init)
        o_ref[0, r0:r0 + rows, :] = o.astype(BF16)


def _diff_attention(qkv, diff_lam, subln_g, lam_init):
    b, s, _ = qkv.shape
    tq = 1024
    g2 = jnp.tile(subln_g.reshape(1, HEAD_DIM), (1, 2))
    return pl.pallas_call(
        functools.partial(_diff_kernel, lam_init=lam_init, rows=512),
        grid=(b, 2, s // tq),
        in_specs=[pl.BlockSpec((4, DIFF_QK_DIM), lambda bi, hp, i: (0, 0)),
                  pl.BlockSpec((1, tq, LANES), lambda bi, hp, i: (bi, i, hp)),
                  pl.BlockSpec((1, s, LANES), lambda bi, hp, i: (bi, 0, 2 + hp)),
                  pl.BlockSpec((1, s, LANES), lambda bi, hp, i: (bi, 0, 4 + hp)),
                  pl.BlockSpec((1, LANES), lambda bi, hp, i: (0, 0))],
        out_specs=pl.BlockSpec((1, tq, LANES), lambda bi, hp, i: (bi, i, hp)),
        out_shape=jax.ShapeDtypeStruct((b, s, 2 * LANES), BF16),
        compiler_params=_cparams(("parallel", "parallel", "arbitrary")),
        name="diff_attn",
    )(diff_lam, qkv, qkv, qkv, g2)


POOL_PAD = 16
POOL_CHUNK = 512
POOL_HALO = 8


def _pool_kernel(u_ref, w_ref, sc_ref, o_ref, p_ref):
    s = u_ref.shape[1]
    width = u_ref.shape[2]
    p_ref[0:POOL_PAD, :] = jnp.zeros((POOL_PAD, width), F32)
    p_ref[POOL_PAD + s:POOL_PAD + s + POOL_PAD, :] = jnp.zeros((POOL_PAD, width), F32)
    p_ref[POOL_PAD:POOL_PAD + s, :] = u_ref[0]
    n = POOL_CHUNK + 2 * POOL_HALO
    lane = _lane((POOL_CHUNK, width))
    row = lax.broadcasted_iota(jnp.int32, (POOL_CHUNK, width), 0)
    w_of_lane = jnp.where(lane < POOL_GROUP, 2, jnp.where(lane < 2 * POOL_GROUP, 4,
                          jnp.where(lane < 3 * POOL_GROUP, 8, 16)))

    def body(ci, carry):
        c0 = pl.multiple_of(ci * POOL_CHUNK, POOL_CHUNK)
        x = p_ref[pl.ds(c0 + POOL_PAD - POOL_HALO, n), :]
        w2 = x + pltpu.roll(x, 1, 0)
        w4 = pltpu.roll(w2, 1, 0) + pltpu.roll(w2, n - 1, 0)
        w8 = pltpu.roll(w4, 2, 0) + pltpu.roll(w4, n - 2, 0)
        w16 = pltpu.roll(w8, 4, 0) + pltpu.roll(w8, n - 4, 0)
        u = x[POOL_HALO:POOL_HALO + POOL_CHUNK]
        wsum = jnp.where(lane < POOL_GROUP, w2[POOL_HALO:POOL_HALO + POOL_CHUNK],
                         jnp.where(lane < 2 * POOL_GROUP, w4[POOL_HALO:POOL_HALO + POOL_CHUNK],
                                   jnp.where(lane < 3 * POOL_GROUP, w8[POOL_HALO:POOL_HALO + POOL_CHUNK],
                                             w16[POOL_HALO:POOL_HALO + POOL_CHUNK])))
        t = row + c0
        half_w = w_of_lane // 2
        lo = jnp.maximum(t - half_w, 0)
        hi = jnp.minimum(t + w_of_lane - 1 - half_w, s - 1)
        cnt = (hi - lo + 1).astype(F32)
        dlt = wsum / cnt - u
        y = _dot(dlt.astype(BF16), w_ref[...]) * sc_ref[...]
        o_ref[0, pl.ds(c0, POOL_CHUNK), :] = y.astype(BF16)
        return carry

    lax.fori_loop(0, s // POOL_CHUNK, body, 0)


def _pool(u3, pool_w, pool_scale):
    b, s, width = u3.shape
    wbd = jax.scipy.linalg.block_diag(*[pool_w[g] for g in range(len(POOL_WINDOWS))]).astype(BF16)
    return pl.pallas_call(
        _pool_kernel,
        grid=(b,),
        in_specs=[pl.BlockSpec((1, s, width), lambda bi: (bi, 0, 0)),
                  pl.BlockSpec((width, width), lambda bi: (0, 0)),
                  pl.BlockSpec((1, width), lambda bi: (0, 0))],
        out_specs=pl.BlockSpec((1, s, width), lambda bi: (bi, 0, 0)),
        out_shape=jax.ShapeDtypeStruct((b, s, width), BF16),
        scratch_shapes=[pltpu.VMEM((s + 2 * POOL_PAD, width), F32)],
        compiler_params=_cparams(("parallel",)),
        name="pool",
    )(u3, wbd, pool_scale.reshape(1, width))


SWA_Q = 128
SWA_BAND = SWA_Q + 2 * DIL_HALF


def _swa_kernel(q_ref, k_ref, v_ref, o_ref, l_ref, *, length):
    lane = _lane((SWA_Q, LANES))
    rel = (lax.broadcasted_iota(jnp.int32, (2 * SWA_Q, SWA_BAND), 1)
           - (lax.broadcasted_iota(jnp.int32, (2 * SWA_Q, SWA_BAND), 0) & (SWA_Q - 1)))
    masks = {}
    for i in range(q_ref.shape[1] // SWA_Q):
        r0 = i * SWA_Q
        l0 = r0 % length
        lo = r0 - l0 + min(max(l0 - DIL_HALF, 0), length - SWA_BAND)
        if lo - r0 not in masks:
            d = rel + (lo - r0)
            masks[lo - r0] = jnp.where((d >= -DIL_HALF) & (d <= DIL_HALF), 0.0, NEG)
        q2 = _stack_heads(q_ref[0, r0:r0 + SWA_Q, :], lane)
        sc = _dot_nt(q2, k_ref[0, lo:lo + SWA_BAND, :]) + masks[lo - r0]
        m = jnp.max(sc, axis=-1, keepdims=True)
        e = jnp.exp2(sc - m)
        den = jnp.sum(e, axis=-1, keepdims=True)
        pv = _dot(e.astype(BF16), v_ref[0, lo:lo + SWA_BAND, :]) / den
        lse2 = m + jnp.log2(den)
        o_ref[0, i * SWA_Q:(i + 1) * SWA_Q, :] = jnp.where(lane < HEAD_DIM, pv[0:SWA_Q], pv[SWA_Q:])
        l_ref[0, i * SWA_Q:(i + 1) * SWA_Q, :] = jnp.where(lane < HEAD_DIM, lse2[0:SWA_Q], lse2[SWA_Q:])


def _swa(qkv, length):
    b, s, _ = qkv.shape
    shp = jax.ShapeDtypeStruct((b, s, 2 * LANES), F32)
    seq = lambda col: pl.BlockSpec((1, s, LANES), lambda bi, hp: (bi, 0, col + hp))
    return pl.pallas_call(
        functools.partial(_swa_kernel, length=length),
        grid=(b, 2),
        in_specs=[seq(0), seq(2), seq(4)],
        out_specs=[seq(0), seq(0)],
        out_shape=[shp, shp],
        compiler_params=_cparams(("parallel", "parallel")),
        name=f"swa_l{length}",
    )(qkv, qkv, qkv)


def _dil_combine(dil_refs, stage_refs, tm):
    (o0_ref, l0_ref), (o1_ref, l1_ref), (o2_ref, l2_ref) = dil_refs
    so1, sl1, so2, sl2 = stage_refs
    r1 = DIL_PATTERNS[1][1]
    r2 = DIL_PATTERNS[2][1]
    halves = []
    for hp in range(2):
        cols = slice(hp * LANES, (hp + 1) * LANES)
        for m in range(r1):
            so1[pl.ds(m, tm // r1, stride=r1), :] = o1_ref[0, m, :, cols]
            sl1[pl.ds(m, tm // r1, stride=r1), :] = l1_ref[0, m, :, cols]
        for m in range(r2):
            so2[pl.ds(m, tm // r2, stride=r2), :] = o2_ref[0, m, :, cols]
            sl2[pl.ds(m, tm // r2, stride=r2), :] = l2_ref[0, m, :, cols]
        l0, l1, l2 = l0_ref[0, 0, :, cols], sl1[...], sl2[...]
        mx = jnp.maximum(jnp.maximum(l0, l1), l2)
        e0, e1, e2 = jnp.exp2(l0 - mx), jnp.exp2(l1 - mx), jnp.exp2(l2 - mx)
        den = e0 + e1 + e2
        halves.append((e0 / den) * o0_ref[0, 0, :, cols] + (e1 / den) * so1[...] + (e2 / den) * so2[...])
    return jnp.concatenate(halves, axis=1)


def _merge_kernel(ya_ref, yb_ref, yc_ref, o0_ref, l0_ref, o1_ref, l1_ref, o2_ref, l2_ref,
                  wg_ref, bg_ref, wb_ref, wo_ref, h_ref, g_ref, b_ref, wr_ref, br_ref,
                  o_ref, op_ref, r_ref, cnt_ref, so1, sl1, so2, sl2, carry, before, *, alpha):
    tm, d = h_ref.shape

    @pl.when(pl.program_id(0) == 0)
    def _():
        _route_init(carry, before)

    hb = h_ref[...].astype(BF16)
    y_d = _dil_combine(((o0_ref, l0_ref), (o1_ref, l1_ref), (o2_ref, l2_ref)),
                       (so1, sl1, so2, sl2), tm).astype(BF16)
    merged = None
    for n, y in enumerate((ya_ref[...], yb_ref[...], yc_ref[...], y_d)):
        zg = _dot(hb, wg_ref[:, n * d:(n + 1) * d]) + bg_ref[:, n * d:(n + 1) * d]
        term = _sigmoid(zg) * _dot(y, wb_ref[n])
        merged = term if merged is None else merged + term
    mix = _dot(merged.astype(BF16), wo_ref[...])
    h1 = _layer_norm(alpha * h_ref[...] + mix, g_ref[...], b_ref[...])
    o_ref[...] = h1
    op_ref[...] = _pack_bf16_pairs(h1)
    _route_finish(_route_select(h1, wr_ref, br_ref), r_ref, cnt_ref, carry, before)


def _merge(ys, dil, wg, b_gate, wb, wo, h2, g, bb, router, alpha):
    t, d = h2.shape
    w_route, b_route = router
    b, s, bw = dil[0][0].shape
    nb = len(ys) + 1
    tm = QKV_TM
    nt = s // tm
    yspec = pl.BlockSpec((tm, bw), lambda i: (i, 0))
    vec = pl.BlockSpec((1, d), lambda i: (0, 0))
    dil_args, dil_specs = [], []
    for (o, lse), (_, r) in zip(dil, DIL_PATTERNS):
        spec = pl.BlockSpec((1, r, tm // r, bw), lambda i: (i // nt, 0, i % nt, 0))
        dil_args += [o.reshape(b, r, s // r, bw), lse.reshape(b, r, s // r, bw)]
        dil_specs += [spec, spec]
    return pl.pallas_call(
        functools.partial(_merge_kernel, alpha=alpha),
        grid=(t // tm,),
        in_specs=[yspec, yspec, yspec] + dil_specs + [
                  pl.BlockSpec((d, nb * d), lambda i: (0, 0)),
                  pl.BlockSpec((1, nb * d), lambda i: (0, 0)),
                  pl.BlockSpec((nb, bw, d), lambda i: (0, 0, 0)),
                  pl.BlockSpec((d, d), lambda i: (0, 0)),
                  pl.BlockSpec((tm, d), lambda i: (i, 0)),
                  vec, vec,
                  pl.BlockSpec((LANES, d), lambda i: (0, 0)),
                  pl.BlockSpec((LANES, 1), lambda i: (0, 0))],
        out_specs=[pl.BlockSpec((tm, d), lambda i: (i, 0)),
                   pl.BlockSpec((tm, d // 2), lambda i: (i, 0)),
                   pl.BlockSpec((tm, LANES), lambda i: (i, 0)),
                   pl.BlockSpec((LANES, 1), lambda i: (0, 0))],
        out_shape=[jax.ShapeDtypeStruct((t, d), F32), jax.ShapeDtypeStruct((t, d // 2), jnp.int32),
                   jax.ShapeDtypeStruct((t, LANES), F32), jax.ShapeDtypeStruct((LANES, 1), F32)],
        scratch_shapes=[pltpu.VMEM((tm, LANES), F32)] * 4 + [pltpu.VMEM((LANES, 1), F32),
                                                               pltpu.VMEM((tm, tm), BF16)],
        compiler_params=_cparams(("arbitrary",)),
        name="merge",
    )(*ys, *dil_args, wg, b_gate.reshape(1, nb * d), wb, wo, h2, g.reshape(1, d), bb.reshape(1, d),
      w_route, b_route)


def _route_init(carry, before):
    tm = before.shape[0]
    carry[...] = jnp.zeros_like(carry)
    before[...] = (lax.broadcasted_iota(jnp.int32, (tm, tm), 0)
                   < lax.broadcasted_iota(jnp.int32, (tm, tm), 1)).astype(BF16)


def _route_select(h, w_ref, b_ref):
    tm = h.shape[0]
    logits = lax.dot_general(w_ref[...], h, (((1,), (1,)), ((), ())),
                             preferred_element_type=F32,
                             precision=lax.Precision.HIGHEST) + b_ref[...]
    row = lax.broadcasted_iota(jnp.int32, (LANES, tm), 0)
    big = jnp.int32(1 << 20)
    is_g = row < N_GROUPS
    gl = jnp.where(is_g, logits, -jnp.inf)
    gmax = jnp.max(gl, axis=0, keepdims=True)
    gsel = jnp.min(jnp.where(is_g & (gl == gmax), row, big), axis=0, keepdims=True)
    pg = 1.0 / jnp.sum(jnp.exp(gl - gmax), axis=0, keepdims=True)
    e_lo = N_GROUPS + gsel * EXPERTS_PER_GROUP
    in_grp = (row >= e_lo) & (row < e_lo + EXPERTS_PER_GROUP)
    el = jnp.where(in_grp, logits, -jnp.inf)
    v1 = jnp.max(el, axis=0, keepdims=True)
    i1 = jnp.min(jnp.where(in_grp & (el == v1), row, big), axis=0, keepdims=True)
    el2 = jnp.where(row == i1, -jnp.inf, el)
    v2 = jnp.max(el2, axis=0, keepdims=True)
    i2 = jnp.min(jnp.where(in_grp & (row != i1) & (el2 == v2), row, big), axis=0, keepdims=True)
    t2 = jnp.exp(v2 - v1)
    g1 = pg / (1.0 + t2)
    g2 = pg * t2 / (1.0 + t2)
    return dict(row=row, i1=i1, i2=i2, g1=g1, g2=g2,
                oh1=row == i1, oh2=row == i2)


def _route_finish(sel, o_ref, cnt_ref, carry, before):
    row, i1, i2, g1, g2, oh1, oh2 = (sel[k] for k in ('row', 'i1', 'i2', 'g1', 'g2', 'oh1', 'oh2'))
    oh1b = oh1.astype(BF16)
    oh2b = oh2.astype(BF16)
    c0 = carry[...]
    c1 = c0 + jnp.sum(oh1b.astype(F32), axis=1, keepdims=True)
    rank1 = jnp.sum(jnp.where(oh1, _dot(oh1b, before[...]) + c0, 0.0), axis=0, keepdims=True)
    rank2 = jnp.sum(jnp.where(oh2, _dot(oh2b, before[...]) + c1, 0.0), axis=0, keepdims=True)
    c2 = c1 + jnp.sum(oh2b.astype(F32), axis=1, keepdims=True)
    carry[...] = c2
    cnt_ref[...] = c2
    e1 = (i1 - N_GROUPS).astype(F32)
    e2 = (i2 - N_GROUPS).astype(F32)
    out_t = jnp.where(row == 0, e1, jnp.where(row == 1, e2, jnp.where(row == 2, g1, jnp.where(
        row == 3, g2, jnp.where(row == 4, rank1, jnp.where(row == 5, rank2, 0.0))))))
    o_ref[...] = out_t.T


def _router_params(rg_w, rg_b, re_w, re_b):
    d = rg_w.shape[0]
    pad = LANES - N_GROUPS - N_EXPERTS
    w_t = jnp.concatenate([rg_w.T, re_w.T, jnp.zeros((pad, d), F32)], axis=0)
    bias = jnp.concatenate([rg_b, re_b, jnp.zeros((pad,), F32)]).reshape(LANES, 1)
    return w_t, bias


MOE_FIRST, MOE_SLOT, MOE_HAS_NEXT = 1, 2, 4


def _moe_kernel(be_ref, nv_ref, fl_ref, nx_ref, x_ref, wg_hbm, wu_hbm, wd_hbm, o_ref,
                wgb, wub, wdb, stage_g, stage_u, stage_d, sem, *, layer):
    j = pl.program_id(0)
    n_valid = nv_ref[j]
    flags = fl_ref[j]
    slot = (flags // MOE_SLOT) % 2

    def weight_copies(e, s):
        return [pltpu.make_async_copy(w_hbm.at[layer, e], stage.at[s], sem.at[s, k])
                for k, (w_hbm, stage) in enumerate(((wg_hbm, stage_g), (wu_hbm, stage_u),
                                                    (wd_hbm, stage_d)))]

    @pl.when(flags % 2 == MOE_FIRST)
    def _():
        @pl.when(j == 0)
        def _():
            for cp in weight_copies(be_ref[j], slot):
                cp.start()

        for cp in weight_copies(be_ref[j], slot):
            cp.wait()

        @pl.when((flags // MOE_HAS_NEXT) % 2 == 1)
        def _():
            for cp in weight_copies(nx_ref[j], 1 - slot):
                cp.start()

        wgb[...] = stage_g[slot].astype(BF16)
        wub[...] = stage_u[slot].astype(BF16)
        wdb[...] = stage_d[slot].astype(BF16)

    @pl.when(n_valid > 0)
    def _():
        row = lax.broadcasted_iota(jnp.int32, x_ref.shape, 0)
        x = _unpack_bf16_pairs(jnp.where(row < n_valid, x_ref[...], 0)).astype(BF16)
        g = _dot(x, wgb[...])
        u = _dot(x, wub[...])
        hmid = (g * _sigmoid(g)) * u
        o_ref[...] = _pack_bf16_pairs(_dot(hmid.astype(BF16), wdb[...]))

    @pl.when(n_valid <= 0)
    def _():
        o_ref[...] = jnp.zeros_like(o_ref)


def _moe_experts(xs, block_e, n_valid, flags, next_e, wg, wu, wd, l):
    cap, dp = xs.shape
    d = 2 * dp
    de = wg.shape[3]
    n_blocks = cap // MOE_ROWS
    hbm = pl.BlockSpec(memory_space=pl.ANY)
    grid_spec = pltpu.PrefetchScalarGridSpec(
        num_scalar_prefetch=4,
        grid=(n_blocks,),
        in_specs=[pl.BlockSpec((MOE_ROWS, dp), lambda j, *_: (j, 0)), hbm, hbm, hbm],
        out_specs=pl.BlockSpec((MOE_ROWS, dp), lambda j, *_: (j, 0)),
        scratch_shapes=[pltpu.VMEM((d, de), BF16), pltpu.VMEM((d, de), BF16), pltpu.VMEM((de, d), BF16),
                        pltpu.VMEM((2, d, de), F32), pltpu.VMEM((2, d, de), F32),
                        pltpu.VMEM((2, de, d), F32), pltpu.SemaphoreType.DMA((2, 3))],
    )
    return pl.pallas_call(
        functools.partial(_moe_kernel, layer=l),
        grid_spec=grid_spec,
        out_shape=jax.ShapeDtypeStruct((cap, dp), jnp.int32),
        compiler_params=_cparams(("arbitrary",)),
        name="moe_experts",
    )(block_e, n_valid, flags, next_e, xs, wg, wu, wd)


SC_CORES = 2
SC_SUBCORES = 16
SC_WORKERS = SC_CORES * SC_SUBCORES
SC_CHUNK = 64


def _sc_mesh():
    return plsc.VectorSubcoreMesh(core_axis_name="c", subcore_axis_name="s",
                                  num_cores=SC_CORES, num_subcores=SC_SUBCORES)


def _sc_scatter_rows(src, idx0, idx1, n_out):
    t, d = src.shape
    per_w = t // SC_WORKERS
    n_chunks = per_w // SC_CHUNK
    idx_shape = (SC_WORKERS, n_chunks, SC_CHUNK)

    @functools.partial(
        pl.kernel, mesh=_sc_mesh(),
        out_type=jax.ShapeDtypeStruct((n_out, d), src.dtype),
        scratch_types=[pltpu.VMEM((n_chunks, SC_CHUNK), jnp.int32),
                       pltpu.VMEM((n_chunks, SC_CHUNK), jnp.int32),
                       pltpu.VMEM((2, SC_CHUNK, d), src.dtype),
                       pltpu.SemaphoreType.DMA((2,)), pltpu.SemaphoreType.DMA((2,))],
        name="sc_dispatch",
    )
    def k(src_hbm, i0_hbm, i1_hbm, out_hbm, i0_v, i1_v, rows_v, sem_in, sem_out):
        wid = lax.axis_index("s") * SC_CORES + lax.axis_index("c")
        base = wid * per_w
        pltpu.sync_copy(i0_hbm.at[wid], i0_v)
        pltpu.sync_copy(i1_hbm.at[wid], i1_v)

        def load(ci):
            s = ci % 2
            return pltpu.make_async_copy(src_hbm.at[pl.ds(base + ci * SC_CHUNK, SC_CHUNK)],
                                         rows_v.at[s], sem_in.at[s])

        def scatter(ci, idx_v):
            s = ci % 2
            return pltpu.make_async_copy(rows_v.at[s], out_hbm.at[idx_v.at[ci]], sem_out.at[s])

        load(0).start()
        for ci in range(n_chunks):
            load(ci).wait()
            if ci >= 1:
                scatter(ci - 1, i0_v).wait()
                scatter(ci - 1, i1_v).wait()
            if ci + 1 < n_chunks:
                load(ci + 1).start()
            scatter(ci, i0_v).start()
            scatter(ci, i1_v).start()
        scatter(n_chunks - 1, i0_v).wait()
        scatter(n_chunks - 1, i1_v).wait()

    return k(src, idx0.reshape(idx_shape), idx1.reshape(idx_shape))


def _sc_gather_rows(table, idx):
    n = idx.shape[0]
    d = table.shape[1]
    per_w = n // SC_WORKERS
    n_chunks = per_w // SC_CHUNK

    @functools.partial(
        pl.kernel, mesh=_sc_mesh(),
        out_type=jax.ShapeDtypeStruct((n, d), table.dtype),
        scratch_types=[pltpu.VMEM((n_chunks, SC_CHUNK), jnp.int32),
                       pltpu.VMEM((2, SC_CHUNK, d), table.dtype),
                       pltpu.SemaphoreType.DMA((2,)), pltpu.SemaphoreType.DMA((2,))],
        name="sc_collect",
    )
    def k(table_hbm, idx_hbm, out_hbm, idx_v, rows_v, sem_in, sem_out):
        wid = lax.axis_index("s") * SC_CORES + lax.axis_index("c")
        base = wid * per_w
        pltpu.sync_copy(idx_hbm.at[wid], idx_v)

        def gather(ci):
            s = ci % 2
            return pltpu.make_async_copy(table_hbm.at[idx_v.at[ci]], rows_v.at[s], sem_in.at[s])

        def store(ci):
            s = ci % 2
            return pltpu.make_async_copy(rows_v.at[s], out_hbm.at[pl.ds(base + ci * SC_CHUNK, SC_CHUNK)],
                                         sem_out.at[s])

        gather(0).start()
        for ci in range(n_chunks):
            gather(ci).wait()
            if ci >= 1:
                store(ci - 1).wait()
            if ci + 1 < n_chunks:
                gather(ci + 1).start()
            store(ci).start()
        store(n_chunks - 1).wait()

    return k(table, idx.reshape(SC_WORKERS, n_chunks, SC_CHUNK))


def _combine_kernel(h_ref, r_ref, y1_ref, y2_ref, g_ref, b_ref, o_ref, *, alpha):
    g1 = r_ref[:, 2:3]
    g2 = r_ref[:, 3:4]
    ffn = g1 * _unpack_bf16_pairs(y1_ref[...]) + g2 * _unpack_bf16_pairs(y2_ref[...])
    o_ref[...] = _layer_norm(alpha * h_ref[...] + ffn, g_ref[...], b_ref[...])


def _combine(h2, routing, y12, g, bb, alpha):
    t, d = h2.shape
    tm = 512
    nt = t // tm
    row = pl.BlockSpec((tm, d), lambda i: (i, 0))
    vec = pl.BlockSpec((1, d), lambda i: (0, 0))
    return pl.pallas_call(
        functools.partial(_combine_kernel, alpha=alpha),
        grid=(nt,),
        in_specs=[row, pl.BlockSpec((tm, LANES), lambda i: (i, 0)),
                  pl.BlockSpec((tm, d // 2), lambda i: (i, 0)),
                  pl.BlockSpec((tm, d // 2), lambda i: (nt + i, 0)), vec, vec],
        out_specs=row,
        out_shape=jax.ShapeDtypeStruct((t, d), F32),
        compiler_params=_cparams(("parallel",)),
        name="combine_ln",
    )(h2, routing, y12, y12, g.reshape(1, d), bb.reshape(1, d))


def _moe(h2, h2_packed, routing, counts, wg, wu, wd, l, g, bb, alpha):
    t, d = h2.shape
    eid = routing[:, 0:2].astype(jnp.int32)
    rank = routing[:, 4:6].astype(jnp.int32)
    cnt = counts[N_GROUPS:N_GROUPS + N_EXPERTS, 0].astype(jnp.int32)
    padded = (cnt + MOE_ROWS - 1) // MOE_ROWS * MOE_ROWS
    pad_end = jnp.cumsum(padded)
    pad_start = pad_end - padded
    experts = jnp.arange(N_EXPERTS, dtype=jnp.int32)
    dest = jnp.sum(jnp.where(eid[..., None] == experts, pad_start, 0), axis=-1) + rank
    n_slots = t * TOP_K
    n_blocks = (n_slots + N_EXPERTS * (MOE_ROWS - 1) + MOE_ROWS - 1) // MOE_ROWS
    cap = n_blocks * MOE_ROWS
    blk_row = jnp.arange(n_blocks, dtype=jnp.int32) * MOE_ROWS
    block_e = jnp.minimum(jnp.sum((pad_end[None, :] <= blk_row[:, None]).astype(jnp.int32), axis=1),
                          N_EXPERTS - 1)
    is_e = block_e[:, None] == experts
    blk_cnt = jnp.sum(jnp.where(is_e, cnt, 0), axis=1)
    blk_start = jnp.sum(jnp.where(is_e, pad_start, 0), axis=1)
    n_valid = jnp.clip(blk_cnt - (blk_row - blk_start), 0, MOE_ROWS).astype(jnp.int32)
    used = cnt > 0
    later = (experts[None, :] > experts[:, None]) & used[None, :]
    next_used = jnp.min(jnp.where(later, experts[None, :], N_EXPERTS), axis=1)
    rank = jnp.cumsum(used.astype(jnp.int32)) - 1
    blk_first = (n_valid > 0) & (blk_row == blk_start)
    blk_next = jnp.sum(jnp.where(is_e, next_used, 0), axis=1)
    blk_slot = jnp.sum(jnp.where(is_e, rank, 0), axis=1) % 2
    flags = (blk_first.astype(jnp.int32) * MOE_FIRST + blk_slot * MOE_SLOT
             + (blk_next < N_EXPERTS).astype(jnp.int32) * MOE_HAS_NEXT).astype(jnp.int32)
    next_e = jnp.minimum(blk_next, N_EXPERTS - 1).astype(jnp.int32)
    xs = _sc_scatter_rows(h2_packed, dest[:, 0], dest[:, 1], cap)
    ys = _moe_experts(xs, block_e, n_valid, flags, next_e, wg, wu, wd, l)
    y12 = _sc_gather_rows(ys, jnp.concatenate([dest[:, 0], dest[:, 1]]))
    return _combine(h2, routing, y12, g, bb, alpha)


def _mixing_layer(h, b, s, l, p, lam_init, input_norm=None):
    t = b * s
    alpha = (2 * p['w_in'].shape[0]) ** 0.25
    d = h.shape[1]
    w_in = p['w_in']
    tab_diff = _rope_tables(s, DIFF_QK_DIM)
    tab_dil = _rope_tables(s, HEAD_DIM)
    proj = _proj_branches(h.reshape(b, s, d), w_in, l, tab_diff, tab_dil, input_norm)
    if input_norm:
        h = proj.pop(0).reshape(t, d)
    qkv_na, qkv_diff, qkv_d0, qkv_d1, qkv_d2, u = proj
    y_a = _na_attention(qkv_na, _na_bias_table(p['na_rpb'][l]))
    y_b = _diff_attention(qkv_diff, p['diff_lam'][l], p['diff_subln_g'][l], lam_init)
    y_c = _pool(u, p['pool_w'][l], p['pool_scale'][l])
    dil = [_swa(qkv, s // r) for qkv, (_, r) in zip((qkv_d0, qkv_d1, qkv_d2), DIL_PATTERNS)]
    ys = [a.reshape(t, -1) for a in (y_a, y_b, y_c)]
    wg = lax.slice_in_dim(w_in[l], COL_GATE, COL_GATE + (len(ys) + 1) * d, axis=1).astype(BF16)
    router = _router_params(p['router_group_w'][l], p['router_group_b'][l], p['router_expert_w'][l],
                            p['router_expert_b'][l])
    return _merge(ys, dil, wg, p['b_gate'][l], p['w_branch'][l].astype(BF16), p['w_out'][l].astype(BF16), h,
                  p['ln1_g'][l], p['ln1_b'][l], router, alpha)


def kernel(x, emb_ln_g, emb_ln_b, w_in, b_gate, na_rpb, diff_lam, diff_subln_g, pool_w, pool_scale,
           w_branch, w_out, ln1_g, ln1_b, router_group_w, router_group_b, router_expert_w,
           router_expert_b, expert_w_gate, expert_w_up, expert_w_down, ln2_g, ln2_b):
    b, s, d = x.shape
    depth = w_in.shape[0]
    alpha = (2 * depth) ** 0.25
    p = dict(w_in=w_in, b_gate=b_gate, na_rpb=na_rpb, diff_lam=diff_lam, diff_subln_g=diff_subln_g,
             pool_w=pool_w, pool_scale=pool_scale, w_branch=w_branch, w_out=w_out, ln1_g=ln1_g, ln1_b=ln1_b,
             router_group_w=router_group_w, router_group_b=router_group_b,
             router_expert_w=router_expert_w, router_expert_b=router_expert_b)
    h = x.reshape(b * s, d)
    for l in range(depth):
        lam_init = 0.8 - 0.6 * math.exp(-0.3 * l)
        h, h_packed, routing, counts = _mixing_layer(h, b, s, l, p, lam_init,
                                                     (emb_ln_g, emb_ln_b) if l == 0 else None)
        h = _moe(h, h_packed, routing, counts, expert_w_gate, expert_w_up, expert_w_down, l,
                 ln2_g[l], ln2_b[l], alpha)
    return h.reshape(b, s, d)
```

```python
import functools
import math

import jax
import jax.numpy as jnp
import numpy as np
from jax import lax
from jax.experimental import pallas as pl
from jax.experimental.pallas import tpu as pltpu
from jax.experimental.pallas import tpu_sc as plsc

F32 = jnp.float32
BF16 = jnp.bfloat16

LANES = 128
GRID_W = 64
HEAD_DIM = 64
ROPE_THETA = 500000.0
LN_EPS = 1e-5
NA_ROWS = 8
NA_COLS = 16
DIFF_QK_DIM = 32
POOL_WINDOWS = (2, 4, 8, 16)
POOL_GROUP = 64
DIL_PATTERNS = ((128, 1), (512, 4), (2048, 16))
DIL_HALF = 64
N_GROUPS = 4
EXPERTS_PER_GROUP = 8
N_EXPERTS = N_GROUPS * EXPERTS_PER_GROUP
TOP_K = 2
MOE_ROWS = 512
NEG = -1e30

COL_NA = 0
COL_DIFF = 768
COL_POOL = 1536
COL_DIL = 1792
COL_GATE = 4096

V7X_VMEM_BYTES = 64 * 1024 * 1024
VMEM_LIMIT = V7X_VMEM_BYTES - 8 * 1024 * 1024


def _cparams(sem):
    return pltpu.CompilerParams(dimension_semantics=sem, vmem_limit_bytes=VMEM_LIMIT)


def _layer_norm(x, g, b):
    mu = jnp.mean(x, axis=-1, keepdims=True)
    xc = x - mu
    var = jnp.mean(xc * xc, axis=-1, keepdims=True)
    return xc * lax.rsqrt(var + LN_EPS) * g + b


def _sigmoid(x):
    return 0.5 * jnp.tanh(0.5 * x) + 0.5


def _dot(a, b):
    return jnp.dot(a, b, preferred_element_type=F32)


def _dot_nt(a, b):
    return lax.dot_general(a, b, (((1,), (1,)), ((), ())), preferred_element_type=F32)


def _lane(shape):
    return lax.broadcasted_iota(jnp.int32, shape, len(shape) - 1)


HI16 = -65536


def _pack_bf16_pairs(x):
    w = x.shape[1] // 2
    hi = lax.bitcast_convert_type(x[:, :w].astype(BF16).astype(F32), jnp.int32)
    lo = lax.bitcast_convert_type(x[:, w:].astype(BF16).astype(F32), jnp.int32)
    return (hi & HI16) | lax.shift_right_logical(lo, 16)


def _unpack_bf16_pairs(p):
    hi = lax.bitcast_convert_type(p & HI16, F32)
    lo = lax.bitcast_convert_type(lax.shift_left(p, 16), F32)
    return jnp.concatenate([hi, lo], axis=1)


QKV_W = 6 * LANES
WCOL = 256


QKV_TM = 512
QKV_SEGMENTS = ((COL_NA, 1, 0, HEAD_DIM), (COL_DIFF, 1, DIFF_QK_DIM // 8, DIFF_QK_DIM),
                (COL_DIL, 1, HEAD_DIM // 8, HEAD_DIM), (COL_DIL + QKV_W, 4, HEAD_DIM // 8, HEAD_DIM),
                (COL_DIL + 2 * QKV_W, 16, HEAD_DIM // 8, HEAD_DIM))


def _qkv_epilogue(z, t_ref, o_ref, zs_ref, r, half, q_scale):
    tm = z.shape[0]
    n = tm // r
    for c in range(QKV_W // LANES):
        blk = z[:, c * LANES:(c + 1) * LANES]
        if half and c < 4:
            blk = _rope(blk, t_ref, half)
        if c < 2:
            blk = blk * q_scale
        if r == 1:
            o_ref[0, 0, :, c * LANES:(c + 1) * LANES] = blk.astype(BF16)
        else:
            zs_ref[c] = blk
            for m in range(r):
                o_ref[0, m, :, c * LANES:(c + 1) * LANES] = zs_ref[
                    c, pl.ds(m, n, stride=r), :].astype(BF16)


def _proj_kernel(*refs, norm_input):
    if norm_input:
        x_ref, g_ref, b_ref, w_ref, td_ref, tl_ref, h_ref = refs[:7]
        refs = refs[7:]
    else:
        x_ref, w_ref, td_ref, tl_ref = refs[:4]
        refs = refs[4:]
    na_ref, df_ref, d0_ref, d1_ref, d2_ref, u_ref, wb_ref, zs1_ref, zs2_ref = refs

    @pl.when((pl.program_id(0) == 0) & (pl.program_id(1) == 0))
    def _():
        wb_ref[...] = w_ref[...].astype(BF16)

    x = x_ref[0]
    if norm_input:
        x = _layer_norm(x, g_ref[...], b_ref[...])
        h_ref[0] = x
    xb = x.astype(BF16)
    outs = (na_ref, df_ref, d0_ref, d1_ref, d2_ref)
    stage = (None, None, None, zs1_ref, zs2_ref)
    for (col0, r, half, head_w), o_ref, zs_ref in zip(QKV_SEGMENTS, outs, stage):
        z = _dot(xb, wb_ref[:, col0:col0 + QKV_W])
        t_ref = td_ref if head_w == DIFF_QK_DIM else tl_ref
        _qkv_epilogue(z, t_ref, o_ref, zs_ref, r, half, head_w ** -0.5 * LOG2E)
    u_ref[0] = _dot(xb, wb_ref[:, COL_POOL:COL_POOL + WCOL])


def _proj_branches(x3, w_in, l, tab_diff, tab_dil, norm=None):
    b, s, d = x3.shape
    tm = QKV_TM
    qkv_spec = lambda r: pl.BlockSpec((1, r, tm // r, QKV_W), lambda bi, i: (bi, 0, i, 0))
    qkv_shape = lambda r: jax.ShapeDtypeStruct((b, r, s // r, QKV_W), BF16)
    tab_spec = pl.BlockSpec((3, tm, LANES), lambda bi, i: (0, i, 0))
    row_spec = pl.BlockSpec((1, tm, d), lambda bi, i: (bi, i, 0))
    vec = pl.BlockSpec((1, d), lambda bi, i: (0, 0))
    w_spec = pl.BlockSpec((None, d, COL_GATE), lambda bi, i: (l, 0, 0), pipeline_mode=pl.Buffered(1))
    rs = [seg[1] for seg in QKV_SEGMENTS]
    in_specs = [row_spec] + ([vec, vec] if norm else []) + [w_spec, tab_spec, tab_spec]
    args = [x3] + ([a.reshape(1, d) for a in norm] if norm else []) + [w_in, tab_diff, tab_dil]
    outs = pl.pallas_call(
        functools.partial(_proj_kernel, norm_input=bool(norm)),
        grid=(b, s // tm),
        in_specs=in_specs,
        out_specs=([row_spec] if norm else []) + [qkv_spec(r) for r in rs] + [
            pl.BlockSpec((1, tm, WCOL), lambda bi, i: (bi, i, 0))],
        out_shape=([jax.ShapeDtypeStruct((b, s, d), F32)] if norm else []) + [
            qkv_shape(r) for r in rs] + [jax.ShapeDtypeStruct((b, s, WCOL), F32)],
        scratch_shapes=[pltpu.VMEM((d, COL_GATE), BF16)] + [pltpu.VMEM((QKV_W // LANES, tm, LANES), F32)] * 2,
        compiler_params=_cparams(("arbitrary", "arbitrary")),
        name="proj_branches",
    )(*args)
    n_qkv = len(rs)
    head = list(outs[:-n_qkv - 1])
    return head + [o.reshape(b, s, QKV_W) for o in outs[-n_qkv - 1:-1]] + [outs[-1]]


def _rope_tables(seq, head_w):
    rot = head_w // 4
    half = rot // 2
    inv_freq = jnp.exp(jnp.arange(half, dtype=F32) * (-2.0 * math.log(ROPE_THETA) / rot))
    ang = jnp.arange(seq, dtype=jnp.int32).astype(F32)[:, None] * inv_freq[None, :]
    cos, sin = jnp.cos(ang), jnp.sin(ang)
    zero = jnp.zeros((seq, head_w - rot), F32)
    zh = jnp.zeros((seq, half), F32)
    t0 = jnp.concatenate([cos, cos, jnp.ones((seq, head_w - rot), F32)], axis=1)
    t1 = jnp.concatenate([-sin, zh, zero], axis=1)
    t2 = jnp.concatenate([zh, sin, zero], axis=1)
    reps = LANES // head_w
    return jnp.stack([jnp.tile(t0, (1, reps)), jnp.tile(t1, (1, reps)), jnp.tile(t2, (1, reps))])


def _rope(x, t_ref, half):
    return (x * t_ref[0] + pltpu.roll(x, LANES - half, 1) * t_ref[1]
            + pltpu.roll(x, half, 1) * t_ref[2])


LOG2E = math.log2(math.e)


def _na_bias_table(rpb):
    kr, kc = NA_ROWS, NA_COLS
    n_heads = rpb.shape[0]
    col = np.arange(GRID_W)
    col_start = np.clip(col - kc // 2, 0, GRID_W - kc)
    in_win = (col[None, :] >= col_start[:, None]) & (col[None, :] < col_start[:, None] + kc)
    dc = np.clip(col[None, :] - col[:, None] + kc - 1, 0, 2 * kc - 2)
    sel_c = ((dc[..., None] == np.arange(2 * kc - 1)) & in_win[..., None]).astype(np.float32)
    rp = rpb.astype(F32).reshape(n_heads // 2, 2, 2 * kr - 1, 2 * kc - 1) * LOG2E
    tiles = jnp.einsum('phaj,qxj->phaqx', rp, jnp.asarray(sel_c), precision=lax.Precision.HIGHEST)
    tiles = tiles + jnp.asarray(np.where(in_win, 0.0, NEG).astype(np.float32))
    tiles = jnp.concatenate([tiles, jnp.full_like(tiles[:, :, :1], NEG)], axis=2)
    return jnp.concatenate([tiles, tiles], axis=-1)


NA_QROWS = 4
NA_KROWS = 12
NA_MASKED = 2 * NA_ROWS - 1


def _na_row_offsets(n_rows):
    table = []
    for blk in (0, 1, n_rows // NA_QROWS - 1):
        ws = _na_window_start(blk, n_rows)
        rows = []
        for i in range(NA_QROWS):
            r = blk * NA_QROWS + i
            start = min(max(r - NA_ROWS // 2, 0), n_rows - NA_ROWS)
            rows.append([ws + kk - r + NA_ROWS - 1 if start <= ws + kk < start + NA_ROWS else NA_MASKED
                         for kk in range(NA_KROWS)])
        table.append(rows)
    return table


def _na_window_start(blk, n_rows):
    lo = blk * NA_QROWS - NA_ROWS // 2
    hi = n_rows - NA_KROWS
    if isinstance(blk, int):
        return min(max(lo, 0), hi)
    return jnp.clip(lo, 0, hi)


def _stack_heads(q, lane):
    zero = jnp.zeros_like(q)
    return jnp.concatenate([jnp.where(lane < HEAD_DIM, q, zero), jnp.where(lane < HEAD_DIM, zero, q)],
                           axis=0)


def _na_kernel(q_ref, k_ref, v_ref, tile_ref, o_ref, bias_ref, *, blocks_per_step, n_rows):
    step = pl.program_id(2)
    nq = NA_QROWS * GRID_W
    kwin = NA_KROWS * GRID_W
    lane = _lane((nq, LANES))
    last_blk = n_rows // NA_QROWS - 1

    @pl.when(step == 0)
    def _():
        low = _lane((GRID_W, LANES)) < GRID_W
        for case, rows in enumerate(_na_row_offsets(n_rows)):
            for hh in range(2):
                for i, offs in enumerate(rows):
                    r0 = (hh * NA_QROWS + i) * GRID_W
                    for kp in range(NA_KROWS // 2):
                        bias_ref[case, r0:r0 + GRID_W, kp * LANES:(kp + 1) * LANES] = jnp.where(
                            low, tile_ref[0, hh, offs[2 * kp]], tile_ref[0, hh, offs[2 * kp + 1]])

    for i in range(blocks_per_step):
        blk = step * blocks_per_step + i
        case = jnp.where(blk == 0, 0, jnp.where(blk == last_blk, 2, 1))
        k0 = pl.multiple_of(_na_window_start(blk, n_rows) * GRID_W, GRID_W)
        q2 = _stack_heads(q_ref[0, i * nq:(i + 1) * nq, :], lane)
        sc = _dot_nt(q2, k_ref[0, pl.ds(k0, kwin), :]) + bias_ref[case]
        m = jnp.max(sc, axis=-1, keepdims=True)
        e = jnp.exp2(sc - m)
        den = jnp.sum(e, axis=-1, keepdims=True)
        pv = _dot(e.astype(BF16), v_ref[0, pl.ds(k0, kwin), :]) / den
        o = jnp.where(lane < HEAD_DIM, pv[0:nq], pv[nq:2 * nq])
        o_ref[0, i * nq:(i + 1) * nq, :] = o.astype(BF16)


def _na_attention(qkv, tiles):
    b, s, _ = qkv.shape
    n_rows = s // GRID_W
    bps = 8
    rps = bps * NA_QROWS
    tq = rps * GRID_W
    return pl.pallas_call(
        functools.partial(_na_kernel, blocks_per_step=bps, n_rows=n_rows),
        grid=(b, 2, n_rows // rps),
        in_specs=[pl.BlockSpec((1, tq, LANES), lambda bi, hp, i: (bi, i, hp)),
                  pl.BlockSpec((1, s, LANES), lambda bi, hp, i: (bi, 0, 2 + hp)),
                  pl.BlockSpec((1, s, LANES), lambda bi, hp, i: (bi, 0, 4 + hp)),
                  pl.BlockSpec((1, 2, NA_MASKED + 1, GRID_W, LANES), lambda bi, hp, i: (hp, 0, 0, 0, 0))],
        out_specs=pl.BlockSpec((1, tq, LANES), lambda bi, hp, i: (bi, i, hp)),
        out_shape=jax.ShapeDtypeStruct((b, s, 2 * LANES), BF16),
        scratch_shapes=[pltpu.VMEM((3, 2 * NA_QROWS * GRID_W, NA_KROWS * GRID_W), F32)],
        compiler_params=_cparams(("parallel", "parallel", "arbitrary")),
        name="na_attn",
    )(qkv, qkv, qkv, tiles)


def _diff_kernel(lam_ref, q_ref, k_ref, v_ref, g_ref, o_ref, *, lam_init, rows):
    tq = q_ref.shape[1]
    lane = _lane((rows, LANES))
    dl = lam_ref[...]
    lam = (jnp.exp(jnp.sum(dl[0:1] * dl[1:2], axis=-1, keepdims=True))
           - jnp.exp(jnp.sum(dl[2:3] * dl[3:4], axis=-1, keepdims=True)) + lam_init)
    v = v_ref[0]
    v_lane = _lane(v.shape)
    one = jnp.ones_like(v)
    v_ext = (jnp.where(v_lane < HEAD_DIM, v, one), jnp.where(v_lane < HEAD_DIM, one, v))
    in_h0 = lane < HEAD_DIM
    for r0 in range(0, tq, rows):
        q = q_ref[0, r0:r0 + rows, :]
        zero = jnp.zeros_like(q)
        pv = []
        for lo in range(0, LANES, DIFF_QK_DIM):
            qm = jnp.where((lane >= lo) & (lane < lo + DIFF_QK_DIM), q, zero)
            sc = _dot_nt(qm, k_ref[0])
            e = jnp.exp2(sc - jnp.max(sc, axis=-1, keepdims=True))
            ev = _dot(e.astype(BF16), v_ext[lo // HEAD_DIM])
            pv.append(ev / pltpu.roll(ev, HEAD_DIM, 1))
        o = jnp.where(in_h0, pv[0] - lam * pv[1], pv[2] - lam * pv[3])
        o2 = o * o
        ms0 = jnp.sum(jnp.where(in_h0, o2, 0.0), axis=-1, keepdims=True) / HEAD_DIM
        ms1 = jnp.sum(jnp.where(in_h0, 0.0, o2), axis=-1, keepdims=True) / HEAD_DIM
        ms = jnp.where(in_h0, ms0, ms1)
        o = o * lax.rsqrt(ms + LN_EPS) * g_ref[...] * (1.0 - lam_init)
        o_ref[0, r0:r0 + rows, :] = o.astype(BF16)


def _diff_attention(qkv, diff_lam, subln_g, lam_init):
    b, s, _ = qkv.shape
    tq = 1024
    g2 = jnp.tile(subln_g.reshape(1, HEAD_DIM), (1, 2))
    return pl.pallas_call(
        functools.partial(_diff_kernel, lam_init=lam_init, rows=512),
        grid=(b, 2, s // tq),
        in_specs=[pl.BlockSpec((4, DIFF_QK_DIM), lambda bi, hp, i: (0, 0)),
                  pl.BlockSpec((1, tq, LANES), lambda bi, hp, i: (bi, i, hp)),
                  pl.BlockSpec((1, s, LANES), lambda bi, hp, i: (bi, 0, 2 + hp)),
                  pl.BlockSpec((1, s, LANES), lambda bi, hp, i: (bi, 0, 4 + hp)),
                  pl.BlockSpec((1, LANES), lambda bi, hp, i: (0, 0))],
        out_specs=pl.BlockSpec((1, tq, LANES), lambda bi, hp, i: (bi, i, hp)),
        out_shape=jax.ShapeDtypeStruct((b, s, 2 * LANES), BF16),
        compiler_params=_cparams(("parallel", "parallel", "arbitrary")),
        name="diff_attn",
    )(diff_lam, qkv, qkv, qkv, g2)


POOL_PAD = 16
POOL_CHUNK = 512
POOL_HALO = 8


def _pool_kernel(u_ref, w_ref, sc_ref, o_ref, p_ref):
    s = u_ref.shape[1]
    width = u_ref.shape[2]
    p_ref[0:POOL_PAD, :] = jnp.zeros((POOL_PAD, width), F32)
    p_ref[POOL_PAD + s:POOL_PAD + s + POOL_PAD, :] = jnp.zeros((POOL_PAD, width), F32)
    p_ref[POOL_PAD:POOL_PAD + s, :] = u_ref[0]
    n = POOL_CHUNK + 2 * POOL_HALO
    lane = _lane((POOL_CHUNK, width))
    row = lax.broadcasted_iota(jnp.int32, (POOL_CHUNK, width), 0)
    w_of_lane = jnp.where(lane < POOL_GROUP, 2, jnp.where(lane < 2 * POOL_GROUP, 4,
                          jnp.where(lane < 3 * POOL_GROUP, 8, 16)))

    def body(ci, carry):
        c0 = pl.multiple_of(ci * POOL_CHUNK, POOL_CHUNK)
        x = p_ref[pl.ds(c0 + POOL_PAD - POOL_HALO, n), :]
        w2 = x + pltpu.roll(x, 1, 0)
        w4 = pltpu.roll(w2, 1, 0) + pltpu.roll(w2, n - 1, 0)
        w8 = pltpu.roll(w4, 2, 0) + pltpu.roll(w4, n - 2, 0)
        w16 = pltpu.roll(w8, 4, 0) + pltpu.roll(w8, n - 4, 0)
        u = x[POOL_HALO:POOL_HALO + POOL_CHUNK]
        wsum = jnp.where(lane < POOL_GROUP, w2[POOL_HALO:POOL_HALO + POOL_CHUNK],
                         jnp.where(lane < 2 * POOL_GROUP, w4[POOL_HALO:POOL_HALO + POOL_CHUNK],
                                   jnp.where(lane < 3 * POOL_GROUP, w8[POOL_HALO:POOL_HALO + POOL_CHUNK],
                                             w16[POOL_HALO:POOL_HALO + POOL_CHUNK])))
        t = row + c0
        half_w = w_of_lane // 2
        lo = jnp.maximum(t - half_w, 0)
        hi = jnp.minimum(t + w_of_lane - 1 - half_w, s - 1)
        cnt = (hi - lo + 1).astype(F32)
        dlt = wsum / cnt - u
        y = _dot(dlt.astype(BF16), w_ref[...]) * sc_ref[...]
        o_ref[0, pl.ds(c0, POOL_CHUNK), :] = y.astype(BF16)
        return carry

    lax.fori_loop(0, s // POOL_CHUNK, body, 0)


def _pool(u3, pool_w, pool_scale):
    b, s, width = u3.shape
    wbd = jax.scipy.linalg.block_diag(*[pool_w[g] for g in range(len(POOL_WINDOWS))]).astype(BF16)
    return pl.pallas_call(
        _pool_kernel,
        grid=(b,),
        in_specs=[pl.BlockSpec((1, s, width), lambda bi: (bi, 0, 0)),
                  pl.BlockSpec((width, width), lambda bi: (0, 0)),
                  pl.BlockSpec((1, width), lambda bi: (0, 0))],
        out_specs=pl.BlockSpec((1, s, width), lambda bi: (bi, 0, 0)),
        out_shape=jax.ShapeDtypeStruct((b, s, width), BF16),
        scratch_shapes=[pltpu.VMEM((s + 2 * POOL_PAD, width), F32)],
        compiler_params=_cparams(("parallel",)),
        name="pool",
    )(u3, wbd, pool_scale.reshape(1, width))


SWA_Q = 128
SWA_BAND = SWA_Q + 2 * DIL_HALF


def _swa_kernel(q_ref, k_ref, v_ref, o_ref, l_ref, *, length):
    lane = _lane((SWA_Q, LANES))
    rel = (lax.broadcasted_iota(jnp.int32, (2 * SWA_Q, SWA_BAND), 1)
           - (lax.broadcasted_iota(jnp.int32, (2 * SWA_Q, SWA_BAND), 0) & (SWA_Q - 1)))
    masks = {}
    for i in range(q_ref.shape[1] // SWA_Q):
        r0 = i * SWA_Q
        l0 = r0 % length
        lo = r0 - l0 + min(max(l0 - DIL_HALF, 0), length - SWA_BAND)
        if lo - r0 not in masks:
            d = rel + (lo - r0)
            masks[lo - r0] = jnp.where((d >= -DIL_HALF) & (d <= DIL_HALF), 0.0, NEG)
        q2 = _stack_heads(q_ref[0, r0:r0 + SWA_Q, :], lane)
        sc = _dot_nt(q2, k_ref[0, lo:lo + SWA_BAND, :]) + masks[lo - r0]
        m = jnp.max(sc, axis=-1, keepdims=True)
        e = jnp.exp2(sc - m)
        den = jnp.sum(e, axis=-1, keepdims=True)
        pv = _dot(e.astype(BF16), v_ref[0, lo:lo + SWA_BAND, :]) / den
        lse2 = m + jnp.log2(den)
        o_ref[0, i * SWA_Q:(i + 1) * SWA_Q, :] = jnp.where(lane < HEAD_DIM, pv[0:SWA_Q], pv[SWA_Q:])
        l_ref[0, i * SWA_Q:(i + 1) * SWA_Q, :] = jnp.where(lane < HEAD_DIM, lse2[0:SWA_Q], lse2[SWA_Q:])


def _swa(qkv, length):
    b, s, _ = qkv.shape
    shp = jax.ShapeDtypeStruct((b, s, 2 * LANES), F32)
    seq = lambda col: pl.BlockSpec((1, s, LANES), lambda bi, hp: (bi, 0, col + hp))
    return pl.pallas_call(
        functools.partial(_swa_kernel, length=length),
        grid=(b, 2),
        in_specs=[seq(0), seq(2), seq(4)],
        out_specs=[seq(0), seq(0)],
        out_shape=[shp, shp],
        compiler_params=_cparams(("parallel", "parallel")),
        name=f"swa_l{length}",
    )(qkv, qkv, qkv)


def _dil_combine(dil_refs, stage_refs, tm):
    (o0_ref, l0_ref), (o1_ref, l1_ref), (o2_ref, l2_ref) = dil_refs
    so1, sl1, so2, sl2 = stage_refs
    r1 = DIL_PATTERNS[1][1]
    r2 = DIL_PATTERNS[2][1]
    halves = []
    for hp in range(2):
        cols = slice(hp * LANES, (hp + 1) * LANES)
        for m in range(r1):
            so1[pl.ds(m, tm // r1, stride=r1), :] = o1_ref[0, m, :, cols]
            sl1[pl.ds(m, tm // r1, stride=r1), :] = l1_ref[0, m, :, cols]
        for m in range(r2):
            so2[pl.ds(m, tm // r2, stride=r2), :] = o2_ref[0, m, :, cols]
            sl2[pl.ds(m, tm // r2, stride=r2), :] = l2_ref[0, m, :, cols]
        l0, l1, l2 = l0_ref[0, 0, :, cols], sl1[...], sl2[...]
        mx = jnp.maximum(jnp.maximum(l0, l1), l2)
        e0, e1, e2 = jnp.exp2(l0 - mx), jnp.exp2(l1 - mx), jnp.exp2(l2 - mx)
        den = e0 + e1 + e2
        halves.append((e0 / den) * o0_ref[0, 0, :, cols] + (e1 / den) * so1[...] + (e2 / den) * so2[...])
    return jnp.concatenate(halves, axis=1)


def _merge_kernel(ya_ref, yb_ref, yc_ref, o0_ref, l0_ref, o1_ref, l1_ref, o2_ref, l2_ref,
                  wg_ref, bg_ref, wb_ref, wo_ref, h_ref, g_ref, b_ref, wr_ref, br_ref,
                  o_ref, op_ref, r_ref, cnt_ref, so1, sl1, so2, sl2, carry, before, *, alpha):
    tm, d = h_ref.shape

    @pl.when(pl.program_id(0) == 0)
    def _():
        _route_init(carry, before)

    hb = h_ref[...].astype(BF16)
    y_d = _dil_combine(((o0_ref, l0_ref), (o1_ref, l1_ref), (o2_ref, l2_ref)),
                       (so1, sl1, so2, sl2), tm).astype(BF16)
    merged = None
    for n, y in enumerate((ya_ref[...], yb_ref[...], yc_ref[...], y_d)):
        zg = _dot(hb, wg_ref[:, n * d:(n + 1) * d]) + bg_ref[:, n * d:(n + 1) * d]
        term = _sigmoid(zg) * _dot(y, wb_ref[n])
        merged = term if merged is None else merged + term
    mix = _dot(merged.astype(BF16), wo_ref[...])
    h1 = _layer_norm(alpha * h_ref[...] + mix, g_ref[...], b_ref[...])
    o_ref[...] = h1
    op_ref[...] = _pack_bf16_pairs(h1)
    _route_finish(_route_select(h1, wr_ref, br_ref), r_ref, cnt_ref, carry, before)


def _merge(ys, dil, wg, b_gate, wb, wo, h2, g, bb, router, alpha):
    t, d = h2.shape
    w_route, b_route = router
    b, s, bw = dil[0][0].shape
    nb = len(ys) + 1
    tm = QKV_TM
    nt = s // tm
    yspec = pl.BlockSpec((tm, bw), lambda i: (i, 0))
    vec = pl.BlockSpec((1, d), lambda i: (0, 0))
    dil_args, dil_specs = [], []
    for (o, lse), (_, r) in zip(dil, DIL_PATTERNS):
        spec = pl.BlockSpec((1, r, tm // r, bw), lambda i: (i // nt, 0, i % nt, 0))
        dil_args += [o.reshape(b, r, s // r, bw), lse.reshape(b, r, s // r, bw)]
        dil_specs += [spec, spec]
    return pl.pallas_call(
        functools.partial(_merge_kernel, alpha=alpha),
        grid=(t // tm,),
        in_specs=[yspec, yspec, yspec] + dil_specs + [
                  pl.BlockSpec((d, nb * d), lambda i: (0, 0)),
                  pl.BlockSpec((1, nb * d), lambda i: (0, 0)),
                  pl.BlockSpec((nb, bw, d), lambda i: (0, 0, 0)),
                  pl.BlockSpec((d, d), lambda i: (0, 0)),
                  pl.BlockSpec((tm, d), lambda i: (i, 0)),
                  vec, vec,
                  pl.BlockSpec((LANES, d), lambda i: (0, 0)),
                  pl.BlockSpec((LANES, 1), lambda i: (0, 0))],
        out_specs=[pl.BlockSpec((tm, d), lambda i: (i, 0)),
                   pl.BlockSpec((tm, d // 2), lambda i: (i, 0)),
                   pl.BlockSpec((tm, LANES), lambda i: (i, 0)),
                   pl.BlockSpec((LANES, 1), lambda i: (0, 0))],
        out_shape=[jax.ShapeDtypeStruct((t, d), F32), jax.ShapeDtypeStruct((t, d // 2), jnp.int32),
                   jax.ShapeDtypeStruct((t, LANES), F32), jax.ShapeDtypeStruct((LANES, 1), F32)],
        scratch_shapes=[pltpu.VMEM((tm, LANES), F32)] * 4 + [pltpu.VMEM((LANES, 1), F32),
                                                               pltpu.VMEM((tm, tm), BF16)],
        compiler_params=_cparams(("arbitrary",)),
        name="merge",
    )(*ys, *dil_args, wg, b_gate.reshape(1, nb * d), wb, wo, h2, g.reshape(1, d), bb.reshape(1, d),
      w_route, b_route)


def _route_init(carry, before):
    tm = before.shape[0]
    carry[...] = jnp.zeros_like(carry)
    before[...] = (lax.broadcasted_iota(jnp.int32, (tm, tm), 0)
                   < lax.broadcasted_iota(jnp.int32, (tm, tm), 1)).astype(BF16)


def _route_select(h, w_ref, b_ref):
    tm = h.shape[0]
    logits = lax.dot_general(w_ref[...], h, (((1,), (1,)), ((), ())),
                             preferred_element_type=F32,
                             precision=lax.Precision.HIGHEST) + b_ref[...]
    row = lax.broadcasted_iota(jnp.int32, (LANES, tm), 0)
    big = jnp.int32(1 << 20)
    is_g = row < N_GROUPS
    gl = jnp.where(is_g, logits, -jnp.inf)
    gmax = jnp.max(gl, axis=0, keepdims=True)
    gsel = jnp.min(jnp.where(is_g & (gl == gmax), row, big), axis=0, keepdims=True)
    pg = 1.0 / jnp.sum(jnp.exp(gl - gmax), axis=0, keepdims=True)
    e_lo = N_GROUPS + gsel * EXPERTS_PER_GROUP
    in_grp = (row >= e_lo) & (row < e_lo + EXPERTS_PER_GROUP)
    el = jnp.where(in_grp, logits, -jnp.inf)
    v1 = jnp.max(el, axis=0, keepdims=True)
    i1 = jnp.min(jnp.where(in_grp & (el == v1), row, big), axis=0, keepdims=True)
    el2 = jnp.where(row == i1, -jnp.inf, el)
    v2 = jnp.max(el2, axis=0, keepdims=True)
    i2 = jnp.min(jnp.where(in_grp & (row != i1) & (el2 == v2), row, big), axis=0, keepdims=True)
    t2 = jnp.exp(v2 - v1)
    g1 = pg / (1.0 + t2)
    g2 = pg * t2 / (1.0 + t2)
    return dict(row=row, i1=i1, i2=i2, g1=g1, g2=g2,
                oh1=row == i1, oh2=row == i2)


def _route_finish(sel, o_ref, cnt_ref, carry, before):
    row, i1, i2, g1, g2, oh1, oh2 = (sel[k] for k in ('row', 'i1', 'i2', 'g1', 'g2', 'oh1', 'oh2'))
    oh1b = oh1.astype(BF16)
    oh2b = oh2.astype(BF16)
    c0 = carry[...]
    c1 = c0 + jnp.sum(oh1b.astype(F32), axis=1, keepdims=True)
    rank1 = jnp.sum(jnp.where(oh1, _dot(oh1b, before[...]) + c0, 0.0), axis=0, keepdims=True)
    rank2 = jnp.sum(jnp.where(oh2, _dot(oh2b, before[...]) + c1, 0.0), axis=0, keepdims=True)
    c2 = c1 + jnp.sum(oh2b.astype(F32), axis=1, keepdims=True)
    carry[...] = c2
    cnt_ref[...] = c2
    e1 = (i1 - N_GROUPS).astype(F32)
    e2 = (i2 - N_GROUPS).astype(F32)
    out_t = jnp.where(row == 0, e1, jnp.where(row == 1, e2, jnp.where(row == 2, g1, jnp.where(
        row == 3, g2, jnp.where(row == 4, rank1, jnp.where(row == 5, rank2, 0.0))))))
    o_ref[...] = out_t.T


def _router_params(rg_w, rg_b, re_w, re_b):
    d = rg_w.shape[0]
    pad = LANES - N_GROUPS - N_EXPERTS
    w_t = jnp.concatenate([rg_w.T, re_w.T, jnp.zeros((pad, d), F32)], axis=0)
    bias = jnp.concatenate([rg_b, re_b, jnp.zeros((pad,), F32)]).reshape(LANES, 1)
    return w_t, bias


MOE_FIRST, MOE_SLOT, MOE_HAS_NEXT = 1, 2, 4


def _moe_kernel(be_ref, nv_ref, fl_ref, nx_ref, x_ref, wg_hbm, wu_hbm, wd_hbm, o_ref,
                wgb, wub, wdb, stage_g, stage_u, stage_d, sem, *, layer):
    j = pl.program_id(0)
    n_valid = nv_ref[j]
    flags = fl_ref[j]
    slot = (flags // MOE_SLOT) % 2

    def weight_copies(e, s):
        return [pltpu.make_async_copy(w_hbm.at[layer, e], stage.at[s], sem.at[s, k])
                for k, (w_hbm, stage) in enumerate(((wg_hbm, stage_g), (wu_hbm, stage_u),
                                                    (wd_hbm, stage_d)))]

    @pl.when(flags % 2 == MOE_FIRST)
    def _():
        @pl.when(j == 0)
        def _():
            for cp in weight_copies(be_ref[j], slot):
                cp.start()

        for cp in weight_copies(be_ref[j], slot):
            cp.wait()

        @pl.when((flags // MOE_HAS_NEXT) % 2 == 1)
        def _():
            for cp in weight_copies(nx_ref[j], 1 - slot):
                cp.start()

        wgb[...] = stage_g[slot].astype(BF16)
        wub[...] = stage_u[slot].astype(BF16)
        wdb[...] = stage_d[slot].astype(BF16)

    @pl.when(n_valid > 0)
    def _():
        row = lax.broadcasted_iota(jnp.int32, x_ref.shape, 0)
        x = _unpack_bf16_pairs(jnp.where(row < n_valid, x_ref[...], 0)).astype(BF16)
        g = _dot(x, wgb[...])
        u = _dot(x, wub[...])
        hmid = (g * _sigmoid(g)) * u
        o_ref[...] = _pack_bf16_pairs(_dot(hmid.astype(BF16), wdb[...]))

    @pl.when(n_valid <= 0)
    def _():
        o_ref[...] = jnp.zeros_like(o_ref)


def _moe_experts(xs, block_e, n_valid, flags, next_e, wg, wu, wd, l):
    cap, dp = xs.shape
    d = 2 * dp
    de = wg.shape[3]
    n_blocks = cap // MOE_ROWS
    hbm = pl.BlockSpec(memory_space=pl.ANY)
    grid_spec = pltpu.PrefetchScalarGridSpec(
        num_scalar_prefetch=4,
        grid=(n_blocks,),
        in_specs=[pl.BlockSpec((MOE_ROWS, dp), lambda j, *_: (j, 0)), hbm, hbm, hbm],
        out_specs=pl.BlockSpec((MOE_ROWS, dp), lambda j, *_: (j, 0)),
        scratch_shapes=[pltpu.VMEM((d, de), BF16), pltpu.VMEM((d, de), BF16), pltpu.VMEM((de, d), BF16),
                        pltpu.VMEM((2, d, de), F32), pltpu.VMEM((2, d, de), F32),
                        pltpu.VMEM((2, de, d), F32), pltpu.SemaphoreType.DMA((2, 3))],
    )
    return pl.pallas_call(
        functools.partial(_moe_kernel, layer=l),
        grid_spec=grid_spec,
        out_shape=jax.ShapeDtypeStruct((cap, dp), jnp.int32),
        compiler_params=_cparams(("arbitrary",)),
        name="moe_experts",
    )(block_e, n_valid, flags, next_e, xs, wg, wu, wd)


SC_CORES = 2
SC_SUBCORES = 16
SC_WORKERS = SC_CORES * SC_SUBCORES
SC_CHUNK = 64


def _sc_mesh():
    return plsc.VectorSubcoreMesh(core_axis_name="c", subcore_axis_name="s",
                                  num_cores=SC_CORES, num_subcores=SC_SUBCORES)


def _sc_scatter_rows(src, idx0, idx1, n_out):
    t, d = src.shape
    per_w = t // SC_WORKERS
    n_chunks = per_w // SC_CHUNK
    idx_shape = (SC_WORKERS, n_chunks, SC_CHUNK)

    @functools.partial(
        pl.kernel, mesh=_sc_mesh(),
        out_type=jax.ShapeDtypeStruct((n_out, d), src.dtype),
        scratch_types=[pltpu.VMEM((n_chunks, SC_CHUNK), jnp.int32),
                       pltpu.VMEM((n_chunks, SC_CHUNK), jnp.int32),
                       pltpu.VMEM((2, SC_CHUNK, d), src.dtype),
                       pltpu.SemaphoreType.DMA((2,)), pltpu.SemaphoreType.DMA((2,))],
        name="sc_dispatch",
    )
    def k(src_hbm, i0_hbm, i1_hbm, out_hbm, i0_v, i1_v, rows_v, sem_in, sem_out):
        wid = lax.axis_index("s") * SC_CORES + lax.axis_index("c")
        base = wid * per_w
        pltpu.sync_copy(i0_hbm.at[wid], i0_v)
        pltpu.sync_copy(i1_hbm.at[wid], i1_v)

        def load(ci):
            s = ci % 2
            return pltpu.make_async_copy(src_hbm.at[pl.ds(base + ci * SC_CHUNK, SC_CHUNK)],
                                         rows_v.at[s], sem_in.at[s])

        def scatter(ci, idx_v):
            s = ci % 2
            return pltpu.make_async_copy(rows_v.at[s], out_hbm.at[idx_v.at[ci]], sem_out.at[s])

        load(0).start()
        for ci in range(n_chunks):
            load(ci).wait()
            if ci >= 1:
                scatter(ci - 1, i0_v).wait()
                scatter(ci - 1, i1_v).wait()
            if ci + 1 < n_chunks:
                load(ci + 1).start()
            scatter(ci, i0_v).start()
            scatter(ci, i1_v).start()
        scatter(n_chunks - 1, i0_v).wait()
        scatter(n_chunks - 1, i1_v).wait()

    return k(src, idx0.reshape(idx_shape), idx1.reshape(idx_shape))


def _sc_gather_rows(table, idx):
    n = idx.shape[0]
    d = table.shape[1]
    per_w = n // SC_WORKERS
    n_chunks = per_w // SC_CHUNK

    @functools.partial(
        pl.kernel, mesh=_sc_mesh(),
        out_type=jax.ShapeDtypeStruct((n, d), table.dtype),
        scratch_types=[pltpu.VMEM((n_chunks, SC_CHUNK), jnp.int32),
                       pltpu.VMEM((2, SC_CHUNK, d), table.dtype),
                       pltpu.SemaphoreType.DMA((2,)), pltpu.SemaphoreType.DMA((2,))],
        name="sc_collect",
    )
    def k(table_hbm, idx_hbm, out_hbm, idx_v, rows_v, sem_in, sem_out):
        wid = lax.axis_index("s") * SC_CORES + lax.axis_index("c")
        base = wid * per_w
        pltpu.sync_copy(idx_hbm.at[wid], idx_v)

        def gather(ci):
            s = ci % 2
            return pltpu.make_async_copy(table_hbm.at[idx_v.at[ci]], rows_v.at[s], sem_in.at[s])

        def store(ci):
            s = ci % 2
            return pltpu.make_async_copy(rows_v.at[s], out_hbm.at[pl.ds(base + ci * SC_CHUNK, SC_CHUNK)],
                                         sem_out.at[s])

        gather(0).start()
        for ci in range(n_chunks):
            gather(ci).wait()
            if ci >= 1:
                store(ci - 1).wait()
            if ci + 1 < n_chunks:
                gather(ci + 1).start()
            store(ci).start()
        store(n_chunks - 1).wait()

    return k(table, idx.reshape(SC_WORKERS, n_chunks, SC_CHUNK))


def _combine_kernel(h_ref, r_ref, y1_ref, y2_ref, g_ref, b_ref, o_ref, *, alpha):
    g1 = r_ref[:, 2:3]
    g2 = r_ref[:, 3:4]
    ffn = g1 * _unpack_bf16_pairs(y1_ref[...]) + g2 * _unpack_bf16_pairs(y2_ref[...])
    o_ref[...] = _layer_norm(alpha * h_ref[...] + ffn, g_ref[...], b_ref[...])


def _combine(h2, routing, y12, g, bb, alpha):
    t, d = h2.shape
    tm = 512
    nt = t // tm
    row = pl.BlockSpec((tm, d), lambda i: (i, 0))
    vec = pl.BlockSpec((1, d), lambda i: (0, 0))
    return pl.pallas_call(
        functools.partial(_combine_kernel, alpha=alpha),
        grid=(nt,),
        in_specs=[row, pl.BlockSpec((tm, LANES), lambda i: (i, 0)),
                  pl.BlockSpec((tm, d // 2), lambda i: (i, 0)),
                  pl.BlockSpec((tm, d // 2), lambda i: (nt + i, 0)), vec, vec],
        out_specs=row,
        out_shape=jax.ShapeDtypeStruct((t, d), F32),
        compiler_params=_cparams(("parallel",)),
        name="combine_ln",
    )(h2, routing, y12, y12, g.reshape(1, d), bb.reshape(1, d))


def _moe(h2, h2_packed, routing, counts, wg, wu, wd, l, g, bb, alpha):
    t, d = h2.shape
    eid = routing[:, 0:2].astype(jnp.int32)
    rank = routing[:, 4:6].astype(jnp.int32)
    cnt = counts[N_GROUPS:N_GROUPS + N_EXPERTS, 0].astype(jnp.int32)
    padded = (cnt + MOE_ROWS - 1) // MOE_ROWS * MOE_ROWS
    pad_end = jnp.cumsum(padded)
    pad_start = pad_end - padded
    experts = jnp.arange(N_EXPERTS, dtype=jnp.int32)
    dest = jnp.sum(jnp.where(eid[..., None] == experts, pad_start, 0), axis=-1) + rank
    n_slots = t * TOP_K
    n_blocks = (n_slots + N_EXPERTS * (MOE_ROWS - 1) + MOE_ROWS - 1) // MOE_ROWS
    cap = n_blocks * MOE_ROWS
    blk_row = jnp.arange(n_blocks, dtype=jnp.int32) * MOE_ROWS
    block_e = jnp.minimum(jnp.sum((pad_end[None, :] <= blk_row[:, None]).astype(jnp.int32), axis=1),
                          N_EXPERTS - 1)
    is_e = block_e[:, None] == experts
    blk_cnt = jnp.sum(jnp.where(is_e, cnt, 0), axis=1)
    blk_start = jnp.sum(jnp.where(is_e, pad_start, 0), axis=1)
    n_valid = jnp.clip(blk_cnt - (blk_row - blk_start), 0, MOE_ROWS).astype(jnp.int32)
    used = cnt > 0
    later = (experts[None, :] > experts[:, None]) & used[None, :]
    next_used = jnp.min(jnp.where(later, experts[None, :], N_EXPERTS), axis=1)
    rank = jnp.cumsum(used.astype(jnp.int32)) - 1
    blk_first = (n_valid > 0) & (blk_row == blk_start)
    blk_next = jnp.sum(jnp.where(is_e, next_used, 0), axis=1)
    blk_slot = jnp.sum(jnp.where(is_e, rank, 0), axis=1) % 2
    flags = (blk_first.astype(jnp.int32) * MOE_FIRST + blk_slot * MOE_SLOT
             + (blk_next < N_EXPERTS).astype(jnp.int32) * MOE_HAS_NEXT).astype(jnp.int32)
    next_e = jnp.minimum(blk_next, N_EXPERTS - 1).astype(jnp.int32)
    xs = _sc_scatter_rows(h2_packed, dest[:, 0], dest[:, 1], cap)
    ys = _moe_experts(xs, block_e, n_valid, flags, next_e, wg, wu, wd, l)
    y12 = _sc_gather_rows(ys, jnp.concatenate([dest[:, 0], dest[:, 1]]))
    return _combine(h2, routing, y12, g, bb, alpha)


def _mixing_layer(h, b, s, l, p, lam_init, input_norm=None):
    t = b * s
    alpha = (2 * p['w_in'].shape[0]) ** 0.25
    d = h.shape[1]
    w_in = p['w_in']
    tab_diff = _rope_tables(s, DIFF_QK_DIM)
    tab_dil = _rope_tables(s, HEAD_DIM)
    proj = _proj_branches(h.reshape(b, s, d), w_in, l, tab_diff, tab_dil, input_norm)
    if input_norm:
        h = proj.pop(0).reshape(t, d)
    qkv_na, qkv_diff, qkv_d0, qkv_d1, qkv_d2, u = proj
    y_a = _na_attention(qkv_na, _na_bias_table(p['na_rpb'][l]))
    y_b = _diff_attention(qkv_diff, p['diff_lam'][l], p['diff_subln_g'][l], lam_init)
    y_c = _pool(u, p['pool_w'][l], p['pool_scale'][l])
    dil = [_swa(qkv, s // r) for qkv, (_, r) in zip((qkv_d0, qkv_d1, qkv_d2), DIL_PATTERNS)]
    ys = [a.reshape(t, -1) for a in (y_a, y_b, y_c)]
    wg = lax.slice_in_dim(w_in[l], COL_GATE, COL_GATE + (len(ys) + 1) * d, axis=1).astype(BF16)
    router = _router_params(p['router_group_w'][l], p['router_group_b'][l], p['router_expert_w'][l],
                            p['router_expert_b'][l])
    return _merge(ys, dil, wg, p['b_gate'][l], p['w_branch'][l].astype(BF16), p['w_out'][l].astype(BF16), h,
                  p['ln1_g'][l], p['ln1_b'][l], router, alpha)


def kernel(x, emb_ln_g, emb_ln_b, w_in, b_gate, na_rpb, diff_lam, diff_subln_g, pool_w, pool_scale,
           w_branch, w_out, ln1_g, ln1_b, router_group_w, router_group_b, router_expert_w,
           router_expert_b, expert_w_gate, expert_w_up, expert_w_down, ln2_g, ln2_b):
    b, s, d = x.shape
    depth = w_in.shape[0]
    alpha = (2 * depth) ** 0.25
    p = dict(w_in=w_in, b_gate=b_gate, na_rpb=na_rpb, diff_lam=diff_lam, diff_subln_g=diff_subln_g,
             pool_w=pool_w, pool_scale=pool_scale, w_branch=w_branch, w_out=w_out, ln1_g=ln1_g, ln1_b=ln1_b,
             router_group_w=router_group_w, router_group_b=router_group_b,
             router_expert_w=router_expert_w, router_expert_b=router_expert_b)
    h = x.reshape(b * s, d)
    for l in range(depth):
        lam_init = 0.8 - 0.6 * math.exp(-0.3 * l)
        h, h_packed, routing, counts = _mixing_layer(h, b, s, l, p, lam_init,
                                                     (emb_ln_g, emb_ln_b) if l == 0 else None)
        h = _moe(h, h_packed, routing, counts, expert_w_gate, expert_w_up, expert_w_down, l,
                 ln2_g[l], ln2_b[l], alpha)
    return h.reshape(b, s, d)
```

```python
import functools
import math

import jax
import jax.numpy as jnp
import numpy as np
from jax import lax
from jax.experimental import pallas as pl
from jax.experimental.pallas import tpu as pltpu
from jax.experimental.pallas import tpu_sc as plsc

F32 = jnp.float32
BF16 = jnp.bfloat16

LANES = 128
GRID_W = 64
HEAD_DIM = 64
ROPE_THETA = 500000.0
LN_EPS = 1e-5
NA_ROWS = 8
NA_COLS = 16
DIFF_QK_DIM = 32
POOL_WINDOWS = (2, 4, 8, 16)
POOL_GROUP = 64
DIL_PATTERNS = ((128, 1), (512, 4), (2048, 16))
DIL_HALF = 64
N_GROUPS = 4
EXPERTS_PER_GROUP = 8
N_EXPERTS = N_GROUPS * EXPERTS_PER_GROUP
TOP_K = 2
MOE_ROWS = 512
NEG = -1e30

COL_NA = 0
COL_DIFF = 768
COL_POOL = 1536
COL_DIL = 1792
COL_GATE = 4096

V7X_VMEM_BYTES = 64 * 1024 * 1024
VMEM_LIMIT = V7X_VMEM_BYTES - 8 * 1024 * 1024


def _cparams(sem):
    return pltpu.CompilerParams(dimension_semantics=sem, vmem_limit_bytes=VMEM_LIMIT)


def _layer_norm(x, g, b):
    mu = jnp.mean(x, axis=-1, keepdims=True)
    xc = x - mu
    var = jnp.mean(xc * xc, axis=-1, keepdims=True)
    return xc * lax.rsqrt(var + LN_EPS) * g + b


def _sigmoid(x):
    return 0.5 * jnp.tanh(0.5 * x) + 0.5


def _dot(a, b):
    return jnp.dot(a, b, preferred_element_type=F32)


def _dot_nt(a, b):
    return lax.dot_general(a, b, (((1,), (1,)), ((), ())), preferred_element_type=F32)


def _lane(shape):
    return lax.broadcasted_iota(jnp.int32, shape, len(shape) - 1)


HI16 = -65536


def _pack_bf16_pairs(x):
    w = x.shape[1] // 2
    hi = lax.bitcast_convert_type(x[:, :w].astype(BF16).astype(F32), jnp.int32)
    lo = lax.bitcast_convert_type(x[:, w:].astype(BF16).astype(F32), jnp.int32)
    return (hi & HI16) | lax.shift_right_logical(lo, 16)


def _unpack_bf16_pairs(p):
    hi = lax.bitcast_convert_type(p & HI16, F32)
    lo = lax.bitcast_convert_type(lax.shift_left(p, 16), F32)
    return jnp.concatenate([hi, lo], axis=1)


QKV_W = 6 * LANES
WCOL = 256


QKV_TM = 512
QKV_SEGMENTS = ((COL_NA, 1, 0, HEAD_DIM), (COL_DIFF, 1, DIFF_QK_DIM // 8, DIFF_QK_DIM),
                (COL_DIL, 1, HEAD_DIM // 8, HEAD_DIM), (COL_DIL + QKV_W, 4, HEAD_DIM // 8, HEAD_DIM),
                (COL_DIL + 2 * QKV_W, 16, HEAD_DIM // 8, HEAD_DIM))


def _qkv_epilogue(z, t_ref, o_ref, zs_ref, r, half, q_scale):
    tm = z.shape[0]
    n = tm // r
    for c in range(QKV_W // LANES):
        blk = z[:, c * LANES:(c + 1) * LANES]
        if half and c < 4:
            blk = _rope(blk, t_ref, half)
        if c < 2:
            blk = blk * q_scale
        if r == 1:
            o_ref[0, 0, :, c * LANES:(c + 1) * LANES] = blk.astype(BF16)
        else:
            zs_ref[c] = blk
            for m in range(r):
                o_ref[0, m, :, c * LANES:(c + 1) * LANES] = zs_ref[
                    c, pl.ds(m, n, stride=r), :].astype(BF16)


def _proj_kernel(*refs, prologue, alpha):
    if prologue == 'moe':
        x_ref, r_ref, y1_ref, y2_ref, g_ref, b_ref, w_ref, td_ref, tl_ref, h_ref = refs[:10]
        refs = refs[10:]
    elif prologue == 'norm':
        x_ref, g_ref, b_ref, w_ref, td_ref, tl_ref, h_ref = refs[:7]
        refs = refs[7:]
    else:
        x_ref, w_ref, td_ref, tl_ref = refs[:4]
        refs = refs[4:]
    na_ref, df_ref, d0_ref, d1_ref, d2_ref, u_ref, wb_ref, zs1_ref, zs2_ref = refs

    @pl.when((pl.program_id(0) == 0) & (pl.program_id(1) == 0))
    def _():
        wb_ref[...] = w_ref[...].astype(BF16)

    x = x_ref[0]
    if prologue == 'moe':
        ffn = (r_ref[0, :, 2:3] * _unpack_bf16_pairs(y1_ref[0, 0])
               + r_ref[0, :, 3:4] * _unpack_bf16_pairs(y2_ref[0, 0]))
        x = _layer_norm(alpha * x + ffn, g_ref[...], b_ref[...])
        h_ref[0] = x
    elif prologue == 'norm':
        x = _layer_norm(x, g_ref[...], b_ref[...])
        h_ref[0] = x
    xb = x.astype(BF16)
    outs = (na_ref, df_ref, d0_ref, d1_ref, d2_ref)
    stage = (None, None, None, zs1_ref, zs2_ref)
    for (col0, r, half, head_w), o_ref, zs_ref in zip(QKV_SEGMENTS, outs, stage):
        z = _dot(xb, wb_ref[:, col0:col0 + QKV_W])
        t_ref = td_ref if head_w == DIFF_QK_DIM else tl_ref
        _qkv_epilogue(z, t_ref, o_ref, zs_ref, r, half, head_w ** -0.5 * LOG2E)
    u_ref[0] = _dot(xb, wb_ref[:, COL_POOL:COL_POOL + WCOL])


def _proj_branches(x3, w_in, l, tab_diff, tab_dil, norm=None, moe=None):
    b, s, d = x3.shape
    tm = QKV_TM
    alpha = None
    if moe:
        routing, y12, moe_g, moe_b, alpha = moe
        y4 = y12.reshape(2, b, s, d // 2)
        pre_specs = [pl.BlockSpec((1, tm, LANES), lambda bi, i: (bi, i, 0)),
                     pl.BlockSpec((1, 1, tm, d // 2), lambda bi, i: (0, bi, i, 0)),
                     pl.BlockSpec((1, 1, tm, d // 2), lambda bi, i: (1, bi, i, 0))]
        pre_args = [routing.reshape(b, s, LANES), y4, y4]
        norm = (moe_g, moe_b)
    else:
        pre_specs, pre_args = [], []
    prologue = 'moe' if moe else ('norm' if norm else None)
    qkv_spec = lambda r: pl.BlockSpec((1, r, tm // r, QKV_W), lambda bi, i: (bi, 0, i, 0))
    qkv_shape = lambda r: jax.ShapeDtypeStruct((b, r, s // r, QKV_W), BF16)
    tab_spec = pl.BlockSpec((3, tm, LANES), lambda bi, i: (0, i, 0))
    row_spec = pl.BlockSpec((1, tm, d), lambda bi, i: (bi, i, 0))
    vec = pl.BlockSpec((1, d), lambda bi, i: (0, 0))
    w_spec = pl.BlockSpec((None, d, COL_GATE), lambda bi, i: (l, 0, 0), pipeline_mode=pl.Buffered(1))
    rs = [seg[1] for seg in QKV_SEGMENTS]
    in_specs = [row_spec] + pre_specs + ([vec, vec] if norm else []) + [w_spec, tab_spec, tab_spec]
    args = [x3] + pre_args + ([a.reshape(1, d) for a in norm] if norm else []) + [w_in, tab_diff, tab_dil]
    outs = pl.pallas_call(
        functools.partial(_proj_kernel, prologue=prologue, alpha=alpha),
        grid=(b, s // tm),
        in_specs=in_specs,
        out_specs=([row_spec] if norm else []) + [qkv_spec(r) for r in rs] + [
            pl.BlockSpec((1, tm, WCOL), lambda bi, i: (bi, i, 0))],
        out_shape=([jax.ShapeDtypeStruct((b, s, d), F32)] if norm else []) + [
            qkv_shape(r) for r in rs] + [jax.ShapeDtypeStruct((b, s, WCOL), F32)],
        scratch_shapes=[pltpu.VMEM((d, COL_GATE), BF16)] + [pltpu.VMEM((QKV_W // LANES, tm, LANES), F32)] * 2,
        compiler_params=_cparams(("arbitrary", "arbitrary")),
        name="proj_branches",
    )(*args)
    n_qkv = len(rs)
    head = list(outs[:-n_qkv - 1])
    return head + [o.reshape(b, s, QKV_W) for o in outs[-n_qkv - 1:-1]] + [outs[-1]]


def _rope_tables(seq, head_w):
    rot = head_w // 4
    half = rot // 2
    inv_freq = jnp.exp(jnp.arange(half, dtype=F32) * (-2.0 * math.log(ROPE_THETA) / rot))
    ang = jnp.arange(seq, dtype=jnp.int32).astype(F32)[:, None] * inv_freq[None, :]
    cos, sin = jnp.cos(ang), jnp.sin(ang)
    zero = jnp.zeros((seq, head_w - rot), F32)
    zh = jnp.zeros((seq, half), F32)
    t0 = jnp.concatenate([cos, cos, jnp.ones((seq, head_w - rot), F32)], axis=1)
    t1 = jnp.concatenate([-sin, zh, zero], axis=1)
    t2 = jnp.concatenate([zh, sin, zero], axis=1)
    reps = LANES // head_w
    return jnp.stack([jnp.tile(t0, (1, reps)), jnp.tile(t1, (1, reps)), jnp.tile(t2, (1, reps))])


def _rope(x, t_ref, half):
    return (x * t_ref[0] + pltpu.roll(x, LANES - half, 1) * t_ref[1]
            + pltpu.roll(x, half, 1) * t_ref[2])


LOG2E = math.log2(math.e)


def _na_bias_table(rpb):
    kr, kc = NA_ROWS, NA_COLS
    n_heads = rpb.shape[0]
    col = np.arange(GRID_W)
    col_start = np.clip(col - kc // 2, 0, GRID_W - kc)
    in_win = (col[None, :] >= col_start[:, None]) & (col[None, :] < col_start[:, None] + kc)
    dc = np.clip(col[None, :] - col[:, None] + kc - 1, 0, 2 * kc - 2)
    sel_c = ((dc[..., None] == np.arange(2 * kc - 1)) & in_win[..., None]).astype(np.float32)
    rp = rpb.astype(F32).reshape(n_heads // 2, 2, 2 * kr - 1, 2 * kc - 1) * LOG2E
    tiles = jnp.einsum('phaj,qxj->phaqx', rp, jnp.asarray(sel_c), precision=lax.Precision.HIGHEST)
    tiles = tiles + jnp.asarray(np.where(in_win, 0.0, NEG).astype(np.float32))
    tiles = jnp.concatenate([tiles, jnp.full_like(tiles[:, :, :1], NEG)], axis=2)
    return jnp.concatenate([tiles, tiles], axis=-1)


NA_QROWS = 4
NA_KROWS = 12
NA_MASKED = 2 * NA_ROWS - 1


def _na_row_offsets(n_rows):
    table = []
    for blk in (0, 1, n_rows // NA_QROWS - 1):
        ws = _na_window_start(blk, n_rows)
        rows = []
        for i in range(NA_QROWS):
            r = blk * NA_QROWS + i
            start = min(max(r - NA_ROWS // 2, 0), n_rows - NA_ROWS)
            rows.append([ws + kk - r + NA_ROWS - 1 if start <= ws + kk < start + NA_ROWS else NA_MASKED
                         for kk in range(NA_KROWS)])
        table.append(rows)
    return table


def _na_window_start(blk, n_rows):
    lo = blk * NA_QROWS - NA_ROWS // 2
    hi = n_rows - NA_KROWS
    if isinstance(blk, int):
        return min(max(lo, 0), hi)
    return jnp.clip(lo, 0, hi)


def _stack_heads(q, lane):
    zero = jnp.zeros_like(q)
    return jnp.concatenate([jnp.where(lane < HEAD_DIM, q, zero), jnp.where(lane < HEAD_DIM, zero, q)],
                           axis=0)


def _na_kernel(q_ref, k_ref, v_ref, tile_ref, o_ref, bias_ref, *, blocks_per_step, n_rows):
    step = pl.program_id(2)
    nq = NA_QROWS * GRID_W
    kwin = NA_KROWS * GRID_W
    lane = _lane((nq, LANES))
    last_blk = n_rows // NA_QROWS - 1

    @pl.when(step == 0)
    def _():
        low = _lane((GRID_W, LANES)) < GRID_W
        for case, rows in enumerate(_na_row_offsets(n_rows)):
            for hh in range(2):
                for i, offs in enumerate(rows):
                    r0 = (hh * NA_QROWS + i) * GRID_W
                    for kp in range(NA_KROWS // 2):
                        bias_ref[case, r0:r0 + GRID_W, kp * LANES:(kp + 1) * LANES] = jnp.where(
                            low, tile_ref[0, hh, offs[2 * kp]], tile_ref[0, hh, offs[2 * kp + 1]])

    for i in range(blocks_per_step):
        blk = step * blocks_per_step + i
        case = jnp.where(blk == 0, 0, jnp.where(blk == last_blk, 2, 1))
        k0 = pl.multiple_of(_na_window_start(blk, n_rows) * GRID_W, GRID_W)
        q2 = _stack_heads(q_ref[0, i * nq:(i + 1) * nq, :], lane)
        sc = _dot_nt(q2, k_ref[0, pl.ds(k0, kwin), :]) + bias_ref[case]
        m = jnp.max(sc, axis=-1, keepdims=True)
        e = jnp.exp2(sc - m)
        den = jnp.sum(e, axis=-1, keepdims=True)
        pv = _dot(e.astype(BF16), v_ref[0, pl.ds(k0, kwin), :]) / den
        o = jnp.where(lane < HEAD_DIM, pv[0:nq], pv[nq:2 * nq])
        o_ref[0, i * nq:(i + 1) * nq, :] = o.astype(BF16)


def _na_attention(qkv, tiles):
    b, s, _ = qkv.shape
    n_rows = s // GRID_W
    bps = 8
    rps = bps * NA_QROWS
    tq = rps * GRID_W
    return pl.pallas_call(
        functools.partial(_na_kernel, blocks_per_step=bps, n_rows=n_rows),
        grid=(b, 2, n_rows // rps),
        in_specs=[pl.BlockSpec((1, tq, LANES), lambda bi, hp, i: (bi, i, hp)),
                  pl.BlockSpec((1, s, LANES), lambda bi, hp, i: (bi, 0, 2 + hp)),
                  pl.BlockSpec((1, s, LANES), lambda bi, hp, i: (bi, 0, 4 + hp)),
                  pl.BlockSpec((1, 2, NA_MASKED + 1, GRID_W, LANES), lambda bi, hp, i: (hp, 0, 0, 0, 0))],
        out_specs=pl.BlockSpec((1, tq, LANES), lambda bi, hp, i: (bi, i, hp)),
        out_shape=jax.ShapeDtypeStruct((b, s, 2 * LANES), BF16),
        scratch_shapes=[pltpu.VMEM((3, 2 * NA_QROWS * GRID_W, NA_KROWS * GRID_W), F32)],
        compiler_params=_cparams(("parallel", "parallel", "arbitrary")),
        name="na_attn",
    )(qkv, qkv, qkv, tiles)


def _diff_kernel(lam_ref, q_ref, k_ref, v_ref, g_ref, o_ref, *, lam_init, rows):
    tq = q_ref.shape[1]
    lane = _lane((rows, LANES))
    dl = lam_ref[...]
    lam = (jnp.exp(jnp.sum(dl[0:1] * dl[1:2], axis=-1, keepdims=True))
           - jnp.exp(jnp.sum(dl[2:3] * dl[3:4], axis=-1, keepdims=True)) + lam_init)
    v = v_ref[0]
    v_lane = _lane(v.shape)
    one = jnp.ones_like(v)
    v_ext = (jnp.where(v_lane < HEAD_DIM, v, one), jnp.where(v_lane < HEAD_DIM, one, v))
    in_h0 = lane < HEAD_DIM
    for r0 in range(0, tq, rows):
        q = q_ref[0, r0:r0 + rows, :]
        zero = jnp.zeros_like(q)
        pv = []
        for lo in range(0, LANES, DIFF_QK_DIM):
            qm = jnp.where((lane >= lo) & (lane < lo + DIFF_QK_DIM), q, zero)
            sc = _dot_nt(qm, k_ref[0])
            e = jnp.exp2(sc - jnp.max(sc, axis=-1, keepdims=True))
            ev = _dot(e.astype(BF16), v_ext[lo // HEAD_DIM])
            pv.append(ev / pltpu.roll(ev, HEAD_DIM, 1))
        o = jnp.where(in_h0, pv[0] - lam * pv[1], pv[2] - lam * pv[3])
        o2 = o * o
        ms0 = jnp.sum(jnp.where(in_h0, o2, 0.0), axis=-1, keepdims=True) / HEAD_DIM
        ms1 = jnp.sum(jnp.where(in_h0, 0.0, o2), axis=-1, keepdims=True) / HEAD_DIM
        ms = jnp.where(in_h0, ms0, ms1)
        o = o * lax.rsqrt(ms + LN_EPS) * g_ref[...] * (1.0 - lam_init)
        o_ref[0, r0:r0 + rows, :] = o.astype(BF16)


def _diff_attention(qkv, diff_lam, subln_g, lam_init):
    b, s, _ = qkv.shape
    tq = 1024
    g2 = jnp.tile(subln_g.reshape(1, HEAD_DIM), (1, 2))
    return pl.pallas_call(
        functools.partial(_diff_kernel, lam_init=lam_init, rows=512),
        grid=(b, 2, s // tq),
        in_specs=[pl.BlockSpec((4, DIFF_QK_DIM), lambda bi, hp, i: (0, 0)),
                  pl.BlockSpec((1, tq, LANES), lambda bi, hp, i: (bi, i, hp)),
                  pl.BlockSpec((1, s, LANES), lambda bi, hp, i: (bi, 0, 2 + hp)),
                  pl.BlockSpec((1, s, LANES), lambda bi, hp, i: (bi, 0, 4 + hp)),
                  pl.BlockSpec((1, LANES), lambda bi, hp, i: (0, 0))],
        out_specs=pl.BlockSpec((1, tq, LANES), lambda bi, hp, i: (bi, i, hp)),
        out_shape=jax.ShapeDtypeStruct((b, s, 2 * LANES), BF16),
        compiler_params=_cparams(("parallel", "parallel", "arbitrary")),
        name="diff_attn",
    )(diff_lam, qkv, qkv, qkv, g2)


POOL_PAD = 16
POOL_CHUNK = 512
POOL_HALO = 8


def _pool_kernel(u_ref, w_ref, sc_ref, o_ref, p_ref):
    s = u_ref.shape[1]
    width = u_ref.shape[2]
    p_ref[0:POOL_PAD, :] = jnp.zeros((POOL_PAD, width), F32)
    p_ref[POOL_PAD + s:POOL_PAD + s + POOL_PAD, :] = jnp.zeros((POOL_PAD, width), F32)
    p_ref[POOL_PAD:POOL_PAD + s, :] = u_ref[0]
    n = POOL_CHUNK + 2 * POOL_HALO
    lane = _lane((POOL_CHUNK, width))
    row = lax.broadcasted_iota(jnp.int32, (POOL_CHUNK, width), 0)
    w_of_lane = jnp.where(lane < POOL_GROUP, 2, jnp.where(lane < 2 * POOL_GROUP, 4,
                          jnp.where(lane < 3 * POOL_GROUP, 8, 16)))

    def body(ci, carry):
        c0 = pl.multiple_of(ci * POOL_CHUNK, POOL_CHUNK)
        x = p_ref[pl.ds(c0 + POOL_PAD - POOL_HALO, n), :]
        w2 = x + pltpu.roll(x, 1, 0)
        w4 = pltpu.roll(w2, 1, 0) + pltpu.roll(w2, n - 1, 0)
        w8 = pltpu.roll(w4, 2, 0) + pltpu.roll(w4, n - 2, 0)
        w16 = pltpu.roll(w8, 4, 0) + pltpu.roll(w8, n - 4, 0)
        u = x[POOL_HALO:POOL_HALO + POOL_CHUNK]
        wsum = jnp.where(lane < POOL_GROUP, w2[POOL_HALO:POOL_HALO + POOL_CHUNK],
                         jnp.where(lane < 2 * POOL_GROUP, w4[POOL_HALO:POOL_HALO + POOL_CHUNK],
                                   jnp.where(lane < 3 * POOL_GROUP, w8[POOL_HALO:POOL_HALO + POOL_CHUNK],
                                             w16[POOL_HALO:POOL_HALO + POOL_CHUNK])))
        t = row + c0
        half_w = w_of_lane // 2
        lo = jnp.maximum(t - half_w, 0)
        hi = jnp.minimum(t + w_of_lane - 1 - half_w, s - 1)
        cnt = (hi - lo + 1).astype(F32)
        dlt = wsum / cnt - u
        y = _dot(dlt.astype(BF16), w_ref[...]) * sc_ref[...]
        o_ref[0, pl.ds(c0, POOL_CHUNK), :] = y.astype(BF16)
        return carry

    lax.fori_loop(0, s // POOL_CHUNK, body, 0)


def _pool(u3, pool_w, pool_scale):
    b, s, width = u3.shape
    wbd = jax.scipy.linalg.block_diag(*[pool_w[g] for g in range(len(POOL_WINDOWS))]).astype(BF16)
    return pl.pallas_call(
        _pool_kernel,
        grid=(b,),
        in_specs=[pl.BlockSpec((1, s, width), lambda bi: (bi, 0, 0)),
                  pl.BlockSpec((width, width), lambda bi: (0, 0)),
                  pl.BlockSpec((1, width), lambda bi: (0, 0))],
        out_specs=pl.BlockSpec((1, s, width), lambda bi: (bi, 0, 0)),
        out_shape=jax.ShapeDtypeStruct((b, s, width), BF16),
        scratch_shapes=[pltpu.VMEM((s + 2 * POOL_PAD, width), F32)],
        compiler_params=_cparams(("parallel",)),
        name="pool",
    )(u3, wbd, pool_scale.reshape(1, width))


SWA_Q = 128
SWA_BAND = SWA_Q + 2 * DIL_HALF


def _swa_kernel(q_ref, k_ref, v_ref, o_ref, l_ref, *, length):
    lane = _lane((SWA_Q, LANES))
    rel = (lax.broadcasted_iota(jnp.int32, (2 * SWA_Q, SWA_BAND), 1)
           - (lax.broadcasted_iota(jnp.int32, (2 * SWA_Q, SWA_BAND), 0) & (SWA_Q - 1)))
    masks = {}
    for i in range(q_ref.shape[1] // SWA_Q):
        r0 = i * SWA_Q
        l0 = r0 % length
        lo = r0 - l0 + min(max(l0 - DIL_HALF, 0), length - SWA_BAND)
        if lo - r0 not in masks:
            d = rel + (lo - r0)
            masks[lo - r0] = jnp.where((d >= -DIL_HALF) & (d <= DIL_HALF), 0.0, NEG)
        q2 = _stack_heads(q_ref[0, r0:r0 + SWA_Q, :], lane)
        sc = _dot_nt(q2, k_ref[0, lo:lo + SWA_BAND, :]) + masks[lo - r0]
        m = jnp.max(sc, axis=-1, keepdims=True)
        e = jnp.exp2(sc - m)
        den = jnp.sum(e, axis=-1, keepdims=True)
        pv = _dot(e.astype(BF16), v_ref[0, lo:lo + SWA_BAND, :]) / den
        lse2 = m + jnp.log2(den)
        o_ref[0, i * SWA_Q:(i + 1) * SWA_Q, :] = jnp.where(lane < HEAD_DIM, pv[0:SWA_Q], pv[SWA_Q:])
        l_ref[0, i * SWA_Q:(i + 1) * SWA_Q, :] = jnp.where(lane < HEAD_DIM, lse2[0:SWA_Q], lse2[SWA_Q:])


def _swa(qkv, length):
    b, s, _ = qkv.shape
    shp = jax.ShapeDtypeStruct((b, s, 2 * LANES), F32)
    seq = lambda col: pl.BlockSpec((1, s, LANES), lambda bi, hp: (bi, 0, col + hp))
    return pl.pallas_call(
        functools.partial(_swa_kernel, length=length),
        grid=(b, 2),
        in_specs=[seq(0), seq(2), seq(4)],
        out_specs=[seq(0), seq(0)],
        out_shape=[shp, shp],
        compiler_params=_cparams(("parallel", "parallel")),
        name=f"swa_l{length}",
    )(qkv, qkv, qkv)


def _dil_combine(dil_refs, stage_refs, tm):
    (o0_ref, l0_ref), (o1_ref, l1_ref), (o2_ref, l2_ref) = dil_refs
    so1, sl1, so2, sl2 = stage_refs
    r1 = DIL_PATTERNS[1][1]
    r2 = DIL_PATTERNS[2][1]
    halves = []
    for hp in range(2):
        cols = slice(hp * LANES, (hp + 1) * LANES)
        for m in range(r1):
            so1[pl.ds(m, tm // r1, stride=r1), :] = o1_ref[0, m, :, cols]
            sl1[pl.ds(m, tm // r1, stride=r1), :] = l1_ref[0, m, :, cols]
        for m in range(r2):
            so2[pl.ds(m, tm // r2, stride=r2), :] = o2_ref[0, m, :, cols]
            sl2[pl.ds(m, tm // r2, stride=r2), :] = l2_ref[0, m, :, cols]
        l0, l1, l2 = l0_ref[0, 0, :, cols], sl1[...], sl2[...]
        mx = jnp.maximum(jnp.maximum(l0, l1), l2)
        e0, e1, e2 = jnp.exp2(l0 - mx), jnp.exp2(l1 - mx), jnp.exp2(l2 - mx)
        den = e0 + e1 + e2
        halves.append((e0 / den) * o0_ref[0, 0, :, cols] + (e1 / den) * so1[...] + (e2 / den) * so2[...])
    return jnp.concatenate(halves, axis=1)


def _merge_kernel(ya_ref, yb_ref, yc_ref, o0_ref, l0_ref, o1_ref, l1_ref, o2_ref, l2_ref,
                  wg_ref, bg_ref, wb_ref, wo_ref, h_ref, g_ref, b_ref, wr_ref, br_ref,
                  o_ref, op_ref, r_ref, cnt_ref, so1, sl1, so2, sl2, carry, before, *, alpha):
    tm, d = h_ref.shape

    @pl.when(pl.program_id(0) == 0)
    def _():
        _route_init(carry, before)

    hb = h_ref[...].astype(BF16)
    y_d = _dil_combine(((o0_ref, l0_ref), (o1_ref, l1_ref), (o2_ref, l2_ref)),
                       (so1, sl1, so2, sl2), tm).astype(BF16)
    merged = None
    for n, y in enumerate((ya_ref[...], yb_ref[...], yc_ref[...], y_d)):
        zg = _dot(hb, wg_ref[:, n * d:(n + 1) * d]) + bg_ref[:, n * d:(n + 1) * d]
        term = _sigmoid(zg) * _dot(y, wb_ref[n])
        merged = term if merged is None else merged + term
    mix = _dot(merged.astype(BF16), wo_ref[...])
    h1 = _layer_norm(alpha * h_ref[...] + mix, g_ref[...], b_ref[...])
    o_ref[...] = h1
    op_ref[...] = _pack_bf16_pairs(h1)
    _route_finish(_route_select(h1, wr_ref, br_ref), r_ref, cnt_ref, carry, before)


def _merge(ys, dil, wg, b_gate, wb, wo, h2, g, bb, router, alpha):
    t, d = h2.shape
    w_route, b_route = router
    b, s, bw = dil[0][0].shape
    nb = len(ys) + 1
    tm = QKV_TM
    nt = s // tm
    yspec = pl.BlockSpec((tm, bw), lambda i: (i, 0))
    vec = pl.BlockSpec((1, d), lambda i: (0, 0))
    dil_args, dil_specs = [], []
    for (o, lse), (_, r) in zip(dil, DIL_PATTERNS):
        spec = pl.BlockSpec((1, r, tm // r, bw), lambda i: (i // nt, 0, i % nt, 0))
        dil_args += [o.reshape(b, r, s // r, bw), lse.reshape(b, r, s // r, bw)]
        dil_specs += [spec, spec]
    return pl.pallas_call(
        functools.partial(_merge_kernel, alpha=alpha),
        grid=(t // tm,),
        in_specs=[yspec, yspec, yspec] + dil_specs + [
                  pl.BlockSpec((d, nb * d), lambda i: (0, 0)),
                  pl.BlockSpec((1, nb * d), lambda i: (0, 0)),
                  pl.BlockSpec((nb, bw, d), lambda i: (0, 0, 0)),
                  pl.BlockSpec((d, d), lambda i: (0, 0)),
                  pl.BlockSpec((tm, d), lambda i: (i, 0)),
                  vec, vec,
                  pl.BlockSpec((LANES, d), lambda i: (0, 0)),
                  pl.BlockSpec((LANES, 1), lambda i: (0, 0))],
        out_specs=[pl.BlockSpec((tm, d), lambda i: (i, 0)),
                   pl.BlockSpec((tm, d // 2), lambda i: (i, 0)),
                   pl.BlockSpec((tm, LANES), lambda i: (i, 0)),
                   pl.BlockSpec((LANES, 1), lambda i: (0, 0))],
        out_shape=[jax.ShapeDtypeStruct((t, d), F32), jax.ShapeDtypeStruct((t, d // 2), jnp.int32),
                   jax.ShapeDtypeStruct((t, LANES), F32), jax.ShapeDtypeStruct((LANES, 1), F32)],
        scratch_shapes=[pltpu.VMEM((tm, LANES), F32)] * 4 + [pltpu.VMEM((LANES, 1), F32),
                                                               pltpu.VMEM((tm, tm), BF16)],
        compiler_params=_cparams(("arbitrary",)),
        name="merge",
    )(*ys, *dil_args, wg, b_gate.reshape(1, nb * d), wb, wo, h2, g.reshape(1, d), bb.reshape(1, d),
      w_route, b_route)


def _route_init(carry, before):
    tm = before.shape[0]
    carry[...] = jnp.zeros_like(carry)
    before[...] = (lax.broadcasted_iota(jnp.int32, (tm, tm), 0)
                   < lax.broadcasted_iota(jnp.int32, (tm, tm), 1)).astype(BF16)


def _route_select(h, w_ref, b_ref):
    tm = h.shape[0]
    logits = lax.dot_general(w_ref[...], h, (((1,), (1,)), ((), ())),
                             preferred_element_type=F32,
                             precision=lax.Precision.HIGHEST) + b_ref[...]
    row = lax.broadcasted_iota(jnp.int32, (LANES, tm), 0)
    big = jnp.int32(1 << 20)
    is_g = row < N_GROUPS
    gl = jnp.where(is_g, logits, -jnp.inf)
    gmax = jnp.max(gl, axis=0, keepdims=True)
    gsel = jnp.min(jnp.where(is_g & (gl == gmax), row, big), axis=0, keepdims=True)
    pg = 1.0 / jnp.sum(jnp.exp(gl - gmax), axis=0, keepdims=True)
    e_lo = N_GROUPS + gsel * EXPERTS_PER_GROUP
    in_grp = (row >= e_lo) & (row < e_lo + EXPERTS_PER_GROUP)
    el = jnp.where(in_grp, logits, -jnp.inf)
    v1 = jnp.max(el, axis=0, keepdims=True)
    i1 = jnp.min(jnp.where(in_grp & (el == v1), row, big), axis=0, keepdims=True)
    el2 = jnp.where(row == i1, -jnp.inf, el)
    v2 = jnp.max(el2, axis=0, keepdims=True)
    i2 = jnp.min(jnp.where(in_grp & (row != i1) & (el2 == v2), row, big), axis=0, keepdims=True)
    t2 = jnp.exp(v2 - v1)
    g1 = pg / (1.0 + t2)
    g2 = pg * t2 / (1.0 + t2)
    return dict(row=row, i1=i1, i2=i2, g1=g1, g2=g2,
                oh1=row == i1, oh2=row == i2)


def _route_finish(sel, o_ref, cnt_ref, carry, before):
    row, i1, i2, g1, g2, oh1, oh2 = (sel[k] for k in ('row', 'i1', 'i2', 'g1', 'g2', 'oh1', 'oh2'))
    oh1b = oh1.astype(BF16)
    oh2b = oh2.astype(BF16)
    c0 = carry[...]
    c1 = c0 + jnp.sum(oh1b.astype(F32), axis=1, keepdims=True)
    rank1 = jnp.sum(jnp.where(oh1, _dot(oh1b, before[...]) + c0, 0.0), axis=0, keepdims=True)
    rank2 = jnp.sum(jnp.where(oh2, _dot(oh2b, before[...]) + c1, 0.0), axis=0, keepdims=True)
    c2 = c1 + jnp.sum(oh2b.astype(F32), axis=1, keepdims=True)
    carry[...] = c2
    cnt_ref[...] = c2
    e1 = (i1 - N_GROUPS).astype(F32)
    e2 = (i2 - N_GROUPS).astype(F32)
    out_t = jnp.where(row == 0, e1, jnp.where(row == 1, e2, jnp.where(row == 2, g1, jnp.where(
        row == 3, g2, jnp.where(row == 4, rank1, jnp.where(row == 5, rank2, 0.0))))))
    o_ref[...] = out_t.T


def _router_params(rg_w, rg_b, re_w, re_b):
    d = rg_w.shape[0]
    pad = LANES - N_GROUPS - N_EXPERTS
    w_t = jnp.concatenate([rg_w.T, re_w.T, jnp.zeros((pad, d), F32)], axis=0)
    bias = jnp.concatenate([rg_b, re_b, jnp.zeros((pad,), F32)]).reshape(LANES, 1)
    return w_t, bias


MOE_FIRST, MOE_SLOT, MOE_HAS_NEXT = 1, 2, 4


def _moe_kernel(be_ref, nv_ref, fl_ref, nx_ref, x_ref, wg_hbm, wu_hbm, wd_hbm, o_ref,
                wgb, wub, wdb, stage_g, stage_u, stage_d, sem, *, layer):
    j = pl.program_id(0)
    n_valid = nv_ref[j]
    flags = fl_ref[j]
    slot = (flags // MOE_SLOT) % 2

    def weight_copies(e, s):
        return [pltpu.make_async_copy(w_hbm.at[layer, e], stage.at[s], sem.at[s, k])
                for k, (w_hbm, stage) in enumerate(((wg_hbm, stage_g), (wu_hbm, stage_u),
                                                    (wd_hbm, stage_d)))]

    @pl.when(flags % 2 == MOE_FIRST)
    def _():
        @pl.when(j == 0)
        def _():
            for cp in weight_copies(be_ref[j], slot):
                cp.start()

        for cp in weight_copies(be_ref[j], slot):
            cp.wait()

        @pl.when((flags // MOE_HAS_NEXT) % 2 == 1)
        def _():
            for cp in weight_copies(nx_ref[j], 1 - slot):
                cp.start()

        wgb[...] = stage_g[slot].astype(BF16)
        wub[...] = stage_u[slot].astype(BF16)
        wdb[...] = stage_d[slot].astype(BF16)

    @pl.when(n_valid > 0)
    def _():
        row = lax.broadcasted_iota(jnp.int32, x_ref.shape, 0)
        x = _unpack_bf16_pairs(jnp.where(row < n_valid, x_ref[...], 0)).astype(BF16)
        g = _dot(x, wgb[...])
        u = _dot(x, wub[...])
        hmid = (g * _sigmoid(g)) * u
        o_ref[...] = _pack_bf16_pairs(_dot(hmid.astype(BF16), wdb[...]))

    @pl.when(n_valid <= 0)
    def _():
        o_ref[...] = jnp.zeros_like(o_ref)


def _moe_experts(xs, block_e, n_valid, flags, next_e, wg, wu, wd, l):
    cap, dp = xs.shape
    d = 2 * dp
    de = wg.shape[3]
    n_blocks = cap // MOE_ROWS
    hbm = pl.BlockSpec(memory_space=pl.ANY)
    grid_spec = pltpu.PrefetchScalarGridSpec(
        num_scalar_prefetch=4,
        grid=(n_blocks,),
        in_specs=[pl.BlockSpec((MOE_ROWS, dp), lambda j, *_: (j, 0)), hbm, hbm, hbm],
        out_specs=pl.BlockSpec((MOE_ROWS, dp), lambda j, *_: (j, 0)),
        scratch_shapes=[pltpu.VMEM((d, de), BF16), pltpu.VMEM((d, de), BF16), pltpu.VMEM((de, d), BF16),
                        pltpu.VMEM((2, d, de), F32), pltpu.VMEM((2, d, de), F32),
                        pltpu.VMEM((2, de, d), F32), pltpu.SemaphoreType.DMA((2, 3))],
    )
    return pl.pallas_call(
        functools.partial(_moe_kernel, layer=l),
        grid_spec=grid_spec,
        out_shape=jax.ShapeDtypeStruct((cap, dp), jnp.int32),
        compiler_params=_cparams(("arbitrary",)),
        name="moe_experts",
    )(block_e, n_valid, flags, next_e, xs, wg, wu, wd)


SC_CORES = 2
SC_SUBCORES = 16
SC_WORKERS = SC_CORES * SC_SUBCORES
SC_CHUNK = 64


def _sc_mesh():
    return plsc.VectorSubcoreMesh(core_axis_name="c", subcore_axis_name="s",
                                  num_cores=SC_CORES, num_subcores=SC_SUBCORES)


def _sc_scatter_rows(src, idx0, idx1, n_out):
    t, d = src.shape
    per_w = t // SC_WORKERS
    n_chunks = per_w // SC_CHUNK
    idx_shape = (SC_WORKERS, n_chunks, SC_CHUNK)

    @functools.partial(
        pl.kernel, mesh=_sc_mesh(),
        out_type=jax.ShapeDtypeStruct((n_out, d), src.dtype),
        scratch_types=[pltpu.VMEM((n_chunks, SC_CHUNK), jnp.int32),
                       pltpu.VMEM((n_chunks, SC_CHUNK), jnp.int32),
                       pltpu.VMEM((2, SC_CHUNK, d), src.dtype),
                       pltpu.SemaphoreType.DMA((2,)), pltpu.SemaphoreType.DMA((2,))],
        name="sc_dispatch",
    )
    def k(src_hbm, i0_hbm, i1_hbm, out_hbm, i0_v, i1_v, rows_v, sem_in, sem_out):
        wid = lax.axis_index("s") * SC_CORES + lax.axis_index("c")
        base = wid * per_w
        pltpu.sync_copy(i0_hbm.at[wid], i0_v)
        pltpu.sync_copy(i1_hbm.at[wid], i1_v)

        def load(ci):
            s = ci % 2
            return pltpu.make_async_copy(src_hbm.at[pl.ds(base + ci * SC_CHUNK, SC_CHUNK)],
                                         rows_v.at[s], sem_in.at[s])

        def scatter(ci, idx_v):
            s = ci % 2
            return pltpu.make_async_copy(rows_v.at[s], out_hbm.at[idx_v.at[ci]], sem_out.at[s])

        load(0).start()
        for ci in range(n_chunks):
            load(ci).wait()
            if ci >= 1:
                scatter(ci - 1, i0_v).wait()
                scatter(ci - 1, i1_v).wait()
            if ci + 1 < n_chunks:
                load(ci + 1).start()
            scatter(ci, i0_v).start()
            scatter(ci, i1_v).start()
        scatter(n_chunks - 1, i0_v).wait()
        scatter(n_chunks - 1, i1_v).wait()

    return k(src, idx0.reshape(idx_shape), idx1.reshape(idx_shape))


def _sc_gather_rows(table, idx):
    n = idx.shape[0]
    d = table.shape[1]
    per_w = n // SC_WORKERS
    n_chunks = per_w // SC_CHUNK

    @functools.partial(
        pl.kernel, mesh=_sc_mesh(),
        out_type=jax.ShapeDtypeStruct((n, d), table.dtype),
        scratch_types=[pltpu.VMEM((n_chunks, SC_CHUNK), jnp.int32),
                       pltpu.VMEM((2, SC_CHUNK, d), table.dtype),
                       pltpu.SemaphoreType.DMA((2,)), pltpu.SemaphoreType.DMA((2,))],
        name="sc_collect",
    )
    def k(table_hbm, idx_hbm, out_hbm, idx_v, rows_v, sem_in, sem_out):
        wid = lax.axis_index("s") * SC_CORES + lax.axis_index("c")
        base = wid * per_w
        pltpu.sync_copy(idx_hbm.at[wid], idx_v)

        def gather(ci):
            s = ci % 2
            return pltpu.make_async_copy(table_hbm.at[idx_v.at[ci]], rows_v.at[s], sem_in.at[s])

        def store(ci):
            s = ci % 2
            return pltpu.make_async_copy(rows_v.at[s], out_hbm.at[pl.ds(base + ci * SC_CHUNK, SC_CHUNK)],
                                         sem_out.at[s])

        gather(0).start()
        for ci in range(n_chunks):
            gather(ci).wait()
            if ci >= 1:
                store(ci - 1).wait()
            if ci + 1 < n_chunks:
                gather(ci + 1).start()
            store(ci).start()
        store(n_chunks - 1).wait()

    return k(table, idx.reshape(SC_WORKERS, n_chunks, SC_CHUNK))


def _combine_kernel(h_ref, r_ref, y1_ref, y2_ref, g_ref, b_ref, o_ref, *, alpha):
    g1 = r_ref[:, 2:3]
    g2 = r_ref[:, 3:4]
    ffn = g1 * _unpack_bf16_pairs(y1_ref[...]) + g2 * _unpack_bf16_pairs(y2_ref[...])
    o_ref[...] = _layer_norm(alpha * h_ref[...] + ffn, g_ref[...], b_ref[...])


def _combine(h2, routing, y12, g, bb, alpha):
    t, d = h2.shape
    tm = 512
    nt = t // tm
    row = pl.BlockSpec((tm, d), lambda i: (i, 0))
    vec = pl.BlockSpec((1, d), lambda i: (0, 0))
    return pl.pallas_call(
        functools.partial(_combine_kernel, alpha=alpha),
        grid=(nt,),
        in_specs=[row, pl.BlockSpec((tm, LANES), lambda i: (i, 0)),
                  pl.BlockSpec((tm, d // 2), lambda i: (i, 0)),
                  pl.BlockSpec((tm, d // 2), lambda i: (nt + i, 0)), vec, vec],
        out_specs=row,
        out_shape=jax.ShapeDtypeStruct((t, d), F32),
        compiler_params=_cparams(("parallel",)),
        name="combine_ln",
    )(h2, routing, y12, y12, g.reshape(1, d), bb.reshape(1, d))


def _moe(h2, h2_packed, routing, counts, wg, wu, wd, l, g, bb, alpha):
    y12 = _moe_outputs(h2_packed, routing, counts, wg, wu, wd, l)
    return _combine(h2, routing, y12, g, bb, alpha)


def _moe_outputs(h2_packed, routing, counts, wg, wu, wd, l):
    t = h2_packed.shape[0]
    eid = routing[:, 0:2].astype(jnp.int32)
    rank = routing[:, 4:6].astype(jnp.int32)
    cnt = counts[N_GROUPS:N_GROUPS + N_EXPERTS, 0].astype(jnp.int32)
    padded = (cnt + MOE_ROWS - 1) // MOE_ROWS * MOE_ROWS
    pad_end = jnp.cumsum(padded)
    pad_start = pad_end - padded
    experts = jnp.arange(N_EXPERTS, dtype=jnp.int32)
    dest = jnp.sum(jnp.where(eid[..., None] == experts, pad_start, 0), axis=-1) + rank
    n_slots = t * TOP_K
    n_blocks = (n_slots + N_EXPERTS * (MOE_ROWS - 1) + MOE_ROWS - 1) // MOE_ROWS
    cap = n_blocks * MOE_ROWS
    blk_row = jnp.arange(n_blocks, dtype=jnp.int32) * MOE_ROWS
    block_e = jnp.minimum(jnp.sum((pad_end[None, :] <= blk_row[:, None]).astype(jnp.int32), axis=1),
                          N_EXPERTS - 1)
    is_e = block_e[:, None] == experts
    blk_cnt = jnp.sum(jnp.where(is_e, cnt, 0), axis=1)
    blk_start = jnp.sum(jnp.where(is_e, pad_start, 0), axis=1)
    n_valid = jnp.clip(blk_cnt - (blk_row - blk_start), 0, MOE_ROWS).astype(jnp.int32)
    used = cnt > 0
    later = (experts[None, :] > experts[:, None]) & used[None, :]
    next_used = jnp.min(jnp.where(later, experts[None, :], N_EXPERTS), axis=1)
    rank = jnp.cumsum(used.astype(jnp.int32)) - 1
    blk_first = (n_valid > 0) & (blk_row == blk_start)
    blk_next = jnp.sum(jnp.where(is_e, next_used, 0), axis=1)
    blk_slot = jnp.sum(jnp.where(is_e, rank, 0), axis=1) % 2
    flags = (blk_first.astype(jnp.int32) * MOE_FIRST + blk_slot * MOE_SLOT
             + (blk_next < N_EXPERTS).astype(jnp.int32) * MOE_HAS_NEXT).astype(jnp.int32)
    next_e = jnp.minimum(blk_next, N_EXPERTS - 1).astype(jnp.int32)
    xs = _sc_scatter_rows(h2_packed, dest[:, 0], dest[:, 1], cap)
    ys = _moe_experts(xs, block_e, n_valid, flags, next_e, wg, wu, wd, l)
    return _sc_gather_rows(ys, jnp.concatenate([dest[:, 0], dest[:, 1]]))


def _mixing_layer(h, b, s, l, p, lam_init, input_norm=None, input_moe=None):
    t = b * s
    alpha = (2 * p['w_in'].shape[0]) ** 0.25
    d = h.shape[1]
    w_in = p['w_in']
    tab_diff = _rope_tables(s, DIFF_QK_DIM)
    tab_dil = _rope_tables(s, HEAD_DIM)
    proj = _proj_branches(h.reshape(b, s, d), w_in, l, tab_diff, tab_dil, input_norm, input_moe)
    if input_norm or input_moe:
        h = proj.pop(0).reshape(t, d)
    qkv_na, qkv_diff, qkv_d0, qkv_d1, qkv_d2, u = proj
    y_a = _na_attention(qkv_na, _na_bias_table(p['na_rpb'][l]))
    y_b = _diff_attention(qkv_diff, p['diff_lam'][l], p['diff_subln_g'][l], lam_init)
    y_c = _pool(u, p['pool_w'][l], p['pool_scale'][l])
    dil = [_swa(qkv, s // r) for qkv, (_, r) in zip((qkv_d0, qkv_d1, qkv_d2), DIL_PATTERNS)]
    ys = [a.reshape(t, -1) for a in (y_a, y_b, y_c)]
    wg = lax.slice_in_dim(w_in[l], COL_GATE, COL_GATE + (len(ys) + 1) * d, axis=1).astype(BF16)
    router = _router_params(p['router_group_w'][l], p['router_group_b'][l], p['router_expert_w'][l],
                            p['router_expert_b'][l])
    return _merge(ys, dil, wg, p['b_gate'][l], p['w_branch'][l].astype(BF16), p['w_out'][l].astype(BF16), h,
                  p['ln1_g'][l], p['ln1_b'][l], router, alpha)


def kernel(x, emb_ln_g, emb_ln_b, w_in, b_gate, na_rpb, diff_lam, diff_subln_g, pool_w, pool_scale,
           w_branch, w_out, ln1_g, ln1_b, router_group_w, router_group_b, router_expert_w,
           router_expert_b, expert_w_gate, expert_w_up, expert_w_down, ln2_g, ln2_b):
    b, s, d = x.shape
    depth = w_in.shape[0]
    alpha = (2 * depth) ** 0.25
    p = dict(w_in=w_in, b_gate=b_gate, na_rpb=na_rpb, diff_lam=diff_lam, diff_subln_g=diff_subln_g,
             pool_w=pool_w, pool_scale=pool_scale, w_branch=w_branch, w_out=w_out, ln1_g=ln1_g, ln1_b=ln1_b,
             router_group_w=router_group_w, router_group_b=router_group_b,
             router_expert_w=router_expert_w, router_expert_b=router_expert_b)
    h = x.reshape(b * s, d)
    prev_moe = None
    for l in range(depth):
        lam_init = 0.8 - 0.6 * math.exp(-0.3 * l)
        h, h_packed, routing, counts = _mixing_layer(h, b, s, l, p, lam_init,
                                                     (emb_ln_g, emb_ln_b) if l == 0 else None, prev_moe)
        y12 = _moe_outputs(h_packed, routing, counts, expert_w_gate, expert_w_up, expert_w_down, l)
        prev_moe = (routing, y12, ln2_g[l], ln2_b[l], alpha)
    return _combine(h, routing, y12, ln2_g[depth - 1], ln2_b[depth - 1], alpha).reshape(b, s, d)
```
